```python
import jax, jax.numpy as jnp
from jax import lax
import numpy as np

D_MODEL = 1024
BATCH = 4
SEQ = 4096
DEPTH = 1
DEC_BATCH = 128
DEC_SEQ = 8
PAST_LEN = 8192
PAGE_SIZE = 128

N_HEADS_A = 4
HEAD_DIM_A = 128
WIDTH_A = N_HEADS_A * HEAD_DIM_A
N_HEADS_B = 8
HEAD_DIM_B = 64
WIDTH_B = N_HEADS_B * HEAD_DIM_B
MIX_WIDTH = WIDTH_A + WIDTH_B
CONV_WIDTH = 4
CONV_CH = 3 * WIDTH_A
DELTA_CHUNK = 64
DILATED_PATTERNS = ((128, 1), (512, 4), (2048, 16))
MAX_WINDOW = 2048
Q_BLOCK = 128
D_FF = 4 * D_MODEL
NORM_EPS = 1e-6
SPLIT_SIZES = (3 * WIDTH_A, WIDTH_A, N_HEADS_A, N_HEADS_A, WIDTH_B, WIDTH_B, WIDTH_B)
PROJ_COLS = sum(SPLIT_SIZES)

kernel_name = "hymba_gdn_dilated_swa_step"


def rmsnorm(x, g):
    xf = x.astype(jnp.float32)
    xf = xf * lax.rsqrt(jnp.mean(xf * xf, axis=-1, keepdims=True) + NORM_EPS)
    return (xf * g.astype(jnp.float32)).astype(x.dtype)


def l2norm(x):
    xf = x.astype(jnp.float32)
    return xf * lax.rsqrt(jnp.sum(xf * xf, axis=-1, keepdims=True) + NORM_EPS)


def short_conv(u, buf, w):
    t = u.shape[1]
    up = jnp.concatenate([buf.astype(u.dtype), u], axis=1)
    y = up[:, 0:t] * w[0]
    for i in range(1, CONV_WIDTH):
        y = y + up[:, i:i + t] * w[i]
    return jax.nn.silu(y), up[:, -(CONV_WIDTH - 1):]


def gated_delta_chunked(q, k, v, beta, g, s0):
    b, t, h, _ = q.shape
    dv = v.shape[-1]
    c = DELTA_CHUNK
    n = -(-t // c)
    pad = n * c - t

    def to_chunks(a):
        a = jnp.pad(a.astype(jnp.float32), [(0, 0), (0, pad)] + [(0, 0)] * (a.ndim - 2))
        a = jnp.moveaxis(a, 2, 1)
        return a.reshape(b, h, n, c, *a.shape[3:])

    qc, kc, vc, bc, gc = (to_chunks(a) for a in (q, k, v, beta, g))
    G = jnp.cumsum(gc, axis=-1)
    causal = jnp.tril(jnp.ones((c, c), bool))
    strict = jnp.tril(jnp.ones((c, c), bool), -1)
    diff = G[..., :, None] - G[..., None, :]
    decay = jnp.where(causal, jnp.exp(jnp.where(causal, diff, 0.0)), 0.0)
    kb = kc * bc[..., None]
    vb = vc * bc[..., None]
    lmat = jnp.where(strict, jnp.einsum('bhnik,bhnjk->bhnij', kb, kc) * decay, 0.0)
    a_mat = lmat + jnp.eye(c, dtype=jnp.float32)
    u = lax.linalg.triangular_solve(a_mat, vb, left_side=True, lower=True, unit_diagonal=True)
    w = lax.linalg.triangular_solve(a_mat, kb * jnp.exp(G)[..., None], left_side=True, lower=True,
                                    unit_diagonal=True)
    qk = jnp.einsum('bhnik,bhnjk->bhnij', qc, kc) * decay
    q_dec = qc * jnp.exp(G)[..., None]
    g_last = G[..., -1]
    k_dec = kc * jnp.exp(g_last[..., None] - G)[..., None]

    def step(s, xs):
        u_n, w_n, qk_n, qd_n, kd_n, gl_n = xs
        v_new = u_n - jnp.einsum('bhck,bhkv->bhcv', w_n, s)
        o_n = jnp.einsum('bhck,bhkv->bhcv', qd_n, s) + jnp.einsum('bhij,bhjv->bhiv', qk_n, v_new)
        s = s * jnp.exp(gl_n)[..., None, None] + jnp.einsum('bhck,bhcv->bhkv', kd_n, v_new)
        return s, o_n

    xs = tuple(jnp.moveaxis(a, 2, 0) for a in (u, w, qk, q_dec, k_dec, g_last))
    s_final, o = lax.scan(step, s0.astype(jnp.float32), xs)
    o = jnp.moveaxis(o, 0, 2).reshape(b, h, n * c, dv)[:, :, :t]
    return jnp.moveaxis(o, 1, 2), s_final


def dilated_mixture(q, k_all, v_all, q_index):
    scale = HEAD_DIM_B ** -0.5
    qf = q.astype(jnp.float32)
    outs, lses = [], []
    for window, dil in DILATED_PATTERNS:
        offs = jnp.arange(window // dil + 1) * dil
        idx = q_index[:, None] - offs[None, :]
        valid = idx >= 0
        idx = jnp.maximum(idx, 0)
        kg = jnp.take(k_all, idx, axis=1).astype(jnp.float32)
        vg = jnp.take(v_all, idx, axis=1).astype(jnp.float32)
        s = jnp.einsum('bqhd,bqjhd->bqhj', qf, kg) * scale
        s = jnp.where(valid[None, :, None, :], s, -jnp.inf)
        lse = jax.nn.logsumexp(s, axis=-1)
        p = jnp.exp(s - lse[..., None])
        outs.append(jnp.einsum('bqhj,bqjhd->bqhd', p, vg))
        lses.append(lse)
    wts = jax.nn.softmax(jnp.stack(lses), axis=0)
    return jnp.einsum('gbqh,gbqhd->bqhd', wts, jnp.stack(outs))


def dilated_attention(q, k_all, v_all, base):
    b, t, h, dh = q.shape
    if t % Q_BLOCK == 0:
        nb = t // Q_BLOCK
        qb = q.reshape(b, nb, Q_BLOCK, h, dh).swapaxes(0, 1)

        def blk(args):
            i, qblk = args
            return dilated_mixture(qblk, k_all, v_all, base + i * Q_BLOCK + jnp.arange(Q_BLOCK))

        o = lax.map(blk, (jnp.arange(nb), qb))
        return o.swapaxes(0, 1).reshape(b, t, h, dh)
    return dilated_mixture(q, k_all, v_all, base + jnp.arange(t))


def trunk_layer(x, conv_buf, s0, k_past, v_past, norm1_g, w_in, conv_w, a_log, dt_bias,
                delta_norm_g, q_norm_g, k_norm_g, w_o, norm2_g, w_up, w_down):
    b, t, _ = x.shape
    hn = rmsnorm(x, norm1_g)
    proj = hn @ w_in
    cuts = [int(c) for c in np.cumsum(SPLIT_SIZES)[:-1]]
    qkv_a, z_a, b_a, a_a, q_b, k_b, v_b = jnp.split(proj, cuts, axis=-1)

    qkv_a, conv_new = short_conv(qkv_a, conv_buf, conv_w)
    qa, ka, va = jnp.split(qkv_a, 3, axis=-1)
    qa = l2norm(qa.reshape(b, t, N_HEADS_A, HEAD_DIM_A)) * (HEAD_DIM_A ** -0.5)
    ka = l2norm(ka.reshape(b, t, N_HEADS_A, HEAD_DIM_A))
    va = va.reshape(b, t, N_HEADS_A, HEAD_DIM_A)
    beta = jax.nn.sigmoid(b_a.astype(jnp.float32))
    g = -jnp.exp(a_log.astype(jnp.float32)) * jax.nn.softplus(
        a_a.astype(jnp.float32) + dt_bias.astype(jnp.float32))
    oa, s_new = gated_delta_chunked(qa, ka, va, beta, g, s0)
    oa = rmsnorm(oa, delta_norm_g) * jax.nn.silu(
        z_a.reshape(b, t, N_HEADS_A, HEAD_DIM_A).astype(jnp.float32))
    oa = oa.reshape(b, t, WIDTH_A).astype(x.dtype)

    qb = rmsnorm(q_b.reshape(b, t, N_HEADS_B, HEAD_DIM_B), q_norm_g)
    kb = rmsnorm(k_b.reshape(b, t, N_HEADS_B, HEAD_DIM_B), k_norm_g)
    vb = v_b.reshape(b, t, N_HEADS_B, HEAD_DIM_B)
    k_all = jnp.concatenate([k_past.astype(kb.dtype), kb], axis=1)
    v_all = jnp.concatenate([v_past.astype(vb.dtype), vb], axis=1)
    ob = dilated_attention(qb, k_all, v_all, k_past.shape[1])
    ob = ob.reshape(b, t, WIDTH_B).astype(x.dtype)

    h1 = x + jnp.concatenate([oa, ob], axis=-1) @ w_o
    hid = jnp.square(jax.nn.relu(rmsnorm(h1, norm2_g) @ w_up))
    y = h1 + hid @ w_down
    return y, kb, vb, s_new, conv_new


def setup_inputs(seed: int = 0) -> dict:
    key = jax.random.key(seed)
    ks = jax.random.split(key, 20)
    wbuf = min(MAX_WINDOW, PAST_LEN)
    f32 = jnp.float32
    dt = jnp.exp(jax.random.uniform(ks[10], (DEPTH, N_HEADS_A), f32, np.log(1e-3), np.log(1e-1)))
    return {
        "x_prompt": jax.random.normal(ks[0], (BATCH, SEQ, D_MODEL), f32),
        "x_sample": jax.random.normal(ks[1], (DEC_BATCH, DEC_SEQ, D_MODEL), f32),
        "cache_swa_k": jax.random.normal(ks[2], (DEPTH, DEC_BATCH, wbuf, N_HEADS_B, HEAD_DIM_B), f32),
        "cache_swa_v": jax.random.normal(ks[3], (DEPTH, DEC_BATCH, wbuf, N_HEADS_B, HEAD_DIM_B), f32),
        "state_delta": 0.1 * jax.random.normal(ks[4], (DEPTH, DEC_BATCH, N_HEADS_A, HEAD_DIM_A, HEAD_DIM_A), f32),
        "state_conv": jax.random.normal(ks[5], (DEPTH, DEC_BATCH, CONV_WIDTH - 1, CONV_CH), f32),
        "norm1_g": 1.0 + 0.05 * jax.random.normal(ks[6], (DEPTH, D_MODEL), f32),
        "w_in": jax.random.normal(ks[7], (DEPTH, D_MODEL, PROJ_COLS), f32) * D_MODEL ** -0.5,
        "conv_w": jax.random.normal(ks[8], (DEPTH, CONV_WIDTH, CONV_CH), f32) * CONV_WIDTH ** -0.5,
        "a_log": jnp.log(jax.random.uniform(ks[9], (DEPTH, N_HEADS_A), f32, 1.0, 16.0)),
        "dt_bias": dt + jnp.log(-jnp.expm1(-dt)),
        "delta_norm_g": 1.0 + 0.05 * jax.random.normal(ks[11], (DEPTH, HEAD_DIM_A), f32),
        "q_norm_g": 1.0 + 0.05 * jax.random.normal(ks[12], (DEPTH, HEAD_DIM_B), f32),
        "k_norm_g": 1.0 + 0.05 * jax.random.normal(ks[13], (DEPTH, HEAD_DIM_B), f32),
        "w_o": jax.random.normal(ks[14], (DEPTH, MIX_WIDTH, D_MODEL), f32) * MIX_WIDTH ** -0.5,
        "norm2_g": 1.0 + 0.05 * jax.random.normal(ks[15], (DEPTH, D_MODEL), f32),
        "w_up": jax.random.normal(ks[16], (DEPTH, D_MODEL, D_FF), f32) * D_MODEL ** -0.5,
        "w_down": jax.random.normal(ks[17], (DEPTH, D_FF, D_MODEL), f32) * D_FF ** -0.5,
    }


def reference(x_prompt, x_sample, cache_swa_k, cache_swa_v, state_delta, state_conv, norm1_g, w_in,
              conv_w, a_log, dt_bias, delta_norm_g, q_norm_g, k_norm_g, w_o, norm2_g, w_up, w_down):
    b, s, _ = x_prompt.shape
    pbuf = min(MAX_WINDOW, s)
    yp, ys = x_prompt, x_sample
    kp_l, vp_l, dp_l, cp_l, ks_l, vs_l, ds_l, cs_l = [], [], [], [], [], [], [], []
    for layer in range(DEPTH):
        params = (norm1_g[layer], w_in[layer], conv_w[layer], a_log[layer], dt_bias[layer],
                  delta_norm_g[layer], q_norm_g[layer], k_norm_g[layer], w_o[layer], norm2_g[layer],
                  w_up[layer], w_down[layer])
        zero_kv = jnp.zeros((b, 0, N_HEADS_B, HEAD_DIM_B), yp.dtype)
        yp, kp, vp, dp, cp = trunk_layer(
            yp, jnp.zeros((b, CONV_WIDTH - 1, CONV_CH), yp.dtype),
            jnp.zeros((b, N_HEADS_A, HEAD_DIM_A, HEAD_DIM_A), jnp.float32),
            zero_kv, zero_kv, *params)
        kp_l.append(kp[:, -pbuf:]); vp_l.append(vp[:, -pbuf:]); dp_l.append(dp); cp_l.append(cp)
        ys, kn, vn, dn, cn = trunk_layer(
            ys, state_conv[layer], state_delta[layer], cache_swa_k[layer], cache_swa_v[layer], *params)
        ks_l.append(kn); vs_l.append(vn); ds_l.append(dn); cs_l.append(cn)
    return (yp, ys, jnp.stack(kp_l), jnp.stack(vp_l), jnp.stack(dp_l), jnp.stack(cp_l),
            jnp.stack(ks_l), jnp.stack(vs_l), jnp.stack(ds_l), jnp.stack(cs_l))
```

```python
import functools

import numpy as np
import jax
import jax.numpy as jnp
from jax import lax
from jax.experimental import pallas as pl
from jax.experimental.pallas import tpu as pltpu

F32 = jnp.float32
BF16 = jnp.bfloat16

N_HEADS_A = 4
HEAD_DIM_A = 128
WIDTH_A = N_HEADS_A * HEAD_DIM_A
N_HEADS_B = 8
HEAD_DIM_B = 64
WIDTH_B = N_HEADS_B * HEAD_DIM_B
CONV_WIDTH = 4
CONV_CH = 3 * WIDTH_A
DELTA_CHUNK = 64
MAX_WINDOW = 2048
NORM_EPS = 1e-6
LANES = 128
SUBLANES = 8
NEG = -1e30

C_QKV, C_Z, C_QB, C_KB, C_VB, C_GATE = 0, 1536, 2048, 2560, 3072, 3584
PROJ_PAD = C_GATE + LANES

ATTN_TILE = 2048
VMEM_LIMIT = 56 * 1024 * 1024


def _dot(a, b):
    return jnp.dot(a.astype(BF16), b.astype(BF16), preferred_element_type=F32)


def _dot_nt(a, b):
    return lax.dot_general(a.astype(BF16), b.astype(BF16), (((1,), (1,)), ((), ())),
                           preferred_element_type=F32)


def _dot_tn(a, b):
    return lax.dot_general(a.astype(BF16), b.astype(BF16), (((0,), (0,)), ((), ())),
                           preferred_element_type=F32)


def _split2(x):
    hi = x.astype(BF16)
    lo = (x - hi.astype(F32)).astype(BF16)
    return hi, lo


def _split3(x):
    hi = x.astype(BF16)
    r = x - hi.astype(F32)
    mid = r.astype(BF16)
    lo = (r - mid.astype(F32)).astype(BF16)
    return hi, mid, lo


def _dot_exact_lhs(mask_bf16, x):
    hi, mid, lo = _split3(x)
    d = lambda p: jnp.dot(mask_bf16, p, preferred_element_type=F32)
    return d(hi) + d(mid) + d(lo)


def _sigmoid(x):
    return 1.0 / (1.0 + jnp.exp(-x))


def _silu(x):
    return x * _sigmoid(x)


def _softplus(x):
    return jnp.maximum(x, 0.0) + jnp.log1p(jnp.exp(-jnp.abs(x)))


def _rmsnorm(x, g):
    return x * lax.rsqrt(jnp.mean(x * x, axis=-1, keepdims=True) + NORM_EPS) * g


def _proj_body(x, ext_ref, shift, pad, refs):
    n1_ref, w_ref, cw_ref, al_ref, dtb_ref, qng_ref, kng_ref, hm_ref = refs
    tm = x.shape[0]
    hn = _rmsnorm(x, n1_ref[...]).astype(BF16)

    u = jnp.dot(hn, w_ref[:, C_QKV:C_QKV + CONV_CH], preferred_element_type=F32)
    ext_ref[pad:pad + tm, :] = u
    cw = cw_ref[...]
    y = u * cw[3:4, :]
    for i in range(CONV_WIDTH - 1):
        off = pad - (CONV_WIDTH - 1 - i) * shift
        y = y + ext_ref[off:off + tm, :] * cw[i:i + 1, :]
    y = _silu(y)
    qa, ka = [], []
    for h in range(N_HEADS_A):
        lo = h * HEAD_DIM_A
        qh = y[:, lo:lo + HEAD_DIM_A]
        qa.append(qh * (lax.rsqrt(
            jnp.sum(qh * qh, axis=-1, keepdims=True) + NORM_EPS) * HEAD_DIM_A ** -0.5))
        kh = y[:, WIDTH_A + lo:WIDTH_A + lo + HEAD_DIM_A]
        ka.append(kh * lax.rsqrt(jnp.sum(kh * kh, axis=-1, keepdims=True) + NORM_EPS))
    qa = jnp.concatenate(qa, axis=1)
    ka = jnp.concatenate(ka, axis=1)
    va = y[:, 2 * WIDTH_A:3 * WIDTH_A]

    z = jnp.dot(hn, w_ref[:, C_Z:C_Z + WIDTH_A], preferred_element_type=F32)

    hm = hm_ref[...]

    def headnorm(v, g):
        hi, lo_ = _split2(v * v)
        ms = (jnp.dot(hi, hm, preferred_element_type=F32)
              + jnp.dot(lo_, hm, preferred_element_type=F32))
        return v * lax.rsqrt(ms + NORM_EPS) * g

    qb = jnp.dot(hn, w_ref[:, C_QB:C_QB + WIDTH_B], preferred_element_type=F32)
    qb = headnorm(qb, qng_ref[...])
    kb = jnp.dot(hn, w_ref[:, C_KB:C_KB + WIDTH_B], preferred_element_type=F32)
    kb = headnorm(kb, kng_ref[...])
    vb = jnp.dot(hn, w_ref[:, C_VB:C_VB + WIDTH_B], preferred_element_type=F32)

    gc = jnp.dot(hn, w_ref[:, C_GATE:C_GATE + LANES], preferred_element_type=F32)
    lane = lax.broadcasted_iota(jnp.int32, gc.shape, 1)
    beta = _sigmoid(gc)
    g = -jnp.exp(al_ref[...]) * _softplus(gc + dtb_ref[...])
    gates = jnp.where(lane < N_HEADS_A, beta, g)
    return qa, ka, va, z, gates, qb, kb, vb


def _proj_prompt_kernel(x_ref, n1_ref, w_ref, cw_ref, al_ref, dtb_ref, qng_ref, kng_ref, hm_ref,
                        qa_ref, ka_ref, va_ref, z_ref, gt_ref, qb_ref, kb_ref, vb_ref, cn_ref,
                        ext_ref):
    t = pl.program_id(1)
    tm = x_ref.shape[1]

    @pl.when(t == 0)
    def _():
        ext_ref[0:SUBLANES, :] = jnp.zeros((SUBLANES, CONV_CH), F32)

    @pl.when(t > 0)
    def _():
        ext_ref[0:SUBLANES, :] = ext_ref[tm:tm + SUBLANES, :]

    vals = _proj_body(x_ref[0], ext_ref, 1, SUBLANES,
                      (n1_ref, w_ref, cw_ref, al_ref, dtb_ref, qng_ref, kng_ref, hm_ref))
    for r, v in zip((qa_ref, ka_ref, va_ref, z_ref, gt_ref, qb_ref, kb_ref, vb_ref), vals):
        r[0] = v
    cn_ref[0] = ext_ref[tm + SUBLANES - (CONV_WIDTH - 1):tm + SUBLANES, :]


def _proj_sample_kernel(x_ref, st_ref, n1_ref, w_ref, cw_ref, al_ref, dtb_ref, qng_ref, kng_ref,
                        hm_ref, qa_ref, ka_ref, va_ref, z_ref, gt_ref, qb_ref, kb_ref, vb_ref,
                        cn_ref, ext_ref):
    nt, nb, d = x_ref.shape
    ncv = CONV_WIDTH - 1
    tm = nb * nt
    pad = ncv * nb
    ext_ref[0:pad, :] = st_ref[...].reshape(pad, CONV_CH)
    vals = _proj_body(x_ref[...].reshape(tm, d), ext_ref, nb, pad,
                      (n1_ref, w_ref, cw_ref, al_ref, dtb_ref, qng_ref, kng_ref, hm_ref))
    for r, v in zip((qa_ref, ka_ref, va_ref, z_ref, gt_ref, qb_ref, kb_ref, vb_ref), vals):
        r[...] = v.reshape(r.shape)
    cn_ref[...] = ext_ref[tm:tm + pad, :].reshape(ncv, nb, CONV_CH)


def _proj_params(p):
    full = lambda a: pl.BlockSpec(a.shape, lambda *_: (0,) * a.ndim)
    arrs = (p["n1"], p["w_in"], p["conv_w"], p["alog"], p["dtb"], p["qng"], p["kng"], p["hm"])
    return arrs, [full(a) for a in arrs]


def _proj_prompt(x, p, tm):
    b, t, d = x.shape
    arrs, specs = _proj_params(p)
    row = lambda w: pl.BlockSpec((1, tm, w), lambda i, j: (i, j, 0))
    widths = (WIDTH_A, WIDTH_A, WIDTH_A, WIDTH_A, LANES, WIDTH_B, WIDTH_B, WIDTH_B)
    out_shape = [jax.ShapeDtypeStruct((b, t, w), F32) for w in widths]
    out_shape.append(jax.ShapeDtypeStruct((b, CONV_WIDTH - 1, CONV_CH), F32))
    out_specs = [row(w) for w in widths]
    out_specs.append(pl.BlockSpec((1, CONV_WIDTH - 1, CONV_CH), lambda i, j: (i, 0, 0)))
    return pl.pallas_call(
        _proj_prompt_kernel,
        grid=(b, t // tm),
        in_specs=[row(d)] + specs,
        out_specs=out_specs,
        out_shape=out_shape,
        scratch_shapes=[pltpu.VMEM((tm + SUBLANES, CONV_CH), F32)],
        compiler_params=pltpu.CompilerParams(
            dimension_semantics=("arbitrary", "arbitrary"), vmem_limit_bytes=VMEM_LIMIT),
        name="proj_prompt",
    )(x, *arrs)


def _proj_sample(x, state_conv, p, bt):
    nb, nt, d = x.shape
    ncv = CONV_WIDTH - 1
    arrs, specs = _proj_params(p)
    blk = lambda r, w: pl.BlockSpec((r, bt, w), lambda i: (0, i, 0))
    widths = (WIDTH_A, WIDTH_A, WIDTH_A, WIDTH_A, LANES, WIDTH_B, WIDTH_B, WIDTH_B)
    out_shape = [jax.ShapeDtypeStruct((nt, nb, w), F32) for w in widths]
    out_shape.append(jax.ShapeDtypeStruct((ncv, nb, CONV_CH), F32))
    out_specs = [blk(nt, w) for w in widths] + [blk(ncv, CONV_CH)]
    outs = pl.pallas_call(
        _proj_sample_kernel,
        grid=(nb // bt,),
        in_specs=[blk(nt, d), blk(ncv, CONV_CH)] + specs,
        out_specs=out_specs,
        out_shape=out_shape,
        scratch_shapes=[pltpu.VMEM((bt * (nt + ncv), CONV_CH), F32)],
        compiler_params=pltpu.CompilerParams(
            dimension_semantics=("arbitrary",), vmem_limit_bytes=VMEM_LIMIT),
        name="proj_sample",
    )(x.transpose(1, 0, 2), state_conv.transpose(1, 0, 2), *arrs)
    return [o.transpose(1, 0, 2) for o in outs]


def _delta_kernel(q_ref, k_ref, v_ref, z_ref, gt_ref, s0_ref, ng_ref, o_ref, sf_ref, s_scr,
                  *, chunk, nchunk, bb):
    c = chunk
    t = pl.program_id(1)

    @pl.when(t == 0)
    def _():
        s_scr[...] = s0_ref[...]

    ii = lax.broadcasted_iota(jnp.int32, (c, c), 0)
    jj = lax.broadcasted_iota(jnp.int32, (c, c), 1)
    causal = ii >= jj
    strict = ii > jj
    tril_bf = causal.astype(F32).astype(BF16)
    eye = (ii == jj).astype(F32)
    level_masks = []
    s = 1
    while s < c:
        level_masks.append(((ii // (2 * s)) == (jj // (2 * s))) & ((ii & s) != 0) & ((jj & s) == 0))
        s *= 2
    ng = ng_ref[...]

    for bi in range(bb):
        for ci in range(nchunk):
            rows = slice(ci * c, (ci + 1) * c)
            gt = gt_ref[bi, rows, :]
            gcum = _dot_exact_lhs(tril_bf, gt)
            if c < LANES:
                gpad = jnp.concatenate([gcum, jnp.zeros((LANES - c, LANES), F32)], axis=0)
            else:
                gpad = gcum
            gcum_t = gpad.T
            for h in range(N_HEADS_A):
                lo = h * HEAD_DIM_A
                q = q_ref[bi, rows, lo:lo + HEAD_DIM_A]
                k = k_ref[bi, rows, lo:lo + HEAD_DIM_A]
                v = v_ref[bi, rows, lo:lo + HEAD_DIM_A]
                beta = gt[:, h:h + 1]
                g_col = gcum[:, N_HEADS_A + h:N_HEADS_A + h + 1]
                g_row = gcum_t[N_HEADS_A + h:N_HEADS_A + h + 1, 0:c]
                g_last = gcum[c - 1:c, N_HEADS_A + h:N_HEADS_A + h + 1]
                decay = jnp.where(causal, jnp.exp(jnp.where(causal, g_col - g_row, 0.0)), 0.0)
                exp_g = jnp.exp(g_col)
                kb = k * beta
                vb = v * beta
                lmat = jnp.where(strict, _dot_nt(kb, k) * decay, 0.0)
                tinv = eye
                for msk in level_masks:
                    e = jnp.where(msk, lmat, 0.0)
                    tinv = tinv - _dot(_dot(tinv, e), tinv)
                uw = _dot(tinv, jnp.concatenate([vb, kb * exp_g], axis=1))
                u = uw[:, 0:HEAD_DIM_A]
                w = uw[:, HEAD_DIM_A:2 * HEAD_DIM_A]
                qk = _dot_nt(q, k) * decay
                q_dec = q * exp_g
                k_dec = k * jnp.exp(g_last - g_col)
                st = s_scr[bi, h]
                ws = _dot(jnp.concatenate([w, q_dec], axis=0), st)
                v_new = u - ws[0:c]
                o = ws[c:2 * c] + _dot(qk, v_new)
                s_scr[bi, h] = st * jnp.exp(g_last) + _dot_tn(k_dec, v_new)
                zz = z_ref[bi, rows, lo:lo + HEAD_DIM_A]
                o_ref[bi, rows, lo:lo + HEAD_DIM_A] = (_rmsnorm(o, ng) * _silu(zz)).astype(o_ref.dtype)

    @pl.when(t == pl.num_programs(1) - 1)
    def _():
        sf_ref[...] = s_scr[...]


def _delta(qa, ka, va, z, gt, s0, ng, *, chunk, nchunk, bb, out_dtype):
    b, t, _ = qa.shape
    tc = chunk * nchunk
    row = lambda w: pl.BlockSpec((bb, tc, w), lambda i, j: (i, j, 0))
    sspec = pl.BlockSpec((bb, N_HEADS_A, HEAD_DIM_A, HEAD_DIM_A), lambda i, j: (i, 0, 0, 0))
    return pl.pallas_call(
        functools.partial(_delta_kernel, chunk=chunk, nchunk=nchunk, bb=bb),
        grid=(b // bb, t // tc),
        in_specs=[row(WIDTH_A)] * 4 + [row(LANES), sspec,
                                       pl.BlockSpec((1, HEAD_DIM_A), lambda i, j: (0, 0))],
        out_specs=[row(WIDTH_A), sspec],
        out_shape=[jax.ShapeDtypeStruct((b, t, WIDTH_A), out_dtype),
                   jax.ShapeDtypeStruct((b, N_HEADS_A, HEAD_DIM_A, HEAD_DIM_A), F32)],
        scratch_shapes=[pltpu.VMEM((bb, N_HEADS_A, HEAD_DIM_A, HEAD_DIM_A), F32)],
        compiler_params=pltpu.CompilerParams(
            dimension_semantics=("arbitrary", "arbitrary"), vmem_limit_bytes=VMEM_LIMIT),
        name="delta_c%d" % chunk,
    )(qa, ka, va, z, gt, s0, ng)


def _attn_prompt_kernel(q_ref, kp_ref, kc_ref, vp_ref, vc_ref, o_ref, qm, kall, vm, acc, *, tq):
    t = pl.program_id(2)
    blk_rows = LANES
    lane_t = lax.broadcasted_iota(jnp.int32, (tq, LANES), 1)
    kall[0:tq, :] = kp_ref[0]
    kall[tq:2 * tq, :] = kc_ref[0]
    scale = HEAD_DIM_B ** -0.5
    for hh in range(2):
        own = (lane_t >= hh * HEAD_DIM_B) & (lane_t < (hh + 1) * HEAD_DIM_B)
        qm[hh] = jnp.where(own, q_ref[0] * scale, 0.0)
        vm[hh, 0:tq, :] = jnp.where(own, vp_ref[0], 1.0)
        vm[hh, tq:2 * tq, :] = jnp.where(own, vc_ref[0], 1.0)

    ii = lax.broadcasted_iota(jnp.int32, (blk_rows, blk_rows), 0)
    jj = lax.broadcasted_iota(jnp.int32, (blk_rows, blk_rows), 1)
    lower = jj <= ii
    upper = jj >= ii
    lane = lax.broadcasted_iota(jnp.int32, (blk_rows, LANES), 1)

    for hh in range(2):
        rel = lane ^ (hh * HEAD_DIM_B)
        is_max = rel >= HEAD_DIM_B + HEAD_DIM_B // 2
        max_col = (HEAD_DIM_B + HEAD_DIM_B // 2) ^ (hh * HEAD_DIM_B)
        for pi, d in enumerate((1, 4, 16)):
            nb = tq // (blk_rows * d)
            nb_log2 = nb.bit_length() - 1
            assert nb == 1 << nb_log2

            def unit(u, carry, d=d, nb=nb, nb_log2=nb_log2, hh=hh, pi=pi, is_max=is_max,
                     max_col=max_col):
                r = lax.shift_right_logical(u, nb_log2)
                blk = u & (nb - 1)
                qs = r + d * blk_rows * blk
                rows_q = pl.ds(qs, blk_rows, stride=d)
                rows_c = pl.ds(tq + qs, blk_rows, stride=d)
                rows_p = pl.ds(tq + qs - d * blk_rows, blk_rows, stride=d)
                qv = qm[hh, rows_q, :]
                s_c = jnp.where(lower, _dot_nt(qv, kall[rows_c, :]), NEG)
                pen = jnp.where((blk > 0) | (t > 0), 0.0, NEG)
                s_p = jnp.where(upper, _dot_nt(qv, kall[rows_p, :]) + pen, NEG)
                m = jnp.maximum(jnp.max(s_c, axis=-1, keepdims=True),
                                jnp.max(s_p, axis=-1, keepdims=True))
                pv = (_dot(jnp.exp(s_c - m), vm[hh, rows_c, :])
                      + _dot(jnp.exp(s_p - m), vm[hh, rows_p, :]))
                if pi == 0:
                    acc[hh, rows_q, :] = jnp.where(is_max, m, pv)
                else:
                    old = acc[hh, rows_q, :]
                    m_old = old[:, max_col:max_col + 1]
                    m_new = jnp.maximum(m_old, m)
                    merged = old * jnp.exp(m_old - m_new) + pv * jnp.exp(m - m_new)
                    acc[hh, rows_q, :] = jnp.where(is_max, m_new, merged)
                return carry

            lax.fori_loop(0, tq // blk_rows, unit, 0)

    a0 = acc[0]
    a1 = acc[1]
    l0 = a0[:, HEAD_DIM_B:HEAD_DIM_B + 1]
    l1 = a1[:, 0:1]
    o_ref[0] = jnp.where(lane_t < HEAD_DIM_B, a0 / l0, a1 / l1).astype(o_ref.dtype)


def _attn_prompt(qb, kb, vb):
    b, t, _ = qb.shape
    tq = ATTN_TILE
    nt = t // tq
    cur = pl.BlockSpec((1, tq, LANES), lambda i, h, j: (i, j, h))
    prev = pl.BlockSpec((1, tq, LANES), lambda i, h, j: (i, jnp.maximum(j - 1, 0), h))
    return pl.pallas_call(
        functools.partial(_attn_prompt_kernel, tq=tq),
        grid=(b, WIDTH_B // LANES, nt),
        in_specs=[cur, prev, cur, prev, cur],
        out_specs=cur,
        out_shape=jax.ShapeDtypeStruct((b, t, WIDTH_B), BF16),
        scratch_shapes=[pltpu.VMEM((2, tq, LANES), F32), pltpu.VMEM((2 * tq, LANES), F32),
                        pltpu.VMEM((2, 2 * tq, LANES), F32), pltpu.VMEM((2, tq, LANES), F32)],
        compiler_params=pltpu.CompilerParams(
            dimension_semantics=("arbitrary", "arbitrary", "arbitrary"),
            vmem_limit_bytes=VMEM_LIMIT),
        name="attn_prompt",
    )(qb, kb, kb, vb, vb)


SAMPLE_FAR_PERIOD = 16
SAMPLE_NEAR = 512


def _multiplicity(delta):
    delta = np.asarray(delta)
    ok = delta >= 0
    m = ((delta <= 128).astype(np.float32)
         + ((delta <= 512) & (delta % 4 == 0)).astype(np.float32)
         + ((delta <= 2048) & (delta % 16 == 0)).astype(np.float32))
    return np.where(ok, m, 0.0).astype(np.float32)


def _sample_masks(wbuf, nt):
    col_i = np.tile(np.arange(nt), N_HEADS_B)[None, :]
    n_far = (wbuf - SAMPLE_NEAR) // SAMPLE_FAR_PERIOD
    p_far = (np.arange(n_far)[:, None] * SAMPLE_FAR_PERIOD + np.arange(nt)[None, :]).reshape(-1, 1)
    p_near = (wbuf - SAMPLE_NEAR + np.arange(SAMPLE_NEAR))[:, None]
    p_new = (wbuf + np.arange(nt))[:, None]
    m_far, m_near, m_new = (_multiplicity(wbuf + col_i - p) for p in (p_far, p_near, p_new))
    return m_far, m_near, np.concatenate([m_new, np.zeros_like(m_new)], axis=0)


def _attn_sample_kernel(q_ref, kn_ref, vn_ref, kf_ref, kr_ref, vf_ref, vr_ref, mf_ref, mr_ref,
                        mn_ref, o_ref, *, nt):
    ncol = N_HEADS_B * nt
    q = q_ref[0] * (HEAD_DIM_B ** -0.5)
    qrep = jnp.concatenate([q] * N_HEADS_B, axis=0)
    rr = lax.broadcasted_iota(jnp.int32, (ncol, WIDTH_B), 0)
    cc = lax.broadcasted_iota(jnp.int32, (ncol, WIDTH_B), 1)
    same_head = (rr // nt) == (cc // HEAD_DIM_B)
    qblk = jnp.where(same_head, qrep, 0.0).astype(BF16)

    n_far = kf_ref.shape[1] * kf_ref.shape[2]
    parts = (
        (kf_ref[0].reshape(n_far, WIDTH_B), vf_ref[0].reshape(n_far, WIDTH_B), mf_ref[...]),
        (kr_ref[0], vr_ref[0], mr_ref[...]),
        (jnp.concatenate([kn_ref[0], jnp.zeros((nt, WIDTH_B), F32)], axis=0),
         jnp.concatenate([vn_ref[0], jnp.zeros((nt, WIDTH_B), F32)], axis=0), mn_ref[...]),
    )
    scores = []
    m = None
    for kk, _, mult in parts:
        s = jnp.where(mult > 0.0, _dot_nt(kk, qblk), NEG)
        scores.append(s)
        pm = jnp.max(s, axis=0, keepdims=True)
        m = pm if m is None else jnp.maximum(m, pm)
    out = jnp.zeros((ncol, WIDTH_B), F32)
    den = jnp.zeros((ncol, LANES), F32)
    for (kk, vv, mult), s in zip(parts, scores):
        p = (mult * jnp.exp(s - m)).astype(BF16)
        out = out + _dot_tn(p, vv)
        den = den + _dot_tn(p, jnp.ones((p.shape[0], LANES), BF16))
    full = jnp.where(same_head, out / den[:, 0:1], 0.0)
    res = full[0:nt]
    for h in range(1, N_HEADS_B):
        res = res + full[h * nt:(h + 1) * nt]
    o_ref[0] = res.astype(o_ref.dtype)


def _attn_sample(qb, kn, vn, cache_k, cache_v):
    nb, nt, _ = qb.shape
    wbuf = cache_k.shape[1]
    n_far = (wbuf - SAMPLE_NEAR) // SAMPLE_FAR_PERIOD
    masks = [jnp.asarray(m) for m in _sample_masks(wbuf, nt)]
    ck4 = cache_k.reshape(nb, wbuf // SAMPLE_FAR_PERIOD, SAMPLE_FAR_PERIOD, WIDTH_B)
    cv4 = cache_v.reshape(nb, wbuf // SAMPLE_FAR_PERIOD, SAMPLE_FAR_PERIOD, WIDTH_B)
    ck3 = cache_k.reshape(nb, wbuf, WIDTH_B)
    cv3 = cache_v.reshape(nb, wbuf, WIDTH_B)
    tok = pl.BlockSpec((1, nt, WIDTH_B), lambda i: (i, 0, 0))
    far = pl.BlockSpec((1, n_far, nt, WIDTH_B), lambda i: (i, 0, 0, 0))
    near = pl.BlockSpec((1, SAMPLE_NEAR, WIDTH_B), lambda i: (i, wbuf // SAMPLE_NEAR - 1, 0))
    mspec = lambda m: pl.BlockSpec(m.shape, lambda i: (0, 0))
    return pl.pallas_call(
        functools.partial(_attn_sample_kernel, nt=nt),
        grid=(nb,),
        in_specs=[tok, tok, tok, far, near, far, near] + [mspec(m) for m in masks],
        out_specs=tok,
        out_shape=jax.ShapeDtypeStruct((nb, nt, WIDTH_B), BF16),
        compiler_params=pltpu.CompilerParams(
            dimension_semantics=("arbitrary",), vmem_limit_bytes=VMEM_LIMIT),
        name="attn_sample",
    )(qb, kn, vn, ck4, ck3, cv4, cv3, *masks)


FF_CHUNK = 1024


def _mlp_kernel(x_ref, oa_ref, ob_ref, wo_ref, n2_ref, wu_ref, wd_ref, y_ref):
    mix = jnp.concatenate([oa_ref[...].astype(BF16), ob_ref[...].astype(BF16)], axis=1)
    h1 = x_ref[...] + jnp.dot(mix, wo_ref[...], preferred_element_type=F32)
    hn = _rmsnorm(h1, n2_ref[...]).astype(BF16)
    y = h1
    for c0 in range(0, wu_ref.shape[1], FF_CHUNK):
        hid = jnp.dot(hn, wu_ref[:, c0:c0 + FF_CHUNK], preferred_element_type=F32)
        hid = jnp.square(jnp.maximum(hid, 0.0)).astype(BF16)
        y = y + jnp.dot(hid, wd_ref[c0:c0 + FF_CHUNK, :], preferred_element_type=F32)
    y_ref[...] = y


def _mlp(x, oa, ob, p, tm):
    n, d = x.shape
    row = lambda w: pl.BlockSpec((tm, w), lambda i: (i, 0))
    const = lambda a: pl.BlockSpec(a.shape, lambda i: (0, 0))
    return pl.pallas_call(
        _mlp_kernel,
        grid=(n // tm,),
        in_specs=[row(d), row(WIDTH_A), row(WIDTH_B), const(p["w_o"]), const(p["n2"]),
                  const(p["w_up"]), const(p["w_down"])],
        out_specs=row(d),
        out_shape=jax.ShapeDtypeStruct((n, d), F32),
        compiler_params=pltpu.CompilerParams(
            dimension_semantics=("arbitrary",), vmem_limit_bytes=VMEM_LIMIT),
        name="mlp",
    )(x, oa, ob, p["w_o"], p["n2"], p["w_up"], p["w_down"])


def _layer_params(norm1_g, w_in, conv_w, a_log, dt_bias, delta_norm_g, q_norm_g, k_norm_g, w_o,
                  norm2_g, w_up, w_down):
    d = w_in.shape[0]
    n_gate = 2 * N_HEADS_A
    gate0 = CONV_CH + WIDTH_A
    w_re = jnp.concatenate(
        [w_in[:, :gate0], w_in[:, gate0 + n_gate:], w_in[:, gate0:gate0 + n_gate],
         jnp.zeros((d, LANES - n_gate), w_in.dtype)], axis=1).astype(BF16)
    lane_pad = lambda a: jnp.zeros((1, LANES), F32).at[0, N_HEADS_A:n_gate].set(a.astype(F32))
    hid = np.arange(WIDTH_B) // HEAD_DIM_B
    head_mean = jnp.asarray((hid[:, None] == hid[None, :]).astype(np.float32) / HEAD_DIM_B, BF16)
    return {
        "n1": norm1_g.reshape(1, d).astype(F32), "w_in": w_re, "conv_w": conv_w.astype(F32),
        "alog": lane_pad(a_log), "dtb": lane_pad(dt_bias),
        "qng": jnp.tile(q_norm_g.astype(F32), N_HEADS_B).reshape(1, WIDTH_B),
        "kng": jnp.tile(k_norm_g.astype(F32), N_HEADS_B).reshape(1, WIDTH_B),
        "hm": head_mean, "dng": delta_norm_g.reshape(1, HEAD_DIM_A).astype(F32),
        "w_o": w_o.astype(BF16), "n2": norm2_g.reshape(1, d).astype(F32),
        "w_up": w_up.astype(BF16), "w_down": w_down.astype(BF16),
    }


def _prompt_layer(x, p):
    b, t, d = x.shape
    qa, ka, va, z, gt, qb, kb, vb, conv_new = _proj_prompt(x, p, tm=256)
    s0 = jnp.zeros((b, N_HEADS_A, HEAD_DIM_A, HEAD_DIM_A), F32)
    oa, s_new = _delta(qa, ka, va, z, gt, s0, p["dng"], chunk=DELTA_CHUNK, nchunk=2, bb=1,
                       out_dtype=BF16)
    ob = _attn_prompt(qb, kb, vb)
    y = _mlp(x.reshape(b * t, d), oa.reshape(b * t, WIDTH_A), ob.reshape(b * t, WIDTH_B), p, tm=256)
    return y.reshape(b, t, d), kb, vb, s_new, conv_new


def _sample_layer(x, state_conv, s0, cache_k, cache_v, p):
    nb, nt, d = x.shape
    qa, ka, va, z, gt, qb, kb, vb, conv_new = _proj_sample(x, state_conv, p, bt=min(nb, 64))
    oa, s_new = _delta(qa, ka, va, z, gt, s0, p["dng"], chunk=nt, nchunk=1, bb=min(nb, 4),
                       out_dtype=F32)
    ob = _attn_sample(qb, kb, vb, cache_k, cache_v)
    y = _mlp(x.reshape(nb * nt, d), oa.reshape(nb * nt, WIDTH_A), ob.reshape(nb * nt, WIDTH_B), p,
             tm=min(256, nb * nt))
    return y.reshape(nb, nt, d), kb, vb, s_new, conv_new


def kernel(x_prompt, x_sample, cache_swa_k, cache_swa_v, state_delta, state_conv, norm1_g, w_in,
           conv_w, a_log, dt_bias, delta_norm_g, q_norm_g, k_norm_g, w_o, norm2_g, w_up, w_down):
    depth = w_in.shape[0]
    b, s, _ = x_prompt.shape
    nb, nt, _ = x_sample.shape
    wbuf = cache_swa_k.shape[2]
    assert s % ATTN_TILE == 0 and nt == SUBLANES and wbuf == MAX_WINDOW
    pbuf = min(MAX_WINDOW, s)
    yp, ys = x_prompt, x_sample
    outs = [[] for _ in range(8)]
    for layer in range(depth):
        p = _layer_params(norm1_g[layer], w_in[layer], conv_w[layer], a_log[layer], dt_bias[layer],
                          delta_norm_g[layer], q_norm_g[layer], k_norm_g[layer], w_o[layer],
                          norm2_g[layer], w_up[layer], w_down[layer])
        yp, kp, vp, dp, cp = _prompt_layer(yp, p)
        ys, kn, vn, dn, cn = _sample_layer(ys, state_conv[layer], state_delta[layer],
                                           cache_swa_k[layer], cache_swa_v[layer], p)
        heads = lambda a: a.reshape(a.shape[0], a.shape[1], N_HEADS_B, HEAD_DIM_B)
        for lst, val in zip(outs, (heads(kp[:, -pbuf:]), heads(vp[:, -pbuf:]), dp, cp,
                                   heads(kn), heads(vn), dn, cn)):
            lst.append(val)
    return (yp, ys) + tuple(jnp.stack(o) for o in outs)
```

```python
import functools

import numpy as np
import jax
import jax.numpy as jnp
from jax import lax
from jax.experimental import pallas as pl
from jax.experimental.pallas import tpu as pltpu

F32 = jnp.float32
BF16 = jnp.bfloat16

N_HEADS_A = 4
HEAD_DIM_A = 128
WIDTH_A = N_HEADS_A * HEAD_DIM_A
N_HEADS_B = 8
HEAD_DIM_B = 64
WIDTH_B = N_HEADS_B * HEAD_DIM_B
CONV_WIDTH = 4
CONV_CH = 3 * WIDTH_A
DELTA_CHUNK = 64
MAX_WINDOW = 2048
NORM_EPS = 1e-6
LANES = 128
SUBLANES = 8
NEG = -1e30

C_QKV, C_Z, C_QB, C_KB, C_VB, C_GATE = 0, 1536, 2048, 2560, 3072, 3584
PROJ_PAD = C_GATE + LANES

ATTN_TILE = 2048
ATTN_UNROLL = 4
VMEM_LIMIT = 56 * 1024 * 1024


def _dot(a, b):
    return jnp.dot(a.astype(BF16), b.astype(BF16), preferred_element_type=F32)


def _dot_nt(a, b):
    return lax.dot_general(a.astype(BF16), b.astype(BF16), (((1,), (1,)), ((), ())),
                           preferred_element_type=F32)


def _dot_tn(a, b):
    return lax.dot_general(a.astype(BF16), b.astype(BF16), (((0,), (0,)), ((), ())),
                           preferred_element_type=F32)


def _split2(x):
    hi = x.astype(BF16)
    lo = (x - hi.astype(F32)).astype(BF16)
    return hi, lo


def _split3(x):
    hi = x.astype(BF16)
    r = x - hi.astype(F32)
    mid = r.astype(BF16)
    lo = (r - mid.astype(F32)).astype(BF16)
    return hi, mid, lo


def _dot_exact_lhs(mask_bf16, x):
    hi, mid, lo = _split3(x)
    d = lambda p: jnp.dot(mask_bf16, p, preferred_element_type=F32)
    return d(hi) + d(mid) + d(lo)


def _sigmoid(x):
    return 1.0 / (1.0 + jnp.exp(-x))


def _silu(x):
    return x * _sigmoid(x)


def _softplus(x):
    return jnp.maximum(x, 0.0) + jnp.log1p(jnp.exp(-jnp.abs(x)))


def _rmsnorm(x, g):
    return x * lax.rsqrt(jnp.mean(x * x, axis=-1, keepdims=True) + NORM_EPS) * g


def _proj_body(x, ext_ref, shift, pad, refs):
    n1_ref, w_ref, cw_ref, al_ref, dtb_ref, qng_ref, kng_ref, hm_ref = refs
    tm = x.shape[0]
    hn = _rmsnorm(x, n1_ref[...]).astype(BF16)

    u = jnp.dot(hn, w_ref[:, C_QKV:C_QKV + CONV_CH], preferred_element_type=F32)
    ext_ref[pad:pad + tm, :] = u
    cw = cw_ref[...]
    y = u * cw[3:4, :]
    for i in range(CONV_WIDTH - 1):
        off = pad - (CONV_WIDTH - 1 - i) * shift
        y = y + ext_ref[off:off + tm, :] * cw[i:i + 1, :]
    y = _silu(y)
    qa, ka = [], []
    for h in range(N_HEADS_A):
        lo = h * HEAD_DIM_A
        qh = y[:, lo:lo + HEAD_DIM_A]
        qa.append(qh * (lax.rsqrt(
            jnp.sum(qh * qh, axis=-1, keepdims=True) + NORM_EPS) * HEAD_DIM_A ** -0.5))
        kh = y[:, WIDTH_A + lo:WIDTH_A + lo + HEAD_DIM_A]
        ka.append(kh * lax.rsqrt(jnp.sum(kh * kh, axis=-1, keepdims=True) + NORM_EPS))
    qa = jnp.concatenate(qa, axis=1)
    ka = jnp.concatenate(ka, axis=1)
    va = y[:, 2 * WIDTH_A:3 * WIDTH_A]

    z = jnp.dot(hn, w_ref[:, C_Z:C_Z + WIDTH_A], preferred_element_type=F32)

    hm = hm_ref[...]

    def headnorm(v, g):
        hi, lo_ = _split2(v * v)
        ms = (jnp.dot(hi, hm, preferred_element_type=F32)
              + jnp.dot(lo_, hm, preferred_element_type=F32))
        return v * lax.rsqrt(ms + NORM_EPS) * g

    qb = jnp.dot(hn, w_ref[:, C_QB:C_QB + WIDTH_B], preferred_element_type=F32)
    qb = headnorm(qb, qng_ref[...])
    kb = jnp.dot(hn, w_ref[:, C_KB:C_KB + WIDTH_B], preferred_element_type=F32)
    kb = headnorm(kb, kng_ref[...])
    vb = jnp.dot(hn, w_ref[:, C_VB:C_VB + WIDTH_B], preferred_element_type=F32)

    gc = jnp.dot(hn, w_ref[:, C_GATE:C_GATE + LANES], preferred_element_type=F32)
    lane = lax.broadcasted_iota(jnp.int32, gc.shape, 1)
    beta = _sigmoid(gc)
    g = -jnp.exp(al_ref[...]) * _softplus(gc + dtb_ref[...])
    gates = jnp.where(lane < N_HEADS_A, beta, g)
    return qa, ka, va, z, gates, qb, kb, vb


def _proj_prompt_kernel(x_ref, n1_ref, w_ref, cw_ref, al_ref, dtb_ref, qng_ref, kng_ref, hm_ref,
                        qa_ref, ka_ref, va_ref, z_ref, gt_ref, qb_ref, kb_ref, vb_ref, cn_ref,
                        ext_ref):
    t = pl.program_id(1)
    tm = x_ref.shape[1]

    @pl.when(t == 0)
    def _():
        ext_ref[0:SUBLANES, :] = jnp.zeros((SUBLANES, CONV_CH), F32)

    @pl.when(t > 0)
    def _():
        ext_ref[0:SUBLANES, :] = ext_ref[tm:tm + SUBLANES, :]

    vals = _proj_body(x_ref[0], ext_ref, 1, SUBLANES,
                      (n1_ref, w_ref, cw_ref, al_ref, dtb_ref, qng_ref, kng_ref, hm_ref))
    for r, v in zip((qa_ref, ka_ref, va_ref, z_ref, gt_ref, qb_ref, kb_ref, vb_ref), vals):
        r[0] = v
    cn_ref[0] = ext_ref[tm + SUBLANES - (CONV_WIDTH - 1):tm + SUBLANES, :]


def _proj_sample_kernel(x_ref, st_ref, n1_ref, w_ref, cw_ref, al_ref, dtb_ref, qng_ref, kng_ref,
                        hm_ref, qa_ref, ka_ref, va_ref, z_ref, gt_ref, qb_ref, kb_ref, vb_ref,
                        cn_ref, ext_ref):
    nt, nb, d = x_ref.shape
    ncv = CONV_WIDTH - 1
    tm = nb * nt
    pad = ncv * nb
    ext_ref[0:pad, :] = st_ref[...].reshape(pad, CONV_CH)
    vals = _proj_body(x_ref[...].reshape(tm, d), ext_ref, nb, pad,
                      (n1_ref, w_ref, cw_ref, al_ref, dtb_ref, qng_ref, kng_ref, hm_ref))
    for r, v in zip((qa_ref, ka_ref, va_ref, z_ref, gt_ref, qb_ref, kb_ref, vb_ref), vals):
        r[...] = v.reshape(r.shape)
    cn_ref[...] = ext_ref[tm:tm + pad, :].reshape(ncv, nb, CONV_CH)


def _proj_params(p):
    full = lambda a: pl.BlockSpec(a.shape, lambda *_: (0,) * a.ndim)
    arrs = (p["n1"], p["w_in"], p["conv_w"], p["alog"], p["dtb"], p["qng"], p["kng"], p["hm"])
    return arrs, [full(a) for a in arrs]


def _proj_prompt(x, p, tm):
    b, t, d = x.shape
    arrs, specs = _proj_params(p)
    row = lambda w: pl.BlockSpec((1, tm, w), lambda i, j: (i, j, 0))
    widths = (WIDTH_A, WIDTH_A, WIDTH_A, WIDTH_A, LANES, WIDTH_B, WIDTH_B, WIDTH_B)
    out_shape = [jax.ShapeDtypeStruct((b, t, w), F32) for w in widths]
    out_shape.append(jax.ShapeDtypeStruct((b, CONV_WIDTH - 1, CONV_CH), F32))
    out_specs = [row(w) for w in widths]
    out_specs.append(pl.BlockSpec((1, CONV_WIDTH - 1, CONV_CH), lambda i, j: (i, 0, 0)))
    return pl.pallas_call(
        _proj_prompt_kernel,
        grid=(b, t // tm),
        in_specs=[row(d)] + specs,
        out_specs=out_specs,
        out_shape=out_shape,
        scratch_shapes=[pltpu.VMEM((tm + SUBLANES, CONV_CH), F32)],
        compiler_params=pltpu.CompilerParams(
            dimension_semantics=("arbitrary", "arbitrary"), vmem_limit_bytes=VMEM_LIMIT),
        name="proj_prompt",
    )(x, *arrs)


def _proj_sample(x, state_conv, p, bt):
    nb, nt, d = x.shape
    ncv = CONV_WIDTH - 1
    arrs, specs = _proj_params(p)
    blk = lambda r, w: pl.BlockSpec((r, bt, w), lambda i: (0, i, 0))
    widths = (WIDTH_A, WIDTH_A, WIDTH_A, WIDTH_A, LANES, WIDTH_B, WIDTH_B, WIDTH_B)
    out_shape = [jax.ShapeDtypeStruct((nt, nb, w), F32) for w in widths]
    out_shape.append(jax.ShapeDtypeStruct((ncv, nb, CONV_CH), F32))
    out_specs = [blk(nt, w) for w in widths] + [blk(ncv, CONV_CH)]
    outs = pl.pallas_call(
        _proj_sample_kernel,
        grid=(nb // bt,),
        in_specs=[blk(nt, d), blk(ncv, CONV_CH)] + specs,
        out_specs=out_specs,
        out_shape=out_shape,
        scratch_shapes=[pltpu.VMEM((bt * (nt + ncv), CONV_CH), F32)],
        compiler_params=pltpu.CompilerParams(
            dimension_semantics=("arbitrary",), vmem_limit_bytes=VMEM_LIMIT),
        name="proj_sample",
    )(x.transpose(1, 0, 2), state_conv.transpose(1, 0, 2), *arrs)
    return [o.transpose(1, 0, 2) for o in outs]


def _delta_kernel(q_ref, k_ref, v_ref, z_ref, gt_ref, s0_ref, ng_ref, o_ref, sf_ref, s_scr,
                  *, chunk, nchunk, bb):
    c = chunk
    t = pl.program_id(1)

    @pl.when(t == 0)
    def _():
        s_scr[...] = s0_ref[...]

    ii = lax.broadcasted_iota(jnp.int32, (c, c), 0)
    jj = lax.broadcasted_iota(jnp.int32, (c, c), 1)
    causal = ii >= jj
    strict = ii > jj
    tril_bf = causal.astype(F32).astype(BF16)
    eye = (ii == jj).astype(F32)
    level_masks = []
    s = 1
    while s < c:
        level_masks.append(((ii // (2 * s)) == (jj // (2 * s))) & ((ii & s) != 0) & ((jj & s) == 0))
        s *= 2
    ng = ng_ref[...]

    heads = range(N_HEADS_A)
    items = [(bi, ci, h) for bi in range(bb) for ci in range(nchunk) for h in heads]
    pre = {}
    for bi in range(bb):
        for ci in range(nchunk):
            rows = slice(ci * c, (ci + 1) * c)
            gt = gt_ref[bi, rows, :]
            gcum = _dot_exact_lhs(tril_bf, gt)
            if c < LANES:
                gpad = jnp.concatenate([gcum, jnp.zeros((LANES - c, LANES), F32)], axis=0)
            else:
                gpad = gcum
            gcum_t = gpad.T
            for h in heads:
                lo = h * HEAD_DIM_A
                q = q_ref[bi, rows, lo:lo + HEAD_DIM_A]
                k = k_ref[bi, rows, lo:lo + HEAD_DIM_A]
                v = v_ref[bi, rows, lo:lo + HEAD_DIM_A]
                beta = gt[:, h:h + 1]
                g_col = gcum[:, N_HEADS_A + h:N_HEADS_A + h + 1]
                g_row = gcum_t[N_HEADS_A + h:N_HEADS_A + h + 1, 0:c]
                g_last = gcum[c - 1:c, N_HEADS_A + h:N_HEADS_A + h + 1]
                decay = jnp.where(causal, jnp.exp(jnp.where(causal, g_col - g_row, 0.0)), 0.0)
                exp_g = jnp.exp(g_col)
                kb = k * beta
                pre[bi, ci, h] = dict(
                    q=q, k=k, kb=kb, decay=decay, g_last=g_last,
                    rhs=jnp.concatenate([v * beta, kb * exp_g], axis=1),
                    q_dec=q * exp_g, k_dec=k * jnp.exp(g_last - g_col))
    lmat = {it: jnp.where(strict, _dot_nt(pre[it]["kb"], pre[it]["k"]) * pre[it]["decay"], 0.0)
            for it in items}
    qk = {it: _dot_nt(pre[it]["q"], pre[it]["k"]) * pre[it]["decay"] for it in items}
    tinv = {it: eye - jnp.where(level_masks[0], lmat[it], 0.0) for it in items}
    for msk in level_masks[1:]:
        te = {it: _dot(tinv[it], jnp.where(msk, lmat[it], 0.0)) for it in items}
        tinv = {it: tinv[it] - _dot(te[it], tinv[it]) for it in items}
    uw = {it: _dot(tinv[it], pre[it]["rhs"]) for it in items}

    for bi in range(bb):
        st = {h: s_scr[bi, h] for h in heads}
        for ci in range(nchunk):
            rows = slice(ci * c, (ci + 1) * c)
            ws = {h: _dot(jnp.concatenate([uw[bi, ci, h][:, HEAD_DIM_A:2 * HEAD_DIM_A],
                                           pre[bi, ci, h]["q_dec"]], axis=0), st[h])
                  for h in heads}
            v_new = {h: uw[bi, ci, h][:, 0:HEAD_DIM_A] - ws[h][0:c] for h in heads}
            o = {h: ws[h][c:2 * c] + _dot(qk[bi, ci, h], v_new[h]) for h in heads}
            st = {h: st[h] * jnp.exp(pre[bi, ci, h]["g_last"])
                  + _dot_tn(pre[bi, ci, h]["k_dec"], v_new[h]) for h in heads}
            for h in heads:
                lo = h * HEAD_DIM_A
                zz = z_ref[bi, rows, lo:lo + HEAD_DIM_A]
                o_ref[bi, rows, lo:lo + HEAD_DIM_A] = (
                    _rmsnorm(o[h], ng) * _silu(zz)).astype(o_ref.dtype)
        for h in heads:
            s_scr[bi, h] = st[h]

    @pl.when(t == pl.num_programs(1) - 1)
    def _():
        sf_ref[...] = s_scr[...]


def _delta(qa, ka, va, z, gt, s0, ng, *, chunk, nchunk, bb, out_dtype):
    b, t, _ = qa.shape
    tc = chunk * nchunk
    row = lambda w: pl.BlockSpec((bb, tc, w), lambda i, j: (i, j, 0))
    sspec = pl.BlockSpec((bb, N_HEADS_A, HEAD_DIM_A, HEAD_DIM_A), lambda i, j: (i, 0, 0, 0))
    return pl.pallas_call(
        functools.partial(_delta_kernel, chunk=chunk, nchunk=nchunk, bb=bb),
        grid=(b // bb, t // tc),
        in_specs=[row(WIDTH_A)] * 4 + [row(LANES), sspec,
                                       pl.BlockSpec((1, HEAD_DIM_A), lambda i, j: (0, 0))],
        out_specs=[row(WIDTH_A), sspec],
        out_shape=[jax.ShapeDtypeStruct((b, t, WIDTH_A), out_dtype),
                   jax.ShapeDtypeStruct((b, N_HEADS_A, HEAD_DIM_A, HEAD_DIM_A), F32)],
        scratch_shapes=[pltpu.VMEM((bb, N_HEADS_A, HEAD_DIM_A, HEAD_DIM_A), F32)],
        compiler_params=pltpu.CompilerParams(
            dimension_semantics=("arbitrary", "arbitrary"), vmem_limit_bytes=VMEM_LIMIT),
        name="delta_c%d" % chunk,
    )(qa, ka, va, z, gt, s0, ng)


def _attn_prompt_kernel(q_ref, kp_ref, kc_ref, vp_ref, vc_ref, o_ref, qm, kall, vm, acc, *, tq):
    t = pl.program_id(2)
    blk_rows = LANES
    lane_t = lax.broadcasted_iota(jnp.int32, (tq, LANES), 1)
    kall[0:tq, :] = kp_ref[0]
    kall[tq:2 * tq, :] = kc_ref[0]
    scale = HEAD_DIM_B ** -0.5
    for hh in range(2):
        own = (lane_t >= hh * HEAD_DIM_B) & (lane_t < (hh + 1) * HEAD_DIM_B)
        qm[hh] = jnp.where(own, q_ref[0] * scale, 0.0)
        vm[hh, 0:tq, :] = jnp.where(own, vp_ref[0], 1.0)
        vm[hh, tq:2 * tq, :] = jnp.where(own, vc_ref[0], 1.0)

    ii = lax.broadcasted_iota(jnp.int32, (blk_rows, blk_rows), 0)
    jj = lax.broadcasted_iota(jnp.int32, (blk_rows, blk_rows), 1)
    lower = jj <= ii
    upper = jj >= ii
    lane = lax.broadcasted_iota(jnp.int32, (blk_rows, LANES), 1)

    is_max = [(lane ^ (hh * HEAD_DIM_B)) >= HEAD_DIM_B + HEAD_DIM_B // 2 for hh in range(2)]
    max_col = [(HEAD_DIM_B + HEAD_DIM_B // 2) ^ (hh * HEAD_DIM_B) for hh in range(2)]
    n_units = tq // blk_rows
    for pi, d in enumerate((1, 4, 16)):
        nb = tq // (blk_rows * d)
        nb_log2 = nb.bit_length() - 1
        assert nb == 1 << nb_log2

        def body(it, carry, d=d, nb=nb, nb_log2=nb_log2, pi=pi):
            us = [it * ATTN_UNROLL + j for j in range(ATTN_UNROLL)]
            geo = []
            for u in us:
                r = lax.shift_right_logical(u, nb_log2)
                blk = u & (nb - 1)
                qs = r + d * blk_rows * blk
                geo.append((pl.ds(qs, blk_rows, stride=d),
                            pl.ds(tq + qs, blk_rows, stride=d),
                            pl.ds(tq + qs - d * blk_rows, blk_rows, stride=d),
                            jnp.where((blk > 0) | (t > 0), 0.0, NEG)))
            kc = [kall[g[1], :].astype(BF16) for g in geo]
            kp = [kall[g[2], :].astype(BF16) for g in geo]
            chains = [(j, hh) for j in range(ATTN_UNROLL) for hh in range(2)]
            qv = {ch: qm[ch[1], geo[ch[0]][0], :].astype(BF16) for ch in chains}
            s_c = {ch: jnp.where(lower, _dot_nt(qv[ch], kc[ch[0]]), NEG) for ch in chains}
            s_p = {ch: jnp.where(upper, _dot_nt(qv[ch], kp[ch[0]]) + geo[ch[0]][3], NEG)
                   for ch in chains}
            m = {ch: jnp.maximum(jnp.max(s_c[ch], axis=-1, keepdims=True),
                                 jnp.max(s_p[ch], axis=-1, keepdims=True)) for ch in chains}
            pv = {ch: (_dot(jnp.exp(s_c[ch] - m[ch]), vm[ch[1], geo[ch[0]][1], :])
                       + _dot(jnp.exp(s_p[ch] - m[ch]), vm[ch[1], geo[ch[0]][2], :]))
                  for ch in chains}
            for ch in chains:
                j, hh = ch
                rows_q = geo[j][0]
                if pi == 0:
                    acc[hh, rows_q, :] = jnp.where(is_max[hh], m[ch], pv[ch])
                else:
                    old = acc[hh, rows_q, :]
                    m_old = old[:, max_col[hh]:max_col[hh] + 1]
                    m_new = jnp.maximum(m_old, m[ch])
                    merged = old * jnp.exp(m_old - m_new) + pv[ch] * jnp.exp(m[ch] - m_new)
                    acc[hh, rows_q, :] = jnp.where(is_max[hh], m_new, merged)
            return carry

        lax.fori_loop(0, n_units // ATTN_UNROLL, body, 0)

    a0 = acc[0]
    a1 = acc[1]
    l0 = a0[:, HEAD_DIM_B:HEAD_DIM_B + 1]
    l1 = a1[:, 0:1]
    o_ref[0] = jnp.where(lane_t < HEAD_DIM_B, a0 / l0, a1 / l1).astype(o_ref.dtype)


def _attn_prompt(qb, kb, vb):
    b, t, _ = qb.shape
    tq = ATTN_TILE
    nt = t // tq
    cur = pl.BlockSpec((1, tq, LANES), lambda i, h, j: (i, j, h))
    prev = pl.BlockSpec((1, tq, LANES), lambda i, h, j: (i, jnp.maximum(j - 1, 0), h))
    return pl.pallas_call(
        functools.partial(_attn_prompt_kernel, tq=tq),
        grid=(b, WIDTH_B // LANES, nt),
        in_specs=[cur, prev, cur, prev, cur],
        out_specs=cur,
        out_shape=jax.ShapeDtypeStruct((b, t, WIDTH_B), BF16),
        scratch_shapes=[pltpu.VMEM((2, tq, LANES), F32), pltpu.VMEM((2 * tq, LANES), F32),
                        pltpu.VMEM((2, 2 * tq, LANES), F32), pltpu.VMEM((2, tq, LANES), F32)],
        compiler_params=pltpu.CompilerParams(
            dimension_semantics=("arbitrary", "arbitrary", "arbitrary"),
            vmem_limit_bytes=VMEM_LIMIT),
        name="attn_prompt",
    )(qb, kb, kb, vb, vb)


SAMPLE_FAR_PERIOD = 16
SAMPLE_NEAR = 512


def _multiplicity(delta):
    delta = np.asarray(delta)
    ok = delta >= 0
    m = ((delta <= 128).astype(np.float32)
         + ((delta <= 512) & (delta % 4 == 0)).astype(np.float32)
         + ((delta <= 2048) & (delta % 16 == 0)).astype(np.float32))
    return np.where(ok, m, 0.0).astype(np.float32)


def _sample_masks(wbuf, nt):
    nh = N_HEADS_B
    row_i = np.repeat(np.arange(nt), nh)[:, None]
    row_h = np.tile(np.arange(nh), nt)[:, None]
    n_far = (wbuf - SAMPLE_NEAR) // SAMPLE_FAR_PERIOD
    p_far = (np.arange(n_far)[:, None] * SAMPLE_FAR_PERIOD + np.arange(nt)[None, :]).reshape(-1)
    p_near = wbuf - SAMPLE_NEAR + np.arange(SAMPLE_NEAR)
    p_new = wbuf + np.arange(nt)

    def build(p):
        col_p = np.repeat(p, nh)[None, :]
        col_h = np.tile(np.arange(nh), p.shape[0])[None, :]
        return _multiplicity(wbuf + row_i - col_p) * (row_h == col_h)

    return build(p_far), build(p_near), build(p_new)


def _attn_sample_kernel(q_ref, kn_ref, vn_ref, kf_ref, kr_ref, vf_ref, vr_ref, mf_ref, mr_ref,
                        mn_ref, o_ref):
    dh = HEAD_DIM_B
    rows2 = lambda r: r[...].reshape(-1, dh)
    q2 = (rows2(q_ref) * (dh ** -0.5)).astype(BF16)
    parts = ((kf_ref, vf_ref, mf_ref), (kr_ref, vr_ref, mr_ref), (kn_ref, vn_ref, mn_ref))
    scores = []
    m = None
    for k_ref, _, mult_ref in parts:
        s = jnp.where(mult_ref[...] > 0.0, _dot_nt(q2, rows2(k_ref)), NEG)
        scores.append(s)
        pm = jnp.max(s, axis=-1, keepdims=True)
        m = pm if m is None else jnp.maximum(m, pm)
    out = jnp.zeros((q2.shape[0], dh), F32)
    den = jnp.zeros((q2.shape[0], 1), F32)
    for (_, v_ref, mult_ref), s in zip(parts, scores):
        p = (mult_ref[...] * jnp.exp(s - m)).astype(BF16)
        out = out + jnp.dot(p, rows2(v_ref).astype(BF16), preferred_element_type=F32)
        den = den + jnp.sum(p.astype(F32), axis=-1, keepdims=True)
    o_ref[...] = (out / den).reshape(o_ref.shape)


def _attn_sample(qb, kn, vn, cache_k, cache_v):
    nb, nt, nh, dh = qb.shape
    wbuf = cache_k.shape[1]
    per = SAMPLE_FAR_PERIOD
    n_far = (wbuf - SAMPLE_NEAR) // per
    masks = [jnp.asarray(m) for m in _sample_masks(wbuf, nt)]
    ck6 = cache_k.reshape(nb, wbuf // per, per, nh, dh)
    cv6 = cache_v.reshape(nb, wbuf // per, per, nh, dh)
    tok = pl.BlockSpec((1, nt, nh, dh), lambda i: (i, 0, 0, 0))
    far = pl.BlockSpec((1, n_far, nt, nh, dh), lambda i: (i, 0, 0, 0, 0))
    near = pl.BlockSpec((1, SAMPLE_NEAR, nh, dh), lambda i: (i, wbuf // SAMPLE_NEAR - 1, 0, 0))
    mspec = lambda m: pl.BlockSpec(m.shape, lambda i: (0, 0))
    return pl.pallas_call(
        _attn_sample_kernel,
        grid=(nb,),
        in_specs=[tok, tok, tok, far, near, far, near] + [mspec(m) for m in masks],
        out_specs=tok,
        out_shape=jax.ShapeDtypeStruct((nb, nt, nh, dh), F32),
        compiler_params=pltpu.CompilerParams(
            dimension_semantics=("arbitrary",), vmem_limit_bytes=VMEM_LIMIT),
        name="attn_sample",
    )(qb, kn, vn, ck6, cache_k, cv6, cache_v, *masks)


FF_CHUNK = 1024


def _mlp_kernel(x_ref, oa_ref, ob_ref, wo_ref, n2_ref, wu_ref, wd_ref, y_ref):
    mix = jnp.concatenate([oa_ref[...].astype(BF16), ob_ref[...].astype(BF16)], axis=1)
    h1 = x_ref[...] + jnp.dot(mix, wo_ref[...], preferred_element_type=F32)
    hn = _rmsnorm(h1, n2_ref[...]).astype(BF16)
    y = h1
    for c0 in range(0, wu_ref.shape[1], FF_CHUNK):
        hid = jnp.dot(hn, wu_ref[:, c0:c0 + FF_CHUNK], preferred_element_type=F32)
        hid = jnp.square(jnp.maximum(hid, 0.0)).astype(BF16)
        y = y + jnp.dot(hid, wd_ref[c0:c0 + FF_CHUNK, :], preferred_element_type=F32)
    y_ref[...] = y


def _mlp(x, oa, ob, p, tm):
    n, d = x.shape
    row = lambda w: pl.BlockSpec((tm, w), lambda i: (i, 0))
    const = lambda a: pl.BlockSpec(a.shape, lambda i: (0, 0))
    return pl.pallas_call(
        _mlp_kernel,
        grid=(n // tm,),
        in_specs=[row(d), row(WIDTH_A), row(WIDTH_B), const(p["w_o"]), const(p["n2"]),
                  const(p["w_up"]), const(p["w_down"])],
        out_specs=row(d),
        out_shape=jax.ShapeDtypeStruct((n, d), F32),
        compiler_params=pltpu.CompilerParams(
            dimension_semantics=("arbitrary",), vmem_limit_bytes=VMEM_LIMIT),
        name="mlp",
    )(x, oa, ob, p["w_o"], p["n2"], p["w_up"], p["w_down"])


def _layer_params(norm1_g, w_in, conv_w, a_log, dt_bias, delta_norm_g, q_norm_g, k_norm_g, w_o,
                  norm2_g, w_up, w_down):
    d = w_in.shape[0]
    n_gate = 2 * N_HEADS_A
    gate0 = CONV_CH + WIDTH_A
    w_re = jnp.concatenate(
        [w_in[:, :gate0], w_in[:, gate0 + n_gate:], w_in[:, gate0:gate0 + n_gate],
         jnp.zeros((d, LANES - n_gate), w_in.dtype)], axis=1).astype(BF16)
    lane_pad = lambda a: jnp.zeros((1, LANES), F32).at[0, N_HEADS_A:n_gate].set(a.astype(F32))
    hid = np.arange(WIDTH_B) // HEAD_DIM_B
    head_mean = jnp.asarray((hid[:, None] == hid[None, :]).astype(np.float32) / HEAD_DIM_B, BF16)
    return {
        "n1": norm1_g.reshape(1, d).astype(F32), "w_in": w_re, "conv_w": conv_w.astype(F32),
        "alog": lane_pad(a_log), "dtb": lane_pad(dt_bias),
        "qng": jnp.tile(q_norm_g.astype(F32), N_HEADS_B).reshape(1, WIDTH_B),
        "kng": jnp.tile(k_norm_g.astype(F32), N_HEADS_B).reshape(1, WIDTH_B),
        "hm": head_mean, "dng": delta_norm_g.reshape(1, HEAD_DIM_A).astype(F32),
        "w_o": w_o.astype(BF16), "n2": norm2_g.reshape(1, d).astype(F32),
        "w_up": w_up.astype(BF16), "w_down": w_down.astype(BF16),
    }


def _prompt_layer(x, p):
    b, t, d = x.shape
    qa, ka, va, z, gt, qb, kb, vb, conv_new = _proj_prompt(x, p, tm=256)
    s0 = jnp.zeros((b, N_HEADS_A, HEAD_DIM_A, HEAD_DIM_A), F32)
    oa, s_new = _delta(qa, ka, va, z, gt, s0, p["dng"], chunk=DELTA_CHUNK, nchunk=2, bb=1,
                       out_dtype=BF16)
    ob = _attn_prompt(qb, kb, vb)
    y = _mlp(x.reshape(b * t, d), oa.reshape(b * t, WIDTH_A), ob.reshape(b * t, WIDTH_B), p, tm=256)
    return y.reshape(b, t, d), kb, vb, s_new, conv_new


def _sample_layer(x, state_conv, s0, cache_k, cache_v, p):
    nb, nt, d = x.shape
    qa, ka, va, z, gt, qb, kb, vb, conv_new = _proj_sample(x, state_conv, p, bt=min(nb, 64))
    oa, s_new = _delta(qa, ka, va, z, gt, s0, p["dng"], chunk=nt, nchunk=1, bb=min(nb, 4),
                       out_dtype=F32)
    heads = lambda a: a.reshape(nb, nt, N_HEADS_B, HEAD_DIM_B)
    kb, vb = heads(kb), heads(vb)
    ob = _attn_sample(heads(qb), kb, vb, cache_k, cache_v)
    y = _mlp(x.reshape(nb * nt, d), oa.reshape(nb * nt, WIDTH_A), ob.reshape(nb * nt, WIDTH_B), p,
             tm=min(256, nb * nt))
    return y.reshape(nb, nt, d), kb, vb, s_new, conv_new


def kernel(x_prompt, x_sample, cache_swa_k, cache_swa_v, state_delta, state_conv, norm1_g, w_in,
           conv_w, a_log, dt_bias, delta_norm_g, q_norm_g, k_norm_g, w_o, norm2_g, w_up, w_down):
    depth = w_in.shape[0]
    b, s, _ = x_prompt.shape
    nb, nt, _ = x_sample.shape
    wbuf = cache_swa_k.shape[2]
    assert s % ATTN_TILE == 0 and nt == SUBLANES and wbuf == MAX_WINDOW
    pbuf = min(MAX_WINDOW, s)
    yp, ys = x_prompt, x_sample
    outs = [[] for _ in range(8)]
    for layer in range(depth):
        p = _layer_params(norm1_g[layer], w_in[layer], conv_w[layer], a_log[layer], dt_bias[layer],
                          delta_norm_g[layer], q_norm_g[layer], k_norm_g[layer], w_o[layer],
                          norm2_g[layer], w_up[layer], w_down[layer])
        yp, kp, vp, dp, cp = _prompt_layer(yp, p)
        ys, kn, vn, dn, cn = _sample_layer(ys, state_conv[layer], state_delta[layer],
                                           cache_swa_k[layer], cache_swa_v[layer], p)
        heads = lambda a: a.reshape(a.shape[0], a.shape[1], N_HEADS_B, HEAD_DIM_B)
        for lst, val in zip(outs, (heads(kp[:, -pbuf:]), heads(vp[:, -pbuf:]), dp, cp,
                                   kn, vn, dn, cn)):
            lst.append(val)
    return (yp, ys) + tuple(jnp.stack(o) for o in outs)
```

```python
import functools

import numpy as np
import jax
import jax.numpy as jnp
from jax import lax
from jax.experimental import pallas as pl
from jax.experimental.pallas import tpu as pltpu

F32 = jnp.float32
BF16 = jnp.bfloat16

N_HEADS_A = 4
HEAD_DIM_A = 128
WIDTH_A = N_HEADS_A * HEAD_DIM_A
N_HEADS_B = 8
HEAD_DIM_B = 64
WIDTH_B = N_HEADS_B * HEAD_DIM_B
CONV_WIDTH = 4
CONV_CH = 3 * WIDTH_A
DELTA_CHUNK = 64
MAX_WINDOW = 2048
NORM_EPS = 1e-6
LANES = 128
SUBLANES = 8
NEG = -1e30
LOG2_E = 1.4426950408889634

C_QKV, C_Z, C_QB, C_KB, C_VB, C_GATE = 0, 1536, 2048, 2560, 3072, 3584
PROJ_PAD = C_GATE + LANES

ATTN_DILATIONS = (1, 4, 16)
ATTN_BAND = 128
ATTN_TILE = 2048
ATTN_UNROLL = 4
VMEM_LIMIT = 56 * 1024 * 1024


def _dot(a, b):
    return jnp.dot(a.astype(BF16), b.astype(BF16), preferred_element_type=F32)


def _dot_nt(a, b):
    return lax.dot_general(a.astype(BF16), b.astype(BF16), (((1,), (1,)), ((), ())),
                           preferred_element_type=F32)


def _dot_tn(a, b):
    return lax.dot_general(a.astype(BF16), b.astype(BF16), (((0,), (0,)), ((), ())),
                           preferred_element_type=F32)


def _split2(x):
    hi = x.astype(BF16)
    lo = (x - hi.astype(F32)).astype(BF16)
    return hi, lo


def _split3(x):
    hi = x.astype(BF16)
    r = x - hi.astype(F32)
    mid = r.astype(BF16)
    lo = (r - mid.astype(F32)).astype(BF16)
    return hi, mid, lo


def _dot_exact_lhs(mask_bf16, x):
    hi, mid, lo = _split3(x)
    d = lambda p: jnp.dot(mask_bf16, p, preferred_element_type=F32)
    return d(hi) + d(mid) + d(lo)


def _sigmoid(x):
    return 1.0 / (1.0 + jnp.exp(-x))


def _silu(x):
    return x * _sigmoid(x)


def _softplus(x):
    return jnp.maximum(x, 0.0) + jnp.log1p(jnp.exp(-jnp.abs(x)))


def _rmsnorm(x, g):
    return x * lax.rsqrt(jnp.mean(x * x, axis=-1, keepdims=True) + NORM_EPS) * g


def _proj_body(x, ext_ref, shift, pad, refs):
    n1_ref, w_ref, cw_ref, al_ref, dtb_ref, qng_ref, kng_ref, hm_ref = refs
    tm = x.shape[0]
    hn = _rmsnorm(x, n1_ref[...]).astype(BF16)

    u = jnp.dot(hn, w_ref[:, C_QKV:C_QKV + CONV_CH], preferred_element_type=F32)
    ext_ref[pad:pad + tm, :] = u
    cw = cw_ref[...]
    y = u * cw[3:4, :]
    for i in range(CONV_WIDTH - 1):
        off = pad - (CONV_WIDTH - 1 - i) * shift
        y = y + ext_ref[off:off + tm, :] * cw[i:i + 1, :]
    y = _silu(y)
    qa, ka = [], []
    for h in range(N_HEADS_A):
        lo = h * HEAD_DIM_A
        qh = y[:, lo:lo + HEAD_DIM_A]
        qa.append(qh * (lax.rsqrt(
            jnp.sum(qh * qh, axis=-1, keepdims=True) + NORM_EPS) * HEAD_DIM_A ** -0.5))
        kh = y[:, WIDTH_A + lo:WIDTH_A + lo + HEAD_DIM_A]
        ka.append(kh * lax.rsqrt(jnp.sum(kh * kh, axis=-1, keepdims=True) + NORM_EPS))
    qa = jnp.concatenate(qa, axis=1)
    ka = jnp.concatenate(ka, axis=1)
    va = y[:, 2 * WIDTH_A:3 * WIDTH_A]

    z = jnp.dot(hn, w_ref[:, C_Z:C_Z + WIDTH_A], preferred_element_type=F32)

    hm = hm_ref[...]

    def headnorm(v, g):
        hi, lo_ = _split2(v * v)
        ms = (jnp.dot(hi, hm, preferred_element_type=F32)
              + jnp.dot(lo_, hm, preferred_element_type=F32))
        return v * lax.rsqrt(ms + NORM_EPS) * g

    qb = jnp.dot(hn, w_ref[:, C_QB:C_QB + WIDTH_B], preferred_element_type=F32)
    qb = headnorm(qb, qng_ref[...])
    kb = jnp.dot(hn, w_ref[:, C_KB:C_KB + WIDTH_B], preferred_element_type=F32)
    kb = headnorm(kb, kng_ref[...])
    vb = jnp.dot(hn, w_ref[:, C_VB:C_VB + WIDTH_B], preferred_element_type=F32)

    gc = jnp.dot(hn, w_ref[:, C_GATE:C_GATE + LANES], preferred_element_type=F32)
    lane = lax.broadcasted_iota(jnp.int32, gc.shape, 1)
    beta = _sigmoid(gc)
    g = -jnp.exp(al_ref[...]) * _softplus(gc + dtb_ref[...])
    gates = jnp.where(lane < N_HEADS_A, beta, g)
    return qa, ka, va, z, gates, qb, kb, vb


def _proj_prompt_kernel(x_ref, n1_ref, w_ref, cw_ref, al_ref, dtb_ref, qng_ref, kng_ref, hm_ref,
                        qa_ref, ka_ref, va_ref, z_ref, gt_ref, kb_ref, vb_ref, cn_ref,
                        q1_ref, q4_ref, q16_ref, k1_ref, k4_ref, k16_ref, v1_ref, v4_ref, v16_ref,
                        ext_ref, dil_ref):
    t = pl.program_id(1)
    tm = x_ref.shape[1]

    @pl.when(t == 0)
    def _():
        ext_ref[0:SUBLANES, :] = jnp.zeros((SUBLANES, CONV_CH), F32)

    @pl.when(t > 0)
    def _():
        ext_ref[0:SUBLANES, :] = ext_ref[tm:tm + SUBLANES, :]

    qa, ka, va, z, gates, qb, kb, vb = _proj_body(
        x_ref[0], ext_ref, 1, SUBLANES,
        (n1_ref, w_ref, cw_ref, al_ref, dtb_ref, qng_ref, kng_ref, hm_ref))
    for r, v in zip((qa_ref, ka_ref, va_ref, z_ref, gt_ref, kb_ref, vb_ref),
                    (qa, ka, va, z, gates, kb, vb)):
        r[0] = v
    cn_ref[0] = ext_ref[tm + SUBLANES - (CONV_WIDTH - 1):tm + SUBLANES, :]

    nchunk = WIDTH_B // LANES
    q_att = qb * (HEAD_DIM_B ** -0.5 * LOG2_E)
    for ai, (val, r1, r4, r16) in enumerate(((q_att, q1_ref, q4_ref, q16_ref),
                                              (kb, k1_ref, k4_ref, k16_ref),
                                              (vb, v1_ref, v4_ref, v16_ref))):
        r1[0] = val.astype(BF16)
        for c in range(nchunk):
            dil_ref[ai * nchunk + c] = val[:, c * LANES:(c + 1) * LANES]
        for d, rd in ((4, r4), (16, r16)):
            for r in range(d):
                rd[0, r] = jnp.concatenate(
                    [dil_ref[ai * nchunk + c, pl.ds(r, tm // d, stride=d), :]
                     for c in range(nchunk)], axis=1).astype(BF16)


def _proj_sample_kernel(x_ref, st_ref, n1_ref, w_ref, cw_ref, al_ref, dtb_ref, qng_ref, kng_ref,
                        hm_ref, qa_ref, ka_ref, va_ref, z_ref, gt_ref, qb_ref, kb_ref, vb_ref,
                        cn_ref, ext_ref):
    nt, nb, d = x_ref.shape
    ncv = CONV_WIDTH - 1
    tm = nb * nt
    pad = ncv * nb
    ext_ref[0:pad, :] = st_ref[...].reshape(pad, CONV_CH)
    vals = _proj_body(x_ref[...].reshape(tm, d), ext_ref, nb, pad,
                      (n1_ref, w_ref, cw_ref, al_ref, dtb_ref, qng_ref, kng_ref, hm_ref))
    for r, v in zip((qa_ref, ka_ref, va_ref, z_ref, gt_ref, qb_ref, kb_ref, vb_ref), vals):
        r[...] = v.reshape(r.shape)
    cn_ref[...] = ext_ref[tm:tm + pad, :].reshape(ncv, nb, CONV_CH)


def _proj_params(p):
    full = lambda a: pl.BlockSpec(a.shape, lambda *_: (0,) * a.ndim)
    arrs = (p["n1"], p["w_in"], p["conv_w"], p["alog"], p["dtb"], p["qng"], p["kng"], p["hm"])
    return arrs, [full(a) for a in arrs]


def _proj_prompt(x, p, tm):
    b, t, d = x.shape
    arrs, specs = _proj_params(p)
    row = lambda w: pl.BlockSpec((1, tm, w), lambda i, j: (i, j, 0))
    widths = (WIDTH_A, WIDTH_A, WIDTH_A, WIDTH_A, LANES, WIDTH_B, WIDTH_B)
    out_shape = [jax.ShapeDtypeStruct((b, t, w), F32) for w in widths]
    out_shape.append(jax.ShapeDtypeStruct((b, CONV_WIDTH - 1, CONV_CH), F32))
    out_specs = [row(w) for w in widths]
    out_specs.append(pl.BlockSpec((1, CONV_WIDTH - 1, CONV_CH), lambda i, j: (i, 0, 0)))
    for _ in range(3):
        out_shape.append(jax.ShapeDtypeStruct((b, t, WIDTH_B), BF16))
        out_specs.append(row(WIDTH_B))
        for dil in ATTN_DILATIONS[1:]:
            out_shape.append(jax.ShapeDtypeStruct((b, dil, t // dil, WIDTH_B), BF16))
            out_specs.append(pl.BlockSpec((1, dil, tm // dil, WIDTH_B), lambda i, j: (i, 0, j, 0)))
    return pl.pallas_call(
        _proj_prompt_kernel,
        grid=(b, t // tm),
        in_specs=[row(d)] + specs,
        out_specs=out_specs,
        out_shape=out_shape,
        scratch_shapes=[pltpu.VMEM((tm + SUBLANES, CONV_CH), F32),
                        pltpu.VMEM((3 * WIDTH_B // LANES, tm, LANES), F32)],
        compiler_params=pltpu.CompilerParams(
            dimension_semantics=("arbitrary", "arbitrary"), vmem_limit_bytes=VMEM_LIMIT),
        name="proj_prompt",
    )(x, *arrs)


def _proj_sample(x, state_conv, p, bt):
    nb, nt, d = x.shape
    ncv = CONV_WIDTH - 1
    arrs, specs = _proj_params(p)
    blk = lambda r, w: pl.BlockSpec((r, bt, w), lambda i: (0, i, 0))
    widths = (WIDTH_A, WIDTH_A, WIDTH_A, WIDTH_A, LANES, WIDTH_B, WIDTH_B, WIDTH_B)
    out_shape = [jax.ShapeDtypeStruct((nt, nb, w), F32) for w in widths]
    out_shape.append(jax.ShapeDtypeStruct((ncv, nb, CONV_CH), F32))
    out_specs = [blk(nt, w) for w in widths] + [blk(ncv, CONV_CH)]
    outs = pl.pallas_call(
        _proj_sample_kernel,
        grid=(nb // bt,),
        in_specs=[blk(nt, d), blk(ncv, CONV_CH)] + specs,
        out_specs=out_specs,
        out_shape=out_shape,
        scratch_shapes=[pltpu.VMEM((bt * (nt + ncv), CONV_CH), F32)],
        compiler_params=pltpu.CompilerParams(
            dimension_semantics=("arbitrary",), vmem_limit_bytes=VMEM_LIMIT),
        name="proj_sample",
    )(x.transpose(1, 0, 2), state_conv.transpose(1, 0, 2), *arrs)
    return [o.transpose(1, 0, 2) for o in outs]


def _delta_kernel(q_ref, k_ref, v_ref, z_ref, gt_ref, s0_ref, ng_ref, o_ref, sf_ref, s_scr,
                  *, chunk, nchunk, bb):
    c = chunk
    t = pl.program_id(1)

    @pl.when(t == 0)
    def _():
        s_scr[...] = s0_ref[...]

    ii = lax.broadcasted_iota(jnp.int32, (c, c), 0)
    jj = lax.broadcasted_iota(jnp.int32, (c, c), 1)
    causal = ii >= jj
    strict = ii > jj
    tril_bf = causal.astype(F32).astype(BF16)
    eye = (ii == jj).astype(F32)
    level_masks = []
    s = 1
    while s < c:
        level_masks.append(((ii // (2 * s)) == (jj // (2 * s))) & ((ii & s) != 0) & ((jj & s) == 0))
        s *= 2
    ng = ng_ref[...]

    heads = range(N_HEADS_A)
    items = [(bi, ci, h) for bi in range(bb) for ci in range(nchunk) for h in heads]
    pre = {}
    for bi in range(bb):
        for ci in range(nchunk):
            rows = slice(ci * c, (ci + 1) * c)
            gt = gt_ref[bi, rows, :]
            gcum = _dot_exact_lhs(tril_bf, gt)
            if c < LANES:
                gpad = jnp.concatenate([gcum, jnp.zeros((LANES - c, LANES), F32)], axis=0)
            else:
                gpad = gcum
            gcum_t = gpad.T
            for h in heads:
                lo = h * HEAD_DIM_A
                q = q_ref[bi, rows, lo:lo + HEAD_DIM_A]
                k = k_ref[bi, rows, lo:lo + HEAD_DIM_A]
                v = v_ref[bi, rows, lo:lo + HEAD_DIM_A]
                beta = gt[:, h:h + 1]
                g_col = gcum[:, N_HEADS_A + h:N_HEADS_A + h + 1]
                g_row = gcum_t[N_HEADS_A + h:N_HEADS_A + h + 1, 0:c]
                g_last = gcum[c - 1:c, N_HEADS_A + h:N_HEADS_A + h + 1]
                decay = jnp.where(causal, jnp.exp(jnp.where(causal, g_col - g_row, 0.0)), 0.0)
                exp_g = jnp.exp(g_col)
                kb = k * beta
                pre[bi, ci, h] = dict(
                    q=q, k=k, kb=kb, decay=decay, g_last=g_last,
                    rhs=jnp.concatenate([v * beta, kb * exp_g], axis=1),
                    q_dec=q * exp_g, k_dec=k * jnp.exp(g_last - g_col))
    lmat = {it: jnp.where(strict, _dot_nt(pre[it]["kb"], pre[it]["k"]) * pre[it]["decay"], 0.0)
            for it in items}
    qk = {it: _dot_nt(pre[it]["q"], pre[it]["k"]) * pre[it]["decay"] for it in items}
    tinv = {it: eye - jnp.where(level_masks[0], lmat[it], 0.0) for it in items}
    for msk in level_masks[1:]:
        te = {it: _dot(tinv[it], jnp.where(msk, lmat[it], 0.0)) for it in items}
        tinv = {it: tinv[it] - _dot(te[it], tinv[it]) for it in items}
    uw = {it: _dot(tinv[it], pre[it]["rhs"]) for it in items}

    for bi in range(bb):
        st = {h: s_scr[bi, h] for h in heads}
        for ci in range(nchunk):
            rows = slice(ci * c, (ci + 1) * c)
            ws = {h: _dot(jnp.concatenate([uw[bi, ci, h][:, HEAD_DIM_A:2 * HEAD_DIM_A],
                                           pre[bi, ci, h]["q_dec"]], axis=0), st[h])
                  for h in heads}
            v_new = {h: uw[bi, ci, h][:, 0:HEAD_DIM_A] - ws[h][0:c] for h in heads}
            o = {h: ws[h][c:2 * c] + _dot(qk[bi, ci, h], v_new[h]) for h in heads}
            st = {h: st[h] * jnp.exp(pre[bi, ci, h]["g_last"])
                  + _dot_tn(pre[bi, ci, h]["k_dec"], v_new[h]) for h in heads}
            for h in heads:
                lo = h * HEAD_DIM_A
                zz = z_ref[bi, rows, lo:lo + HEAD_DIM_A]
                o_ref[bi, rows, lo:lo + HEAD_DIM_A] = (
                    _rmsnorm(o[h], ng) * _silu(zz)).astype(o_ref.dtype)
        for h in heads:
            s_scr[bi, h] = st[h]

    @pl.when(t == pl.num_programs(1) - 1)
    def _():
        sf_ref[...] = s_scr[...]


def _delta(qa, ka, va, z, gt, s0, ng, *, chunk, nchunk, bb, out_dtype):
    b, t, _ = qa.shape
    tc = chunk * nchunk
    row = lambda w: pl.BlockSpec((bb, tc, w), lambda i, j: (i, j, 0))
    sspec = pl.BlockSpec((bb, N_HEADS_A, HEAD_DIM_A, HEAD_DIM_A), lambda i, j: (i, 0, 0, 0))
    return pl.pallas_call(
        functools.partial(_delta_kernel, chunk=chunk, nchunk=nchunk, bb=bb),
        grid=(b // bb, t // tc),
        in_specs=[row(WIDTH_A)] * 4 + [row(LANES), sspec,
                                       pl.BlockSpec((1, HEAD_DIM_A), lambda i, j: (0, 0))],
        out_specs=[row(WIDTH_A), sspec],
        out_shape=[jax.ShapeDtypeStruct((b, t, WIDTH_A), out_dtype),
                   jax.ShapeDtypeStruct((b, N_HEADS_A, HEAD_DIM_A, HEAD_DIM_A), F32)],
        scratch_shapes=[pltpu.VMEM((bb, N_HEADS_A, HEAD_DIM_A, HEAD_DIM_A), F32)],
        compiler_params=pltpu.CompilerParams(
            dimension_semantics=("arbitrary", "arbitrary"), vmem_limit_bytes=VMEM_LIMIT),
        name="delta_c%d" % chunk,
    )(qa, ka, va, z, gt, s0, ng)


def _attn_prompt_kernel(q1, q4, q16, k1p, k1c, k4p, k4c, k16p, k16c, v1p, v1c, v4p, v4c, v16p, v16c,
                        o_ref, acc, mrep, *, tq):
    t = pl.program_id(2)
    rr = ATTN_BAND
    ii = lax.broadcasted_iota(jnp.int32, (rr, rr), 0)
    jj = lax.broadcasted_iota(jnp.int32, (rr, rr), 1)
    lower = jj <= ii
    upper = jj >= ii
    lane = lax.broadcasted_iota(jnp.int32, (rr, LANES), 1)
    own = [(lane >= hh * HEAD_DIM_B) & (lane < (hh + 1) * HEAD_DIM_B) for hh in range(2)]
    pen_t = jnp.where(t > 0, jnp.float32(0.0), jnp.float32(NEG))
    one = jnp.ones((rr, LANES), BF16)
    zero = jnp.zeros((rr, LANES), BF16)

    def process(units, first):
        chains = [(j, hh) for j in range(len(units)) for hh in range(2)]
        qv = {ch: jnp.where(own[ch[1]], units[ch[0]]["q"], zero) for ch in chains}
        s_c = {ch: jnp.where(lower, _dot_nt(qv[ch], units[ch[0]]["kc"]), NEG) for ch in chains}
        s_p = {}
        for ch in chains:
            s = _dot_nt(qv[ch], units[ch[0]]["kp"])
            if units[ch[0]]["pen"] is not None:
                s = s + units[ch[0]]["pen"]
            s_p[ch] = jnp.where(upper, s, NEG)
        m = {ch: jnp.max(jnp.maximum(s_c[ch], s_p[ch]), axis=-1, keepdims=True) for ch in chains}
        pv = {ch: (_dot(jnp.exp2(s_c[ch] - m[ch]), jnp.where(own[ch[1]], units[ch[0]]["vc"], one))
                   + _dot(jnp.exp2(s_p[ch] - m[ch]), jnp.where(own[ch[1]], units[ch[0]]["vp"], one)))
              for ch in chains}
        for ch in chains:
            j, hh = ch
            rows = units[j]["rows"]
            m_b = jnp.broadcast_to(m[ch], (rr, LANES))
            if first:
                acc[hh, rows, :] = pv[ch]
                mrep[hh, rows, :] = m_b
            else:
                m_old = mrep[hh, rows, :]
                m_new = jnp.maximum(m_old, m_b)
                acc[hh, rows, :] = (acc[hh, rows, :] * jnp.exp2(m_old - m_new)
                                    + pv[ch] * jnp.exp2(m_b - m_new))
                mrep[hh, rows, :] = m_new

    n_units = tq // rr
    uu = ATTN_UNROLL

    d = ATTN_DILATIONS[2]
    assert tq // d == rr

    def body16(it, carry, d=d):
        units = []
        for j in range(uu):
            r = it * uu + j
            units.append(dict(q=q16[0, r], kc=k16c[0, r], kp=k16p[0, r], vc=v16c[0, r],
                              vp=v16p[0, r], pen=pen_t, rows=pl.ds(r, rr, stride=d)))
        process(units, True)
        return carry

    lax.fori_loop(0, d // uu, body16, 0)

    def body1(it, carry):
        kp, vp, pen = carry
        units = []
        for j in range(uu):
            r0 = pl.multiple_of((it * uu + j) * rr, rr)
            kc, vc = k1c[0, pl.ds(r0, rr), :], v1c[0, pl.ds(r0, rr), :]
            units.append(dict(q=q1[0, pl.ds(r0, rr), :], kc=kc, kp=kp, vc=vc, vp=vp,
                              pen=pen if j == 0 else None, rows=pl.ds(r0, rr)))
            kp, vp = kc, vc
        process(units, False)
        return kp, vp, jnp.float32(0.0)

    lax.fori_loop(0, n_units // uu, body1,
                  (k1p[0, tq - rr:tq, :], v1p[0, tq - rr:tq, :], pen_t))

    d = ATTN_DILATIONS[1]
    ln = tq // d
    nblk = ln // rr

    def body4(r, carry, d=d, ln=ln, nblk=nblk):
        units = []
        kp, vp = k4p[0, r, ln - rr:ln, :], v4p[0, r, ln - rr:ln, :]
        for j in range(nblk):
            kc, vc = k4c[0, r, j * rr:(j + 1) * rr, :], v4c[0, r, j * rr:(j + 1) * rr, :]
            units.append(dict(q=q4[0, r, j * rr:(j + 1) * rr, :], kc=kc, kp=kp, vc=vc, vp=vp,
                              pen=pen_t if j == 0 else None,
                              rows=pl.ds(r + d * rr * j, rr, stride=d)))
            kp, vp = kc, vc
        process(units, False)
        return carry

    lax.fori_loop(0, d, body4, 0)

    lane_t = lax.broadcasted_iota(jnp.int32, (tq, LANES), 1)
    a0 = acc[0]
    a1 = acc[1]
    l0 = a0[:, HEAD_DIM_B:HEAD_DIM_B + 1]
    l1 = a1[:, 0:1]
    o_ref[0] = jnp.where(lane_t < HEAD_DIM_B, a0 / l0, a1 / l1).astype(o_ref.dtype)


def _attn_prompt(q, k, v):
    b, t, _ = q[0].shape
    tq = ATTN_TILE
    nt = t // tq
    prev_j = lambda j: jnp.maximum(j - 1, 0)
    cur, prev = [], []
    for dil in ATTN_DILATIONS:
        if dil == 1:
            cur.append(pl.BlockSpec((1, tq, LANES), lambda i, h, j: (i, j, h)))
            prev.append(pl.BlockSpec((1, tq, LANES), lambda i, h, j: (i, prev_j(j), h)))
        else:
            cur.append(pl.BlockSpec((1, dil, tq // dil, LANES), lambda i, h, j: (i, 0, j, h)))
            prev.append(pl.BlockSpec((1, dil, tq // dil, LANES),
                                     lambda i, h, j: (i, 0, prev_j(j), h)))
    kv_specs = [s for pc in zip(prev, cur) for s in pc]
    kv_args = lambda x: [a for xd in x for a in (xd, xd)]
    return pl.pallas_call(
        functools.partial(_attn_prompt_kernel, tq=tq),
        grid=(b, WIDTH_B // LANES, nt),
        in_specs=cur + kv_specs + kv_specs,
        out_specs=cur[0],
        out_shape=jax.ShapeDtypeStruct((b, t, WIDTH_B), BF16),
        scratch_shapes=[pltpu.VMEM((2, tq, LANES), F32), pltpu.VMEM((2, tq, LANES), F32)],
        compiler_params=pltpu.CompilerParams(
            dimension_semantics=("arbitrary", "arbitrary", "arbitrary"),
            vmem_limit_bytes=VMEM_LIMIT),
        name="attn_prompt",
    )(*q, *kv_args(k), *kv_args(v))


def _multiplicity(delta):
    delta = np.asarray(delta)
    ok = delta >= 0
    m = ((delta <= 128).astype(np.float32)
         + ((delta <= 512) & (delta % 4 == 0)).astype(np.float32)
         + ((delta <= 2048) & (delta % 16 == 0)).astype(np.float32))
    return np.where(ok, m, 0.0).astype(np.float32)


def _sample_masks(wbuf, nt):
    i = np.arange(nt)[:, None]
    return (_multiplicity(wbuf + i - np.arange(wbuf)[None, :]),
            _multiplicity(i - np.arange(nt)[None, :]))


def _attn_sample_kernel(q_ref, kn_ref, vn_ref, kt_ref, vt_ref, mc_ref, mn_ref, o_ref):
    dh = HEAD_DIM_B
    heads = range(N_HEADS_B)
    mc = mc_ref[...]
    mn = mn_ref[...]
    sl = lambda r, h: r[0, :, h * dh:(h + 1) * dh]
    q = {h: (sl(q_ref, h) * (dh ** -0.5)).astype(BF16) for h in heads}
    s_c = {h: jnp.where(mc > 0.0, _dot(q[h], kt_ref[0, h]), NEG) for h in heads}
    s_n = {h: jnp.where(mn > 0.0, _dot_nt(q[h], sl(kn_ref, h)), NEG) for h in heads}
    m = {h: jnp.maximum(jnp.max(s_c[h], axis=-1, keepdims=True),
                        jnp.max(s_n[h], axis=-1, keepdims=True)) for h in heads}
    p_c = {h: (mc * jnp.exp(s_c[h] - m[h])).astype(BF16) for h in heads}
    p_n = {h: (mn * jnp.exp(s_n[h] - m[h])).astype(BF16) for h in heads}
    den = {h: (jnp.sum(p_c[h].astype(F32), axis=-1, keepdims=True)
               + jnp.sum(p_n[h].astype(F32), axis=-1, keepdims=True)) for h in heads}
    out = {h: _dot_nt(p_c[h], vt_ref[0, h]) + _dot(p_n[h], sl(vn_ref, h)) for h in heads}
    o_ref[0] = jnp.concatenate([out[h] / den[h] for h in heads], axis=1)


def _attn_sample(qb, kn, vn, cache_kt, cache_vt):
    nb, nt, _ = qb.shape
    _, nh, dh, wbuf = cache_kt.shape
    masks = [jnp.asarray(m) for m in _sample_masks(wbuf, nt)]
    tok = pl.BlockSpec((1, nt, WIDTH_B), lambda i: (i, 0, 0))
    cache = pl.BlockSpec((1, nh, dh, wbuf), lambda i: (i, 0, 0, 0))
    mspec = lambda m: pl.BlockSpec(m.shape, lambda i: (0, 0))
    return pl.pallas_call(
        _attn_sample_kernel,
        grid=(nb,),
        in_specs=[tok, tok, tok, cache, cache] + [mspec(m) for m in masks],
        out_specs=tok,
        out_shape=jax.ShapeDtypeStruct((nb, nt, WIDTH_B), F32),
        compiler_params=pltpu.CompilerParams(
            dimension_semantics=("arbitrary",), vmem_limit_bytes=VMEM_LIMIT),
        name="attn_sample",
    )(qb, kn, vn, cache_kt, cache_vt, *masks)


FF_CHUNK = 1024


def _mlp_kernel(x_ref, oa_ref, ob_ref, wo_ref, n2_ref, wu_ref, wd_ref, y_ref):
    mix = jnp.concatenate([oa_ref[...].astype(BF16), ob_ref[...].astype(BF16)], axis=1)
    h1 = x_ref[...] + jnp.dot(mix, wo_ref[...], preferred_element_type=F32)
    hn = _rmsnorm(h1, n2_ref[...]).astype(BF16)
    y = h1
    for c0 in range(0, wu_ref.shape[1], FF_CHUNK):
        hid = jnp.dot(hn, wu_ref[:, c0:c0 + FF_CHUNK], preferred_element_type=F32)
        hid = jnp.square(jnp.maximum(hid, 0.0)).astype(BF16)
        y = y + jnp.dot(hid, wd_ref[c0:c0 + FF_CHUNK, :], preferred_element_type=F32)
    y_ref[...] = y


def _mlp(x, oa, ob, p, tm):
    n, d = x.shape
    row = lambda w: pl.BlockSpec((tm, w), lambda i: (i, 0))
    const = lambda a: pl.BlockSpec(a.shape, lambda i: (0, 0))
    return pl.pallas_call(
        _mlp_kernel,
        grid=(n // tm,),
        in_specs=[row(d), row(WIDTH_A), row(WIDTH_B), const(p["w_o"]), const(p["n2"]),
                  const(p["w_up"]), const(p["w_down"])],
        out_specs=row(d),
        out_shape=jax.ShapeDtypeStruct((n, d), F32),
        compiler_params=pltpu.CompilerParams(
            dimension_semantics=("arbitrary",), vmem_limit_bytes=VMEM_LIMIT),
        name="mlp",
    )(x, oa, ob, p["w_o"], p["n2"], p["w_up"], p["w_down"])


def _layer_params(norm1_g, w_in, conv_w, a_log, dt_bias, delta_norm_g, q_norm_g, k_norm_g, w_o,
                  norm2_g, w_up, w_down):
    d = w_in.shape[0]
    n_gate = 2 * N_HEADS_A
    gate0 = CONV_CH + WIDTH_A
    w_re = jnp.concatenate(
        [w_in[:, :gate0], w_in[:, gate0 + n_gate:], w_in[:, gate0:gate0 + n_gate],
         jnp.zeros((d, LANES - n_gate), w_in.dtype)], axis=1).astype(BF16)
    lane_pad = lambda a: jnp.zeros((1, LANES), F32).at[0, N_HEADS_A:n_gate].set(a.astype(F32))
    hid = np.arange(WIDTH_B) // HEAD_DIM_B
    head_mean = jnp.asarray((hid[:, None] == hid[None, :]).astype(np.float32) / HEAD_DIM_B, BF16)
    return {
        "n1": norm1_g.reshape(1, d).astype(F32), "w_in": w_re, "conv_w": conv_w.astype(F32),
        "alog": lane_pad(a_log), "dtb": lane_pad(dt_bias),
        "qng": jnp.tile(q_norm_g.astype(F32), N_HEADS_B).reshape(1, WIDTH_B),
        "kng": jnp.tile(k_norm_g.astype(F32), N_HEADS_B).reshape(1, WIDTH_B),
        "hm": head_mean, "dng": delta_norm_g.reshape(1, HEAD_DIM_A).astype(F32),
        "w_o": w_o.astype(BF16), "n2": norm2_g.reshape(1, d).astype(F32),
        "w_up": w_up.astype(BF16), "w_down": w_down.astype(BF16),
    }


def _prompt_layer(x, p):
    b, t, d = x.shape
    qa, ka, va, z, gt, kb, vb, conv_new, *dil = _proj_prompt(x, p, tm=256)
    s0 = jnp.zeros((b, N_HEADS_A, HEAD_DIM_A, HEAD_DIM_A), F32)
    oa, s_new = _delta(qa, ka, va, z, gt, s0, p["dng"], chunk=DELTA_CHUNK, nchunk=2, bb=1,
                       out_dtype=BF16)
    ob = _attn_prompt(dil[0:3], dil[3:6], dil[6:9])
    y = _mlp(x.reshape(b * t, d), oa.reshape(b * t, WIDTH_A), ob.reshape(b * t, WIDTH_B), p, tm=256)
    return y.reshape(b, t, d), kb, vb, s_new, conv_new


def _sample_layer(x, state_conv, s0, cache_k, cache_v, p):
    nb, nt, d = x.shape
    qa, ka, va, z, gt, qb, kb, vb, conv_new = _proj_sample(x, state_conv, p, bt=min(nb, 64))
    oa, s_new = _delta(qa, ka, va, z, gt, s0, p["dng"], chunk=nt, nchunk=1, bb=min(nb, 4),
                       out_dtype=F32)
    ob = _attn_sample(qb, kb, vb, cache_k.transpose(0, 2, 3, 1), cache_v.transpose(0, 2, 3, 1))
    heads = lambda a: a.reshape(nb, nt, N_HEADS_B, HEAD_DIM_B)
    kb, vb = heads(kb), heads(vb)
    y = _mlp(x.reshape(nb * nt, d), oa.reshape(nb * nt, WIDTH_A), ob.reshape(nb * nt, WIDTH_B), p,
             tm=min(256, nb * nt))
    return y.reshape(nb, nt, d), kb, vb, s_new, conv_new


def kernel(x_prompt, x_sample, cache_swa_k, cache_swa_v, state_delta, state_conv, norm1_g, w_in,
           conv_w, a_log, dt_bias, delta_norm_g, q_norm_g, k_norm_g, w_o, norm2_g, w_up, w_down):
    depth = w_in.shape[0]
    b, s, _ = x_prompt.shape
    nb, nt, _ = x_sample.shape
    wbuf = cache_swa_k.shape[2]
    assert s % ATTN_TILE == 0 and nt == SUBLANES and wbuf == MAX_WINDOW
    pbuf = min(MAX_WINDOW, s)
    yp, ys = x_prompt, x_sample
    outs = [[] for _ in range(8)]
    for layer in range(depth):
        p = _layer_params(norm1_g[layer], w_in[layer], conv_w[layer], a_log[layer], dt_bias[layer],
                          delta_norm_g[layer], q_norm_g[layer], k_norm_g[layer], w_o[layer],
                          norm2_g[layer], w_up[layer], w_down[layer])
        yp, kp, vp, dp, cp = _prompt_layer(yp, p)
        ys, kn, vn, dn, cn = _sample_layer(ys, state_conv[layer], state_delta[layer],
                                           cache_swa_k[layer], cache_swa_v[layer], p)
        heads = lambda a: a.reshape(a.shape[0], a.shape[1], N_HEADS_B, HEAD_DIM_B)
        for lst, val in zip(outs, (heads(kp[:, -pbuf:]), heads(vp[:, -pbuf:]), dp, cp,
                                   kn, vn, dn, cn)):
            lst.append(val)
    return (yp, ys) + tuple(jnp.stack(o) for o in outs)
```

```python
import functools

import numpy as np
import jax
import jax.numpy as jnp
from jax import lax
from jax.experimental import pallas as pl
from jax.experimental.pallas import tpu as pltpu

F32 = jnp.float32
BF16 = jnp.bfloat16

N_HEADS_A = 4
HEAD_DIM_A = 128
WIDTH_A = N_HEADS_A * HEAD_DIM_A
N_HEADS_B = 8
HEAD_DIM_B = 64
WIDTH_B = N_HEADS_B * HEAD_DIM_B
CONV_WIDTH = 4
CONV_CH = 3 * WIDTH_A
DELTA_CHUNK = 64
MAX_WINDOW = 2048
NORM_EPS = 1e-6
LANES = 128
SUBLANES = 8
NEG = -1e30
LOG2_E = 1.4426950408889634

C_QKV, C_Z, C_QB, C_KB, C_VB, C_GATE = 0, 1536, 2048, 2560, 3072, 3584
PROJ_PAD = C_GATE + LANES

ATTN_DILATIONS = (1, 4, 16)
ATTN_BAND = 128
ATTN_TILE = 2048
ATTN_UNROLL = 4
VMEM_LIMIT = 56 * 1024 * 1024


def _dot(a, b):
    return jnp.dot(a.astype(BF16), b.astype(BF16), preferred_element_type=F32)


def _dot_nt(a, b):
    return lax.dot_general(a.astype(BF16), b.astype(BF16), (((1,), (1,)), ((), ())),
                           preferred_element_type=F32)


def _dot_tn(a, b):
    return lax.dot_general(a.astype(BF16), b.astype(BF16), (((0,), (0,)), ((), ())),
                           preferred_element_type=F32)


def _split2(x):
    hi = x.astype(BF16)
    lo = (x - hi.astype(F32)).astype(BF16)
    return hi, lo


def _split3(x):
    hi = x.astype(BF16)
    r = x - hi.astype(F32)
    mid = r.astype(BF16)
    lo = (r - mid.astype(F32)).astype(BF16)
    return hi, mid, lo


def _dot_exact_lhs(mask_bf16, x):
    hi, mid, lo = _split3(x)
    d = lambda p: jnp.dot(mask_bf16, p, preferred_element_type=F32)
    return d(hi) + d(mid) + d(lo)


def _sigmoid(x):
    return 0.5 * jnp.tanh(0.5 * x) + 0.5


def _silu(x):
    return x * _sigmoid(x)


def _softplus(x):
    return jnp.maximum(x, 0.0) + jnp.log1p(jnp.exp(-jnp.abs(x)))


def _rmsnorm(x, g):
    return x * lax.rsqrt(jnp.mean(x * x, axis=-1, keepdims=True) + NORM_EPS) * g


def _proj_body(x, ext_ref, shift, pad, refs):
    n1_ref, w_ref, cw_ref, al_ref, dtb_ref, qng_ref, kng_ref, hm_ref = refs
    tm = x.shape[0]
    hn = _rmsnorm(x, n1_ref[...]).astype(BF16)

    u = jnp.dot(hn, w_ref[:, C_QKV:C_QKV + CONV_CH], preferred_element_type=F32)
    ext_ref[pad:pad + tm, :] = u
    cw = cw_ref[...]
    y = u * cw[3:4, :]
    for i in range(CONV_WIDTH - 1):
        off = pad - (CONV_WIDTH - 1 - i) * shift
        y = y + ext_ref[off:off + tm, :] * cw[i:i + 1, :]
    y = _silu(y)
    qa, ka = [], []
    for h in range(N_HEADS_A):
        lo = h * HEAD_DIM_A
        qh = y[:, lo:lo + HEAD_DIM_A]
        qa.append(qh * (lax.rsqrt(
            jnp.sum(qh * qh, axis=-1, keepdims=True) + NORM_EPS) * HEAD_DIM_A ** -0.5))
        kh = y[:, WIDTH_A + lo:WIDTH_A + lo + HEAD_DIM_A]
        ka.append(kh * lax.rsqrt(jnp.sum(kh * kh, axis=-1, keepdims=True) + NORM_EPS))
    qa = jnp.concatenate(qa, axis=1)
    ka = jnp.concatenate(ka, axis=1)
    va = y[:, 2 * WIDTH_A:3 * WIDTH_A]

    z = jnp.dot(hn, w_ref[:, C_Z:C_Z + WIDTH_A], preferred_element_type=F32)

    hm = hm_ref[...]

    def headnorm(v, g):
        ms = jnp.dot((v * v).astype(BF16), hm, preferred_element_type=F32)
        return v * lax.rsqrt(ms + NORM_EPS) * g

    qb = jnp.dot(hn, w_ref[:, C_QB:C_QB + WIDTH_B], preferred_element_type=F32)
    qb = headnorm(qb, qng_ref[...])
    kb = jnp.dot(hn, w_ref[:, C_KB:C_KB + WIDTH_B], preferred_element_type=F32)
    kb = headnorm(kb, kng_ref[...])
    vb = jnp.dot(hn, w_ref[:, C_VB:C_VB + WIDTH_B], preferred_element_type=F32)

    gc = jnp.dot(hn, w_ref[:, C_GATE:C_GATE + LANES], preferred_element_type=F32)
    lane = lax.broadcasted_iota(jnp.int32, gc.shape, 1)
    beta = _sigmoid(gc)
    g = -jnp.exp(al_ref[...]) * _softplus(gc + dtb_ref[...])
    gates = jnp.where(lane < N_HEADS_A, beta, g)
    return qa, ka, va, z, gates, qb, kb, vb


def _proj_prompt_kernel(x_ref, n1_ref, w_ref, cw_ref, al_ref, dtb_ref, qng_ref, kng_ref, hm_ref,
                        qa_ref, ka_ref, va_ref, z_ref, gt_ref, kb_ref, vb_ref, cn_ref,
                        q1_ref, q4_ref, q16_ref, k1_ref, k4_ref, k16_ref, v1_ref, v4_ref, v16_ref,
                        ext_ref, dil_ref):
    t = pl.program_id(1)
    tm = x_ref.shape[1]

    @pl.when(t == 0)
    def _():
        ext_ref[0:SUBLANES, :] = jnp.zeros((SUBLANES, CONV_CH), F32)

    @pl.when(t > 0)
    def _():
        ext_ref[0:SUBLANES, :] = ext_ref[tm:tm + SUBLANES, :]

    qa, ka, va, z, gates, qb, kb, vb = _proj_body(
        x_ref[0], ext_ref, 1, SUBLANES,
        (n1_ref, w_ref, cw_ref, al_ref, dtb_ref, qng_ref, kng_ref, hm_ref))
    for r, v in zip((qa_ref, ka_ref, va_ref, z_ref, gt_ref, kb_ref, vb_ref),
                    (qa, ka, va, z, gates, kb, vb)):
        r[0] = v
    cn_ref[0] = ext_ref[tm + SUBLANES - (CONV_WIDTH - 1):tm + SUBLANES, :]

    nchunk = WIDTH_B // LANES
    q_att = qb * (HEAD_DIM_B ** -0.5 * LOG2_E)
    for ai, (val, r1, r4, r16) in enumerate(((q_att, q1_ref, q4_ref, q16_ref),
                                              (kb, k1_ref, k4_ref, k16_ref),
                                              (vb, v1_ref, v4_ref, v16_ref))):
        r1[0] = val.astype(BF16)
        for c in range(nchunk):
            dil_ref[0, ai * nchunk + c] = val[:, c * LANES:(c + 1) * LANES]
        n4, n16 = tm // 4, tm // 16
        for r in range(4):
            parts = [dil_ref[0, ai * nchunk + c, pl.ds(r, n4, stride=4), :] for c in range(nchunk)]
            r4[0, r] = jnp.concatenate(parts, axis=1).astype(BF16)
            for c in range(nchunk):
                dil_ref[1, ai * nchunk + c, r * n4:(r + 1) * n4, :] = parts[c]
        for r in range(4):
            for r2 in range(4):
                r16[0, r + 4 * r2] = jnp.concatenate(
                    [dil_ref[1, ai * nchunk + c, pl.ds(r * n4 + r2, n16, stride=4), :]
                     for c in range(nchunk)], axis=1).astype(BF16)


def _proj_sample_kernel(x_ref, st_ref, n1_ref, w_ref, cw_ref, al_ref, dtb_ref, qng_ref, kng_ref,
                        hm_ref, qa_ref, ka_ref, va_ref, z_ref, gt_ref, qb_ref, kb_ref, vb_ref,
                        cn_ref, ext_ref):
    nt, nb, d = x_ref.shape
    ncv = CONV_WIDTH - 1
    tm = nb * nt
    pad = ncv * nb
    ext_ref[0:pad, :] = st_ref[...].reshape(pad, CONV_CH)
    vals = _proj_body(x_ref[...].reshape(tm, d), ext_ref, nb, pad,
                      (n1_ref, w_ref, cw_ref, al_ref, dtb_ref, qng_ref, kng_ref, hm_ref))
    for r, v in zip((qa_ref, ka_ref, va_ref, z_ref, gt_ref, qb_ref, kb_ref, vb_ref), vals):
        r[...] = v.reshape(r.shape)
    cn_ref[...] = ext_ref[tm:tm + pad, :].reshape(ncv, nb, CONV_CH)


def _proj_params(p):
    full = lambda a: pl.BlockSpec(a.shape, lambda *_: (0,) * a.ndim)
    arrs = (p["n1"], p["w_in"], p["conv_w"], p["alog"], p["dtb"], p["qng"], p["kng"], p["hm"])
    return arrs, [full(a) for a in arrs]


def _proj_prompt(x, p, tm):
    b, t, d = x.shape
    arrs, specs = _proj_params(p)
    row = lambda w: pl.BlockSpec((1, tm, w), lambda i, j: (i, j, 0))
    widths = (WIDTH_A, WIDTH_A, WIDTH_A, WIDTH_A, LANES, WIDTH_B, WIDTH_B)
    out_shape = [jax.ShapeDtypeStruct((b, t, w), F32) for w in widths]
    out_shape.append(jax.ShapeDtypeStruct((b, CONV_WIDTH - 1, CONV_CH), F32))
    out_specs = [row(w) for w in widths]
    out_specs.append(pl.BlockSpec((1, CONV_WIDTH - 1, CONV_CH), lambda i, j: (i, 0, 0)))
    for _ in range(3):
        out_shape.append(jax.ShapeDtypeStruct((b, t, WIDTH_B), BF16))
        out_specs.append(row(WIDTH_B))
        for dil in ATTN_DILATIONS[1:]:
            out_shape.append(jax.ShapeDtypeStruct((b, dil, t // dil, WIDTH_B), BF16))
            out_specs.append(pl.BlockSpec((1, dil, tm // dil, WIDTH_B), lambda i, j: (i, 0, j, 0)))
    return pl.pallas_call(
        _proj_prompt_kernel,
        grid=(b, t // tm),
        in_specs=[row(d)] + specs,
        out_specs=out_specs,
        out_shape=out_shape,
        scratch_shapes=[pltpu.VMEM((tm + SUBLANES, CONV_CH), F32),
                        pltpu.VMEM((2, 3 * WIDTH_B // LANES, tm, LANES), F32)],
        compiler_params=pltpu.CompilerParams(
            dimension_semantics=("arbitrary", "arbitrary"), vmem_limit_bytes=VMEM_LIMIT),
        name="proj_prompt",
    )(x, *arrs)


def _proj_sample(x, state_conv, p, bt):
    nb, nt, d = x.shape
    ncv = CONV_WIDTH - 1
    arrs, specs = _proj_params(p)
    blk = lambda r, w: pl.BlockSpec((r, bt, w), lambda i: (0, i, 0))
    widths = (WIDTH_A, WIDTH_A, WIDTH_A, WIDTH_A, LANES, WIDTH_B, WIDTH_B, WIDTH_B)
    out_shape = [jax.ShapeDtypeStruct((nt, nb, w), F32) for w in widths]
    out_shape.append(jax.ShapeDtypeStruct((ncv, nb, CONV_CH), F32))
    out_specs = [blk(nt, w) for w in widths] + [blk(ncv, CONV_CH)]
    outs = pl.pallas_call(
        _proj_sample_kernel,
        grid=(nb // bt,),
        in_specs=[blk(nt, d), blk(ncv, CONV_CH)] + specs,
        out_specs=out_specs,
        out_shape=out_shape,
        scratch_shapes=[pltpu.VMEM((bt * (nt + ncv), CONV_CH), F32)],
        compiler_params=pltpu.CompilerParams(
            dimension_semantics=("arbitrary",), vmem_limit_bytes=VMEM_LIMIT),
        name="proj_sample",
    )(x.transpose(1, 0, 2), state_conv.transpose(1, 0, 2), *arrs)
    return [o.transpose(1, 0, 2) for o in outs]


def _delta_kernel(q_ref, k_ref, v_ref, z_ref, gt_ref, s0_ref, ng_ref, o_ref, sf_ref, s_scr,
                  *, chunk, nchunk, bb):
    c = chunk
    t = pl.program_id(1)

    @pl.when(t == 0)
    def _():
        s_scr[...] = s0_ref[...]

    ii = lax.broadcasted_iota(jnp.int32, (c, c), 0)
    jj = lax.broadcasted_iota(jnp.int32, (c, c), 1)
    causal = ii >= jj
    strict = ii > jj
    tril_bf = causal.astype(F32).astype(BF16)
    eye = (ii == jj).astype(F32)
    level_masks = []
    s = 1
    while s < c:
        level_masks.append(((ii // (2 * s)) == (jj // (2 * s))) & ((ii & s) != 0) & ((jj & s) == 0))
        s *= 2
    ng = ng_ref[...]

    heads = range(N_HEADS_A)
    items = [(bi, ci, h) for bi in range(bb) for ci in range(nchunk) for h in heads]
    pre = {}
    for bi in range(bb):
        for ci in range(nchunk):
            rows = slice(ci * c, (ci + 1) * c)
            gt = gt_ref[bi, rows, :]
            gcum = _dot_exact_lhs(tril_bf, gt)
            if c < LANES:
                gpad = jnp.concatenate([gcum, jnp.zeros((LANES - c, LANES), F32)], axis=0)
            else:
                gpad = gcum
            gcum_t = gpad.T
            for h in heads:
                lo = h * HEAD_DIM_A
                q = q_ref[bi, rows, lo:lo + HEAD_DIM_A]
                k = k_ref[bi, rows, lo:lo + HEAD_DIM_A]
                v = v_ref[bi, rows, lo:lo + HEAD_DIM_A]
                beta = gt[:, h:h + 1]
                g_col = gcum[:, N_HEADS_A + h:N_HEADS_A + h + 1]
                g_row = gcum_t[N_HEADS_A + h:N_HEADS_A + h + 1, 0:c]
                g_last = gcum[c - 1:c, N_HEADS_A + h:N_HEADS_A + h + 1]
                decay = jnp.where(causal, jnp.exp(jnp.where(causal, g_col - g_row, 0.0)), 0.0)
                exp_g = jnp.exp(g_col)
                kb = k * beta
                pre[bi, ci, h] = dict(
                    q=q, k=k, kb=kb, decay=decay, g_last=g_last,
                    rhs=jnp.concatenate([v * beta, kb * exp_g], axis=1),
                    q_dec=q * exp_g, k_dec=k * jnp.exp(g_last - g_col))
    lmat = {it: jnp.where(strict, _dot_nt(pre[it]["kb"], pre[it]["k"]) * pre[it]["decay"], 0.0)
            for it in items}
    qk = {it: _dot_nt(pre[it]["q"], pre[it]["k"]) * pre[it]["decay"] for it in items}
    tinv = {it: eye - jnp.where(level_masks[0], lmat[it], 0.0) for it in items}
    for msk in level_masks[1:]:
        te = {it: _dot(tinv[it], jnp.where(msk, lmat[it], 0.0)) for it in items}
        tinv = {it: tinv[it] - _dot(te[it], tinv[it]) for it in items}
    uw = {it: _dot(tinv[it], pre[it]["rhs"]) for it in items}

    for bi in range(bb):
        st = {h: s_scr[bi, h] for h in heads}
        for ci in range(nchunk):
            rows = slice(ci * c, (ci + 1) * c)
            ws = {h: _dot(jnp.concatenate([uw[bi, ci, h][:, HEAD_DIM_A:2 * HEAD_DIM_A],
                                           pre[bi, ci, h]["q_dec"]], axis=0), st[h])
                  for h in heads}
            v_new = {h: uw[bi, ci, h][:, 0:HEAD_DIM_A] - ws[h][0:c] for h in heads}
            o = {h: ws[h][c:2 * c] + _dot(qk[bi, ci, h], v_new[h]) for h in heads}
            st = {h: st[h] * jnp.exp(pre[bi, ci, h]["g_last"])
                  + _dot_tn(pre[bi, ci, h]["k_dec"], v_new[h]) for h in heads}
            for h in heads:
                lo = h * HEAD_DIM_A
                zz = z_ref[bi, rows, lo:lo + HEAD_DIM_A]
                o_ref[bi, rows, lo:lo + HEAD_DIM_A] = (
                    _rmsnorm(o[h], ng) * _silu(zz)).astype(o_ref.dtype)
        for h in heads:
            s_scr[bi, h] = st[h]

    @pl.when(t == pl.num_programs(1) - 1)
    def _():
        sf_ref[...] = s_scr[...]


def _delta(qa, ka, va, z, gt, s0, ng, *, chunk, nchunk, bb, out_dtype):
    b, t, _ = qa.shape
    tc = chunk * nchunk
    row = lambda w: pl.BlockSpec((bb, tc, w), lambda i, j: (i, j, 0))
    sspec = pl.BlockSpec((bb, N_HEADS_A, HEAD_DIM_A, HEAD_DIM_A), lambda i, j: (i, 0, 0, 0))
    return pl.pallas_call(
        functools.partial(_delta_kernel, chunk=chunk, nchunk=nchunk, bb=bb),
        grid=(b // bb, t // tc),
        in_specs=[row(WIDTH_A)] * 4 + [row(LANES), sspec,
                                       pl.BlockSpec((1, HEAD_DIM_A), lambda i, j: (0, 0))],
        out_specs=[row(WIDTH_A), sspec],
        out_shape=[jax.ShapeDtypeStruct((b, t, WIDTH_A), out_dtype),
                   jax.ShapeDtypeStruct((b, N_HEADS_A, HEAD_DIM_A, HEAD_DIM_A), F32)],
        scratch_shapes=[pltpu.VMEM((bb, N_HEADS_A, HEAD_DIM_A, HEAD_DIM_A), F32)],
        compiler_params=pltpu.CompilerParams(
            dimension_semantics=("arbitrary", "arbitrary"), vmem_limit_bytes=VMEM_LIMIT),
        name="delta_c%d" % chunk,
    )(qa, ka, va, z, gt, s0, ng)


def _attn_prompt_kernel(q1, q4, q16, k1p, k1c, k4p, k4c, k16p, k16c, v1p, v1c, v4p, v4c, v16p, v16c,
                        o_ref, acc, mrep, *, tq):
    t = pl.program_id(2)
    rr = ATTN_BAND
    ii = lax.broadcasted_iota(jnp.int32, (rr, rr), 0)
    jj = lax.broadcasted_iota(jnp.int32, (rr, rr), 1)
    lower = jj <= ii
    upper = jj >= ii
    lane = lax.broadcasted_iota(jnp.int32, (rr, LANES), 1)
    own = [(lane >= hh * HEAD_DIM_B) & (lane < (hh + 1) * HEAD_DIM_B) for hh in range(2)]
    pen_t = jnp.where(t > 0, jnp.float32(0.0), jnp.float32(NEG))
    one = jnp.ones((rr, LANES), BF16)
    zero = jnp.zeros((rr, LANES), BF16)

    def process(units, first):
        chains = [(j, hh) for j in range(len(units)) for hh in range(2)]
        qv = {ch: jnp.where(own[ch[1]], units[ch[0]]["q"], zero) for ch in chains}
        s_c = {ch: jnp.where(lower, _dot_nt(qv[ch], units[ch[0]]["kc"]), NEG) for ch in chains}
        s_p = {}
        for ch in chains:
            s = _dot_nt(qv[ch], units[ch[0]]["kp"])
            if units[ch[0]]["pen"] is not None:
                s = s + units[ch[0]]["pen"]
            s_p[ch] = jnp.where(upper, s, NEG)
        m = {ch: jnp.max(jnp.maximum(s_c[ch], s_p[ch]), axis=-1, keepdims=True) for ch in chains}
        pv = {ch: (_dot(jnp.exp2(s_c[ch] - m[ch]), jnp.where(own[ch[1]], units[ch[0]]["vc"], one))
                   + _dot(jnp.exp2(s_p[ch] - m[ch]), jnp.where(own[ch[1]], units[ch[0]]["vp"], one)))
              for ch in chains}
        for ch in chains:
            j, hh = ch
            rows = units[j]["rows"]
            m_b = jnp.broadcast_to(m[ch], (rr, LANES))
            if first:
                acc[hh, rows, :] = pv[ch]
                mrep[hh, rows, :] = m_b
            else:
                m_old = mrep[hh, rows, :]
                m_new = jnp.maximum(m_old, m_b)
                acc[hh, rows, :] = (acc[hh, rows, :] * jnp.exp2(m_old - m_new)
                                    + pv[ch] * jnp.exp2(m_b - m_new))
                mrep[hh, rows, :] = m_new

    n_units = tq // rr
    uu = ATTN_UNROLL

    d = ATTN_DILATIONS[2]
    assert tq // d == rr

    def body16(it, carry, d=d):
        units = []
        for j in range(uu):
            r = it * uu + j
            units.append(dict(q=q16[0, r], kc=k16c[0, r], kp=k16p[0, r], vc=v16c[0, r],
                              vp=v16p[0, r], pen=pen_t, rows=pl.ds(r, rr, stride=d)))
        process(units, True)
        return carry

    lax.fori_loop(0, d // uu, body16, 0)

    def body1(it, carry):
        kp, vp, pen = carry
        units = []
        for j in range(uu):
            r0 = pl.multiple_of((it * uu + j) * rr, rr)
            kc, vc = k1c[0, pl.ds(r0, rr), :], v1c[0, pl.ds(r0, rr), :]
            units.append(dict(q=q1[0, pl.ds(r0, rr), :], kc=kc, kp=kp, vc=vc, vp=vp,
                              pen=pen if j == 0 else None, rows=pl.ds(r0, rr)))
            kp, vp = kc, vc
        process(units, False)
        return kp, vp, jnp.float32(0.0)

    lax.fori_loop(0, n_units // uu, body1,
                  (k1p[0, tq - rr:tq, :], v1p[0, tq - rr:tq, :], pen_t))

    d = ATTN_DILATIONS[1]
    ln = tq // d
    nblk = ln // rr

    def body4(r, carry, d=d, ln=ln, nblk=nblk):
        units = []
        kp, vp = k4p[0, r, ln - rr:ln, :], v4p[0, r, ln - rr:ln, :]
        for j in range(nblk):
            kc, vc = k4c[0, r, j * rr:(j + 1) * rr, :], v4c[0, r, j * rr:(j + 1) * rr, :]
            units.append(dict(q=q4[0, r, j * rr:(j + 1) * rr, :], kc=kc, kp=kp, vc=vc, vp=vp,
                              pen=pen_t if j == 0 else None,
                              rows=pl.ds(r + d * rr * j, rr, stride=d)))
            kp, vp = kc, vc
        process(units, False)
        return carry

    lax.fori_loop(0, d, body4, 0)

    lane_t = lax.broadcasted_iota(jnp.int32, (tq, LANES), 1)
    a0 = acc[0]
    a1 = acc[1]
    l0 = a0[:, HEAD_DIM_B:HEAD_DIM_B + 1]
    l1 = a1[:, 0:1]
    o_ref[0] = jnp.where(lane_t < HEAD_DIM_B, a0 / l0, a1 / l1).astype(o_ref.dtype)


def _attn_prompt(q, k, v):
    b, t, _ = q[0].shape
    tq = ATTN_TILE
    nt = t // tq
    prev_j = lambda j: jnp.maximum(j - 1, 0)
    cur, prev = [], []
    for dil in ATTN_DILATIONS:
        if dil == 1:
            cur.append(pl.BlockSpec((1, tq, LANES), lambda i, h, j: (i, j, h)))
            prev.append(pl.BlockSpec((1, tq, LANES), lambda i, h, j: (i, prev_j(j), h)))
        else:
            cur.append(pl.BlockSpec((1, dil, tq // dil, LANES), lambda i, h, j: (i, 0, j, h)))
            prev.append(pl.BlockSpec((1, dil, tq // dil, LANES),
                                     lambda i, h, j: (i, 0, prev_j(j), h)))
    kv_specs = [s for pc in zip(prev, cur) for s in pc]
    kv_args = lambda x: [a for xd in x for a in (xd, xd)]
    return pl.pallas_call(
        functools.partial(_attn_prompt_kernel, tq=tq),
        grid=(b, WIDTH_B // LANES, nt),
        in_specs=cur + kv_specs + kv_specs,
        out_specs=cur[0],
        out_shape=jax.ShapeDtypeStruct((b, t, WIDTH_B), BF16),
        scratch_shapes=[pltpu.VMEM((2, tq, LANES), F32), pltpu.VMEM((2, tq, LANES), F32)],
        compiler_params=pltpu.CompilerParams(
            dimension_semantics=("arbitrary", "arbitrary", "arbitrary"),
            vmem_limit_bytes=VMEM_LIMIT),
        name="attn_prompt",
    )(*q, *kv_args(k), *kv_args(v))


def _multiplicity(delta):
    delta = np.asarray(delta)
    ok = delta >= 0
    m = ((delta <= 128).astype(np.float32)
         + ((delta <= 512) & (delta % 4 == 0)).astype(np.float32)
         + ((delta <= 2048) & (delta % 16 == 0)).astype(np.float32))
    return np.where(ok, m, 0.0).astype(np.float32)


def _sample_masks(wbuf, nt):
    i = np.arange(nt)[:, None]
    return (_multiplicity(wbuf + i - np.arange(wbuf)[None, :]),
            _multiplicity(i - np.arange(nt)[None, :]))


def _attn_sample_kernel(q_ref, kn_ref, vn_ref, kt_ref, vt_ref, mc_ref, mn_ref, o_ref):
    dh = HEAD_DIM_B
    heads = range(N_HEADS_B)
    mc = mc_ref[...]
    mn = mn_ref[...]
    sl = lambda r, h: r[0, :, h * dh:(h + 1) * dh]
    q = {h: (sl(q_ref, h) * (dh ** -0.5)).astype(BF16) for h in heads}
    s_c = {h: jnp.where(mc > 0.0, _dot(q[h], kt_ref[0, h]), NEG) for h in heads}
    s_n = {h: jnp.where(mn > 0.0, _dot_nt(q[h], sl(kn_ref, h)), NEG) for h in heads}
    m = {h: jnp.maximum(jnp.max(s_c[h], axis=-1, keepdims=True),
                        jnp.max(s_n[h], axis=-1, keepdims=True)) for h in heads}
    p_c = {h: (mc * jnp.exp(s_c[h] - m[h])).astype(BF16) for h in heads}
    p_n = {h: (mn * jnp.exp(s_n[h] - m[h])).astype(BF16) for h in heads}
    den = {h: (jnp.sum(p_c[h].astype(F32), axis=-1, keepdims=True)
               + jnp.sum(p_n[h].astype(F32), axis=-1, keepdims=True)) for h in heads}
    out = {h: _dot_nt(p_c[h], vt_ref[0, h]) + _dot(p_n[h], sl(vn_ref, h)) for h in heads}
    o_ref[0] = jnp.concatenate([out[h] / den[h] for h in heads], axis=1)


def _attn_sample(qb, kn, vn, cache_kt, cache_vt):
    nb, nt, _ = qb.shape
    _, nh, dh, wbuf = cache_kt.shape
    masks = [jnp.asarray(m) for m in _sample_masks(wbuf, nt)]
    tok = pl.BlockSpec((1, nt, WIDTH_B), lambda i: (i, 0, 0))
    cache = pl.BlockSpec((1, nh, dh, wbuf), lambda i: (i, 0, 0, 0))
    mspec = lambda m: pl.BlockSpec(m.shape, lambda i: (0, 0))
    return pl.pallas_call(
        _attn_sample_kernel,
        grid=(nb,),
        in_specs=[tok, tok, tok, cache, cache] + [mspec(m) for m in masks],
        out_specs=tok,
        out_shape=jax.ShapeDtypeStruct((nb, nt, WIDTH_B), F32),
        compiler_params=pltpu.CompilerParams(
            dimension_semantics=("arbitrary",), vmem_limit_bytes=VMEM_LIMIT),
        name="attn_sample",
    )(qb, kn, vn, cache_kt, cache_vt, *masks)


FF_CHUNK = 1024


def _mlp_kernel(x_ref, oa_ref, ob_ref, wo_ref, n2_ref, wu_ref, wd_ref, y_ref):
    mix = jnp.concatenate([oa_ref[...].astype(BF16), ob_ref[...].astype(BF16)], axis=1)
    h1 = x_ref[...] + jnp.dot(mix, wo_ref[...], preferred_element_type=F32)
    hn = _rmsnorm(h1, n2_ref[...]).astype(BF16)
    y = h1
    for c0 in range(0, wu_ref.shape[1], FF_CHUNK):
        hid = jnp.dot(hn, wu_ref[:, c0:c0 + FF_CHUNK], preferred_element_type=F32)
        hid = jnp.square(jnp.maximum(hid, 0.0)).astype(BF16)
        y = y + jnp.dot(hid, wd_ref[c0:c0 + FF_CHUNK, :], preferred_element_type=F32)
    y_ref[...] = y


def _mlp(x, oa, ob, p, tm):
    n, d = x.shape
    row = lambda w: pl.BlockSpec((tm, w), lambda i: (i, 0))
    const = lambda a: pl.BlockSpec(a.shape, lambda i: (0, 0))
    return pl.pallas_call(
        _mlp_kernel,
        grid=(n // tm,),
        in_specs=[row(d), row(WIDTH_A), row(WIDTH_B), const(p["w_o"]), const(p["n2"]),
                  const(p["w_up"]), const(p["w_down"])],
        out_specs=row(d),
        out_shape=jax.ShapeDtypeStruct((n, d), F32),
        compiler_params=pltpu.CompilerParams(
            dimension_semantics=("arbitrary",), vmem_limit_bytes=VMEM_LIMIT),
        name="mlp",
    )(x, oa, ob, p["w_o"], p["n2"], p["w_up"], p["w_down"])


def _layer_params(norm1_g, w_in, conv_w, a_log, dt_bias, delta_norm_g, q_norm_g, k_norm_g, w_o,
                  norm2_g, w_up, w_down):
    d = w_in.shape[0]
    n_gate = 2 * N_HEADS_A
    gate0 = CONV_CH + WIDTH_A
    w_re = jnp.concatenate(
        [w_in[:, :gate0], w_in[:, gate0 + n_gate:], w_in[:, gate0:gate0 + n_gate],
         jnp.zeros((d, LANES - n_gate), w_in.dtype)], axis=1).astype(BF16)
    lane_pad = lambda a: jnp.zeros((1, LANES), F32).at[0, N_HEADS_A:n_gate].set(a.astype(F32))
    hid = np.arange(WIDTH_B) // HEAD_DIM_B
    head_mean = jnp.asarray((hid[:, None] == hid[None, :]).astype(np.float32) / HEAD_DIM_B, BF16)
    return {
        "n1": norm1_g.reshape(1, d).astype(F32), "w_in": w_re, "conv_w": conv_w.astype(F32),
        "alog": lane_pad(a_log), "dtb": lane_pad(dt_bias),
        "qng": jnp.tile(q_norm_g.astype(F32), N_HEADS_B).reshape(1, WIDTH_B),
        "kng": jnp.tile(k_norm_g.astype(F32), N_HEADS_B).reshape(1, WIDTH_B),
        "hm": head_mean, "dng": delta_norm_g.reshape(1, HEAD_DIM_A).astype(F32),
        "w_o": w_o.astype(BF16), "n2": norm2_g.reshape(1, d).astype(F32),
        "w_up": w_up.astype(BF16), "w_down": w_down.astype(BF16),
    }


def _prompt_layer(x, p):
    b, t, d = x.shape
    qa, ka, va, z, gt, kb, vb, conv_new, *dil = _proj_prompt(x, p, tm=256)
    s0 = jnp.zeros((b, N_HEADS_A, HEAD_DIM_A, HEAD_DIM_A), F32)
    oa, s_new = _delta(qa, ka, va, z, gt, s0, p["dng"], chunk=DELTA_CHUNK, nchunk=2, bb=b,
                       out_dtype=BF16)
    ob = _attn_prompt(dil[0:3], dil[3:6], dil[6:9])
    y = _mlp(x.reshape(b * t, d), oa.reshape(b * t, WIDTH_A), ob.reshape(b * t, WIDTH_B), p, tm=256)
    return y.reshape(b, t, d), kb, vb, s_new, conv_new


def _sample_layer(x, state_conv, s0, cache_k, cache_v, p):
    nb, nt, d = x.shape
    qa, ka, va, z, gt, qb, kb, vb, conv_new = _proj_sample(x, state_conv, p, bt=min(nb, 64))
    oa, s_new = _delta(qa, ka, va, z, gt, s0, p["dng"], chunk=nt, nchunk=1, bb=min(nb, 4),
                       out_dtype=F32)
    ob = _attn_sample(qb, kb, vb, cache_k.transpose(0, 2, 3, 1), cache_v.transpose(0, 2, 3, 1))
    heads = lambda a: a.reshape(nb, nt, N_HEADS_B, HEAD_DIM_B)
    kb, vb = heads(kb), heads(vb)
    y = _mlp(x.reshape(nb * nt, d), oa.reshape(nb * nt, WIDTH_A), ob.reshape(nb * nt, WIDTH_B), p,
             tm=min(256, nb * nt))
    return y.reshape(nb, nt, d), kb, vb, s_new, conv_new


def kernel(x_prompt, x_sample, cache_swa_k, cache_swa_v, state_delta, state_conv, norm1_g, w_in,
           conv_w, a_log, dt_bias, delta_norm_g, q_norm_g, k_norm_g, w_o, norm2_g, w_up, w_down):
    depth = w_in.shape[0]
    b, s, _ = x_prompt.shape
    nb, nt, _ = x_sample.shape
    wbuf = cache_swa_k.shape[2]
    assert s % ATTN_TILE == 0 and nt == SUBLANES and wbuf == MAX_WINDOW
    pbuf = min(MAX_WINDOW, s)
    yp, ys = x_prompt, x_sample
    outs = [[] for _ in range(8)]
    for layer in range(depth):
        p = _layer_params(norm1_g[layer], w_in[layer], conv_w[layer], a_log[layer], dt_bias[layer],
                          delta_norm_g[layer], q_norm_g[layer], k_norm_g[layer], w_o[layer],
                          norm2_g[layer], w_up[layer], w_down[layer])
        yp, kp, vp, dp, cp = _prompt_layer(yp, p)
        ys, kn, vn, dn, cn = _sample_layer(ys, state_conv[layer], state_delta[layer],
                                           cache_swa_k[layer], cache_swa_v[layer], p)
        heads = lambda a: a.reshape(a.shape[0], a.shape[1], N_HEADS_B, HEAD_DIM_B)
        for lst, val in zip(outs, (heads(kp[:, -pbuf:]), heads(vp[:, -pbuf:]), dp, cp,
                                   kn, vn, dn, cn)):
            lst.append(val)
    return (yp, ys) + tuple(jnp.stack(o) for o in outs)
```

```python
import functools

import numpy as np
import jax
import jax.numpy as jnp
from jax import lax
from jax.experimental import pallas as pl
from jax.experimental.pallas import tpu as pltpu

F32 = jnp.float32
BF16 = jnp.bfloat16

N_HEADS_A = 4
HEAD_DIM_A = 128
WIDTH_A = N_HEADS_A * HEAD_DIM_A
N_HEADS_B = 8
HEAD_DIM_B = 64
WIDTH_B = N_HEADS_B * HEAD_DIM_B
CONV_WIDTH = 4
CONV_CH = 3 * WIDTH_A
DELTA_CHUNK = 64
MAX_WINDOW = 2048
NORM_EPS = 1e-6
LANES = 128
SUBLANES = 8
NEG = -1e30
LOG2_E = 1.4426950408889634

C_QKV, C_Z, C_QB, C_KB, C_VB, C_GATE = 0, 1536, 2048, 2560, 3072, 3584
PROJ_PAD = C_GATE + LANES

ATTN_DILATIONS = (1, 4, 16)
ATTN_BAND = 128
ATTN_TILE = 2048
ATTN_UNROLL = 4
VMEM_LIMIT = 56 * 1024 * 1024


def _dot(a, b):
    return jnp.dot(a.astype(BF16), b.astype(BF16), preferred_element_type=F32)


def _dot_nt(a, b):
    return lax.dot_general(a.astype(BF16), b.astype(BF16), (((1,), (1,)), ((), ())),
                           preferred_element_type=F32)


def _dot_tn(a, b):
    return lax.dot_general(a.astype(BF16), b.astype(BF16), (((0,), (0,)), ((), ())),
                           preferred_element_type=F32)


def _split2(x):
    hi = x.astype(BF16)
    lo = (x - hi.astype(F32)).astype(BF16)
    return hi, lo


def _split3(x):
    hi = x.astype(BF16)
    r = x - hi.astype(F32)
    mid = r.astype(BF16)
    lo = (r - mid.astype(F32)).astype(BF16)
    return hi, mid, lo


def _dot_exact_lhs(mask_bf16, x):
    hi, mid, lo = _split3(x)
    d = lambda p: jnp.dot(mask_bf16, p, preferred_element_type=F32)
    return d(hi) + d(mid) + d(lo)


def _sigmoid(x):
    return 0.5 * jnp.tanh(0.5 * x) + 0.5


def _silu(x):
    return x * _sigmoid(x)


def _softplus(x):
    return jnp.maximum(x, 0.0) + jnp.log1p(jnp.exp(-jnp.abs(x)))


def _rmsnorm(x, g):
    return x * lax.rsqrt(jnp.mean(x * x, axis=-1, keepdims=True) + NORM_EPS) * g


def _proj_body(x, ext_ref, shift, pad, refs):
    n1_ref, w_ref, cw_ref, al_ref, dtb_ref, qng_ref, kng_ref, hm_ref = refs
    tm = x.shape[0]
    hn = _rmsnorm(x, n1_ref[...]).astype(BF16)

    u = jnp.dot(hn, w_ref[:, C_QKV:C_QKV + CONV_CH], preferred_element_type=F32)
    ext_ref[pad:pad + tm, :] = u
    cw = cw_ref[...]
    y = u * cw[3:4, :]
    for i in range(CONV_WIDTH - 1):
        off = pad - (CONV_WIDTH - 1 - i) * shift
        y = y + ext_ref[off:off + tm, :] * cw[i:i + 1, :]
    y = _silu(y)
    qa, ka = [], []
    for h in range(N_HEADS_A):
        lo = h * HEAD_DIM_A
        qh = y[:, lo:lo + HEAD_DIM_A]
        qa.append(qh * (lax.rsqrt(
            jnp.sum(qh * qh, axis=-1, keepdims=True) + NORM_EPS) * HEAD_DIM_A ** -0.5))
        kh = y[:, WIDTH_A + lo:WIDTH_A + lo + HEAD_DIM_A]
        ka.append(kh * lax.rsqrt(jnp.sum(kh * kh, axis=-1, keepdims=True) + NORM_EPS))
    qa = jnp.concatenate(qa, axis=1)
    ka = jnp.concatenate(ka, axis=1)
    va = y[:, 2 * WIDTH_A:3 * WIDTH_A]

    z = jnp.dot(hn, w_ref[:, C_Z:C_Z + WIDTH_A], preferred_element_type=F32)

    hm = hm_ref[...]

    def headnorm(v, g):
        ms = jnp.dot((v * v).astype(BF16), hm, preferred_element_type=F32)
        return v * lax.rsqrt(ms + NORM_EPS) * g

    qb = jnp.dot(hn, w_ref[:, C_QB:C_QB + WIDTH_B], preferred_element_type=F32)
    qb = headnorm(qb, qng_ref[...])
    kb = jnp.dot(hn, w_ref[:, C_KB:C_KB + WIDTH_B], preferred_element_type=F32)
    kb = headnorm(kb, kng_ref[...])
    vb = jnp.dot(hn, w_ref[:, C_VB:C_VB + WIDTH_B], preferred_element_type=F32)

    gc = jnp.dot(hn, w_ref[:, C_GATE:C_GATE + LANES], preferred_element_type=F32)
    lane = lax.broadcasted_iota(jnp.int32, gc.shape, 1)
    beta = _sigmoid(gc)
    g = -jnp.exp(al_ref[...]) * _softplus(gc + dtb_ref[...])
    gates = jnp.where(lane < N_HEADS_A, beta, g)
    return qa, ka, va, z, gates, qb, kb, vb


def _proj_prompt_kernel(x_ref, n1_ref, w_ref, cw_ref, al_ref, dtb_ref, qng_ref, kng_ref, hm_ref,
                        qa_ref, ka_ref, va_ref, z_ref, gt_ref, kb_ref, vb_ref, cn_ref,
                        q1_ref, q4_ref, q16_ref, k1_ref, k4_ref, k16_ref, v1_ref, v4_ref, v16_ref,
                        ext_ref, dil_ref, *, first_win):
    t = pl.program_id(1)
    tm = x_ref.shape[1]

    @pl.when(t == 0)
    def _():
        ext_ref[0:SUBLANES, :] = jnp.zeros((SUBLANES, CONV_CH), F32)

    @pl.when(t > 0)
    def _():
        ext_ref[0:SUBLANES, :] = ext_ref[tm:tm + SUBLANES, :]

    qa, ka, va, z, gates, qb, kb, vb = _proj_body(
        x_ref[0], ext_ref, 1, SUBLANES,
        (n1_ref, w_ref, cw_ref, al_ref, dtb_ref, qng_ref, kng_ref, hm_ref))
    for r, v in zip((qa_ref, ka_ref, va_ref, z_ref, gt_ref), (qa, ka, va, z, gates)):
        r[0] = v
    cn_ref[0] = ext_ref[tm + SUBLANES - (CONV_WIDTH - 1):tm + SUBLANES, :]

    @pl.when(t >= first_win)
    def _():
        kb_ref[0] = kb.T
        vb_ref[0] = vb.T

    nchunk = WIDTH_B // LANES
    q_att = qb * (HEAD_DIM_B ** -0.5 * LOG2_E)
    for ai, (val, r1, r4, r16) in enumerate(((q_att, q1_ref, q4_ref, q16_ref),
                                              (kb, k1_ref, k4_ref, k16_ref),
                                              (vb, v1_ref, v4_ref, v16_ref))):
        r1[0] = val.astype(BF16)
        for c in range(nchunk):
            dil_ref[0, ai * nchunk + c] = val[:, c * LANES:(c + 1) * LANES]
        n4, n16 = tm // 4, tm // 16
        for r in range(4):
            parts = [dil_ref[0, ai * nchunk + c, pl.ds(r, n4, stride=4), :] for c in range(nchunk)]
            r4[0, r] = jnp.concatenate(parts, axis=1).astype(BF16)
            for c in range(nchunk):
                dil_ref[1, ai * nchunk + c, r * n4:(r + 1) * n4, :] = parts[c]
        for r in range(4):
            for r2 in range(4):
                r16[0, r + 4 * r2] = jnp.concatenate(
                    [dil_ref[1, ai * nchunk + c, pl.ds(r * n4 + r2, n16, stride=4), :]
                     for c in range(nchunk)], axis=1).astype(BF16)


def _proj_sample_kernel(x_ref, st_ref, n1_ref, w_ref, cw_ref, al_ref, dtb_ref, qng_ref, kng_ref,
                        hm_ref, qa_ref, ka_ref, va_ref, z_ref, gt_ref, qb_ref, kb_ref, vb_ref,
                        cn_ref, ext_ref):
    nt, nb, d = x_ref.shape
    ncv = CONV_WIDTH - 1
    tm = nb * nt
    pad = ncv * nb
    ext_ref[0:pad, :] = st_ref[...].reshape(pad, CONV_CH)
    vals = _proj_body(x_ref[...].reshape(tm, d), ext_ref, nb, pad,
                      (n1_ref, w_ref, cw_ref, al_ref, dtb_ref, qng_ref, kng_ref, hm_ref))
    for r, v in zip((qa_ref, ka_ref, va_ref, z_ref, gt_ref, qb_ref, kb_ref, vb_ref), vals):
        r[...] = v.reshape(r.shape)
    cn_ref[...] = ext_ref[tm:tm + pad, :].reshape(ncv, nb, CONV_CH)


def _proj_params(p):
    full = lambda a: pl.BlockSpec(a.shape, lambda *_: (0,) * a.ndim)
    arrs = (p["n1"], p["w_in"], p["conv_w"], p["alog"], p["dtb"], p["qng"], p["kng"], p["hm"])
    return arrs, [full(a) for a in arrs]


def _proj_prompt(x, p, tm, pbuf):
    b, t, d = x.shape
    assert pbuf % tm == 0 and t % tm == 0
    first_win = (t - pbuf) // tm
    arrs, specs = _proj_params(p)
    row = lambda w: pl.BlockSpec((1, tm, w), lambda i, j: (i, j, 0))
    widths = (WIDTH_A, WIDTH_A, WIDTH_A, WIDTH_A, LANES)
    out_shape = [jax.ShapeDtypeStruct((b, t, w), F32) for w in widths]
    out_specs = [row(w) for w in widths]
    for _ in range(2):
        out_shape.append(jax.ShapeDtypeStruct((b, WIDTH_B, pbuf), F32))
        out_specs.append(pl.BlockSpec((1, WIDTH_B, tm),
                                      lambda i, j: (i, 0, jnp.maximum(j - first_win, 0))))
    out_shape.append(jax.ShapeDtypeStruct((b, CONV_WIDTH - 1, CONV_CH), F32))
    out_specs.append(pl.BlockSpec((1, CONV_WIDTH - 1, CONV_CH), lambda i, j: (i, 0, 0)))
    for _ in range(3):
        out_shape.append(jax.ShapeDtypeStruct((b, t, WIDTH_B), BF16))
        out_specs.append(row(WIDTH_B))
        for dil in ATTN_DILATIONS[1:]:
            out_shape.append(jax.ShapeDtypeStruct((b, dil, t // dil, WIDTH_B), BF16))
            out_specs.append(pl.BlockSpec((1, dil, tm // dil, WIDTH_B), lambda i, j: (i, 0, j, 0)))
    return pl.pallas_call(
        functools.partial(_proj_prompt_kernel, first_win=first_win),
        grid=(b, t // tm),
        in_specs=[row(d)] + specs,
        out_specs=out_specs,
        out_shape=out_shape,
        scratch_shapes=[pltpu.VMEM((tm + SUBLANES, CONV_CH), F32),
                        pltpu.VMEM((2, 3 * WIDTH_B // LANES, tm, LANES), F32)],
        compiler_params=pltpu.CompilerParams(
            dimension_semantics=("arbitrary", "arbitrary"), vmem_limit_bytes=VMEM_LIMIT),
        name="proj_prompt",
    )(x, *arrs)


def _proj_sample(x, state_conv, p, bt):
    nb, nt, d = x.shape
    ncv = CONV_WIDTH - 1
    arrs, specs = _proj_params(p)
    blk = lambda r, w: pl.BlockSpec((r, bt, w), lambda i: (0, i, 0))
    widths = (WIDTH_A, WIDTH_A, WIDTH_A, WIDTH_A, LANES, WIDTH_B, WIDTH_B, WIDTH_B)
    out_shape = [jax.ShapeDtypeStruct((nt, nb, w), F32) for w in widths]
    out_shape.append(jax.ShapeDtypeStruct((ncv, nb, CONV_CH), F32))
    out_specs = [blk(nt, w) for w in widths] + [blk(ncv, CONV_CH)]
    outs = pl.pallas_call(
        _proj_sample_kernel,
        grid=(nb // bt,),
        in_specs=[blk(nt, d), blk(ncv, CONV_CH)] + specs,
        out_specs=out_specs,
        out_shape=out_shape,
        scratch_shapes=[pltpu.VMEM((bt * (nt + ncv), CONV_CH), F32)],
        compiler_params=pltpu.CompilerParams(
            dimension_semantics=("arbitrary",), vmem_limit_bytes=VMEM_LIMIT),
        name="proj_sample",
    )(x.transpose(1, 0, 2), state_conv.transpose(1, 0, 2), *arrs)
    return [o.transpose(1, 0, 2) for o in outs]


def _delta_kernel(q_ref, k_ref, v_ref, z_ref, gt_ref, s0_ref, ng_ref, o_ref, sf_ref, s_scr,
                  *, chunk, nchunk, bb):
    c = chunk
    t = pl.program_id(1)

    @pl.when(t == 0)
    def _():
        s_scr[...] = s0_ref[...]

    ii = lax.broadcasted_iota(jnp.int32, (c, c), 0)
    jj = lax.broadcasted_iota(jnp.int32, (c, c), 1)
    causal = ii >= jj
    strict = ii > jj
    tril_bf = causal.astype(F32).astype(BF16)
    eye = (ii == jj).astype(F32)
    level_masks = []
    s = 1
    while s < c:
        level_masks.append(((ii // (2 * s)) == (jj // (2 * s))) & ((ii & s) != 0) & ((jj & s) == 0))
        s *= 2
    ng = ng_ref[...]

    heads = range(N_HEADS_A)
    items = [(bi, ci, h) for bi in range(bb) for ci in range(nchunk) for h in heads]
    pre = {}
    for bi in range(bb):
        for ci in range(nchunk):
            rows = slice(ci * c, (ci + 1) * c)
            gt = gt_ref[bi, rows, :]
            gcum = _dot_exact_lhs(tril_bf, gt)
            if c < LANES:
                gpad = jnp.concatenate([gcum, jnp.zeros((LANES - c, LANES), F32)], axis=0)
            else:
                gpad = gcum
            gcum_t = gpad.T
            for h in heads:
                lo = h * HEAD_DIM_A
                q = q_ref[bi, rows, lo:lo + HEAD_DIM_A]
                k = k_ref[bi, rows, lo:lo + HEAD_DIM_A]
                v = v_ref[bi, rows, lo:lo + HEAD_DIM_A]
                beta = gt[:, h:h + 1]
                g_col = gcum[:, N_HEADS_A + h:N_HEADS_A + h + 1]
                g_row = gcum_t[N_HEADS_A + h:N_HEADS_A + h + 1, 0:c]
                g_last = gcum[c - 1:c, N_HEADS_A + h:N_HEADS_A + h + 1]
                decay = jnp.where(causal, jnp.exp(jnp.where(causal, g_col - g_row, 0.0)), 0.0)
                exp_g = jnp.exp(g_col)
                kb = k * beta
                pre[bi, ci, h] = dict(
                    q=q, k=k, kb=kb, decay=decay, g_last=g_last,
                    rhs=jnp.concatenate([v * beta, kb * exp_g], axis=1),
                    q_dec=q * exp_g, k_dec=k * jnp.exp(g_last - g_col))
    lmat = {it: jnp.where(strict, _dot_nt(pre[it]["kb"], pre[it]["k"]) * pre[it]["decay"], 0.0)
            for it in items}
    qk = {it: _dot_nt(pre[it]["q"], pre[it]["k"]) * pre[it]["decay"] for it in items}
    tinv = {it: eye - jnp.where(level_masks[0], lmat[it], 0.0) for it in items}
    for msk in level_masks[1:]:
        te = {it: _dot(tinv[it], jnp.where(msk, lmat[it], 0.0)) for it in items}
        tinv = {it: tinv[it] - _dot(te[it], tinv[it]) for it in items}
    uw = {it: _dot(tinv[it], pre[it]["rhs"]) for it in items}

    for bi in range(bb):
        st = {h: s_scr[bi, h] for h in heads}
        for ci in range(nchunk):
            rows = slice(ci * c, (ci + 1) * c)
            ws = {h: _dot(jnp.concatenate([uw[bi, ci, h][:, HEAD_DIM_A:2 * HEAD_DIM_A],
                                           pre[bi, ci, h]["q_dec"]], axis=0), st[h])
                  for h in heads}
            v_new = {h: uw[bi, ci, h][:, 0:HEAD_DIM_A] - ws[h][0:c] for h in heads}
            o = {h: ws[h][c:2 * c] + _dot(qk[bi, ci, h], v_new[h]) for h in heads}
            st = {h: st[h] * jnp.exp(pre[bi, ci, h]["g_last"])
                  + _dot_tn(pre[bi, ci, h]["k_dec"], v_new[h]) for h in heads}
            for h in heads:
                lo = h * HEAD_DIM_A
                zz = z_ref[bi, rows, lo:lo + HEAD_DIM_A]
                o_ref[bi, rows, lo:lo + HEAD_DIM_A] = (
                    _rmsnorm(o[h], ng) * _silu(zz)).astype(o_ref.dtype)
        for h in heads:
            s_scr[bi, h] = st[h]

    @pl.when(t == pl.num_programs(1) - 1)
    def _():
        sf_ref[...] = s_scr[...]


def _delta(qa, ka, va, z, gt, s0, ng, *, chunk, nchunk, bb, out_dtype):
    b, t, _ = qa.shape
    tc = chunk * nchunk
    row = lambda w: pl.BlockSpec((bb, tc, w), lambda i, j: (i, j, 0))
    sspec = pl.BlockSpec((bb, N_HEADS_A, HEAD_DIM_A, HEAD_DIM_A), lambda i, j: (i, 0, 0, 0))
    return pl.pallas_call(
        functools.partial(_delta_kernel, chunk=chunk, nchunk=nchunk, bb=bb),
        grid=(b // bb, t // tc),
        in_specs=[row(WIDTH_A)] * 4 + [row(LANES), sspec,
                                       pl.BlockSpec((1, HEAD_DIM_A), lambda i, j: (0, 0))],
        out_specs=[row(WIDTH_A), sspec],
        out_shape=[jax.ShapeDtypeStruct((b, t, WIDTH_A), out_dtype),
                   jax.ShapeDtypeStruct((b, N_HEADS_A, HEAD_DIM_A, HEAD_DIM_A), F32)],
        scratch_shapes=[pltpu.VMEM((bb, N_HEADS_A, HEAD_DIM_A, HEAD_DIM_A), F32)],
        compiler_params=pltpu.CompilerParams(
            dimension_semantics=("arbitrary", "arbitrary"), vmem_limit_bytes=VMEM_LIMIT),
        name="delta_c%d" % chunk,
    )(qa, ka, va, z, gt, s0, ng)


def _attn_prompt_kernel(q1, q4, q16, k1p, k1c, k4p, k4c, k16p, k16c, v1p, v1c, v4p, v4c, v16p, v16c,
                        o_ref, acc, mrep, *, tq):
    t = pl.program_id(2)
    rr = ATTN_BAND
    ii = lax.broadcasted_iota(jnp.int32, (rr, rr), 0)
    jj = lax.broadcasted_iota(jnp.int32, (rr, rr), 1)
    lower = jj <= ii
    upper = jj >= ii
    lane = lax.broadcasted_iota(jnp.int32, (rr, LANES), 1)
    own = [(lane >= hh * HEAD_DIM_B) & (lane < (hh + 1) * HEAD_DIM_B) for hh in range(2)]
    pen_t = jnp.where(t > 0, jnp.float32(0.0), jnp.float32(NEG))
    one = jnp.ones((rr, LANES), BF16)
    zero = jnp.zeros((rr, LANES), BF16)

    def process(units, first):
        chains = [(j, hh) for j in range(len(units)) for hh in range(2)]
        qv = {ch: jnp.where(own[ch[1]], units[ch[0]]["q"], zero) for ch in chains}
        s_c = {ch: jnp.where(lower, _dot_nt(qv[ch], units[ch[0]]["kc"]), NEG) for ch in chains}
        s_p = {}
        for ch in chains:
            s = _dot_nt(qv[ch], units[ch[0]]["kp"])
            if units[ch[0]]["pen"] is not None:
                s = s + units[ch[0]]["pen"]
            s_p[ch] = jnp.where(upper, s, NEG)
        m = {ch: jnp.max(jnp.maximum(s_c[ch], s_p[ch]), axis=-1, keepdims=True) for ch in chains}
        pv = {ch: (_dot(jnp.exp2(s_c[ch] - m[ch]), jnp.where(own[ch[1]], units[ch[0]]["vc"], one))
                   + _dot(jnp.exp2(s_p[ch] - m[ch]), jnp.where(own[ch[1]], units[ch[0]]["vp"], one)))
              for ch in chains}
        for ch in chains:
            j, hh = ch
            rows = units[j]["rows"]
            m_b = jnp.broadcast_to(m[ch], (rr, LANES))
            if first:
                acc[hh, rows, :] = pv[ch]
                mrep[hh, rows, :] = m_b
            else:
                m_old = mrep[hh, rows, :]
                m_new = jnp.maximum(m_old, m_b)
                acc[hh, rows, :] = (acc[hh, rows, :] * jnp.exp2(m_old - m_new)
                                    + pv[ch] * jnp.exp2(m_b - m_new))
                mrep[hh, rows, :] = m_new

    n_units = tq // rr
    uu = ATTN_UNROLL

    d = ATTN_DILATIONS[2]
    assert tq // d == rr

    def body16(it, carry, d=d):
        units = []
        for j in range(uu):
            r = it * uu + j
            units.append(dict(q=q16[0, r], kc=k16c[0, r], kp=k16p[0, r], vc=v16c[0, r],
                              vp=v16p[0, r], pen=pen_t, rows=pl.ds(r, rr, stride=d)))
        process(units, True)
        return carry

    lax.fori_loop(0, d // uu, body16, 0)

    def body1(it, carry):
        kp, vp, pen = carry
        units = []
        for j in range(uu):
            r0 = pl.multiple_of((it * uu + j) * rr, rr)
            kc, vc = k1c[0, pl.ds(r0, rr), :], v1c[0, pl.ds(r0, rr), :]
            units.append(dict(q=q1[0, pl.ds(r0, rr), :], kc=kc, kp=kp, vc=vc, vp=vp,
                              pen=pen if j == 0 else None, rows=pl.ds(r0, rr)))
            kp, vp = kc, vc
        process(units, False)
        return kp, vp, jnp.float32(0.0)

    lax.fori_loop(0, n_units // uu, body1,
                  (k1p[0, tq - rr:tq, :], v1p[0, tq - rr:tq, :], pen_t))

    d = ATTN_DILATIONS[1]
    ln = tq // d
    nblk = ln // rr

    def body4(r, carry, d=d, ln=ln, nblk=nblk):
        units = []
        kp, vp = k4p[0, r, ln - rr:ln, :], v4p[0, r, ln - rr:ln, :]
        for j in range(nblk):
            kc, vc = k4c[0, r, j * rr:(j + 1) * rr, :], v4c[0, r, j * rr:(j + 1) * rr, :]
            units.append(dict(q=q4[0, r, j * rr:(j + 1) * rr, :], kc=kc, kp=kp, vc=vc, vp=vp,
                              pen=pen_t if j == 0 else None,
                              rows=pl.ds(r + d * rr * j, rr, stride=d)))
            kp, vp = kc, vc
        process(units, False)
        return carry

    lax.fori_loop(0, d, body4, 0)

    lane_t = lax.broadcasted_iota(jnp.int32, (tq, LANES), 1)
    a0 = acc[0]
    a1 = acc[1]
    l0 = a0[:, HEAD_DIM_B:HEAD_DIM_B + 1]
    l1 = a1[:, 0:1]
    o_ref[0] = jnp.where(lane_t < HEAD_DIM_B, a0 / l0, a1 / l1).astype(o_ref.dtype)


def _attn_prompt(q, k, v):
    b, t, _ = q[0].shape
    tq = ATTN_TILE
    nt = t // tq
    prev_j = lambda j: jnp.maximum(j - 1, 0)
    cur, prev = [], []
    for dil in ATTN_DILATIONS:
        if dil == 1:
            cur.append(pl.BlockSpec((1, tq, LANES), lambda i, h, j: (i, j, h)))
            prev.append(pl.BlockSpec((1, tq, LANES), lambda i, h, j: (i, prev_j(j), h)))
        else:
            cur.append(pl.BlockSpec((1, dil, tq // dil, LANES), lambda i, h, j: (i, 0, j, h)))
            prev.append(pl.BlockSpec((1, dil, tq // dil, LANES),
                                     lambda i, h, j: (i, 0, prev_j(j), h)))
    kv_specs = [s for pc in zip(prev, cur) for s in pc]
    kv_args = lambda x: [a for xd in x for a in (xd, xd)]
    return pl.pallas_call(
        functools.partial(_attn_prompt_kernel, tq=tq),
        grid=(b, WIDTH_B // LANES, nt),
        in_specs=cur + kv_specs + kv_specs,
        out_specs=cur[0],
        out_shape=jax.ShapeDtypeStruct((b, t, WIDTH_B), BF16),
        scratch_shapes=[pltpu.VMEM((2, tq, LANES), F32), pltpu.VMEM((2, tq, LANES), F32)],
        compiler_params=pltpu.CompilerParams(
            dimension_semantics=("arbitrary", "arbitrary", "arbitrary"),
            vmem_limit_bytes=VMEM_LIMIT),
        name="attn_prompt",
    )(*q, *kv_args(k), *kv_args(v))


def _multiplicity(delta):
    delta = np.asarray(delta)
    ok = delta >= 0
    m = ((delta <= 128).astype(np.float32)
         + ((delta <= 512) & (delta % 4 == 0)).astype(np.float32)
         + ((delta <= 2048) & (delta % 16 == 0)).astype(np.float32))
    return np.where(ok, m, 0.0).astype(np.float32)


def _sample_masks(wbuf, nt):
    i = np.arange(nt)[:, None]
    return (_multiplicity(wbuf + i - np.arange(wbuf)[None, :]),
            _multiplicity(i - np.arange(nt)[None, :]))


def _attn_sample_kernel(q_ref, kn_ref, vn_ref, kt_ref, vt_ref, mc_ref, mn_ref, o_ref):
    dh = HEAD_DIM_B
    heads = range(N_HEADS_B)
    mc = mc_ref[...]
    mn = mn_ref[...]
    sl = lambda r, h: r[0, :, h * dh:(h + 1) * dh]
    q = {h: (sl(q_ref, h) * (dh ** -0.5)).astype(BF16) for h in heads}
    s_c = {h: jnp.where(mc > 0.0, _dot(q[h], kt_ref[0, h]), NEG) for h in heads}
    s_n = {h: jnp.where(mn > 0.0, _dot_nt(q[h], sl(kn_ref, h)), NEG) for h in heads}
    m = {h: jnp.maximum(jnp.max(s_c[h], axis=-1, keepdims=True),
                        jnp.max(s_n[h], axis=-1, keepdims=True)) for h in heads}
    p_c = {h: (mc * jnp.exp(s_c[h] - m[h])).astype(BF16) for h in heads}
    p_n = {h: (mn * jnp.exp(s_n[h] - m[h])).astype(BF16) for h in heads}
    den = {h: (jnp.sum(p_c[h].astype(F32), axis=-1, keepdims=True)
               + jnp.sum(p_n[h].astype(F32), axis=-1, keepdims=True)) for h in heads}
    out = {h: _dot_nt(p_c[h], vt_ref[0, h]) + _dot(p_n[h], sl(vn_ref, h)) for h in heads}
    o_ref[0] = jnp.concatenate([out[h] / den[h] for h in heads], axis=1)


def _attn_sample(qb, kn, vn, cache_kt, cache_vt):
    nb, nt, _ = qb.shape
    _, nh, dh, wbuf = cache_kt.shape
    masks = [jnp.asarray(m) for m in _sample_masks(wbuf, nt)]
    tok = pl.BlockSpec((1, nt, WIDTH_B), lambda i: (i, 0, 0))
    cache = pl.BlockSpec((1, nh, dh, wbuf), lambda i: (i, 0, 0, 0))
    mspec = lambda m: pl.BlockSpec(m.shape, lambda i: (0, 0))
    return pl.pallas_call(
        _attn_sample_kernel,
        grid=(nb,),
        in_specs=[tok, tok, tok, cache, cache] + [mspec(m) for m in masks],
        out_specs=tok,
        out_shape=jax.ShapeDtypeStruct((nb, nt, WIDTH_B), F32),
        compiler_params=pltpu.CompilerParams(
            dimension_semantics=("arbitrary",), vmem_limit_bytes=VMEM_LIMIT),
        name="attn_sample",
    )(qb, kn, vn, cache_kt, cache_vt, *masks)


FF_CHUNK = 1024


def _mlp_kernel(x_ref, oa_ref, ob_ref, wo_ref, n2_ref, wu_ref, wd_ref, y_ref):
    mix = jnp.concatenate([oa_ref[...].astype(BF16), ob_ref[...].astype(BF16)], axis=1)
    h1 = x_ref[...] + jnp.dot(mix, wo_ref[...], preferred_element_type=F32)
    hn = _rmsnorm(h1, n2_ref[...]).astype(BF16)
    y = h1
    for c0 in range(0, wu_ref.shape[1], FF_CHUNK):
        hid = jnp.dot(hn, wu_ref[:, c0:c0 + FF_CHUNK], preferred_element_type=F32)
        hid = jnp.square(jnp.maximum(hid, 0.0)).astype(BF16)
        y = y + jnp.dot(hid, wd_ref[c0:c0 + FF_CHUNK, :], preferred_element_type=F32)
    y_ref[...] = y


def _mlp(x, oa, ob, p, tm):
    n, d = x.shape
    row = lambda w: pl.BlockSpec((tm, w), lambda i: (i, 0))
    const = lambda a: pl.BlockSpec(a.shape, lambda i: (0, 0))
    return pl.pallas_call(
        _mlp_kernel,
        grid=(n // tm,),
        in_specs=[row(d), row(WIDTH_A), row(WIDTH_B), const(p["w_o"]), const(p["n2"]),
                  const(p["w_up"]), const(p["w_down"])],
        out_specs=row(d),
        out_shape=jax.ShapeDtypeStruct((n, d), F32),
        compiler_params=pltpu.CompilerParams(
            dimension_semantics=("arbitrary",), vmem_limit_bytes=VMEM_LIMIT),
        name="mlp",
    )(x, oa, ob, p["w_o"], p["n2"], p["w_up"], p["w_down"])


def _layer_params(norm1_g, w_in, conv_w, a_log, dt_bias, delta_norm_g, q_norm_g, k_norm_g, w_o,
                  norm2_g, w_up, w_down):
    d = w_in.shape[0]
    n_gate = 2 * N_HEADS_A
    gate0 = CONV_CH + WIDTH_A
    w_re = jnp.concatenate(
        [w_in[:, :gate0], w_in[:, gate0 + n_gate:], w_in[:, gate0:gate0 + n_gate],
         jnp.zeros((d, LANES - n_gate), w_in.dtype)], axis=1).astype(BF16)
    lane_pad = lambda a: jnp.zeros((1, LANES), F32).at[0, N_HEADS_A:n_gate].set(a.astype(F32))
    hid = np.arange(WIDTH_B) // HEAD_DIM_B
    head_mean = jnp.asarray((hid[:, None] == hid[None, :]).astype(np.float32) / HEAD_DIM_B, BF16)
    return {
        "n1": norm1_g.reshape(1, d).astype(F32), "w_in": w_re, "conv_w": conv_w.astype(F32),
        "alog": lane_pad(a_log), "dtb": lane_pad(dt_bias),
        "qng": jnp.tile(q_norm_g.astype(F32), N_HEADS_B).reshape(1, WIDTH_B),
        "kng": jnp.tile(k_norm_g.astype(F32), N_HEADS_B).reshape(1, WIDTH_B),
        "hm": head_mean, "dng": delta_norm_g.reshape(1, HEAD_DIM_A).astype(F32),
        "w_o": w_o.astype(BF16), "n2": norm2_g.reshape(1, d).astype(F32),
        "w_up": w_up.astype(BF16), "w_down": w_down.astype(BF16),
    }


def _prompt_layer(x, p):
    b, t, d = x.shape
    pbuf = min(MAX_WINDOW, t)
    qa, ka, va, z, gt, kb, vb, conv_new, *dil = _proj_prompt(x, p, tm=256, pbuf=pbuf)
    window = lambda a: a.reshape(b, N_HEADS_B, HEAD_DIM_B, pbuf).transpose(0, 3, 1, 2)
    kb, vb = window(kb), window(vb)
    s0 = jnp.zeros((b, N_HEADS_A, HEAD_DIM_A, HEAD_DIM_A), F32)
    oa, s_new = _delta(qa, ka, va, z, gt, s0, p["dng"], chunk=DELTA_CHUNK, nchunk=2, bb=b,
                       out_dtype=BF16)
    ob = _attn_prompt(dil[0:3], dil[3:6], dil[6:9])
    y = _mlp(x.reshape(b * t, d), oa.reshape(b * t, WIDTH_A), ob.reshape(b * t, WIDTH_B), p, tm=256)
    return y.reshape(b, t, d), kb, vb, s_new, conv_new


def _sample_layer(x, state_conv, s0, cache_k, cache_v, p):
    nb, nt, d = x.shape
    qa, ka, va, z, gt, qb, kb, vb, conv_new = _proj_sample(x, state_conv, p, bt=min(nb, 64))
    oa, s_new = _delta(qa, ka, va, z, gt, s0, p["dng"], chunk=nt, nchunk=1, bb=min(nb, 16),
                       out_dtype=F32)
    ob = _attn_sample(qb, kb, vb, cache_k.transpose(0, 2, 3, 1), cache_v.transpose(0, 2, 3, 1))
    heads = lambda a: a.reshape(nb, nt, N_HEADS_B, HEAD_DIM_B)
    kb, vb = heads(kb), heads(vb)
    y = _mlp(x.reshape(nb * nt, d), oa.reshape(nb * nt, WIDTH_A), ob.reshape(nb * nt, WIDTH_B), p,
             tm=min(256, nb * nt))
    return y.reshape(nb, nt, d), kb, vb, s_new, conv_new


def kernel(x_prompt, x_sample, cache_swa_k, cache_swa_v, state_delta, state_conv, norm1_g, w_in,
           conv_w, a_log, dt_bias, delta_norm_g, q_norm_g, k_norm_g, w_o, norm2_g, w_up, w_down):
    depth = w_in.shape[0]
    b, s, _ = x_prompt.shape
    nb, nt, _ = x_sample.shape
    wbuf = cache_swa_k.shape[2]
    assert s % ATTN_TILE == 0 and nt == SUBLANES and wbuf == MAX_WINDOW
    yp, ys = x_prompt, x_sample
    outs = [[] for _ in range(8)]
    for layer in range(depth):
        p = _layer_params(norm1_g[layer], w_in[layer], conv_w[layer], a_log[layer], dt_bias[layer],
                          delta_norm_g[layer], q_norm_g[layer], k_norm_g[layer], w_o[layer],
                          norm2_g[layer], w_up[layer], w_down[layer])
        yp, kp, vp, dp, cp = _prompt_layer(yp, p)
        ys, kn, vn, dn, cn = _sample_layer(ys, state_conv[layer], state_delta[layer],
                                           cache_swa_k[layer], cache_swa_v[layer], p)
        for lst, val in zip(outs, (kp, vp, dp, cp, kn, vn, dn, cn)):
            lst.append(val)
    return (yp, ys) + tuple(jnp.stack(o) for o in outs)
```

```python
import functools

import numpy as np
import jax
import jax.numpy as jnp
from jax import lax
from jax.experimental import pallas as pl
from jax.experimental.pallas import tpu as pltpu

F32 = jnp.float32
BF16 = jnp.bfloat16

N_HEADS_A = 4
HEAD_DIM_A = 128
WIDTH_A = N_HEADS_A * HEAD_DIM_A
N_HEADS_B = 8
HEAD_DIM_B = 64
WIDTH_B = N_HEADS_B * HEAD_DIM_B
CONV_WIDTH = 4
CONV_CH = 3 * WIDTH_A
DELTA_CHUNK = 64
MAX_WINDOW = 2048
NORM_EPS = 1e-6
LANES = 128
SUBLANES = 8
NEG = -1e30
LOG2_E = 1.4426950408889634

C_QKV, C_Z, C_QB, C_KB, C_VB, C_GATE = 0, 1536, 2048, 2560, 3072, 3584
PROJ_PAD = C_GATE + LANES

ATTN_DILATIONS = (1, 4, 16)
ATTN_BAND = 128
ATTN_TILE = 2048
ATTN_UNROLL = 4
VMEM_LIMIT = 56 * 1024 * 1024


def _dot(a, b):
    return jnp.dot(a.astype(BF16), b.astype(BF16), preferred_element_type=F32)


def _dot_nt(a, b):
    return lax.dot_general(a.astype(BF16), b.astype(BF16), (((1,), (1,)), ((), ())),
                           preferred_element_type=F32)


def _dot_tn(a, b):
    return lax.dot_general(a.astype(BF16), b.astype(BF16), (((0,), (0,)), ((), ())),
                           preferred_element_type=F32)


def _split2(x):
    hi = x.astype(BF16)
    lo = (x - hi.astype(F32)).astype(BF16)
    return hi, lo


def _split3(x):
    hi = x.astype(BF16)
    r = x - hi.astype(F32)
    mid = r.astype(BF16)
    lo = (r - mid.astype(F32)).astype(BF16)
    return hi, mid, lo


def _dot_exact_lhs(mask_bf16, x):
    hi, mid, lo = _split3(x)
    d = lambda p: jnp.dot(mask_bf16, p, preferred_element_type=F32)
    return d(hi) + d(mid) + d(lo)


def _sigmoid(x):
    return 0.5 * jnp.tanh(0.5 * x) + 0.5


def _silu(x):
    return x * _sigmoid(x)


def _softplus(x):
    return jnp.maximum(x, 0.0) + jnp.log1p(jnp.exp(-jnp.abs(x)))


def _rmsnorm(x, g):
    return x * lax.rsqrt(jnp.mean(x * x, axis=-1, keepdims=True) + NORM_EPS) * g


def _proj_body(x, ext_ref, shift, pad, refs):
    n1_ref, w_ref, cw_ref, al_ref, dtb_ref, qng_ref, kng_ref, hm_ref = refs
    tm = x.shape[0]
    hn = _rmsnorm(x, n1_ref[...]).astype(BF16)

    u = jnp.dot(hn, w_ref[:, C_QKV:C_QKV + CONV_CH], preferred_element_type=F32)
    ext_ref[pad:pad + tm, :] = u
    cw = cw_ref[...]
    y = u * cw[3:4, :]
    for i in range(CONV_WIDTH - 1):
        off = pad - (CONV_WIDTH - 1 - i) * shift
        y = y + ext_ref[off:off + tm, :] * cw[i:i + 1, :]
    y = _silu(y)
    qa, ka = [], []
    for h in range(N_HEADS_A):
        lo = h * HEAD_DIM_A
        qh = y[:, lo:lo + HEAD_DIM_A]
        qa.append(qh * (lax.rsqrt(
            jnp.sum(qh * qh, axis=-1, keepdims=True) + NORM_EPS) * HEAD_DIM_A ** -0.5))
        kh = y[:, WIDTH_A + lo:WIDTH_A + lo + HEAD_DIM_A]
        ka.append(kh * lax.rsqrt(jnp.sum(kh * kh, axis=-1, keepdims=True) + NORM_EPS))
    qa = jnp.concatenate(qa, axis=1)
    ka = jnp.concatenate(ka, axis=1)
    va = y[:, 2 * WIDTH_A:3 * WIDTH_A]

    z = jnp.dot(hn, w_ref[:, C_Z:C_Z + WIDTH_A], preferred_element_type=F32)

    hm = hm_ref[...]

    def headnorm(v, g):
        ms = jnp.dot((v * v).astype(BF16), hm, preferred_element_type=F32)
        return v * lax.rsqrt(ms + NORM_EPS) * g

    qb = jnp.dot(hn, w_ref[:, C_QB:C_QB + WIDTH_B], preferred_element_type=F32)
    qb = headnorm(qb, qng_ref[...])
    kb = jnp.dot(hn, w_ref[:, C_KB:C_KB + WIDTH_B], preferred_element_type=F32)
    kb = headnorm(kb, kng_ref[...])
    vb = jnp.dot(hn, w_ref[:, C_VB:C_VB + WIDTH_B], preferred_element_type=F32)

    gc = jnp.dot(hn, w_ref[:, C_GATE:C_GATE + LANES], preferred_element_type=F32)
    lane = lax.broadcasted_iota(jnp.int32, gc.shape, 1)
    beta = _sigmoid(gc)
    g = -jnp.exp(al_ref[...]) * _softplus(gc + dtb_ref[...])
    gates = jnp.where(lane < N_HEADS_A, beta, g)
    return qa, ka, va, z, gates, qb, kb, vb


def _proj_prompt_kernel(x_ref, n1_ref, w_ref, cw_ref, al_ref, dtb_ref, qng_ref, kng_ref, hm_ref,
                        qa_ref, ka_ref, va_ref, z_ref, gt_ref, kb_ref, vb_ref, cn_ref,
                        q1_ref, q4_ref, q16_ref, k1_ref, k4_ref, k16_ref, v1_ref, v4_ref, v16_ref,
                        ext_ref, dil_ref, *, first_win):
    t = pl.program_id(1)
    tm = x_ref.shape[1]

    @pl.when(t == 0)
    def _():
        ext_ref[0:SUBLANES, :] = jnp.zeros((SUBLANES, CONV_CH), F32)

    @pl.when(t > 0)
    def _():
        ext_ref[0:SUBLANES, :] = ext_ref[tm:tm + SUBLANES, :]

    qa, ka, va, z, gates, qb, kb, vb = _proj_body(
        x_ref[0], ext_ref, 1, SUBLANES,
        (n1_ref, w_ref, cw_ref, al_ref, dtb_ref, qng_ref, kng_ref, hm_ref))
    for r, v in zip((qa_ref, ka_ref, va_ref, z_ref, gt_ref), (qa, ka, va, z, gates)):
        r[0] = v
    cn_ref[0] = ext_ref[tm + SUBLANES - (CONV_WIDTH - 1):tm + SUBLANES, :]

    @pl.when(t >= first_win)
    def _():
        kb_ref[0] = kb.T
        vb_ref[0] = vb.T

    nchunk = WIDTH_B // LANES
    q_att = qb * (HEAD_DIM_B ** -0.5 * LOG2_E)
    for ai, (val, r1, r4, r16) in enumerate(((q_att, q1_ref, q4_ref, q16_ref),
                                              (kb, k1_ref, k4_ref, k16_ref),
                                              (vb, v1_ref, v4_ref, v16_ref))):
        r1[0] = val.astype(BF16)
        for c in range(nchunk):
            dil_ref[0, ai * nchunk + c] = val[:, c * LANES:(c + 1) * LANES]
        n4, n16 = tm // 4, tm // 16
        for r in range(4):
            parts = [dil_ref[0, ai * nchunk + c, pl.ds(r, n4, stride=4), :] for c in range(nchunk)]
            r4[0, r] = jnp.concatenate(parts, axis=1).astype(BF16)
            for c in range(nchunk):
                dil_ref[1, ai * nchunk + c, r * n4:(r + 1) * n4, :] = parts[c]
        for r in range(4):
            for r2 in range(4):
                r16[0, r + 4 * r2] = jnp.concatenate(
                    [dil_ref[1, ai * nchunk + c, pl.ds(r * n4 + r2, n16, stride=4), :]
                     for c in range(nchunk)], axis=1).astype(BF16)


def _proj_sample_kernel(x_ref, st_ref, n1_ref, w_ref, cw_ref, al_ref, dtb_ref, qng_ref, kng_ref,
                        hm_ref, qa_ref, ka_ref, va_ref, z_ref, gt_ref, qb_ref, kb_ref, vb_ref,
                        cn_ref, ext_ref):
    nt, nb, d = x_ref.shape
    ncv = CONV_WIDTH - 1
    tm = nb * nt
    pad = ncv * nb
    ext_ref[0:pad, :] = st_ref[...].reshape(pad, CONV_CH)
    vals = _proj_body(x_ref[...].reshape(tm, d), ext_ref, nb, pad,
                      (n1_ref, w_ref, cw_ref, al_ref, dtb_ref, qng_ref, kng_ref, hm_ref))
    for r, v in zip((qa_ref, ka_ref, va_ref, z_ref, gt_ref, qb_ref, kb_ref, vb_ref), vals):
        r[...] = v.reshape(r.shape)
    cn_ref[...] = ext_ref[tm:tm + pad, :].reshape(ncv, nb, CONV_CH)


def _proj_params(p):
    full = lambda a: pl.BlockSpec(a.shape, lambda *_: (0,) * a.ndim)
    arrs = (p["n1"], p["w_in"], p["conv_w"], p["alog"], p["dtb"], p["qng"], p["kng"], p["hm"])
    return arrs, [full(a) for a in arrs]


def _proj_prompt(x, p, tm, pbuf):
    b, t, d = x.shape
    assert pbuf % tm == 0 and t % tm == 0
    first_win = (t - pbuf) // tm
    arrs, specs = _proj_params(p)
    row = lambda w: pl.BlockSpec((1, tm, w), lambda i, j: (i, j, 0))
    widths = (WIDTH_A, WIDTH_A, WIDTH_A, WIDTH_A, LANES)
    out_shape = [jax.ShapeDtypeStruct((b, t, w), F32) for w in widths]
    out_specs = [row(w) for w in widths]
    for _ in range(2):
        out_shape.append(jax.ShapeDtypeStruct((b, WIDTH_B, pbuf), F32))
        out_specs.append(pl.BlockSpec((1, WIDTH_B, tm),
                                      lambda i, j: (i, 0, jnp.maximum(j - first_win, 0))))
    out_shape.append(jax.ShapeDtypeStruct((b, CONV_WIDTH - 1, CONV_CH), F32))
    out_specs.append(pl.BlockSpec((1, CONV_WIDTH - 1, CONV_CH), lambda i, j: (i, 0, 0)))
    for _ in range(3):
        out_shape.append(jax.ShapeDtypeStruct((b, t, WIDTH_B), BF16))
        out_specs.append(row(WIDTH_B))
        for dil in ATTN_DILATIONS[1:]:
            out_shape.append(jax.ShapeDtypeStruct((b, dil, t // dil, WIDTH_B), BF16))
            out_specs.append(pl.BlockSpec((1, dil, tm // dil, WIDTH_B), lambda i, j: (i, 0, j, 0)))
    return pl.pallas_call(
        functools.partial(_proj_prompt_kernel, first_win=first_win),
        grid=(b, t // tm),
        in_specs=[row(d)] + specs,
        out_specs=out_specs,
        out_shape=out_shape,
        scratch_shapes=[pltpu.VMEM((tm + SUBLANES, CONV_CH), F32),
                        pltpu.VMEM((2, 3 * WIDTH_B // LANES, tm, LANES), F32)],
        compiler_params=pltpu.CompilerParams(
            dimension_semantics=("arbitrary", "arbitrary"), vmem_limit_bytes=VMEM_LIMIT),
        name="proj_prompt",
    )(x, *arrs)


def _proj_sample(x, state_conv, p, bt):
    nb, nt, d = x.shape
    ncv = CONV_WIDTH - 1
    arrs, specs = _proj_params(p)
    blk = lambda r, w: pl.BlockSpec((r, bt, w), lambda i: (0, i, 0))
    widths = (WIDTH_A, WIDTH_A, WIDTH_A, WIDTH_A, LANES, WIDTH_B, WIDTH_B, WIDTH_B)
    out_shape = [jax.ShapeDtypeStruct((nt, nb, w), F32) for w in widths]
    out_shape.append(jax.ShapeDtypeStruct((ncv, nb, CONV_CH), F32))
    out_specs = [blk(nt, w) for w in widths] + [blk(ncv, CONV_CH)]
    outs = pl.pallas_call(
        _proj_sample_kernel,
        grid=(nb // bt,),
        in_specs=[blk(nt, d), blk(ncv, CONV_CH)] + specs,
        out_specs=out_specs,
        out_shape=out_shape,
        scratch_shapes=[pltpu.VMEM((bt * (nt + ncv), CONV_CH), F32)],
        compiler_params=pltpu.CompilerParams(
            dimension_semantics=("arbitrary",), vmem_limit_bytes=VMEM_LIMIT),
        name="proj_sample",
    )(x.transpose(1, 0, 2), state_conv.transpose(1, 0, 2), *arrs)
    return [o.transpose(1, 0, 2) for o in outs]


def _delta_kernel(q_ref, k_ref, v_ref, z_ref, gt_ref, s0_ref, ng_ref, o_ref, sf_ref, s_scr,
                  *, chunk, nchunk, bb):
    c = chunk
    t = pl.program_id(1)

    @pl.when(t == 0)
    def _():
        s_scr[...] = s0_ref[...]

    ii = lax.broadcasted_iota(jnp.int32, (c, c), 0)
    jj = lax.broadcasted_iota(jnp.int32, (c, c), 1)
    causal = ii >= jj
    strict = ii > jj
    tril_bf = causal.astype(F32).astype(BF16)
    eye = (ii == jj).astype(F32)
    level_masks = []
    s = 1
    while s < c:
        level_masks.append(((ii // (2 * s)) == (jj // (2 * s))) & ((ii & s) != 0) & ((jj & s) == 0))
        s *= 2
    ng = ng_ref[...]

    heads = range(N_HEADS_A)
    items = [(bi, ci, h) for bi in range(bb) for ci in range(nchunk) for h in heads]
    pre = {}
    for bi in range(bb):
        for ci in range(nchunk):
            rows = slice(ci * c, (ci + 1) * c)
            gt = gt_ref[bi, rows, :]
            gcum = _dot_exact_lhs(tril_bf, gt)
            if c < LANES:
                gpad = jnp.concatenate([gcum, jnp.zeros((LANES - c, LANES), F32)], axis=0)
            else:
                gpad = gcum
            gcum_t = gpad.T
            for h in heads:
                lo = h * HEAD_DIM_A
                q = q_ref[bi, rows, lo:lo + HEAD_DIM_A]
                k = k_ref[bi, rows, lo:lo + HEAD_DIM_A]
                v = v_ref[bi, rows, lo:lo + HEAD_DIM_A]
                beta = gt[:, h:h + 1]
                g_col = gcum[:, N_HEADS_A + h:N_HEADS_A + h + 1]
                g_row = gcum_t[N_HEADS_A + h:N_HEADS_A + h + 1, 0:c]
                g_last = gcum[c - 1:c, N_HEADS_A + h:N_HEADS_A + h + 1]
                decay = jnp.where(causal, jnp.exp(jnp.where(causal, g_col - g_row, 0.0)), 0.0)
                exp_g = jnp.exp(g_col)
                kb = k * beta
                pre[bi, ci, h] = dict(
                    q=q, k=k, kb=kb, decay=decay, g_last=g_last,
                    rhs=jnp.concatenate([v * beta, kb * exp_g], axis=1),
                    q_dec=q * exp_g, k_dec=k * jnp.exp(g_last - g_col))
    lmat = {it: jnp.where(strict, _dot_nt(pre[it]["kb"], pre[it]["k"]) * pre[it]["decay"], 0.0)
            for it in items}
    qk = {it: _dot_nt(pre[it]["q"], pre[it]["k"]) * pre[it]["decay"] for it in items}
    tinv = {it: eye - jnp.where(level_masks[0], lmat[it], 0.0) for it in items}
    for msk in level_masks[1:]:
        te = {it: _dot(tinv[it], jnp.where(msk, lmat[it], 0.0)) for it in items}
        tinv = {it: tinv[it] - _dot(te[it], tinv[it]) for it in items}
    uw = {it: _dot(tinv[it], pre[it]["rhs"]) for it in items}

    for bi in range(bb):
        st = {h: s_scr[bi, h] for h in heads}
        for ci in range(nchunk):
            rows = slice(ci * c, (ci + 1) * c)
            ws = {h: _dot(jnp.concatenate([uw[bi, ci, h][:, HEAD_DIM_A:2 * HEAD_DIM_A],
                                           pre[bi, ci, h]["q_dec"]], axis=0), st[h])
                  for h in heads}
            v_new = {h: uw[bi, ci, h][:, 0:HEAD_DIM_A] - ws[h][0:c] for h in heads}
            o = {h: ws[h][c:2 * c] + _dot(qk[bi, ci, h], v_new[h]) for h in heads}
            st = {h: st[h] * jnp.exp(pre[bi, ci, h]["g_last"])
                  + _dot_tn(pre[bi, ci, h]["k_dec"], v_new[h]) for h in heads}
            for h in heads:
                lo = h * HEAD_DIM_A
                zz = z_ref[bi, rows, lo:lo + HEAD_DIM_A]
                o_ref[bi, rows, lo:lo + HEAD_DIM_A] = (
                    _rmsnorm(o[h], ng) * _silu(zz)).astype(o_ref.dtype)
        for h in heads:
            s_scr[bi, h] = st[h]

    @pl.when(t == pl.num_programs(1) - 1)
    def _():
        sf_ref[...] = s_scr[...]


def _delta(qa, ka, va, z, gt, s0, ng, *, chunk, nchunk, bb, out_dtype):
    b, t, _ = qa.shape
    tc = chunk * nchunk
    row = lambda w: pl.BlockSpec((bb, tc, w), lambda i, j: (i, j, 0))
    sspec = pl.BlockSpec((bb, N_HEADS_A, HEAD_DIM_A, HEAD_DIM_A), lambda i, j: (i, 0, 0, 0))
    return pl.pallas_call(
        functools.partial(_delta_kernel, chunk=chunk, nchunk=nchunk, bb=bb),
        grid=(b // bb, t // tc),
        in_specs=[row(WIDTH_A)] * 4 + [row(LANES), sspec,
                                       pl.BlockSpec((1, HEAD_DIM_A), lambda i, j: (0, 0))],
        out_specs=[row(WIDTH_A), sspec],
        out_shape=[jax.ShapeDtypeStruct((b, t, WIDTH_A), out_dtype),
                   jax.ShapeDtypeStruct((b, N_HEADS_A, HEAD_DIM_A, HEAD_DIM_A), F32)],
        scratch_shapes=[pltpu.VMEM((bb, N_HEADS_A, HEAD_DIM_A, HEAD_DIM_A), F32)],
        compiler_params=pltpu.CompilerParams(
            dimension_semantics=("arbitrary", "arbitrary"), vmem_limit_bytes=VMEM_LIMIT),
        name="delta_c%d" % chunk,
    )(qa, ka, va, z, gt, s0, ng)


def _attn_prompt_kernel(q1, q4, q16, k1p, k1c, k4p, k4c, k16p, k16c, v1p, v1c, v4p, v4c, v16p, v16c,
                        o_ref, acc, mrep, *, tq):
    t = pl.program_id(2)
    rr = ATTN_BAND
    ii = lax.broadcasted_iota(jnp.int32, (rr, rr), 0)
    jj = lax.broadcasted_iota(jnp.int32, (rr, rr), 1)
    lower = jj <= ii
    upper = jj >= ii
    lane = lax.broadcasted_iota(jnp.int32, (rr, LANES), 1)
    own = [(lane >= hh * HEAD_DIM_B) & (lane < (hh + 1) * HEAD_DIM_B) for hh in range(2)]
    pen_t = jnp.where(t > 0, jnp.float32(0.0), jnp.float32(NEG))
    one = jnp.ones((rr, LANES), BF16)
    zero = jnp.zeros((rr, LANES), BF16)

    def process(units, first):
        chains = [(j, hh) for j in range(len(units)) for hh in range(2)]
        qv = {ch: jnp.where(own[ch[1]], units[ch[0]]["q"], zero) for ch in chains}
        s_c = {ch: jnp.where(lower, _dot_nt(qv[ch], units[ch[0]]["kc"]), NEG) for ch in chains}
        s_p = {}
        for ch in chains:
            s = _dot_nt(qv[ch], units[ch[0]]["kp"])
            if units[ch[0]]["pen"] is not None:
                s = s + units[ch[0]]["pen"]
            s_p[ch] = jnp.where(upper, s, NEG)
        m = {ch: jnp.max(jnp.maximum(s_c[ch], s_p[ch]), axis=-1, keepdims=True) for ch in chains}
        pv = {ch: (_dot(jnp.exp2(s_c[ch] - m[ch]), jnp.where(own[ch[1]], units[ch[0]]["vc"], one))
                   + _dot(jnp.exp2(s_p[ch] - m[ch]), jnp.where(own[ch[1]], units[ch[0]]["vp"], one)))
              for ch in chains}
        for ch in chains:
            j, hh = ch
            rows = units[j]["rows"]
            m_b = jnp.broadcast_to(m[ch], (rr, LANES))
            if first:
                acc[hh, rows, :] = pv[ch]
                mrep[hh, rows, :] = m_b
            else:
                m_old = mrep[hh, rows, :]
                m_new = jnp.maximum(m_old, m_b)
                acc[hh, rows, :] = (acc[hh, rows, :] * jnp.exp2(m_old - m_new)
                                    + pv[ch] * jnp.exp2(m_b - m_new))
                mrep[hh, rows, :] = m_new

    n_units = tq // rr
    uu = ATTN_UNROLL

    d = ATTN_DILATIONS[2]
    assert tq // d == rr

    def body16(it, carry, d=d):
        units = []
        for j in range(uu):
            r = it * uu + j
            units.append(dict(q=q16[0, r], kc=k16c[0, r], kp=k16p[0, r], vc=v16c[0, r],
                              vp=v16p[0, r], pen=pen_t, rows=pl.ds(r, rr, stride=d)))
        process(units, True)
        return carry

    lax.fori_loop(0, d // uu, body16, 0)

    def body1(it, carry):
        kp, vp, pen = carry
        units = []
        for j in range(uu):
            r0 = pl.multiple_of((it * uu + j) * rr, rr)
            kc, vc = k1c[0, pl.ds(r0, rr), :], v1c[0, pl.ds(r0, rr), :]
            units.append(dict(q=q1[0, pl.ds(r0, rr), :], kc=kc, kp=kp, vc=vc, vp=vp,
                              pen=pen if j == 0 else None, rows=pl.ds(r0, rr)))
            kp, vp = kc, vc
        process(units, False)
        return kp, vp, jnp.float32(0.0)

    lax.fori_loop(0, n_units // uu, body1,
                  (k1p[0, tq - rr:tq, :], v1p[0, tq - rr:tq, :], pen_t))

    d = ATTN_DILATIONS[1]
    ln = tq // d
    nblk = ln // rr

    def body4(r, carry, d=d, ln=ln, nblk=nblk):
        units = []
        kp, vp = k4p[0, r, ln - rr:ln, :], v4p[0, r, ln - rr:ln, :]
        for j in range(nblk):
            kc, vc = k4c[0, r, j * rr:(j + 1) * rr, :], v4c[0, r, j * rr:(j + 1) * rr, :]
            units.append(dict(q=q4[0, r, j * rr:(j + 1) * rr, :], kc=kc, kp=kp, vc=vc, vp=vp,
                              pen=pen_t if j == 0 else None,
                              rows=pl.ds(r + d * rr * j, rr, stride=d)))
            kp, vp = kc, vc
        process(units, False)
        return carry

    lax.fori_loop(0, d, body4, 0)

    lane_t = lax.broadcasted_iota(jnp.int32, (tq, LANES), 1)
    a0 = acc[0]
    a1 = acc[1]
    l0 = a0[:, HEAD_DIM_B:HEAD_DIM_B + 1]
    l1 = a1[:, 0:1]
    o_ref[0] = jnp.where(lane_t < HEAD_DIM_B, a0 / l0, a1 / l1).astype(o_ref.dtype)


def _attn_prompt(q, k, v):
    b, t, _ = q[0].shape
    tq = ATTN_TILE
    nt = t // tq
    prev_j = lambda j: jnp.maximum(j - 1, 0)
    cur, prev = [], []
    for dil in ATTN_DILATIONS:
        if dil == 1:
            cur.append(pl.BlockSpec((1, tq, LANES), lambda i, h, j: (i, j, h)))
            prev.append(pl.BlockSpec((1, tq, LANES), lambda i, h, j: (i, prev_j(j), h)))
        else:
            cur.append(pl.BlockSpec((1, dil, tq // dil, LANES), lambda i, h, j: (i, 0, j, h)))
            prev.append(pl.BlockSpec((1, dil, tq // dil, LANES),
                                     lambda i, h, j: (i, 0, prev_j(j), h)))
    kv_specs = [s for pc in zip(prev, cur) for s in pc]
    kv_args = lambda x: [a for xd in x for a in (xd, xd)]
    return pl.pallas_call(
        functools.partial(_attn_prompt_kernel, tq=tq),
        grid=(b, WIDTH_B // LANES, nt),
        in_specs=cur + kv_specs + kv_specs,
        out_specs=cur[0],
        out_shape=jax.ShapeDtypeStruct((b, t, WIDTH_B), BF16),
        scratch_shapes=[pltpu.VMEM((2, tq, LANES), F32), pltpu.VMEM((2, tq, LANES), F32)],
        compiler_params=pltpu.CompilerParams(
            dimension_semantics=("arbitrary", "arbitrary", "arbitrary"),
            vmem_limit_bytes=VMEM_LIMIT),
        name="attn_prompt",
    )(*q, *kv_args(k), *kv_args(v))


def _multiplicity(delta):
    delta = np.asarray(delta)
    ok = delta >= 0
    m = ((delta <= 128).astype(np.float32)
         + ((delta <= 512) & (delta % 4 == 0)).astype(np.float32)
         + ((delta <= 2048) & (delta % 16 == 0)).astype(np.float32))
    return np.where(ok, m, 0.0).astype(np.float32)


def _sample_masks(wbuf, nt):
    i = np.arange(nt)[:, None]
    return (_multiplicity(wbuf + i - np.arange(wbuf)[None, :]),
            _multiplicity(i - np.arange(nt)[None, :]))


def _attn_sample_kernel(q_ref, kn_ref, vn_ref, kt_ref, vt_ref, mc_ref, mn_ref, o_ref):
    dh = HEAD_DIM_B
    heads = range(N_HEADS_B)
    mc = mc_ref[...]
    mn = mn_ref[...]
    sl = lambda r, h: r[0, :, h * dh:(h + 1) * dh]
    q = {h: (sl(q_ref, h) * (dh ** -0.5)).astype(BF16) for h in heads}
    s_c = {h: jnp.where(mc > 0.0, _dot(q[h], kt_ref[0, h]), NEG) for h in heads}
    s_n = {h: jnp.where(mn > 0.0, _dot_nt(q[h], sl(kn_ref, h)), NEG) for h in heads}
    m = {h: jnp.maximum(jnp.max(s_c[h], axis=-1, keepdims=True),
                        jnp.max(s_n[h], axis=-1, keepdims=True)) for h in heads}
    p_c = {h: (mc * jnp.exp(s_c[h] - m[h])).astype(BF16) for h in heads}
    p_n = {h: (mn * jnp.exp(s_n[h] - m[h])).astype(BF16) for h in heads}
    den = {h: (jnp.sum(p_c[h].astype(F32), axis=-1, keepdims=True)
               + jnp.sum(p_n[h].astype(F32), axis=-1, keepdims=True)) for h in heads}
    out = {h: _dot_nt(p_c[h], vt_ref[0, h]) + _dot(p_n[h], sl(vn_ref, h)) for h in heads}
    o_ref[0] = jnp.concatenate([out[h] / den[h] for h in heads], axis=1)


def _attn_sample(qb, kn, vn, cache_kt, cache_vt):
    nb, nt, _ = qb.shape
    _, nh, dh, wbuf = cache_kt.shape
    masks = [jnp.asarray(m) for m in _sample_masks(wbuf, nt)]
    tok = pl.BlockSpec((1, nt, WIDTH_B), lambda i: (i, 0, 0))
    cache = pl.BlockSpec((1, nh, dh, wbuf), lambda i: (i, 0, 0, 0))
    mspec = lambda m: pl.BlockSpec(m.shape, lambda i: (0, 0))
    return pl.pallas_call(
        _attn_sample_kernel,
        grid=(nb,),
        in_specs=[tok, tok, tok, cache, cache] + [mspec(m) for m in masks],
        out_specs=tok,
        out_shape=jax.ShapeDtypeStruct((nb, nt, WIDTH_B), F32),
        compiler_params=pltpu.CompilerParams(
            dimension_semantics=("arbitrary",), vmem_limit_bytes=VMEM_LIMIT),
        name="attn_sample",
    )(qb, kn, vn, cache_kt, cache_vt, *masks)


FF_CHUNK = 1024


def _mlp_kernel(x_ref, oa_ref, ob_ref, wo_ref, n2_ref, wu_ref, wd_ref, y_ref):
    mix = jnp.concatenate([oa_ref[...].astype(BF16), ob_ref[...].astype(BF16)], axis=1)
    h1 = x_ref[...] + jnp.dot(mix, wo_ref[...], preferred_element_type=F32)
    hn = _rmsnorm(h1, n2_ref[...]).astype(BF16)
    chunks = range(0, wu_ref.shape[1], FF_CHUNK)
    hid = [jnp.dot(hn, wu_ref[:, c0:c0 + FF_CHUNK], preferred_element_type=F32) for c0 in chunks]
    act = [jnp.square(jnp.maximum(h, 0.0)).astype(BF16) for h in hid]
    y = h1
    for c0, a in zip(chunks, act):
        y = y + jnp.dot(a, wd_ref[c0:c0 + FF_CHUNK, :], preferred_element_type=F32)
    y_ref[...] = y


def _mlp(x, oa, ob, p, tm):
    n, d = x.shape
    row = lambda w: pl.BlockSpec((tm, w), lambda i: (i, 0))
    const = lambda a: pl.BlockSpec(a.shape, lambda i: (0, 0))
    return pl.pallas_call(
        _mlp_kernel,
        grid=(n // tm,),
        in_specs=[row(d), row(WIDTH_A), row(WIDTH_B), const(p["w_o"]), const(p["n2"]),
                  const(p["w_up"]), const(p["w_down"])],
        out_specs=row(d),
        out_shape=jax.ShapeDtypeStruct((n, d), F32),
        compiler_params=pltpu.CompilerParams(
            dimension_semantics=("arbitrary",), vmem_limit_bytes=VMEM_LIMIT),
        name="mlp",
    )(x, oa, ob, p["w_o"], p["n2"], p["w_up"], p["w_down"])


def _mlp_attn_kernel(x_ref, oa_ref, ob_ref, wo_ref, n2_ref, wu_ref, wd_ref,
                     q_ref, kn_ref, vn_ref, kt_ref, vt_ref, mc_ref, mn_ref, y_ref, o_ref):
    _attn_sample_kernel(q_ref, kn_ref, vn_ref, kt_ref, vt_ref, mc_ref, mn_ref, o_ref)
    _mlp_kernel(x_ref, oa_ref, ob_ref, wo_ref, n2_ref, wu_ref, wd_ref, y_ref)


def _mlp_attn(x, oa, ob, p, qb, kn, vn, cache_kt, cache_vt):
    n, d = x.shape
    nb, nt, _ = qb.shape
    _, nh, dh, wbuf = cache_kt.shape
    tm = n // nb
    assert n % nb == 0 and tm % SUBLANES == 0
    masks = [jnp.asarray(m) for m in _sample_masks(wbuf, nt)]
    row = lambda w: pl.BlockSpec((tm, w), lambda i: (i, 0))
    const = lambda a: pl.BlockSpec(a.shape, lambda i: (0, 0), pipeline_mode=pl.Buffered(1))
    tok = pl.BlockSpec((1, nt, WIDTH_B), lambda i: (i, 0, 0))
    cache = pl.BlockSpec((1, nh, dh, wbuf), lambda i: (i, 0, 0, 0))
    return pl.pallas_call(
        _mlp_attn_kernel,
        grid=(nb,),
        in_specs=[row(d), row(WIDTH_A), row(WIDTH_B), const(p["w_o"]), const(p["n2"]),
                  const(p["w_up"]), const(p["w_down"]), tok, tok, tok, cache, cache]
        + [const(m) for m in masks],
        out_specs=[row(d), tok],
        out_shape=[jax.ShapeDtypeStruct((n, d), F32),
                   jax.ShapeDtypeStruct((nb, nt, WIDTH_B), F32)],
        compiler_params=pltpu.CompilerParams(
            dimension_semantics=("arbitrary",), vmem_limit_bytes=VMEM_LIMIT),
        name="mlp_attn",
    )(x, oa, ob, p["w_o"], p["n2"], p["w_up"], p["w_down"], qb, kn, vn, cache_kt, cache_vt, *masks)


def _layer_params(norm1_g, w_in, conv_w, a_log, dt_bias, delta_norm_g, q_norm_g, k_norm_g, w_o,
                  norm2_g, w_up, w_down):
    d = w_in.shape[0]
    n_gate = 2 * N_HEADS_A
    gate0 = CONV_CH + WIDTH_A
    w_re = jnp.concatenate(
        [w_in[:, :gate0], w_in[:, gate0 + n_gate:], w_in[:, gate0:gate0 + n_gate],
         jnp.zeros((d, LANES - n_gate), w_in.dtype)], axis=1).astype(BF16)
    lane_pad = lambda a: jnp.zeros((1, LANES), F32).at[0, N_HEADS_A:n_gate].set(a.astype(F32))
    hid = np.arange(WIDTH_B) // HEAD_DIM_B
    head_mean = jnp.asarray((hid[:, None] == hid[None, :]).astype(np.float32) / HEAD_DIM_B, BF16)
    return {
        "n1": norm1_g.reshape(1, d).astype(F32), "w_in": w_re, "conv_w": conv_w.astype(F32),
        "alog": lane_pad(a_log), "dtb": lane_pad(dt_bias),
        "qng": jnp.tile(q_norm_g.astype(F32), N_HEADS_B).reshape(1, WIDTH_B),
        "kng": jnp.tile(k_norm_g.astype(F32), N_HEADS_B).reshape(1, WIDTH_B),
        "hm": head_mean, "dng": delta_norm_g.reshape(1, HEAD_DIM_A).astype(F32),
        "w_o": w_o.astype(BF16), "n2": norm2_g.reshape(1, d).astype(F32),
        "w_up": w_up.astype(BF16), "w_down": w_down.astype(BF16),
    }


MLP_TILE = 256
FUSED_MLP_MAX_TILE = 256


def _layer(xp, xs, state_conv, s0_s, cache_k, cache_v, p):
    b, t, d = xp.shape
    nb, nt, _ = xs.shape
    n = b * t
    pbuf = min(MAX_WINDOW, t)

    qa, ka, va, z, gt, kp, vp, conv_p, *dil = _proj_prompt(xp, p, tm=256, pbuf=pbuf)
    window = lambda a: a.reshape(b, N_HEADS_B, HEAD_DIM_B, pbuf).transpose(0, 3, 1, 2)
    kp, vp = window(kp), window(vp)
    s0_p = jnp.zeros((b, N_HEADS_A, HEAD_DIM_A, HEAD_DIM_A), F32)
    oa_p, s_p = _delta(qa, ka, va, z, gt, s0_p, p["dng"], chunk=DELTA_CHUNK, nchunk=2, bb=b,
                       out_dtype=BF16)
    ob_p = _attn_prompt(dil[0:3], dil[3:6], dil[6:9])

    qa, ka, va, z, gt, qb, kn, vn, conv_s = _proj_sample(xs, state_conv, p, bt=min(nb, 64))
    oa_s, s_s = _delta(qa, ka, va, z, gt, s0_s, p["dng"], chunk=nt, nchunk=1, bb=min(nb, 16),
                       out_dtype=F32)
    ckt, cvt = cache_k.transpose(0, 2, 3, 1), cache_v.transpose(0, 2, 3, 1)

    mlp_in = (xp.reshape(n, d), oa_p.reshape(n, WIDTH_A), ob_p.reshape(n, WIDTH_B), p)
    tile = n // nb
    if n % nb == 0 and tile % SUBLANES == 0 and tile <= FUSED_MLP_MAX_TILE:
        yp, ob_s = _mlp_attn(*mlp_in, qb, kn, vn, ckt, cvt)
    else:
        yp = _mlp(*mlp_in, tm=MLP_TILE)
        ob_s = _attn_sample(qb, kn, vn, ckt, cvt)
    ys = _mlp(xs.reshape(nb * nt, d), oa_s.reshape(nb * nt, WIDTH_A),
              ob_s.reshape(nb * nt, WIDTH_B), p, tm=min(MLP_TILE, nb * nt))
    heads = lambda a: a.reshape(nb, nt, N_HEADS_B, HEAD_DIM_B)
    return ((yp.reshape(b, t, d), kp, vp, s_p, conv_p),
            (ys.reshape(nb, nt, d), heads(kn), heads(vn), s_s, conv_s))


def kernel(x_prompt, x_sample, cache_swa_k, cache_swa_v, state_delta, state_conv, norm1_g, w_in,
           conv_w, a_log, dt_bias, delta_norm_g, q_norm_g, k_norm_g, w_o, norm2_g, w_up, w_down):
    depth = w_in.shape[0]
    b, s, _ = x_prompt.shape
    nb, nt, _ = x_sample.shape
    wbuf = cache_swa_k.shape[2]
    assert s % ATTN_TILE == 0 and nt == SUBLANES and wbuf == MAX_WINDOW
    yp, ys = x_prompt, x_sample
    outs = [[] for _ in range(8)]
    for layer in range(depth):
        p = _layer_params(norm1_g[layer], w_in[layer], conv_w[layer], a_log[layer], dt_bias[layer],
                          delta_norm_g[layer], q_norm_g[layer], k_norm_g[layer], w_o[layer],
                          norm2_g[layer], w_up[layer], w_down[layer])
        (yp, kp, vp, dp, cp), (ys, kn, vn, dn, cn) = _layer(
            yp, ys, state_conv[layer], state_delta[layer], cache_swa_k[layer], cache_swa_v[layer], p)
        for lst, val in zip(outs, (kp, vp, dp, cp, kn, vn, dn, cn)):
            lst.append(val)
    return (yp, ys) + tuple(jnp.stack(o) for o in outs)
```

```python
import functools

import numpy as np
import jax
import jax.numpy as jnp
from jax import lax
from jax.experimental import pallas as pl
from jax.experimental.pallas import tpu as pltpu

F32 = jnp.float32
BF16 = jnp.bfloat16

N_HEADS_A = 4
HEAD_DIM_A = 128
WIDTH_A = N_HEADS_A * HEAD_DIM_A
N_HEADS_B = 8
HEAD_DIM_B = 64
WIDTH_B = N_HEADS_B * HEAD_DIM_B
CONV_WIDTH = 4
CONV_CH = 3 * WIDTH_A
DELTA_CHUNK = 64
MAX_WINDOW = 2048
NORM_EPS = 1e-6
LANES = 128
SUBLANES = 8
NEG = -1e30
LOG2_E = 1.4426950408889634

C_QKV, C_Z, C_QB, C_KB, C_VB, C_GATE = 0, 1536, 2048, 2560, 3072, 3584
PROJ_PAD = C_GATE + LANES

ATTN_DILATIONS = (1, 4, 16)
ATTN_BAND = 128
ATTN_TILE = 2048
ATTN_UNROLL = 2
VMEM_LIMIT = 56 * 1024 * 1024


def _dot(a, b):
    return jnp.dot(a.astype(BF16), b.astype(BF16), preferred_element_type=F32)


def _dot_nt(a, b):
    return lax.dot_general(a.astype(BF16), b.astype(BF16), (((1,), (1,)), ((), ())),
                           preferred_element_type=F32)


def _dot_tn(a, b):
    return lax.dot_general(a.astype(BF16), b.astype(BF16), (((0,), (0,)), ((), ())),
                           preferred_element_type=F32)


def _split2(x):
    hi = x.astype(BF16)
    lo = (x - hi.astype(F32)).astype(BF16)
    return hi, lo


def _split3(x):
    hi = x.astype(BF16)
    r = x - hi.astype(F32)
    mid = r.astype(BF16)
    lo = (r - mid.astype(F32)).astype(BF16)
    return hi, mid, lo


def _dot_exact_lhs(mask_bf16, x):
    hi, mid, lo = _split3(x)
    d = lambda p: jnp.dot(mask_bf16, p, preferred_element_type=F32)
    return d(hi) + d(mid) + d(lo)


def _sigmoid(x):
    return 0.5 * jnp.tanh(0.5 * x) + 0.5


def _silu(x):
    return x * _sigmoid(x)


def _softplus(x):
    return jnp.maximum(x, 0.0) + jnp.log1p(jnp.exp(-jnp.abs(x)))


def _rmsnorm(x, g):
    return x * lax.rsqrt(jnp.mean(x * x, axis=-1, keepdims=True) + NORM_EPS) * g


def _proj_body(x, ext_ref, shift, pad, refs):
    n1_ref, w_ref, cw_ref, al_ref, dtb_ref, qng_ref, kng_ref, hm_ref = refs
    tm = x.shape[0]
    hn = _rmsnorm(x, n1_ref[...]).astype(BF16)

    u = jnp.dot(hn, w_ref[:, C_QKV:C_QKV + CONV_CH], preferred_element_type=F32)
    ext_ref[pad:pad + tm, :] = u
    cw = cw_ref[...]
    y = u * cw[3:4, :]
    for i in range(CONV_WIDTH - 1):
        off = pad - (CONV_WIDTH - 1 - i) * shift
        y = y + ext_ref[off:off + tm, :] * cw[i:i + 1, :]
    y = _silu(y)
    qa, ka = [], []
    for h in range(N_HEADS_A):
        lo = h * HEAD_DIM_A
        qh = y[:, lo:lo + HEAD_DIM_A]
        qa.append(qh * (lax.rsqrt(
            jnp.sum(qh * qh, axis=-1, keepdims=True) + NORM_EPS) * HEAD_DIM_A ** -0.5))
        kh = y[:, WIDTH_A + lo:WIDTH_A + lo + HEAD_DIM_A]
        ka.append(kh * lax.rsqrt(jnp.sum(kh * kh, axis=-1, keepdims=True) + NORM_EPS))
    qa = jnp.concatenate(qa, axis=1)
    ka = jnp.concatenate(ka, axis=1)
    va = y[:, 2 * WIDTH_A:3 * WIDTH_A]

    z = jnp.dot(hn, w_ref[:, C_Z:C_Z + WIDTH_A], preferred_element_type=F32)

    hm = hm_ref[...]

    def headnorm(v, g):
        ms = jnp.dot((v * v).astype(BF16), hm, preferred_element_type=F32)
        return v * lax.rsqrt(ms + NORM_EPS) * g

    qb = jnp.dot(hn, w_ref[:, C_QB:C_QB + WIDTH_B], preferred_element_type=F32)
    qb = headnorm(qb, qng_ref[...])
    kb = jnp.dot(hn, w_ref[:, C_KB:C_KB + WIDTH_B], preferred_element_type=F32)
    kb = headnorm(kb, kng_ref[...])
    vb = jnp.dot(hn, w_ref[:, C_VB:C_VB + WIDTH_B], preferred_element_type=F32)

    gc = jnp.dot(hn, w_ref[:, C_GATE:C_GATE + LANES], preferred_element_type=F32)
    lane = lax.broadcasted_iota(jnp.int32, gc.shape, 1)
    beta = _sigmoid(gc)
    g = -jnp.exp(al_ref[...]) * _softplus(gc + dtb_ref[...])
    gates = jnp.where(lane < N_HEADS_A, beta, g)
    return qa, ka, va, z, gates, qb, kb, vb


def _proj_prompt_kernel(x_ref, n1_ref, w_ref, cw_ref, al_ref, dtb_ref, qng_ref, kng_ref, hm_ref,
                        qa_ref, ka_ref, va_ref, z_ref, gt_ref, kb_ref, vb_ref, cn_ref,
                        q1_ref, q4_ref, q16_ref, k1_ref, k4_ref, k16_ref, v1_ref, v4_ref, v16_ref,
                        ext_ref, dil_ref, *, first_win):
    t = pl.program_id(1)
    tm = x_ref.shape[1]

    @pl.when(t == 0)
    def _():
        ext_ref[0:SUBLANES, :] = jnp.zeros((SUBLANES, CONV_CH), F32)

    @pl.when(t > 0)
    def _():
        ext_ref[0:SUBLANES, :] = ext_ref[tm:tm + SUBLANES, :]

    qa, ka, va, z, gates, qb, kb, vb = _proj_body(
        x_ref[0], ext_ref, 1, SUBLANES,
        (n1_ref, w_ref, cw_ref, al_ref, dtb_ref, qng_ref, kng_ref, hm_ref))
    for r, v in zip((qa_ref, ka_ref, va_ref, z_ref, gt_ref), (qa, ka, va, z, gates)):
        r[0] = v
    cn_ref[0] = ext_ref[tm + SUBLANES - (CONV_WIDTH - 1):tm + SUBLANES, :]

    @pl.when(t >= first_win)
    def _():
        kb_ref[0] = kb.T
        vb_ref[0] = vb.T

    nchunk = WIDTH_B // LANES
    q_att = qb * (HEAD_DIM_B ** -0.5 * LOG2_E)
    for ai, (val, r1, r4, r16) in enumerate(((q_att, q1_ref, q4_ref, q16_ref),
                                              (kb, k1_ref, k4_ref, k16_ref),
                                              (vb, v1_ref, v4_ref, v16_ref))):
        r1[0] = val.astype(BF16)
        for c in range(nchunk):
            dil_ref[0, ai * nchunk + c] = val[:, c * LANES:(c + 1) * LANES]
        n4, n16 = tm // 4, tm // 16
        for r in range(4):
            parts = [dil_ref[0, ai * nchunk + c, pl.ds(r, n4, stride=4), :] for c in range(nchunk)]
            r4[0, r] = jnp.concatenate(parts, axis=1).astype(BF16)
            for c in range(nchunk):
                dil_ref[1, ai * nchunk + c, r * n4:(r + 1) * n4, :] = parts[c]
        for r in range(4):
            for r2 in range(4):
                r16[0, r + 4 * r2] = jnp.concatenate(
                    [dil_ref[1, ai * nchunk + c, pl.ds(r * n4 + r2, n16, stride=4), :]
                     for c in range(nchunk)], axis=1).astype(BF16)


def _proj_sample_kernel(x_ref, st_ref, n1_ref, w_ref, cw_ref, al_ref, dtb_ref, qng_ref, kng_ref,
                        hm_ref, qa_ref, ka_ref, va_ref, z_ref, gt_ref, qb_ref, kb_ref, vb_ref,
                        cn_ref, ext_ref):
    nt, nb, d = x_ref.shape
    ncv = CONV_WIDTH - 1
    tm = nb * nt
    pad = ncv * nb
    ext_ref[0:pad, :] = st_ref[...].reshape(pad, CONV_CH)
    vals = _proj_body(x_ref[...].reshape(tm, d), ext_ref, nb, pad,
                      (n1_ref, w_ref, cw_ref, al_ref, dtb_ref, qng_ref, kng_ref, hm_ref))
    for r, v in zip((qa_ref, ka_ref, va_ref, z_ref, gt_ref, qb_ref, kb_ref, vb_ref), vals):
        r[...] = v.reshape(r.shape)
    cn_ref[...] = ext_ref[tm:tm + pad, :].reshape(ncv, nb, CONV_CH)


def _proj_params(p):
    full = lambda a: pl.BlockSpec(a.shape, lambda *_: (0,) * a.ndim)
    arrs = (p["n1"], p["w_in"], p["conv_w"], p["alog"], p["dtb"], p["qng"], p["kng"], p["hm"])
    return arrs, [full(a) for a in arrs]


def _proj_prompt(x, p, tm, pbuf):
    b, t, d = x.shape
    assert pbuf % tm == 0 and t % tm == 0
    first_win = (t - pbuf) // tm
    arrs, specs = _proj_params(p)
    row = lambda w: pl.BlockSpec((1, tm, w), lambda i, j: (i, j, 0))
    widths = (WIDTH_A, WIDTH_A, WIDTH_A, WIDTH_A, LANES)
    out_shape = [jax.ShapeDtypeStruct((b, t, w), F32) for w in widths]
    out_specs = [row(w) for w in widths]
    for _ in range(2):
        out_shape.append(jax.ShapeDtypeStruct((b, WIDTH_B, pbuf), F32))
        out_specs.append(pl.BlockSpec((1, WIDTH_B, tm),
                                      lambda i, j: (i, 0, jnp.maximum(j - first_win, 0))))
    out_shape.append(jax.ShapeDtypeStruct((b, CONV_WIDTH - 1, CONV_CH), F32))
    out_specs.append(pl.BlockSpec((1, CONV_WIDTH - 1, CONV_CH), lambda i, j: (i, 0, 0)))
    for _ in range(3):
        out_shape.append(jax.ShapeDtypeStruct((b, t, WIDTH_B), BF16))
        out_specs.append(row(WIDTH_B))
        for dil in ATTN_DILATIONS[1:]:
            out_shape.append(jax.ShapeDtypeStruct((b, dil, t // dil, WIDTH_B), BF16))
            out_specs.append(pl.BlockSpec((1, dil, tm // dil, WIDTH_B), lambda i, j: (i, 0, j, 0)))
    return pl.pallas_call(
        functools.partial(_proj_prompt_kernel, first_win=first_win),
        grid=(b, t // tm),
        in_specs=[row(d)] + specs,
        out_specs=out_specs,
        out_shape=out_shape,
        scratch_shapes=[pltpu.VMEM((tm + SUBLANES, CONV_CH), F32),
                        pltpu.VMEM((2, 3 * WIDTH_B // LANES, tm, LANES), F32)],
        compiler_params=pltpu.CompilerParams(
            dimension_semantics=("arbitrary", "arbitrary"), vmem_limit_bytes=VMEM_LIMIT),
        name="proj_prompt",
    )(x, *arrs)


def _proj_sample(x, state_conv, p, bt):
    nb, nt, d = x.shape
    ncv = CONV_WIDTH - 1
    arrs, specs = _proj_params(p)
    blk = lambda r, w: pl.BlockSpec((r, bt, w), lambda i: (0, i, 0))
    widths = (WIDTH_A, WIDTH_A, WIDTH_A, WIDTH_A, LANES, WIDTH_B, WIDTH_B, WIDTH_B)
    out_shape = [jax.ShapeDtypeStruct((nt, nb, w), F32) for w in widths]
    out_shape.append(jax.ShapeDtypeStruct((ncv, nb, CONV_CH), F32))
    out_specs = [blk(nt, w) for w in widths] + [blk(ncv, CONV_CH)]
    outs = pl.pallas_call(
        _proj_sample_kernel,
        grid=(nb // bt,),
        in_specs=[blk(nt, d), blk(ncv, CONV_CH)] + specs,
        out_specs=out_specs,
        out_shape=out_shape,
        scratch_shapes=[pltpu.VMEM((bt * (nt + ncv), CONV_CH), F32)],
        compiler_params=pltpu.CompilerParams(
            dimension_semantics=("arbitrary",), vmem_limit_bytes=VMEM_LIMIT),
        name="proj_sample",
    )(x.transpose(1, 0, 2), state_conv.transpose(1, 0, 2), *arrs)
    return [o.transpose(1, 0, 2) for o in outs]


def _delta_kernel(q_ref, k_ref, v_ref, z_ref, gt_ref, s0_ref, ng_ref, o_ref, sf_ref, s_scr,
                  *, chunk, nchunk, bb):
    c = chunk
    t = pl.program_id(1)

    @pl.when(t == 0)
    def _():
        s_scr[...] = s0_ref[...]

    ii = lax.broadcasted_iota(jnp.int32, (c, c), 0)
    jj = lax.broadcasted_iota(jnp.int32, (c, c), 1)
    causal = ii >= jj
    strict = ii > jj
    tril_bf = causal.astype(F32).astype(BF16)
    eye = (ii == jj).astype(F32)
    level_masks = []
    s = 1
    while s < c:
        level_masks.append(((ii // (2 * s)) == (jj // (2 * s))) & ((ii & s) != 0) & ((jj & s) == 0))
        s *= 2
    ng = ng_ref[...]

    heads = range(N_HEADS_A)
    items = [(bi, ci, h) for bi in range(bb) for ci in range(nchunk) for h in heads]
    pre = {}
    for bi in range(bb):
        for ci in range(nchunk):
            rows = slice(ci * c, (ci + 1) * c)
            gt = gt_ref[bi, rows, :]
            gcum = _dot_exact_lhs(tril_bf, gt)
            if c < LANES:
                gpad = jnp.concatenate([gcum, jnp.zeros((LANES - c, LANES), F32)], axis=0)
            else:
                gpad = gcum
            gcum_t = gpad.T
            for h in heads:
                lo = h * HEAD_DIM_A
                q = q_ref[bi, rows, lo:lo + HEAD_DIM_A]
                k = k_ref[bi, rows, lo:lo + HEAD_DIM_A]
                v = v_ref[bi, rows, lo:lo + HEAD_DIM_A]
                beta = gt[:, h:h + 1]
                g_col = gcum[:, N_HEADS_A + h:N_HEADS_A + h + 1]
                g_row = gcum_t[N_HEADS_A + h:N_HEADS_A + h + 1, 0:c]
                g_last = gcum[c - 1:c, N_HEADS_A + h:N_HEADS_A + h + 1]
                decay = jnp.where(causal, jnp.exp(jnp.where(causal, g_col - g_row, 0.0)), 0.0)
                exp_g = jnp.exp(g_col)
                kb = k * beta
                pre[bi, ci, h] = dict(
                    q=q, k=k, kb=kb, decay=decay, g_last=g_last,
                    rhs=jnp.concatenate([v * beta, kb * exp_g], axis=1),
                    q_dec=q * exp_g, k_dec=k * jnp.exp(g_last - g_col))
    lmat = {it: jnp.where(strict, _dot_nt(pre[it]["kb"], pre[it]["k"]) * pre[it]["decay"], 0.0)
            for it in items}
    qk = {it: _dot_nt(pre[it]["q"], pre[it]["k"]) * pre[it]["decay"] for it in items}
    tinv = {it: eye - jnp.where(level_masks[0], lmat[it], 0.0) for it in items}
    for msk in level_masks[1:]:
        te = {it: _dot(tinv[it], jnp.where(msk, lmat[it], 0.0)) for it in items}
        tinv = {it: tinv[it] - _dot(te[it], tinv[it]) for it in items}
    uw = {it: _dot(tinv[it], pre[it]["rhs"]) for it in items}

    for bi in range(bb):
        st = {h: s_scr[bi, h] for h in heads}
        for ci in range(nchunk):
            rows = slice(ci * c, (ci + 1) * c)
            ws = {h: _dot(jnp.concatenate([uw[bi, ci, h][:, HEAD_DIM_A:2 * HEAD_DIM_A],
                                           pre[bi, ci, h]["q_dec"]], axis=0), st[h])
                  for h in heads}
            v_new = {h: uw[bi, ci, h][:, 0:HEAD_DIM_A] - ws[h][0:c] for h in heads}
            o = {h: ws[h][c:2 * c] + _dot(qk[bi, ci, h], v_new[h]) for h in heads}
            st = {h: st[h] * jnp.exp(pre[bi, ci, h]["g_last"])
                  + _dot_tn(pre[bi, ci, h]["k_dec"], v_new[h]) for h in heads}
            for h in heads:
                lo = h * HEAD_DIM_A
                zz = z_ref[bi, rows, lo:lo + HEAD_DIM_A]
                o_ref[bi, rows, lo:lo + HEAD_DIM_A] = (
                    _rmsnorm(o[h], ng) * _silu(zz)).astype(o_ref.dtype)
        for h in heads:
            s_scr[bi, h] = st[h]

    @pl.when(t == pl.num_programs(1) - 1)
    def _():
        sf_ref[...] = s_scr[...]


def _delta(qa, ka, va, z, gt, s0, ng, *, chunk, nchunk, bb, out_dtype):
    b, t, _ = qa.shape
    tc = chunk * nchunk
    row = lambda w: pl.BlockSpec((bb, tc, w), lambda i, j: (i, j, 0))
    sspec = pl.BlockSpec((bb, N_HEADS_A, HEAD_DIM_A, HEAD_DIM_A), lambda i, j: (i, 0, 0, 0))
    return pl.pallas_call(
        functools.partial(_delta_kernel, chunk=chunk, nchunk=nchunk, bb=bb),
        grid=(b // bb, t // tc),
        in_specs=[row(WIDTH_A)] * 4 + [row(LANES), sspec,
                                       pl.BlockSpec((1, HEAD_DIM_A), lambda i, j: (0, 0))],
        out_specs=[row(WIDTH_A), sspec],
        out_shape=[jax.ShapeDtypeStruct((b, t, WIDTH_A), out_dtype),
                   jax.ShapeDtypeStruct((b, N_HEADS_A, HEAD_DIM_A, HEAD_DIM_A), F32)],
        scratch_shapes=[pltpu.VMEM((bb, N_HEADS_A, HEAD_DIM_A, HEAD_DIM_A), F32)],
        compiler_params=pltpu.CompilerParams(
            dimension_semantics=("arbitrary", "arbitrary"), vmem_limit_bytes=VMEM_LIMIT),
        name="delta_c%d" % chunk,
    )(qa, ka, va, z, gt, s0, ng)


def _attn_prompt_kernel(q1, q4, q16, k1p, k1c, k4p, k4c, k16p, k16c, v1p, v1c, v4p, v4c, v16p, v16c,
                        o_ref, acc, mrep, *, tq):
    t = pl.program_id(2)
    rr = ATTN_BAND
    ii = lax.broadcasted_iota(jnp.int32, (rr, rr), 0)
    jj = lax.broadcasted_iota(jnp.int32, (rr, rr), 1)
    lower = jj <= ii
    upper = jj >= ii
    lane = lax.broadcasted_iota(jnp.int32, (rr, LANES), 1)
    own = [(lane >= hh * HEAD_DIM_B) & (lane < (hh + 1) * HEAD_DIM_B) for hh in range(2)]
    pen_t = jnp.where(t > 0, jnp.float32(0.0), jnp.float32(NEG))
    one = jnp.ones((rr, LANES), BF16)
    zero = jnp.zeros((rr, LANES), BF16)

    ld = lambda ref_idx: ref_idx[0][ref_idx[1]]

    def weights(units):
        chains = [(j, hh) for j in range(len(units)) for hh in range(2)]
        qv = {ch: jnp.where(own[ch[1]], ld(units[ch[0]]["q"]), zero) for ch in chains}
        kc = [ld(u["kc"]) for u in units]
        kp = [ld(u["kp"]) for u in units]
        raw = {ch: (_dot_nt(qv[ch], kc[ch[0]]), _dot_nt(qv[ch], kp[ch[0]])) for ch in chains}
        out = {}
        for ch in chains:
            s_c = jnp.where(lower, raw[ch][0], NEG)
            s_p = raw[ch][1]
            if units[ch[0]]["pen"] is not None:
                s_p = s_p + units[ch[0]]["pen"]
            s_p = jnp.where(upper, s_p, NEG)
            m = jnp.max(jnp.maximum(s_c, s_p), axis=-1, keepdims=True)
            out[ch] = (jnp.exp2(s_c - m).astype(BF16), jnp.exp2(s_p - m).astype(BF16), m)
        return out

    def accumulate(units, wts, first):
        vc = [ld(u["vc"]) for u in units]
        vp = [ld(u["vp"]) for u in units]
        pv = {ch: (jnp.dot(p_c, jnp.where(own[ch[1]], vc[ch[0]], one), preferred_element_type=F32)
                   + jnp.dot(p_p, jnp.where(own[ch[1]], vp[ch[0]], one),
                             preferred_element_type=F32))
              for ch, (p_c, p_p, _) in wts.items()}
        for ch, (_, _, m) in wts.items():
            j, hh = ch
            rows = units[j]["rows"]
            m_b = jnp.broadcast_to(m, (rr, LANES))
            if first:
                acc[hh, rows, :] = pv[ch]
                mrep[hh, rows, :] = m_b
            else:
                m_old = mrep[hh, rows, :]
                m_new = jnp.maximum(m_old, m_b)
                acc[hh, rows, :] = (acc[hh, rows, :] * jnp.exp2(m_old - m_new)
                                    + pv[ch] * jnp.exp2(m_b - m_new))
                mrep[hh, rows, :] = m_new

    uu = ATTN_UNROLL
    blk = lambda j: slice(j * rr, (j + 1) * rr)

    d16, d4 = ATTN_DILATIONS[2], ATTN_DILATIONS[1]
    assert tq // d16 == rr and (tq // (d4 * rr)) % uu == 0 and d16 % uu == 0
    trips = []
    full = slice(None)
    for it in range(d16 // uu):
        trips.append((True, [dict(
            q=(q16, (0, r)), kc=(k16c, (0, r)), kp=(k16p, (0, r)), vc=(v16c, (0, r)),
            vp=(v16p, (0, r)), pen=pen_t, rows=pl.ds(r, rr, stride=d16))
            for r in range(it * uu, (it + 1) * uu)]))
    for it in range(tq // (rr * uu)):
        units = []
        for j in range(it * uu, (it + 1) * uu):
            cur = (0, blk(j), full)
            prv = (0, blk(j - 1), full) if j > 0 else (0, slice(tq - rr, tq), full)
            units.append(dict(q=(q1, cur), kc=(k1c, cur), vc=(v1c, cur),
                              kp=(k1c if j > 0 else k1p, prv), vp=(v1c if j > 0 else v1p, prv),
                              pen=None if j > 0 else pen_t, rows=pl.ds(j * rr, rr)))
        trips.append((False, units))
    nb4 = tq // d4 // rr
    for r in range(d4):
        for jt in range(nb4 // uu):
            units = []
            for j in range(jt * uu, (jt + 1) * uu):
                cur = (0, r, blk(j), full)
                prv = (0, r, blk(j - 1), full) if j > 0 else (0, r, blk(nb4 - 1), full)
                units.append(dict(q=(q4, cur), kc=(k4c, cur), vc=(v4c, cur),
                                  kp=(k4c if j > 0 else k4p, prv),
                                  vp=(v4c if j > 0 else v4p, prv),
                                  pen=None if j > 0 else pen_t,
                                  rows=pl.ds(r + d4 * rr * j, rr, stride=d4)))
            trips.append((False, units))

    pending = None
    for first, units in trips:
        wts = weights(units)
        if pending is not None:
            accumulate(*pending)
        pending = (units, wts, first)
    accumulate(*pending)

    lane_t = lax.broadcasted_iota(jnp.int32, (tq, LANES), 1)
    a0 = acc[0]
    a1 = acc[1]
    l0 = a0[:, HEAD_DIM_B:HEAD_DIM_B + 1]
    l1 = a1[:, 0:1]
    o_ref[0] = jnp.where(lane_t < HEAD_DIM_B, a0 / l0, a1 / l1).astype(o_ref.dtype)


def _attn_prompt(q, k, v):
    b, t, _ = q[0].shape
    tq = ATTN_TILE
    nt = t // tq
    prev_j = lambda j: jnp.maximum(j - 1, 0)
    cur, prev = [], []
    for dil in ATTN_DILATIONS:
        if dil == 1:
            cur.append(pl.BlockSpec((1, tq, LANES), lambda i, h, j: (i, j, h)))
            prev.append(pl.BlockSpec((1, tq, LANES), lambda i, h, j: (i, prev_j(j), h)))
        else:
            cur.append(pl.BlockSpec((1, dil, tq // dil, LANES), lambda i, h, j: (i, 0, j, h)))
            prev.append(pl.BlockSpec((1, dil, tq // dil, LANES),
                                     lambda i, h, j: (i, 0, prev_j(j), h)))
    kv_specs = [s for pc in zip(prev, cur) for s in pc]
    kv_args = lambda x: [a for xd in x for a in (xd, xd)]
    return pl.pallas_call(
        functools.partial(_attn_prompt_kernel, tq=tq),
        grid=(b, WIDTH_B // LANES, nt),
        in_specs=cur + kv_specs + kv_specs,
        out_specs=cur[0],
        out_shape=jax.ShapeDtypeStruct((b, t, WIDTH_B), BF16),
        scratch_shapes=[pltpu.VMEM((2, tq, LANES), F32), pltpu.VMEM((2, tq, LANES), F32)],
        compiler_params=pltpu.CompilerParams(
            dimension_semantics=("arbitrary", "arbitrary", "arbitrary"),
            vmem_limit_bytes=VMEM_LIMIT),
        name="attn_prompt",
    )(*q, *kv_args(k), *kv_args(v))


def _multiplicity(delta):
    delta = np.asarray(delta)
    ok = delta >= 0
    m = ((delta <= 128).astype(np.float32)
         + ((delta <= 512) & (delta % 4 == 0)).astype(np.float32)
         + ((delta <= 2048) & (delta % 16 == 0)).astype(np.float32))
    return np.where(ok, m, 0.0).astype(np.float32)


def _sample_masks(wbuf, nt):
    i = np.arange(nt)[:, None]
    return (_multiplicity(wbuf + i - np.arange(wbuf)[None, :]),
            _multiplicity(i - np.arange(nt)[None, :]))


def _attn_sample_kernel(q_ref, kn_ref, vn_ref, kt_ref, vt_ref, mc_ref, mn_ref, o_ref):
    dh = HEAD_DIM_B
    heads = range(N_HEADS_B)
    mc = mc_ref[...]
    mn = mn_ref[...]
    sl = lambda r, h: r[0, :, h * dh:(h + 1) * dh]
    q = {h: (sl(q_ref, h) * (dh ** -0.5)).astype(BF16) for h in heads}
    s_c = {h: jnp.where(mc > 0.0, _dot(q[h], kt_ref[0, h]), NEG) for h in heads}
    s_n = {h: jnp.where(mn > 0.0, _dot_nt(q[h], sl(kn_ref, h)), NEG) for h in heads}
    m = {h: jnp.maximum(jnp.max(s_c[h], axis=-1, keepdims=True),
                        jnp.max(s_n[h], axis=-1, keepdims=True)) for h in heads}
    p_c = {h: (mc * jnp.exp(s_c[h] - m[h])).astype(BF16) for h in heads}
    p_n = {h: (mn * jnp.exp(s_n[h] - m[h])).astype(BF16) for h in heads}
    den = {h: (jnp.sum(p_c[h].astype(F32), axis=-1, keepdims=True)
               + jnp.sum(p_n[h].astype(F32), axis=-1, keepdims=True)) for h in heads}
    out = {h: _dot_nt(p_c[h], vt_ref[0, h]) + _dot(p_n[h], sl(vn_ref, h)) for h in heads}
    o_ref[0] = jnp.concatenate([out[h] / den[h] for h in heads], axis=1)


def _attn_sample(qb, kn, vn, cache_kt, cache_vt):
    nb, nt, _ = qb.shape
    _, nh, dh, wbuf = cache_kt.shape
    masks = [jnp.asarray(m) for m in _sample_masks(wbuf, nt)]
    tok = pl.BlockSpec((1, nt, WIDTH_B), lambda i: (i, 0, 0))
    cache = pl.BlockSpec((1, nh, dh, wbuf), lambda i: (i, 0, 0, 0))
    mspec = lambda m: pl.BlockSpec(m.shape, lambda i: (0, 0))
    return pl.pallas_call(
        _attn_sample_kernel,
        grid=(nb,),
        in_specs=[tok, tok, tok, cache, cache] + [mspec(m) for m in masks],
        out_specs=tok,
        out_shape=jax.ShapeDtypeStruct((nb, nt, WIDTH_B), F32),
        compiler_params=pltpu.CompilerParams(
            dimension_semantics=("arbitrary",), vmem_limit_bytes=VMEM_LIMIT),
        name="attn_sample",
    )(qb, kn, vn, cache_kt, cache_vt, *masks)


FF_CHUNK = 1024


def _mlp_kernel(x_ref, oa_ref, ob_ref, wo_ref, n2_ref, wu_ref, wd_ref, y_ref):
    mix = jnp.concatenate([oa_ref[...].astype(BF16), ob_ref[...].astype(BF16)], axis=1)
    h1 = x_ref[...] + jnp.dot(mix, wo_ref[...], preferred_element_type=F32)
    hn = _rmsnorm(h1, n2_ref[...]).astype(BF16)
    chunks = range(0, wu_ref.shape[1], FF_CHUNK)
    hid = [jnp.dot(hn, wu_ref[:, c0:c0 + FF_CHUNK], preferred_element_type=F32) for c0 in chunks]
    act = [jnp.square(jnp.maximum(h, 0.0)).astype(BF16) for h in hid]
    y = h1
    for c0, a in zip(chunks, act):
        y = y + jnp.dot(a, wd_ref[c0:c0 + FF_CHUNK, :], preferred_element_type=F32)
    y_ref[...] = y


def _mlp(x, oa, ob, p, tm):
    n, d = x.shape
    row = lambda w: pl.BlockSpec((tm, w), lambda i: (i, 0))
    const = lambda a: pl.BlockSpec(a.shape, lambda i: (0, 0))
    return pl.pallas_call(
        _mlp_kernel,
        grid=(n // tm,),
        in_specs=[row(d), row(WIDTH_A), row(WIDTH_B), const(p["w_o"]), const(p["n2"]),
                  const(p["w_up"]), const(p["w_down"])],
        out_specs=row(d),
        out_shape=jax.ShapeDtypeStruct((n, d), F32),
        compiler_params=pltpu.CompilerParams(
            dimension_semantics=("arbitrary",), vmem_limit_bytes=VMEM_LIMIT),
        name="mlp",
    )(x, oa, ob, p["w_o"], p["n2"], p["w_up"], p["w_down"])


def _mlp_attn_kernel(x_ref, oa_ref, ob_ref, wo_ref, n2_ref, wu_ref, wd_ref,
                     q_ref, kn_ref, vn_ref, kt_ref, vt_ref, mc_ref, mn_ref, y_ref, o_ref):
    _attn_sample_kernel(q_ref, kn_ref, vn_ref, kt_ref, vt_ref, mc_ref, mn_ref, o_ref)
    _mlp_kernel(x_ref, oa_ref, ob_ref, wo_ref, n2_ref, wu_ref, wd_ref, y_ref)


def _mlp_attn(x, oa, ob, p, qb, kn, vn, cache_kt, cache_vt):
    n, d = x.shape
    nb, nt, _ = qb.shape
    _, nh, dh, wbuf = cache_kt.shape
    tm = n // nb
    assert n % nb == 0 and tm % SUBLANES == 0
    masks = [jnp.asarray(m) for m in _sample_masks(wbuf, nt)]
    row = lambda w: pl.BlockSpec((tm, w), lambda i: (i, 0))
    const = lambda a: pl.BlockSpec(a.shape, lambda i: (0, 0), pipeline_mode=pl.Buffered(1))
    tok = pl.BlockSpec((1, nt, WIDTH_B), lambda i: (i, 0, 0))
    cache = pl.BlockSpec((1, nh, dh, wbuf), lambda i: (i, 0, 0, 0))
    return pl.pallas_call(
        _mlp_attn_kernel,
        grid=(nb,),
        in_specs=[row(d), row(WIDTH_A), row(WIDTH_B), const(p["w_o"]), const(p["n2"]),
                  const(p["w_up"]), const(p["w_down"]), tok, tok, tok, cache, cache]
        + [const(m) for m in masks],
        out_specs=[row(d), tok],
        out_shape=[jax.ShapeDtypeStruct((n, d), F32),
                   jax.ShapeDtypeStruct((nb, nt, WIDTH_B), F32)],
        compiler_params=pltpu.CompilerParams(
            dimension_semantics=("arbitrary",), vmem_limit_bytes=VMEM_LIMIT),
        name="mlp_attn",
    )(x, oa, ob, p["w_o"], p["n2"], p["w_up"], p["w_down"], qb, kn, vn, cache_kt, cache_vt, *masks)


def _layer_params(norm1_g, w_in, conv_w, a_log, dt_bias, delta_norm_g, q_norm_g, k_norm_g, w_o,
                  norm2_g, w_up, w_down):
    d = w_in.shape[0]
    n_gate = 2 * N_HEADS_A
    gate0 = CONV_CH + WIDTH_A
    w_re = jnp.concatenate(
        [w_in[:, :gate0], w_in[:, gate0 + n_gate:], w_in[:, gate0:gate0 + n_gate],
         jnp.zeros((d, LANES - n_gate), w_in.dtype)], axis=1).astype(BF16)
    lane_pad = lambda a: jnp.zeros((1, LANES), F32).at[0, N_HEADS_A:n_gate].set(a.astype(F32))
    hid = np.arange(WIDTH_B) // HEAD_DIM_B
    head_mean = jnp.asarray((hid[:, None] == hid[None, :]).astype(np.float32) / HEAD_DIM_B, BF16)
    return {
        "n1": norm1_g.reshape(1, d).astype(F32), "w_in": w_re, "conv_w": conv_w.astype(F32),
        "alog": lane_pad(a_log), "dtb": lane_pad(dt_bias),
        "qng": jnp.tile(q_norm_g.astype(F32), N_HEADS_B).reshape(1, WIDTH_B),
        "kng": jnp.tile(k_norm_g.astype(F32), N_HEADS_B).reshape(1, WIDTH_B),
        "hm": head_mean, "dng": delta_norm_g.reshape(1, HEAD_DIM_A).astype(F32),
        "w_o": w_o.astype(BF16), "n2": norm2_g.reshape(1, d).astype(F32),
        "w_up": w_up.astype(BF16), "w_down": w_down.astype(BF16),
    }


MLP_TILE = 256
FUSED_MLP_MAX_TILE = 256


def _layer(xp, xs, state_conv, s0_s, cache_k, cache_v, p):
    b, t, d = xp.shape
    nb, nt, _ = xs.shape
    n = b * t
    pbuf = min(MAX_WINDOW, t)

    qa, ka, va, z, gt, kp, vp, conv_p, *dil = _proj_prompt(xp, p, tm=256, pbuf=pbuf)
    window = lambda a: a.reshape(b, N_HEADS_B, HEAD_DIM_B, pbuf).transpose(0, 3, 1, 2)
    kp, vp = window(kp), window(vp)
    s0_p = jnp.zeros((b, N_HEADS_A, HEAD_DIM_A, HEAD_DIM_A), F32)
    oa_p, s_p = _delta(qa, ka, va, z, gt, s0_p, p["dng"], chunk=DELTA_CHUNK, nchunk=2, bb=b,
                       out_dtype=BF16)
    ob_p = _attn_prompt(dil[0:3], dil[3:6], dil[6:9])

    qa, ka, va, z, gt, qb, kn, vn, conv_s = _proj_sample(xs, state_conv, p, bt=min(nb, 64))
    oa_s, s_s = _delta(qa, ka, va, z, gt, s0_s, p["dng"], chunk=nt, nchunk=1, bb=min(nb, 16),
                       out_dtype=F32)
    ckt, cvt = cache_k.transpose(0, 2, 3, 1), cache_v.transpose(0, 2, 3, 1)

    mlp_in = (xp.reshape(n, d), oa_p.reshape(n, WIDTH_A), ob_p.reshape(n, WIDTH_B), p)
    tile = n // nb
    if n % nb == 0 and tile % SUBLANES == 0 and tile <= FUSED_MLP_MAX_TILE:
        yp, ob_s = _mlp_attn(*mlp_in, qb, kn, vn, ckt, cvt)
    else:
        yp = _mlp(*mlp_in, tm=MLP_TILE)
        ob_s = _attn_sample(qb, kn, vn, ckt, cvt)
    ys = _mlp(xs.reshape(nb * nt, d), oa_s.reshape(nb * nt, WIDTH_A),
              ob_s.reshape(nb * nt, WIDTH_B), p, tm=min(MLP_TILE, nb * nt))
    heads = lambda a: a.reshape(nb, nt, N_HEADS_B, HEAD_DIM_B)
    return ((yp.reshape(b, t, d), kp, vp, s_p, conv_p),
            (ys.reshape(nb, nt, d), heads(kn), heads(vn), s_s, conv_s))


def kernel(x_prompt, x_sample, cache_swa_k, cache_swa_v, state_delta, state_conv, norm1_g, w_in,
           conv_w, a_log, dt_bias, delta_norm_g, q_norm_g, k_norm_g, w_o, norm2_g, w_up, w_down):
    depth = w_in.shape[0]
    b, s, _ = x_prompt.shape
    nb, nt, _ = x_sample.shape
    wbuf = cache_swa_k.shape[2]
    assert s % ATTN_TILE == 0 and nt == SUBLANES and wbuf == MAX_WINDOW
    yp, ys = x_prompt, x_sample
    outs = [[] for _ in range(8)]
    for layer in range(depth):
        p = _layer_params(norm1_g[layer], w_in[layer], conv_w[layer], a_log[layer], dt_bias[layer],
                          delta_norm_g[layer], q_norm_g[layer], k_norm_g[layer], w_o[layer],
                          norm2_g[layer], w_up[layer], w_down[layer])
        (yp, kp, vp, dp, cp), (ys, kn, vn, dn, cn) = _layer(
            yp, ys, state_conv[layer], state_delta[layer], cache_swa_k[layer], cache_swa_v[layer], p)
        for lst, val in zip(outs, (kp, vp, dp, cp, kn, vn, dn, cn)):
            lst.append(val)
    return (yp, ys) + tuple(jnp.stack(o) for o in outs)
```

```python
import functools

import numpy as np
import jax
import jax.numpy as jnp
from jax import lax
from jax.experimental import pallas as pl
from jax.experimental.pallas import tpu as pltpu

F32 = jnp.float32
BF16 = jnp.bfloat16

N_HEADS_A = 4
HEAD_DIM_A = 128
WIDTH_A = N_HEADS_A * HEAD_DIM_A
N_HEADS_B = 8
HEAD_DIM_B = 64
WIDTH_B = N_HEADS_B * HEAD_DIM_B
CONV_WIDTH = 4
CONV_CH = 3 * WIDTH_A
DELTA_CHUNK = 64
MAX_WINDOW = 2048
NORM_EPS = 1e-6
LANES = 128
SUBLANES = 8
NEG = -1e30
LOG2_E = 1.4426950408889634

C_QKV, C_Z = 0, CONV_CH
C_QB, C_KB, C_VB = 0, WIDTH_B, 2 * WIDTH_B

ATTN_DILATIONS = (1, 4, 16)
ATTN_BAND = 128
ATTN_TILE = 2048
ATTN_UNROLL = 2
VMEM_LIMIT = 56 * 1024 * 1024


def _dot(a, b):
    return jnp.dot(a.astype(BF16), b.astype(BF16), preferred_element_type=F32)


def _dot_nt(a, b):
    return lax.dot_general(a.astype(BF16), b.astype(BF16), (((1,), (1,)), ((), ())),
                           preferred_element_type=F32)


def _dot_tn(a, b):
    return lax.dot_general(a.astype(BF16), b.astype(BF16), (((0,), (0,)), ((), ())),
                           preferred_element_type=F32)


def _split2(x):
    hi = x.astype(BF16)
    lo = (x - hi.astype(F32)).astype(BF16)
    return hi, lo


def _split3(x):
    hi = x.astype(BF16)
    r = x - hi.astype(F32)
    mid = r.astype(BF16)
    lo = (r - mid.astype(F32)).astype(BF16)
    return hi, mid, lo


def _dot_exact_lhs(mask_bf16, x):
    hi, mid, lo = _split3(x)
    d = lambda p: jnp.dot(mask_bf16, p, preferred_element_type=F32)
    return d(hi) + d(mid) + d(lo)


def _sigmoid(x):
    return 0.5 * jnp.tanh(0.5 * x) + 0.5


def _silu(x):
    return x * _sigmoid(x)


def _softplus(x):
    return jnp.maximum(x, 0.0) + jnp.log1p(jnp.exp(-jnp.abs(x)))


def _rmsnorm(x, g):
    return x * lax.rsqrt(jnp.mean(x * x, axis=-1, keepdims=True) + NORM_EPS) * g


def _proj_body(x, ext_ref, shift, pad, refs):
    n1_ref, wa_ref, wb_ref, wg_ref, cw_ref, al_ref, dtb_ref, qng_ref, kng_ref, hm_ref = refs
    tm = x.shape[0]
    hn = _rmsnorm(x, n1_ref[...]).astype(BF16)

    proj = lambda w: jnp.dot(hn, w, preferred_element_type=F32)
    u = proj(wa_ref[:, C_QKV:C_QKV + CONV_CH])
    qb = proj(wb_ref[:, C_QB:C_QB + WIDTH_B])
    kb = proj(wb_ref[:, C_KB:C_KB + WIDTH_B])
    vb = proj(wb_ref[:, C_VB:C_VB + WIDTH_B])
    z = proj(wa_ref[:, C_Z:C_Z + WIDTH_A])
    gc = proj(wg_ref[...])

    ext_ref[pad:pad + tm, :] = u
    cw = cw_ref[...]
    y = u * cw[3:4, :]
    for i in range(CONV_WIDTH - 1):
        off = pad - (CONV_WIDTH - 1 - i) * shift
        y = y + ext_ref[off:off + tm, :] * cw[i:i + 1, :]
    y = _silu(y)
    qa, ka = [], []
    for h in range(N_HEADS_A):
        lo = h * HEAD_DIM_A
        qh = y[:, lo:lo + HEAD_DIM_A]
        qa.append(qh * (lax.rsqrt(
            jnp.sum(qh * qh, axis=-1, keepdims=True) + NORM_EPS) * HEAD_DIM_A ** -0.5))
        kh = y[:, WIDTH_A + lo:WIDTH_A + lo + HEAD_DIM_A]
        ka.append(kh * lax.rsqrt(jnp.sum(kh * kh, axis=-1, keepdims=True) + NORM_EPS))
    qa = jnp.concatenate(qa, axis=1)
    ka = jnp.concatenate(ka, axis=1)
    va = y[:, 2 * WIDTH_A:3 * WIDTH_A]

    hm = hm_ref[...]

    def headnorm(v, g):
        ms = jnp.dot((v * v).astype(BF16), hm, preferred_element_type=F32)
        return v * lax.rsqrt(ms + NORM_EPS) * g

    qb = headnorm(qb, qng_ref[...])
    kb = headnorm(kb, kng_ref[...])

    lane = lax.broadcasted_iota(jnp.int32, gc.shape, 1)
    beta = _sigmoid(gc)
    g = -jnp.exp(al_ref[...]) * _softplus(gc + dtb_ref[...])
    gates = jnp.where(lane < N_HEADS_A, beta, g)
    return qa, ka, va, z, gates, qb, kb, vb


def _proj_prompt_kernel(x_ref, n1_ref, wa_ref, wb_ref, wg_ref, cw_ref, al_ref, dtb_ref, qng_ref,
                        kng_ref, hm_ref,
                        qa_ref, ka_ref, va_ref, z_ref, gt_ref, kb_ref, vb_ref, cn_ref,
                        q1_ref, q4_ref, q16_ref, k1_ref, k4_ref, k16_ref, v1_ref, v4_ref, v16_ref,
                        ext_ref, dil_ref, *, first_win):
    t = pl.program_id(1)
    tm = x_ref.shape[1]

    @pl.when(t == 0)
    def _():
        ext_ref[0:SUBLANES, :] = jnp.zeros((SUBLANES, CONV_CH), F32)

    @pl.when(t > 0)
    def _():
        ext_ref[0:SUBLANES, :] = ext_ref[tm:tm + SUBLANES, :]

    qa, ka, va, z, gates, qb, kb, vb = _proj_body(
        x_ref[0], ext_ref, 1, SUBLANES,
        (n1_ref, wa_ref, wb_ref, wg_ref, cw_ref, al_ref, dtb_ref, qng_ref, kng_ref, hm_ref))
    for r, v in zip((qa_ref, ka_ref, va_ref, z_ref, gt_ref), (qa, ka, va, z, gates)):
        r[0] = v
    cn_ref[0] = ext_ref[tm + SUBLANES - (CONV_WIDTH - 1):tm + SUBLANES, :]

    @pl.when(t >= first_win)
    def _():
        kb_ref[0] = kb.T
        vb_ref[0] = vb.T

    nchunk = WIDTH_B // LANES
    q_att = qb * (HEAD_DIM_B ** -0.5 * LOG2_E)
    for ai, (val, r1, r4, r16) in enumerate(((q_att, q1_ref, q4_ref, q16_ref),
                                              (kb, k1_ref, k4_ref, k16_ref),
                                              (vb, v1_ref, v4_ref, v16_ref))):
        r1[0] = val.astype(BF16)
        for c in range(nchunk):
            dil_ref[0, ai * nchunk + c] = val[:, c * LANES:(c + 1) * LANES]
        n4, n16 = tm // 4, tm // 16
        for r in range(4):
            parts = [dil_ref[0, ai * nchunk + c, pl.ds(r, n4, stride=4), :] for c in range(nchunk)]
            r4[0, r] = jnp.concatenate(parts, axis=1).astype(BF16)
            for c in range(nchunk):
                dil_ref[1, ai * nchunk + c, r * n4:(r + 1) * n4, :] = parts[c]
        for r in range(4):
            for r2 in range(4):
                r16[0, r + 4 * r2] = jnp.concatenate(
                    [dil_ref[1, ai * nchunk + c, pl.ds(r * n4 + r2, n16, stride=4), :]
                     for c in range(nchunk)], axis=1).astype(BF16)


def _proj_sample_kernel(x_ref, st_ref, n1_ref, wa_ref, wb_ref, wg_ref, cw_ref, al_ref, dtb_ref,
                        qng_ref, kng_ref, hm_ref, qa_ref, ka_ref, va_ref, z_ref, gt_ref, qb_ref, kb_ref, vb_ref,
                        cn_ref, ext_ref):
    nt, nb, d = x_ref.shape
    ncv = CONV_WIDTH - 1
    tm = nb * nt
    pad = ncv * nb
    ext_ref[0:pad, :] = st_ref[...].reshape(pad, CONV_CH)
    vals = _proj_body(x_ref[...].reshape(tm, d), ext_ref, nb, pad,
                      (n1_ref, wa_ref, wb_ref, wg_ref, cw_ref, al_ref, dtb_ref, qng_ref, kng_ref,
                       hm_ref))
    for r, v in zip((qa_ref, ka_ref, va_ref, z_ref, gt_ref, qb_ref, kb_ref, vb_ref), vals):
        r[...] = v.reshape(r.shape)
    cn_ref[...] = ext_ref[tm:tm + pad, :].reshape(ncv, nb, CONV_CH)


def _proj_params(p):
    full = lambda a: pl.BlockSpec(a.shape, lambda *_: (0,) * a.ndim)
    arrs = (p["n1"], p["w_a"], p["w_b"], p["w_g"], p["conv_w"], p["alog"], p["dtb"], p["qng"],
            p["kng"], p["hm"])
    return arrs, [full(a) for a in arrs]


def _proj_prompt(x, p, tm, pbuf):
    b, t, d = x.shape
    assert pbuf % tm == 0 and t % tm == 0
    first_win = (t - pbuf) // tm
    arrs, specs = _proj_params(p)
    row = lambda w: pl.BlockSpec((1, tm, w), lambda i, j: (i, j, 0))
    widths = (WIDTH_A, WIDTH_A, WIDTH_A, WIDTH_A, LANES)
    out_shape = [jax.ShapeDtypeStruct((b, t, w), F32) for w in widths]
    out_specs = [row(w) for w in widths]
    for _ in range(2):
        out_shape.append(jax.ShapeDtypeStruct((b, WIDTH_B, pbuf), F32))
        out_specs.append(pl.BlockSpec((1, WIDTH_B, tm),
                                      lambda i, j: (i, 0, jnp.maximum(j - first_win, 0))))
    out_shape.append(jax.ShapeDtypeStruct((b, CONV_WIDTH - 1, CONV_CH), F32))
    out_specs.append(pl.BlockSpec((1, CONV_WIDTH - 1, CONV_CH), lambda i, j: (i, 0, 0)))
    for _ in range(3):
        out_shape.append(jax.ShapeDtypeStruct((b, t, WIDTH_B), BF16))
        out_specs.append(row(WIDTH_B))
        for dil in ATTN_DILATIONS[1:]:
            out_shape.append(jax.ShapeDtypeStruct((b, dil, t // dil, WIDTH_B), BF16))
            out_specs.append(pl.BlockSpec((1, dil, tm // dil, WIDTH_B), lambda i, j: (i, 0, j, 0)))
    return pl.pallas_call(
        functools.partial(_proj_prompt_kernel, first_win=first_win),
        grid=(b, t // tm),
        in_specs=[row(d)] + specs,
        out_specs=out_specs,
        out_shape=out_shape,
        scratch_shapes=[pltpu.VMEM((tm + SUBLANES, CONV_CH), F32),
                        pltpu.VMEM((2, 3 * WIDTH_B // LANES, tm, LANES), F32)],
        compiler_params=pltpu.CompilerParams(
            dimension_semantics=("arbitrary", "arbitrary"), vmem_limit_bytes=VMEM_LIMIT),
        name="proj_prompt",
    )(x, *arrs)


def _proj_sample(x, state_conv, p, bt):
    nb, nt, d = x.shape
    ncv = CONV_WIDTH - 1
    arrs, specs = _proj_params(p)
    blk = lambda r, w: pl.BlockSpec((r, bt, w), lambda i: (0, i, 0))
    widths = (WIDTH_A, WIDTH_A, WIDTH_A, WIDTH_A, LANES, WIDTH_B, WIDTH_B, WIDTH_B)
    out_shape = [jax.ShapeDtypeStruct((nt, nb, w), F32) for w in widths]
    out_shape.append(jax.ShapeDtypeStruct((ncv, nb, CONV_CH), F32))
    out_specs = [blk(nt, w) for w in widths] + [blk(ncv, CONV_CH)]
    outs = pl.pallas_call(
        _proj_sample_kernel,
        grid=(nb // bt,),
        in_specs=[blk(nt, d), blk(ncv, CONV_CH)] + specs,
        out_specs=out_specs,
        out_shape=out_shape,
        scratch_shapes=[pltpu.VMEM((bt * (nt + ncv), CONV_CH), F32)],
        compiler_params=pltpu.CompilerParams(
            dimension_semantics=("arbitrary",), vmem_limit_bytes=VMEM_LIMIT),
        name="proj_sample",
    )(x.transpose(1, 0, 2), state_conv.transpose(1, 0, 2), *arrs)
    return [o.transpose(1, 0, 2) for o in outs]


def _delta_kernel(q_ref, k_ref, v_ref, z_ref, gt_ref, s0_ref, ng_ref, o_ref, sf_ref, s_scr,
                  *, chunk, nchunk, bb):
    c = chunk
    t = pl.program_id(1)

    @pl.when(t == 0)
    def _():
        s_scr[...] = s0_ref[...]

    ii = lax.broadcasted_iota(jnp.int32, (c, c), 0)
    jj = lax.broadcasted_iota(jnp.int32, (c, c), 1)
    causal = ii >= jj
    strict = ii > jj
    tril_bf = causal.astype(F32).astype(BF16)
    eye = (ii == jj).astype(F32)
    level_masks = []
    s = 1
    while s < c:
        level_masks.append(((ii // (2 * s)) == (jj // (2 * s))) & ((ii & s) != 0) & ((jj & s) == 0))
        s *= 2
    ng = ng_ref[...]

    heads = range(N_HEADS_A)
    items = [(bi, ci, h) for bi in range(bb) for ci in range(nchunk) for h in heads]
    pre = {}
    for bi in range(bb):
        for ci in range(nchunk):
            rows = slice(ci * c, (ci + 1) * c)
            gt = gt_ref[bi, rows, :]
            gcum = _dot_exact_lhs(tril_bf, gt)
            if c < LANES:
                gpad = jnp.concatenate([gcum, jnp.zeros((LANES - c, LANES), F32)], axis=0)
            else:
                gpad = gcum
            gcum_t = gpad.T
            for h in heads:
                lo = h * HEAD_DIM_A
                q = q_ref[bi, rows, lo:lo + HEAD_DIM_A]
                k = k_ref[bi, rows, lo:lo + HEAD_DIM_A]
                v = v_ref[bi, rows, lo:lo + HEAD_DIM_A]
                beta = gt[:, h:h + 1]
                g_col = gcum[:, N_HEADS_A + h:N_HEADS_A + h + 1]
                g_row = gcum_t[N_HEADS_A + h:N_HEADS_A + h + 1, 0:c]
                g_last = gcum[c - 1:c, N_HEADS_A + h:N_HEADS_A + h + 1]
                decay = jnp.where(causal, jnp.exp(jnp.where(causal, g_col - g_row, 0.0)), 0.0)
                exp_g = jnp.exp(g_col)
                kb = k * beta
                pre[bi, ci, h] = dict(
                    q=q, k=k, kb=kb, decay=decay, g_last=g_last,
                    rhs=jnp.concatenate([v * beta, kb * exp_g], axis=1),
                    q_dec=q * exp_g, k_dec=k * jnp.exp(g_last - g_col))
    lmat = {it: jnp.where(strict, _dot_nt(pre[it]["kb"], pre[it]["k"]) * pre[it]["decay"], 0.0)
            for it in items}
    qk = {it: _dot_nt(pre[it]["q"], pre[it]["k"]) * pre[it]["decay"] for it in items}
    tinv = {it: eye - jnp.where(level_masks[0], lmat[it], 0.0) for it in items}
    for msk in level_masks[1:]:
        te = {it: _dot(tinv[it], jnp.where(msk, lmat[it], 0.0)) for it in items}
        tinv = {it: tinv[it] - _dot(te[it], tinv[it]) for it in items}
    uw = {it: _dot(tinv[it], pre[it]["rhs"]) for it in items}

    for bi in range(bb):
        st = {h: s_scr[bi, h] for h in heads}
        for ci in range(nchunk):
            rows = slice(ci * c, (ci + 1) * c)
            ws = {h: _dot(jnp.concatenate([uw[bi, ci, h][:, HEAD_DIM_A:2 * HEAD_DIM_A],
                                           pre[bi, ci, h]["q_dec"]], axis=0), st[h])
                  for h in heads}
            v_new = {h: uw[bi, ci, h][:, 0:HEAD_DIM_A] - ws[h][0:c] for h in heads}
            o = {h: ws[h][c:2 * c] + _dot(qk[bi, ci, h], v_new[h]) for h in heads}
            st = {h: st[h] * jnp.exp(pre[bi, ci, h]["g_last"])
                  + _dot_tn(pre[bi, ci, h]["k_dec"], v_new[h]) for h in heads}
            for h in heads:
                lo = h * HEAD_DIM_A
                zz = z_ref[bi, rows, lo:lo + HEAD_DIM_A]
                o_ref[bi, rows, lo:lo + HEAD_DIM_A] = (
                    _rmsnorm(o[h], ng) * _silu(zz)).astype(o_ref.dtype)
        for h in heads:
            s_scr[bi, h] = st[h]

    @pl.when(t == pl.num_programs(1) - 1)
    def _():
        sf_ref[...] = s_scr[...]


def _delta(qa, ka, va, z, gt, s0, ng, *, chunk, nchunk, bb, out_dtype):
    b, t, _ = qa.shape
    tc = chunk * nchunk
    row = lambda w: pl.BlockSpec((bb, tc, w), lambda i, j: (i, j, 0))
    sspec = pl.BlockSpec((bb, N_HEADS_A, HEAD_DIM_A, HEAD_DIM_A), lambda i, j: (i, 0, 0, 0))
    return pl.pallas_call(
        functools.partial(_delta_kernel, chunk=chunk, nchunk=nchunk, bb=bb),
        grid=(b // bb, t // tc),
        in_specs=[row(WIDTH_A)] * 4 + [row(LANES), sspec,
                                       pl.BlockSpec((1, HEAD_DIM_A), lambda i, j: (0, 0))],
        out_specs=[row(WIDTH_A), sspec],
        out_shape=[jax.ShapeDtypeStruct((b, t, WIDTH_A), out_dtype),
                   jax.ShapeDtypeStruct((b, N_HEADS_A, HEAD_DIM_A, HEAD_DIM_A), F32)],
        scratch_shapes=[pltpu.VMEM((bb, N_HEADS_A, HEAD_DIM_A, HEAD_DIM_A), F32)],
        compiler_params=pltpu.CompilerParams(
            dimension_semantics=("arbitrary", "arbitrary"), vmem_limit_bytes=VMEM_LIMIT),
        name="delta_c%d" % chunk,
    )(qa, ka, va, z, gt, s0, ng)


def _attn_prompt_kernel(q1, q4, q16, k1p, k1c, k4p, k4c, k16p, k16c, v1p, v1c, v4p, v4c, v16p, v16c,
                        o_ref, acc, mrep, *, tq):
    t = pl.program_id(2)
    rr = ATTN_BAND
    ii = lax.broadcasted_iota(jnp.int32, (rr, rr), 0)
    jj = lax.broadcasted_iota(jnp.int32, (rr, rr), 1)
    lower = jj <= ii
    upper = jj >= ii
    lane = lax.broadcasted_iota(jnp.int32, (rr, LANES), 1)
    own = [(lane >= hh * HEAD_DIM_B) & (lane < (hh + 1) * HEAD_DIM_B) for hh in range(2)]
    pen_t = jnp.where(t > 0, jnp.float32(0.0), jnp.float32(NEG))
    one = jnp.ones((rr, LANES), BF16)
    zero = jnp.zeros((rr, LANES), BF16)

    ld = lambda ref_idx: ref_idx[0][ref_idx[1]]

    def weights(units):
        chains = [(j, hh) for j in range(len(units)) for hh in range(2)]
        qv = {ch: jnp.where(own[ch[1]], ld(units[ch[0]]["q"]), zero) for ch in chains}
        kc = [ld(u["kc"]) for u in units]
        kp = [ld(u["kp"]) for u in units]
        raw = {ch: (_dot_nt(qv[ch], kc[ch[0]]), _dot_nt(qv[ch], kp[ch[0]])) for ch in chains}
        out = {}
        for ch in chains:
            s_c = jnp.where(lower, raw[ch][0], NEG)
            s_p = raw[ch][1]
            if units[ch[0]]["pen"] is not None:
                s_p = s_p + units[ch[0]]["pen"]
            s_p = jnp.where(upper, s_p, NEG)
            m = jnp.max(jnp.maximum(s_c, s_p), axis=-1, keepdims=True)
            out[ch] = (jnp.exp2(s_c - m).astype(BF16), jnp.exp2(s_p - m).astype(BF16), m)
        return out

    def accumulate(units, wts, first):
        vc = [ld(u["vc"]) for u in units]
        vp = [ld(u["vp"]) for u in units]
        pv = {ch: (jnp.dot(p_c, jnp.where(own[ch[1]], vc[ch[0]], one), preferred_element_type=F32)
                   + jnp.dot(p_p, jnp.where(own[ch[1]], vp[ch[0]], one),
                             preferred_element_type=F32))
              for ch, (p_c, p_p, _) in wts.items()}
        for ch, (_, _, m) in wts.items():
            j, hh = ch
            rows = units[j]["rows"]
            m_b = jnp.broadcast_to(m, (rr, LANES))
            if first:
                acc[hh, rows, :] = pv[ch]
                mrep[hh, rows, :] = m_b
            else:
                m_old = mrep[hh, rows, :]
                m_new = jnp.maximum(m_old, m_b)
                acc[hh, rows, :] = (acc[hh, rows, :] * jnp.exp2(m_old - m_new)
                                    + pv[ch] * jnp.exp2(m_b - m_new))
                mrep[hh, rows, :] = m_new

    uu = ATTN_UNROLL
    blk = lambda j: slice(j * rr, (j + 1) * rr)

    d16, d4 = ATTN_DILATIONS[2], ATTN_DILATIONS[1]
    assert tq // d16 == rr and (tq // (d4 * rr)) % uu == 0 and d16 % uu == 0
    trips = []
    full = slice(None)
    for it in range(d16 // uu):
        trips.append((True, [dict(
            q=(q16, (0, r)), kc=(k16c, (0, r)), kp=(k16p, (0, r)), vc=(v16c, (0, r)),
            vp=(v16p, (0, r)), pen=pen_t, rows=pl.ds(r, rr, stride=d16))
            for r in range(it * uu, (it + 1) * uu)]))
    for it in range(tq // (rr * uu)):
        units = []
        for j in range(it * uu, (it + 1) * uu):
            cur = (0, blk(j), full)
            prv = (0, blk(j - 1), full) if j > 0 else (0, slice(tq - rr, tq), full)
            units.append(dict(q=(q1, cur), kc=(k1c, cur), vc=(v1c, cur),
                              kp=(k1c if j > 0 else k1p, prv), vp=(v1c if j > 0 else v1p, prv),
                              pen=None if j > 0 else pen_t, rows=pl.ds(j * rr, rr)))
        trips.append((False, units))
    nb4 = tq // d4 // rr
    for r in range(d4):
        for jt in range(nb4 // uu):
            units = []
            for j in range(jt * uu, (jt + 1) * uu):
                cur = (0, r, blk(j), full)
                prv = (0, r, blk(j - 1), full) if j > 0 else (0, r, blk(nb4 - 1), full)
                units.append(dict(q=(q4, cur), kc=(k4c, cur), vc=(v4c, cur),
                                  kp=(k4c if j > 0 else k4p, prv),
                                  vp=(v4c if j > 0 else v4p, prv),
                                  pen=None if j > 0 else pen_t,
                                  rows=pl.ds(r + d4 * rr * j, rr, stride=d4)))
            trips.append((False, units))

    pending = None
    for first, units in trips:
        wts = weights(units)
        if pending is not None:
            accumulate(*pending)
        pending = (units, wts, first)
    accumulate(*pending)

    lane_t = lax.broadcasted_iota(jnp.int32, (tq, LANES), 1)
    a0 = acc[0]
    a1 = acc[1]
    l0 = a0[:, HEAD_DIM_B:HEAD_DIM_B + 1]
    l1 = a1[:, 0:1]
    o_ref[0] = jnp.where(lane_t < HEAD_DIM_B, a0 / l0, a1 / l1).astype(o_ref.dtype)


def _attn_prompt(q, k, v):
    b, t, _ = q[0].shape
    tq = ATTN_TILE
    nt = t // tq
    prev_j = lambda j: jnp.maximum(j - 1, 0)
    cur, prev = [], []
    for dil in ATTN_DILATIONS:
        if dil == 1:
            cur.append(pl.BlockSpec((1, tq, LANES), lambda i, h, j: (i, j, h)))
            prev.append(pl.BlockSpec((1, tq, LANES), lambda i, h, j: (i, prev_j(j), h)))
        else:
            cur.append(pl.BlockSpec((1, dil, tq // dil, LANES), lambda i, h, j: (i, 0, j, h)))
            prev.append(pl.BlockSpec((1, dil, tq // dil, LANES),
                                     lambda i, h, j: (i, 0, prev_j(j), h)))
    kv_specs = [s for pc in zip(prev, cur) for s in pc]
    kv_args = lambda x: [a for xd in x for a in (xd, xd)]
    return pl.pallas_call(
        functools.partial(_attn_prompt_kernel, tq=tq),
        grid=(b, WIDTH_B // LANES, nt),
        in_specs=cur + kv_specs + kv_specs,
        out_specs=cur[0],
        out_shape=jax.ShapeDtypeStruct((b, t, WIDTH_B), BF16),
        scratch_shapes=[pltpu.VMEM((2, tq, LANES), F32), pltpu.VMEM((2, tq, LANES), F32)],
        compiler_params=pltpu.CompilerParams(
            dimension_semantics=("arbitrary", "arbitrary", "arbitrary"),
            vmem_limit_bytes=VMEM_LIMIT),
        name="attn_prompt",
    )(*q, *kv_args(k), *kv_args(v))


def _multiplicity(delta):
    delta = np.asarray(delta)
    ok = delta >= 0
    m = ((delta <= 128).astype(np.float32)
         + ((delta <= 512) & (delta % 4 == 0)).astype(np.float32)
         + ((delta <= 2048) & (delta % 16 == 0)).astype(np.float32))
    return np.where(ok, m, 0.0).astype(np.float32)


def _sample_masks(wbuf, nt):
    i = np.arange(nt)[:, None]
    return (_multiplicity(wbuf + i - np.arange(wbuf)[None, :]),
            _multiplicity(i - np.arange(nt)[None, :]))


def _run_stages(*gens):
    gens = list(gens)
    while gens:
        for g in list(gens):
            if next(g, StopIteration) is StopIteration:
                gens.remove(g)


def _attn_sample_kernel(*refs):
    _run_stages(_attn_sample_stages(*refs))


def _attn_sample_stages(q_ref, kn_ref, vn_ref, kt_ref, vt_ref, mc_ref, mn_ref, o_ref):
    dh = HEAD_DIM_B
    heads = range(N_HEADS_B)
    mc = mc_ref[...]
    mn = mn_ref[...]
    sl = lambda r, h: r[0, :, h * dh:(h + 1) * dh]
    q = {h: (sl(q_ref, h) * (dh ** -0.5)).astype(BF16) for h in heads}
    s_c = {h: jnp.where(mc > 0.0, _dot(q[h], kt_ref[0, h]), NEG) for h in heads}
    s_n = {h: jnp.where(mn > 0.0, _dot_nt(q[h], sl(kn_ref, h)), NEG) for h in heads}
    yield
    m = {h: jnp.maximum(jnp.max(s_c[h], axis=-1, keepdims=True),
                        jnp.max(s_n[h], axis=-1, keepdims=True)) for h in heads}
    p_c = {h: (mc * jnp.exp(s_c[h] - m[h])).astype(BF16) for h in heads}
    p_n = {h: (mn * jnp.exp(s_n[h] - m[h])).astype(BF16) for h in heads}
    den = {h: (jnp.sum(p_c[h].astype(F32), axis=-1, keepdims=True)
               + jnp.sum(p_n[h].astype(F32), axis=-1, keepdims=True)) for h in heads}
    yield
    out = {h: _dot_nt(p_c[h], vt_ref[0, h]) + _dot(p_n[h], sl(vn_ref, h)) for h in heads}
    yield
    o_ref[0] = jnp.concatenate([out[h] / den[h] for h in heads], axis=1)


def _attn_sample(qb, kn, vn, cache_kt, cache_vt):
    nb, nt, _ = qb.shape
    _, nh, dh, wbuf = cache_kt.shape
    masks = [jnp.asarray(m) for m in _sample_masks(wbuf, nt)]
    tok = pl.BlockSpec((1, nt, WIDTH_B), lambda i: (i, 0, 0))
    cache = pl.BlockSpec((1, nh, dh, wbuf), lambda i: (i, 0, 0, 0))
    mspec = lambda m: pl.BlockSpec(m.shape, lambda i: (0, 0))
    return pl.pallas_call(
        _attn_sample_kernel,
        grid=(nb,),
        in_specs=[tok, tok, tok, cache, cache] + [mspec(m) for m in masks],
        out_specs=tok,
        out_shape=jax.ShapeDtypeStruct((nb, nt, WIDTH_B), F32),
        compiler_params=pltpu.CompilerParams(
            dimension_semantics=("arbitrary",), vmem_limit_bytes=VMEM_LIMIT),
        name="attn_sample",
    )(qb, kn, vn, cache_kt, cache_vt, *masks)


FF_CHUNK = 1024


def _mlp_kernel(*refs):
    _run_stages(_mlp_stages(*refs))


def _mlp_stages(x_ref, oa_ref, ob_ref, wo_ref, n2_ref, wu_ref, wd_ref, y_ref):
    mix = jnp.concatenate([oa_ref[...].astype(BF16), ob_ref[...].astype(BF16)], axis=1)
    h1 = x_ref[...] + jnp.dot(mix, wo_ref[...], preferred_element_type=F32)
    hn = _rmsnorm(h1, n2_ref[...]).astype(BF16)
    yield
    chunks = range(0, wu_ref.shape[1], FF_CHUNK)
    hid = [jnp.dot(hn, wu_ref[:, c0:c0 + FF_CHUNK], preferred_element_type=F32) for c0 in chunks]
    act = [jnp.square(jnp.maximum(h, 0.0)).astype(BF16) for h in hid]
    yield
    y = h1
    for c0, a in zip(chunks, act):
        y = y + jnp.dot(a, wd_ref[c0:c0 + FF_CHUNK, :], preferred_element_type=F32)
    yield
    y_ref[...] = y


def _mlp(x, oa, ob, p, tm):
    n, d = x.shape
    row = lambda w: pl.BlockSpec((tm, w), lambda i: (i, 0))
    const = lambda a: pl.BlockSpec(a.shape, lambda i: (0, 0))
    return pl.pallas_call(
        _mlp_kernel,
        grid=(n // tm,),
        in_specs=[row(d), row(WIDTH_A), row(WIDTH_B), const(p["w_o"]), const(p["n2"]),
                  const(p["w_up"]), const(p["w_down"])],
        out_specs=row(d),
        out_shape=jax.ShapeDtypeStruct((n, d), F32),
        compiler_params=pltpu.CompilerParams(
            dimension_semantics=("arbitrary",), vmem_limit_bytes=VMEM_LIMIT),
        name="mlp",
    )(x, oa, ob, p["w_o"], p["n2"], p["w_up"], p["w_down"])


def _mlp_attn_kernel(x_ref, oa_ref, ob_ref, wo_ref, n2_ref, wu_ref, wd_ref,
                     q_ref, kn_ref, vn_ref, kt_ref, vt_ref, mc_ref, mn_ref, y_ref, o_ref):
    _run_stages(_attn_sample_stages(q_ref, kn_ref, vn_ref, kt_ref, vt_ref, mc_ref, mn_ref, o_ref),
                _mlp_stages(x_ref, oa_ref, ob_ref, wo_ref, n2_ref, wu_ref, wd_ref, y_ref))


def _mlp_attn(x, oa, ob, p, qb, kn, vn, cache_kt, cache_vt):
    n, d = x.shape
    nb, nt, _ = qb.shape
    _, nh, dh, wbuf = cache_kt.shape
    tm = n // nb
    assert n % nb == 0 and tm % SUBLANES == 0
    masks = [jnp.asarray(m) for m in _sample_masks(wbuf, nt)]
    row = lambda w: pl.BlockSpec((tm, w), lambda i: (i, 0))
    const = lambda a: pl.BlockSpec(a.shape, lambda i: (0, 0), pipeline_mode=pl.Buffered(1))
    tok = pl.BlockSpec((1, nt, WIDTH_B), lambda i: (i, 0, 0))
    cache = pl.BlockSpec((1, nh, dh, wbuf), lambda i: (i, 0, 0, 0))
    return pl.pallas_call(
        _mlp_attn_kernel,
        grid=(nb,),
        in_specs=[row(d), row(WIDTH_A), row(WIDTH_B), const(p["w_o"]), const(p["n2"]),
                  const(p["w_up"]), const(p["w_down"]), tok, tok, tok, cache, cache]
        + [const(m) for m in masks],
        out_specs=[row(d), tok],
        out_shape=[jax.ShapeDtypeStruct((n, d), F32),
                   jax.ShapeDtypeStruct((nb, nt, WIDTH_B), F32)],
        compiler_params=pltpu.CompilerParams(
            dimension_semantics=("arbitrary",), vmem_limit_bytes=VMEM_LIMIT),
        name="mlp_attn",
    )(x, oa, ob, p["w_o"], p["n2"], p["w_up"], p["w_down"], qb, kn, vn, cache_kt, cache_vt, *masks)


def _layer_params(norm1_g, w_in, conv_w, a_log, dt_bias, delta_norm_g, q_norm_g, k_norm_g, w_o,
                  norm2_g, w_up, w_down):
    d = w_in.shape[0]
    n_gate = 2 * N_HEADS_A
    gate0 = CONV_CH + WIDTH_A
    w_a = w_in[:, :gate0].astype(BF16)
    w_b = w_in[:, gate0 + n_gate:].astype(BF16)
    w_g = jnp.pad(w_in[:, gate0:gate0 + n_gate], ((0, 0), (0, LANES - n_gate))).astype(BF16)
    lane_pad = lambda a: jnp.zeros((1, LANES), F32).at[0, N_HEADS_A:n_gate].set(a.astype(F32))
    hid = np.arange(WIDTH_B) // HEAD_DIM_B
    head_mean = jnp.asarray((hid[:, None] == hid[None, :]).astype(np.float32) / HEAD_DIM_B, BF16)
    return {
        "n1": norm1_g.reshape(1, d).astype(F32), "w_a": w_a, "w_b": w_b, "w_g": w_g, "conv_w": conv_w.astype(F32),
        "alog": lane_pad(a_log), "dtb": lane_pad(dt_bias),
        "qng": jnp.tile(q_norm_g.astype(F32), N_HEADS_B).reshape(1, WIDTH_B),
        "kng": jnp.tile(k_norm_g.astype(F32), N_HEADS_B).reshape(1, WIDTH_B),
        "hm": head_mean, "dng": delta_norm_g.reshape(1, HEAD_DIM_A).astype(F32),
        "w_o": w_o.astype(BF16), "n2": norm2_g.reshape(1, d).astype(F32),
        "w_up": w_up.astype(BF16), "w_down": w_down.astype(BF16),
    }


MLP_TILE = 256
FUSED_MLP_MAX_TILE = 256


def _layer(xp, xs, state_conv, s0_s, cache_k, cache_v, p):
    b, t, d = xp.shape
    nb, nt, _ = xs.shape
    n = b * t
    pbuf = min(MAX_WINDOW, t)

    qa, ka, va, z, gt, kp, vp, conv_p, *dil = _proj_prompt(xp, p, tm=256, pbuf=pbuf)
    window = lambda a: a.reshape(b, N_HEADS_B, HEAD_DIM_B, pbuf).transpose(0, 3, 1, 2)
    kp, vp = window(kp), window(vp)
    s0_p = jnp.zeros((b, N_HEADS_A, HEAD_DIM_A, HEAD_DIM_A), F32)
    oa_p, s_p = _delta(qa, ka, va, z, gt, s0_p, p["dng"], chunk=DELTA_CHUNK, nchunk=2, bb=b,
                       out_dtype=BF16)
    ob_p = _attn_prompt(dil[0:3], dil[3:6], dil[6:9])

    qa, ka, va, z, gt, qb, kn, vn, conv_s = _proj_sample(xs, state_conv, p, bt=min(nb, 64))
    oa_s, s_s = _delta(qa, ka, va, z, gt, s0_s, p["dng"], chunk=nt, nchunk=1, bb=min(nb, 16),
                       out_dtype=F32)
    ckt, cvt = cache_k.transpose(0, 2, 3, 1), cache_v.transpose(0, 2, 3, 1)

    mlp_in = (xp.reshape(n, d), oa_p.reshape(n, WIDTH_A), ob_p.reshape(n, WIDTH_B), p)
    tile = n // nb
    if n % nb == 0 and tile % SUBLANES == 0 and tile <= FUSED_MLP_MAX_TILE:
        yp, ob_s = _mlp_attn(*mlp_in, qb, kn, vn, ckt, cvt)
    else:
        yp = _mlp(*mlp_in, tm=MLP_TILE)
        ob_s = _attn_sample(qb, kn, vn, ckt, cvt)
    ys = _mlp(xs.reshape(nb * nt, d), oa_s.reshape(nb * nt, WIDTH_A),
              ob_s.reshape(nb * nt, WIDTH_B), p, tm=min(MLP_TILE, nb * nt))
    heads = lambda a: a.reshape(nb, nt, N_HEADS_B, HEAD_DIM_B)
    return ((yp.reshape(b, t, d), kp, vp, s_p, conv_p),
            (ys.reshape(nb, nt, d), heads(kn), heads(vn), s_s, conv_s))


def kernel(x_prompt, x_sample, cache_swa_k, cache_swa_v, state_delta, state_conv, norm1_g, w_in,
           conv_w, a_log, dt_bias, delta_norm_g, q_norm_g, k_norm_g, w_o, norm2_g, w_up, w_down):
    depth = w_in.shape[0]
    b, s, _ = x_prompt.shape
    nb, nt, _ = x_sample.shape
    wbuf = cache_swa_k.shape[2]
    assert s % ATTN_TILE == 0 and nt == SUBLANES and wbuf == MAX_WINDOW
    yp, ys = x_prompt, x_sample
    outs = [[] for _ in range(8)]
    for layer in range(depth):
        p = _layer_params(norm1_g[layer], w_in[layer], conv_w[layer], a_log[layer], dt_bias[layer],
                          delta_norm_g[layer], q_norm_g[layer], k_norm_g[layer], w_o[layer],
                          norm2_g[layer], w_up[layer], w_down[layer])
        (yp, kp, vp, dp, cp), (ys, kn, vn, dn, cn) = _layer(
            yp, ys, state_conv[layer], state_delta[layer], cache_swa_k[layer], cache_swa_v[layer], p)
        for lst, val in zip(outs, (kp, vp, dp, cp, kn, vn, dn, cn)):
            lst.append(val)
    return (yp, ys) + tuple(jnp.stack(o) for o in outs)
```

```python
import functools

import numpy as np
import jax
import jax.numpy as jnp
from jax import lax
from jax.experimental import pallas as pl
from jax.experimental.pallas import tpu as pltpu

F32 = jnp.float32
BF16 = jnp.bfloat16

N_HEADS_A = 4
HEAD_DIM_A = 128
WIDTH_A = N_HEADS_A * HEAD_DIM_A
N_HEADS_B = 8
HEAD_DIM_B = 64
WIDTH_B = N_HEADS_B * HEAD_DIM_B
CONV_WIDTH = 4
CONV_CH = 3 * WIDTH_A
DELTA_CHUNK = 64
MAX_WINDOW = 2048
NORM_EPS = 1e-6
LANES = 128
SUBLANES = 8
NEG = -1e30
LOG2_E = 1.4426950408889634

C_QKV, C_Z = 0, CONV_CH
C_QB, C_KB, C_VB = 0, WIDTH_B, 2 * WIDTH_B

ATTN_DILATIONS = (1, 4, 16)
ATTN_BAND = 128
ATTN_TILE = 2048
ATTN_UNROLL = 2
VMEM_LIMIT = 56 * 1024 * 1024


def _dot(a, b):
    return jnp.dot(a.astype(BF16), b.astype(BF16), preferred_element_type=F32)


def _dot_nt(a, b):
    return lax.dot_general(a.astype(BF16), b.astype(BF16), (((1,), (1,)), ((), ())),
                           preferred_element_type=F32)


def _dot_tn(a, b):
    return lax.dot_general(a.astype(BF16), b.astype(BF16), (((0,), (0,)), ((), ())),
                           preferred_element_type=F32)


def _split2(x):
    hi = x.astype(BF16)
    lo = (x - hi.astype(F32)).astype(BF16)
    return hi, lo


def _split3(x):
    hi = x.astype(BF16)
    r = x - hi.astype(F32)
    mid = r.astype(BF16)
    lo = (r - mid.astype(F32)).astype(BF16)
    return hi, mid, lo


def _dot_exact_lhs(mask_bf16, x):
    hi, mid, lo = _split3(x)
    d = lambda p: jnp.dot(mask_bf16, p, preferred_element_type=F32)
    return d(hi) + d(mid) + d(lo)


def _sigmoid(x):
    return 0.5 * jnp.tanh(0.5 * x) + 0.5


def _silu(x):
    return x * _sigmoid(x)


def _softplus(x):
    return jnp.maximum(x, 0.0) + jnp.log1p(jnp.exp(-jnp.abs(x)))


def _rmsnorm(x, g):
    return x * lax.rsqrt(jnp.mean(x * x, axis=-1, keepdims=True) + NORM_EPS) * g


def _proj_body(x, ext_ref, shift, pad, refs):
    n1_ref, wa_ref, wb_ref, wg_ref, cw_ref, al_ref, dtb_ref, qng_ref, kng_ref, hm_ref = refs
    tm = x.shape[0]
    hn = _rmsnorm(x, n1_ref[...]).astype(BF16)

    proj = lambda w: jnp.dot(hn, w, preferred_element_type=F32)
    u = proj(wa_ref[:, C_QKV:C_QKV + CONV_CH])
    qb = proj(wb_ref[:, C_QB:C_QB + WIDTH_B])
    kb = proj(wb_ref[:, C_KB:C_KB + WIDTH_B])
    vb = proj(wb_ref[:, C_VB:C_VB + WIDTH_B])
    z = proj(wa_ref[:, C_Z:C_Z + WIDTH_A])
    gc = proj(wg_ref[...])

    ext_ref[pad:pad + tm, :] = u
    cw = cw_ref[...]
    y = u * cw[3:4, :]
    for i in range(CONV_WIDTH - 1):
        off = pad - (CONV_WIDTH - 1 - i) * shift
        y = y + ext_ref[off:off + tm, :] * cw[i:i + 1, :]
    y = _silu(y)
    qa, ka = [], []
    for h in range(N_HEADS_A):
        lo = h * HEAD_DIM_A
        qh = y[:, lo:lo + HEAD_DIM_A]
        qa.append(qh * (lax.rsqrt(
            jnp.sum(qh * qh, axis=-1, keepdims=True) + NORM_EPS) * HEAD_DIM_A ** -0.5))
        kh = y[:, WIDTH_A + lo:WIDTH_A + lo + HEAD_DIM_A]
        ka.append(kh * lax.rsqrt(jnp.sum(kh * kh, axis=-1, keepdims=True) + NORM_EPS))
    qa = jnp.concatenate(qa, axis=1)
    ka = jnp.concatenate(ka, axis=1)
    va = y[:, 2 * WIDTH_A:3 * WIDTH_A]

    hm = hm_ref[...]

    def headnorm(v, g):
        ms = jnp.dot((v * v).astype(BF16), hm, preferred_element_type=F32)
        return v * lax.rsqrt(ms + NORM_EPS) * g

    qb = headnorm(qb, qng_ref[...])
    kb = headnorm(kb, kng_ref[...])

    lane = lax.broadcasted_iota(jnp.int32, gc.shape, 1)
    beta = _sigmoid(gc)
    g = -jnp.exp(al_ref[...]) * _softplus(gc + dtb_ref[...])
    gates = jnp.where(lane < N_HEADS_A, beta, g)
    return qa, ka, va, z, gates, qb, kb, vb


N_PROJ_PARAMS = 10
N_ATTN_SAMPLE_INPUTS = 7


def _proj_prompt_kernel(*refs, first_win, with_attn):
    x_ref, params = refs[0], refs[1:1 + N_PROJ_PARAMS]
    pos = 1 + N_PROJ_PARAMS
    attn_in = refs[pos:pos + N_ATTN_SAMPLE_INPUTS] if with_attn else ()
    pos += len(attn_in)
    outs = refs[pos:pos + 17]
    pos += 17
    attn_out = refs[pos:pos + 1] if with_attn else ()
    pos += len(attn_out)
    scratch = refs[pos:]
    stages = [_proj_prompt_stages(x_ref, params, outs, scratch, first_win)]
    if with_attn:
        stages.insert(0, _attn_sample_stages(*attn_in, *attn_out))
    _run_stages(*stages)


def _proj_prompt_stages(x_ref, params, outs, scratch, first_win):
    (qa_ref, ka_ref, va_ref, z_ref, gt_ref, kb_ref, vb_ref, cn_ref,
     q1_ref, q4_ref, q16_ref, k1_ref, k4_ref, k16_ref, v1_ref, v4_ref, v16_ref) = outs
    ext_ref, dil_ref = scratch
    t = pl.program_id(1)
    tm = x_ref.shape[1]

    @pl.when(t == 0)
    def _():
        ext_ref[0:SUBLANES, :] = jnp.zeros((SUBLANES, CONV_CH), F32)

    @pl.when(t > 0)
    def _():
        ext_ref[0:SUBLANES, :] = ext_ref[tm:tm + SUBLANES, :]

    qa, ka, va, z, gates, qb, kb, vb = _proj_body(x_ref[0], ext_ref, 1, SUBLANES, params)
    yield
    for r, v in zip((qa_ref, ka_ref, va_ref, z_ref, gt_ref), (qa, ka, va, z, gates)):
        r[0] = v
    cn_ref[0] = ext_ref[tm + SUBLANES - (CONV_WIDTH - 1):tm + SUBLANES, :]

    @pl.when(t >= first_win)
    def _():
        kb_ref[0] = kb.T
        vb_ref[0] = vb.T

    yield
    nchunk = WIDTH_B // LANES
    q_att = qb * (HEAD_DIM_B ** -0.5 * LOG2_E)
    for ai, (val, r1, r4, r16) in enumerate(((q_att, q1_ref, q4_ref, q16_ref),
                                              (kb, k1_ref, k4_ref, k16_ref),
                                              (vb, v1_ref, v4_ref, v16_ref))):
        r1[0] = val.astype(BF16)
        for c in range(nchunk):
            dil_ref[0, ai * nchunk + c] = val[:, c * LANES:(c + 1) * LANES]
        n4, n16 = tm // 4, tm // 16
        for r in range(4):
            parts = [dil_ref[0, ai * nchunk + c, pl.ds(r, n4, stride=4), :] for c in range(nchunk)]
            r4[0, r] = jnp.concatenate(parts, axis=1).astype(BF16)
            for c in range(nchunk):
                dil_ref[1, ai * nchunk + c, r * n4:(r + 1) * n4, :] = parts[c]
        for r in range(4):
            for r2 in range(4):
                r16[0, r + 4 * r2] = jnp.concatenate(
                    [dil_ref[1, ai * nchunk + c, pl.ds(r * n4 + r2, n16, stride=4), :]
                     for c in range(nchunk)], axis=1).astype(BF16)


def _proj_sample_kernel(x_ref, st_ref, n1_ref, wa_ref, wb_ref, wg_ref, cw_ref, al_ref, dtb_ref,
                        qng_ref, kng_ref, hm_ref, qa_ref, ka_ref, va_ref, z_ref, gt_ref, qb_ref, kb_ref, vb_ref,
                        cn_ref, ext_ref):
    nt, nb, d = x_ref.shape
    ncv = CONV_WIDTH - 1
    tm = nb * nt
    pad = ncv * nb
    ext_ref[0:pad, :] = st_ref[...].reshape(pad, CONV_CH)
    vals = _proj_body(x_ref[...].reshape(tm, d), ext_ref, nb, pad,
                      (n1_ref, wa_ref, wb_ref, wg_ref, cw_ref, al_ref, dtb_ref, qng_ref, kng_ref,
                       hm_ref))
    for r, v in zip((qa_ref, ka_ref, va_ref, z_ref, gt_ref, qb_ref, kb_ref, vb_ref), vals):
        r[...] = v.reshape(r.shape)
    cn_ref[...] = ext_ref[tm:tm + pad, :].reshape(ncv, nb, CONV_CH)


def _proj_params(p):
    full = lambda a: pl.BlockSpec(a.shape, lambda *_: (0,) * a.ndim,
                                  pipeline_mode=pl.Buffered(1))
    arrs = (p["n1"], p["w_a"], p["w_b"], p["w_g"], p["conv_w"], p["alog"], p["dtb"], p["qng"],
            p["kng"], p["hm"])
    return arrs, [full(a) for a in arrs]


def _proj_prompt(x, p, tm, pbuf, attn=None):
    b, t, d = x.shape
    assert pbuf % tm == 0 and t % tm == 0
    first_win = (t - pbuf) // tm
    arrs, specs = _proj_params(p)
    assert len(arrs) == N_PROJ_PARAMS
    steps_t = t // tm
    attn_args, attn_in_specs, attn_out_shape, attn_out_specs = [], [], [], []
    if attn is not None:
        qb, kn, vn, ckt, cvt = attn
        assert b * steps_t <= qb.shape[0]
        masks, mk = _attn_sample_specs(qb, ckt, 0)
        tok, cache, tok_out, mspecs = mk(lambda i, j: i * steps_t + j)
        attn_args = [qb, kn, vn, ckt, cvt] + masks
        attn_in_specs = [tok, tok, tok, cache, cache] + mspecs
        assert len(attn_args) == N_ATTN_SAMPLE_INPUTS
        attn_out_shape = [jax.ShapeDtypeStruct((b * steps_t, qb.shape[1], WIDTH_B), F32)]
        attn_out_specs = [tok_out]
    row = lambda w: pl.BlockSpec((1, tm, w), lambda i, j: (i, j, 0))
    widths = (WIDTH_A, WIDTH_A, WIDTH_A, WIDTH_A, LANES)
    out_shape = [jax.ShapeDtypeStruct((b, t, w), F32) for w in widths]
    out_specs = [row(w) for w in widths]
    for _ in range(2):
        out_shape.append(jax.ShapeDtypeStruct((b, WIDTH_B, pbuf), F32))
        out_specs.append(pl.BlockSpec((1, WIDTH_B, tm),
                                      lambda i, j: (i, 0, jnp.maximum(j - first_win, 0))))
    out_shape.append(jax.ShapeDtypeStruct((b, CONV_WIDTH - 1, CONV_CH), F32))
    out_specs.append(pl.BlockSpec((1, CONV_WIDTH - 1, CONV_CH), lambda i, j: (i, 0, 0)))
    for _ in range(3):
        out_shape.append(jax.ShapeDtypeStruct((b, t, WIDTH_B), BF16))
        out_specs.append(row(WIDTH_B))
        for dil in ATTN_DILATIONS[1:]:
            out_shape.append(jax.ShapeDtypeStruct((b, dil, t // dil, WIDTH_B), BF16))
            out_specs.append(pl.BlockSpec((1, dil, tm // dil, WIDTH_B), lambda i, j: (i, 0, j, 0)))
    return pl.pallas_call(
        functools.partial(_proj_prompt_kernel, first_win=first_win, with_attn=attn is not None),
        grid=(b, steps_t),
        in_specs=[row(d)] + specs + attn_in_specs,
        out_specs=out_specs + attn_out_specs,
        out_shape=out_shape + attn_out_shape,
        scratch_shapes=[pltpu.VMEM((tm + SUBLANES, CONV_CH), F32),
                        pltpu.VMEM((2, 3 * WIDTH_B // LANES, tm, LANES), F32)],
        compiler_params=pltpu.CompilerParams(
            dimension_semantics=("arbitrary", "arbitrary"), vmem_limit_bytes=VMEM_LIMIT),
        name="proj_prompt",
    )(x, *arrs, *attn_args)


def _proj_sample(x, state_conv, p, bt):
    nb, nt, d = x.shape
    ncv = CONV_WIDTH - 1
    arrs, specs = _proj_params(p)
    blk = lambda r, w: pl.BlockSpec((r, bt, w), lambda i: (0, i, 0))
    widths = (WIDTH_A, WIDTH_A, WIDTH_A, WIDTH_A, LANES, WIDTH_B, WIDTH_B, WIDTH_B)
    out_shape = [jax.ShapeDtypeStruct((nt, nb, w), F32) for w in widths]
    out_shape.append(jax.ShapeDtypeStruct((ncv, nb, CONV_CH), F32))
    out_specs = [blk(nt, w) for w in widths] + [blk(ncv, CONV_CH)]
    outs = pl.pallas_call(
        _proj_sample_kernel,
        grid=(nb // bt,),
        in_specs=[blk(nt, d), blk(ncv, CONV_CH)] + specs,
        out_specs=out_specs,
        out_shape=out_shape,
        scratch_shapes=[pltpu.VMEM((bt * (nt + ncv), CONV_CH), F32)],
        compiler_params=pltpu.CompilerParams(
            dimension_semantics=("arbitrary",), vmem_limit_bytes=VMEM_LIMIT),
        name="proj_sample",
    )(x.transpose(1, 0, 2), state_conv.transpose(1, 0, 2), *arrs)
    return [o.transpose(1, 0, 2) for o in outs]


def _delta_kernel(q_ref, k_ref, v_ref, z_ref, gt_ref, s0_ref, ng_ref, o_ref, sf_ref, s_scr,
                  *, chunk, nchunk, bb):
    c = chunk
    t = pl.program_id(1)

    @pl.when(t == 0)
    def _():
        s_scr[...] = s0_ref[...]

    ii = lax.broadcasted_iota(jnp.int32, (c, c), 0)
    jj = lax.broadcasted_iota(jnp.int32, (c, c), 1)
    causal = ii >= jj
    strict = ii > jj
    tril_bf = causal.astype(F32).astype(BF16)
    eye = (ii == jj).astype(F32)
    level_masks = []
    s = 1
    while s < c:
        level_masks.append(((ii // (2 * s)) == (jj // (2 * s))) & ((ii & s) != 0) & ((jj & s) == 0))
        s *= 2
    ng = ng_ref[...]

    heads = range(N_HEADS_A)
    items = [(bi, ci, h) for bi in range(bb) for ci in range(nchunk) for h in heads]
    pre = {}
    for bi in range(bb):
        for ci in range(nchunk):
            rows = slice(ci * c, (ci + 1) * c)
            gt = gt_ref[bi, rows, :]
            gcum = _dot_exact_lhs(tril_bf, gt)
            if c < LANES:
                gpad = jnp.concatenate([gcum, jnp.zeros((LANES - c, LANES), F32)], axis=0)
            else:
                gpad = gcum
            gcum_t = gpad.T
            for h in heads:
                lo = h * HEAD_DIM_A
                q = q_ref[bi, rows, lo:lo + HEAD_DIM_A]
                k = k_ref[bi, rows, lo:lo + HEAD_DIM_A]
                v = v_ref[bi, rows, lo:lo + HEAD_DIM_A]
                beta = gt[:, h:h + 1]
                g_col = gcum[:, N_HEADS_A + h:N_HEADS_A + h + 1]
                g_row = gcum_t[N_HEADS_A + h:N_HEADS_A + h + 1, 0:c]
                g_last = gcum[c - 1:c, N_HEADS_A + h:N_HEADS_A + h + 1]
                decay = jnp.where(causal, jnp.exp(jnp.where(causal, g_col - g_row, 0.0)), 0.0)
                exp_g = jnp.exp(g_col)
                kb = k * beta
                pre[bi, ci, h] = dict(
                    q=q, k=k, kb=kb, decay=decay, g_last=g_last,
                    rhs=jnp.concatenate([v * beta, kb * exp_g], axis=1),
                    q_dec=q * exp_g, k_dec=k * jnp.exp(g_last - g_col))
    lmat = {it: jnp.where(strict, _dot_nt(pre[it]["kb"], pre[it]["k"]) * pre[it]["decay"], 0.0)
            for it in items}
    qk = {it: _dot_nt(pre[it]["q"], pre[it]["k"]) * pre[it]["decay"] for it in items}
    tinv = {it: eye - jnp.where(level_masks[0], lmat[it], 0.0) for it in items}
    for msk in level_masks[1:]:
        te = {it: _dot(tinv[it], jnp.where(msk, lmat[it], 0.0)) for it in items}
        tinv = {it: tinv[it] - _dot(te[it], tinv[it]) for it in items}
    uw = {it: _dot(tinv[it], pre[it]["rhs"]) for it in items}

    for bi in range(bb):
        st = {h: s_scr[bi, h] for h in heads}
        for ci in range(nchunk):
            rows = slice(ci * c, (ci + 1) * c)
            ws = {h: _dot(jnp.concatenate([uw[bi, ci, h][:, HEAD_DIM_A:2 * HEAD_DIM_A],
                                           pre[bi, ci, h]["q_dec"]], axis=0), st[h])
                  for h in heads}
            v_new = {h: uw[bi, ci, h][:, 0:HEAD_DIM_A] - ws[h][0:c] for h in heads}
            o = {h: ws[h][c:2 * c] + _dot(qk[bi, ci, h], v_new[h]) for h in heads}
            st = {h: st[h] * jnp.exp(pre[bi, ci, h]["g_last"])
                  + _dot_tn(pre[bi, ci, h]["k_dec"], v_new[h]) for h in heads}
            for h in heads:
                lo = h * HEAD_DIM_A
                zz = z_ref[bi, rows, lo:lo + HEAD_DIM_A]
                o_ref[bi, rows, lo:lo + HEAD_DIM_A] = (
                    _rmsnorm(o[h], ng) * _silu(zz)).astype(o_ref.dtype)
        for h in heads:
            s_scr[bi, h] = st[h]

    @pl.when(t == pl.num_programs(1) - 1)
    def _():
        sf_ref[...] = s_scr[...]


def _delta(qa, ka, va, z, gt, s0, ng, *, chunk, nchunk, bb, out_dtype):
    b, t, _ = qa.shape
    tc = chunk * nchunk
    row = lambda w: pl.BlockSpec((bb, tc, w), lambda i, j: (i, j, 0))
    sspec = pl.BlockSpec((bb, N_HEADS_A, HEAD_DIM_A, HEAD_DIM_A), lambda i, j: (i, 0, 0, 0))
    return pl.pallas_call(
        functools.partial(_delta_kernel, chunk=chunk, nchunk=nchunk, bb=bb),
        grid=(b // bb, t // tc),
        in_specs=[row(WIDTH_A)] * 4 + [row(LANES), sspec,
                                       pl.BlockSpec((1, HEAD_DIM_A), lambda i, j: (0, 0))],
        out_specs=[row(WIDTH_A), sspec],
        out_shape=[jax.ShapeDtypeStruct((b, t, WIDTH_A), out_dtype),
                   jax.ShapeDtypeStruct((b, N_HEADS_A, HEAD_DIM_A, HEAD_DIM_A), F32)],
        scratch_shapes=[pltpu.VMEM((bb, N_HEADS_A, HEAD_DIM_A, HEAD_DIM_A), F32)],
        compiler_params=pltpu.CompilerParams(
            dimension_semantics=("arbitrary", "arbitrary"), vmem_limit_bytes=VMEM_LIMIT),
        name="delta_c%d" % chunk,
    )(qa, ka, va, z, gt, s0, ng)


def _attn_prompt_kernel(q1, q4, q16, k1p, k1c, k4p, k4c, k16p, k16c, v1p, v1c, v4p, v4c, v16p, v16c,
                        o_ref, acc, mrep, *, tq):
    t = pl.program_id(2)
    rr = ATTN_BAND
    ii = lax.broadcasted_iota(jnp.int32, (rr, rr), 0)
    jj = lax.broadcasted_iota(jnp.int32, (rr, rr), 1)
    lower = jj <= ii
    upper = jj >= ii
    lane = lax.broadcasted_iota(jnp.int32, (rr, LANES), 1)
    own = [(lane >= hh * HEAD_DIM_B) & (lane < (hh + 1) * HEAD_DIM_B) for hh in range(2)]
    pen_t = jnp.where(t > 0, jnp.float32(0.0), jnp.float32(NEG))
    one = jnp.ones((rr, LANES), BF16)
    zero = jnp.zeros((rr, LANES), BF16)

    ld = lambda ref_idx: ref_idx[0][ref_idx[1]]

    def weights(units):
        chains = [(j, hh) for j in range(len(units)) for hh in range(2)]
        qv = {ch: jnp.where(own[ch[1]], ld(units[ch[0]]["q"]), zero) for ch in chains}
        kc = [ld(u["kc"]) for u in units]
        kp = [ld(u["kp"]) for u in units]
        raw = {ch: (_dot_nt(qv[ch], kc[ch[0]]), _dot_nt(qv[ch], kp[ch[0]])) for ch in chains}
        out = {}
        for ch in chains:
            s_c = jnp.where(lower, raw[ch][0], NEG)
            s_p = raw[ch][1]
            if units[ch[0]]["pen"] is not None:
                s_p = s_p + units[ch[0]]["pen"]
            s_p = jnp.where(upper, s_p, NEG)
            m = jnp.max(jnp.maximum(s_c, s_p), axis=-1, keepdims=True)
            out[ch] = (jnp.exp2(s_c - m).astype(BF16), jnp.exp2(s_p - m).astype(BF16), m)
        return out

    def accumulate(units, wts, first):
        vc = [ld(u["vc"]) for u in units]
        vp = [ld(u["vp"]) for u in units]
        pv = {ch: (jnp.dot(p_c, jnp.where(own[ch[1]], vc[ch[0]], one), preferred_element_type=F32)
                   + jnp.dot(p_p, jnp.where(own[ch[1]], vp[ch[0]], one),
                             preferred_element_type=F32))
              for ch, (p_c, p_p, _) in wts.items()}
        for ch, (_, _, m) in wts.items():
            j, hh = ch
            rows = units[j]["rows"]
            m_b = jnp.broadcast_to(m, (rr, LANES))
            if first:
                acc[hh, rows, :] = pv[ch]
                mrep[hh, rows, :] = m_b
            else:
                m_old = mrep[hh, rows, :]
                m_new = jnp.maximum(m_old, m_b)
                acc[hh, rows, :] = (acc[hh, rows, :] * jnp.exp2(m_old - m_new)
                                    + pv[ch] * jnp.exp2(m_b - m_new))
                mrep[hh, rows, :] = m_new

    uu = ATTN_UNROLL
    blk = lambda j: slice(j * rr, (j + 1) * rr)

    d16, d4 = ATTN_DILATIONS[2], ATTN_DILATIONS[1]
    assert tq // d16 == rr and (tq // (d4 * rr)) % uu == 0 and d16 % uu == 0
    trips = []
    full = slice(None)
    for it in range(d16 // uu):
        trips.append((True, [dict(
            q=(q16, (0, r)), kc=(k16c, (0, r)), kp=(k16p, (0, r)), vc=(v16c, (0, r)),
            vp=(v16p, (0, r)), pen=pen_t, rows=pl.ds(r, rr, stride=d16))
            for r in range(it * uu, (it + 1) * uu)]))
    for it in range(tq // (rr * uu)):
        units = []
        for j in range(it * uu, (it + 1) * uu):
            cur = (0, blk(j), full)
            prv = (0, blk(j - 1), full) if j > 0 else (0, slice(tq - rr, tq), full)
            units.append(dict(q=(q1, cur), kc=(k1c, cur), vc=(v1c, cur),
                              kp=(k1c if j > 0 else k1p, prv), vp=(v1c if j > 0 else v1p, prv),
                              pen=None if j > 0 else pen_t, rows=pl.ds(j * rr, rr)))
        trips.append((False, units))
    nb4 = tq // d4 // rr
    for r in range(d4):
        for jt in range(nb4 // uu):
            units = []
            for j in range(jt * uu, (jt + 1) * uu):
                cur = (0, r, blk(j), full)
                prv = (0, r, blk(j - 1), full) if j > 0 else (0, r, blk(nb4 - 1), full)
                units.append(dict(q=(q4, cur), kc=(k4c, cur), vc=(v4c, cur),
                                  kp=(k4c if j > 0 else k4p, prv),
                                  vp=(v4c if j > 0 else v4p, prv),
                                  pen=None if j > 0 else pen_t,
                                  rows=pl.ds(r + d4 * rr * j, rr, stride=d4)))
            trips.append((False, units))

    pending = None
    for first, units in trips:
        wts = weights(units)
        if pending is not None:
            accumulate(*pending)
        pending = (units, wts, first)
    accumulate(*pending)

    lane_t = lax.broadcasted_iota(jnp.int32, (tq, LANES), 1)
    a0 = acc[0]
    a1 = acc[1]
    l0 = a0[:, HEAD_DIM_B:HEAD_DIM_B + 1]
    l1 = a1[:, 0:1]
    o_ref[0] = jnp.where(lane_t < HEAD_DIM_B, a0 / l0, a1 / l1).astype(o_ref.dtype)


def _attn_prompt(q, k, v):
    b, t, _ = q[0].shape
    tq = ATTN_TILE
    nt = t // tq
    prev_j = lambda j: jnp.maximum(j - 1, 0)
    cur, prev = [], []
    for dil in ATTN_DILATIONS:
        if dil == 1:
            cur.append(pl.BlockSpec((1, tq, LANES), lambda i, h, j: (i, j, h)))
            prev.append(pl.BlockSpec((1, tq, LANES), lambda i, h, j: (i, prev_j(j), h)))
        else:
            cur.append(pl.BlockSpec((1, dil, tq // dil, LANES), lambda i, h, j: (i, 0, j, h)))
            prev.append(pl.BlockSpec((1, dil, tq // dil, LANES),
                                     lambda i, h, j: (i, 0, prev_j(j), h)))
    kv_specs = [s for pc in zip(prev, cur) for s in pc]
    kv_args = lambda x: [a for xd in x for a in (xd, xd)]
    return pl.pallas_call(
        functools.partial(_attn_prompt_kernel, tq=tq),
        grid=(b, WIDTH_B // LANES, nt),
        in_specs=cur + kv_specs + kv_specs,
        out_specs=cur[0],
        out_shape=jax.ShapeDtypeStruct((b, t, WIDTH_B), BF16),
        scratch_shapes=[pltpu.VMEM((2, tq, LANES), F32), pltpu.VMEM((2, tq, LANES), F32)],
        compiler_params=pltpu.CompilerParams(
            dimension_semantics=("arbitrary", "arbitrary", "arbitrary"),
            vmem_limit_bytes=VMEM_LIMIT),
        name="attn_prompt",
    )(*q, *kv_args(k), *kv_args(v))


def _multiplicity(delta):
    delta = np.asarray(delta)
    ok = delta >= 0
    m = ((delta <= 128).astype(np.float32)
         + ((delta <= 512) & (delta % 4 == 0)).astype(np.float32)
         + ((delta <= 2048) & (delta % 16 == 0)).astype(np.float32))
    return np.where(ok, m, 0.0).astype(np.float32)


def _sample_masks(wbuf, nt):
    i = np.arange(nt)[:, None]
    return (_multiplicity(wbuf + i - np.arange(wbuf)[None, :]),
            _multiplicity(i - np.arange(nt)[None, :]))


def _run_stages(*gens):
    gens = list(gens)
    while gens:
        for g in list(gens):
            if next(g, StopIteration) is StopIteration:
                gens.remove(g)


def _attn_sample_kernel(*refs):
    _run_stages(_attn_sample_stages(*refs))


def _attn_sample_stages(q_ref, kn_ref, vn_ref, kt_ref, vt_ref, mc_ref, mn_ref, o_ref):
    dh = HEAD_DIM_B
    heads = range(N_HEADS_B)
    mc = mc_ref[...]
    mn = mn_ref[...]
    sl = lambda r, h: r[0, :, h * dh:(h + 1) * dh]
    q = {h: (sl(q_ref, h) * (dh ** -0.5)).astype(BF16) for h in heads}
    s_c = {h: jnp.where(mc > 0.0, _dot(q[h], kt_ref[0, h]), NEG) for h in heads}
    s_n = {h: jnp.where(mn > 0.0, _dot_nt(q[h], sl(kn_ref, h)), NEG) for h in heads}
    yield
    m = {h: jnp.maximum(jnp.max(s_c[h], axis=-1, keepdims=True),
                        jnp.max(s_n[h], axis=-1, keepdims=True)) for h in heads}
    p_c = {h: (mc * jnp.exp(s_c[h] - m[h])).astype(BF16) for h in heads}
    p_n = {h: (mn * jnp.exp(s_n[h] - m[h])).astype(BF16) for h in heads}
    den = {h: (jnp.sum(p_c[h].astype(F32), axis=-1, keepdims=True)
               + jnp.sum(p_n[h].astype(F32), axis=-1, keepdims=True)) for h in heads}
    yield
    out = {h: _dot_nt(p_c[h], vt_ref[0, h]) + _dot(p_n[h], sl(vn_ref, h)) for h in heads}
    yield
    o_ref[0] = jnp.concatenate([out[h] / den[h] for h in heads], axis=1)


def _attn_sample(qb, kn, vn, cache_kt, cache_vt):
    nb, nt, _ = qb.shape
    _, nh, dh, wbuf = cache_kt.shape
    masks = [jnp.asarray(m) for m in _sample_masks(wbuf, nt)]
    tok = pl.BlockSpec((1, nt, WIDTH_B), lambda i: (i, 0, 0))
    cache = pl.BlockSpec((1, nh, dh, wbuf), lambda i: (i, 0, 0, 0))
    mspec = lambda m: pl.BlockSpec(m.shape, lambda i: (0, 0))
    return pl.pallas_call(
        _attn_sample_kernel,
        grid=(nb,),
        in_specs=[tok, tok, tok, cache, cache] + [mspec(m) for m in masks],
        out_specs=tok,
        out_shape=jax.ShapeDtypeStruct((nb, nt, WIDTH_B), F32),
        compiler_params=pltpu.CompilerParams(
            dimension_semantics=("arbitrary",), vmem_limit_bytes=VMEM_LIMIT),
        name="attn_sample",
    )(qb, kn, vn, cache_kt, cache_vt, *masks)


FF_CHUNK = 1024


def _mlp_kernel(*refs):
    _run_stages(_mlp_stages(*refs))


def _mlp_stages(x_ref, oa_ref, ob_ref, wo_ref, n2_ref, wu_ref, wd_ref, y_ref):
    mix = jnp.concatenate([oa_ref[...].astype(BF16), ob_ref[...].astype(BF16)], axis=1)
    h1 = x_ref[...] + jnp.dot(mix, wo_ref[...], preferred_element_type=F32)
    hn = _rmsnorm(h1, n2_ref[...]).astype(BF16)
    yield
    chunks = range(0, wu_ref.shape[1], FF_CHUNK)
    hid = [jnp.dot(hn, wu_ref[:, c0:c0 + FF_CHUNK], preferred_element_type=F32) for c0 in chunks]
    act = [jnp.square(jnp.maximum(h, 0.0)).astype(BF16) for h in hid]
    yield
    y = h1
    for c0, a in zip(chunks, act):
        y = y + jnp.dot(a, wd_ref[c0:c0 + FF_CHUNK, :], preferred_element_type=F32)
    yield
    y_ref[...] = y


def _mlp(x, oa, ob, p, tm):
    n, d = x.shape
    row = lambda w: pl.BlockSpec((tm, w), lambda i: (i, 0))
    const = lambda a: pl.BlockSpec(a.shape, lambda i: (0, 0))
    return pl.pallas_call(
        _mlp_kernel,
        grid=(n // tm,),
        in_specs=[row(d), row(WIDTH_A), row(WIDTH_B), const(p["w_o"]), const(p["n2"]),
                  const(p["w_up"]), const(p["w_down"])],
        out_specs=row(d),
        out_shape=jax.ShapeDtypeStruct((n, d), F32),
        compiler_params=pltpu.CompilerParams(
            dimension_semantics=("arbitrary",), vmem_limit_bytes=VMEM_LIMIT),
        name="mlp",
    )(x, oa, ob, p["w_o"], p["n2"], p["w_up"], p["w_down"])


def _mlp_attn_kernel(x_ref, oa_ref, ob_ref, wo_ref, n2_ref, wu_ref, wd_ref,
                     q_ref, kn_ref, vn_ref, kt_ref, vt_ref, mc_ref, mn_ref, y_ref, o_ref):
    _run_stages(_attn_sample_stages(q_ref, kn_ref, vn_ref, kt_ref, vt_ref, mc_ref, mn_ref, o_ref),
                _mlp_stages(x_ref, oa_ref, ob_ref, wo_ref, n2_ref, wu_ref, wd_ref, y_ref))


def _attn_sample_specs(qb, cache_kt, seq0):
    nb, nt, _ = qb.shape
    _, nh, dh, wbuf = cache_kt.shape
    masks = [jnp.asarray(m) for m in _sample_masks(wbuf, nt)]

    def specs(linear_step):
        tok = pl.BlockSpec((1, nt, WIDTH_B), lambda *g: (seq0 + linear_step(*g), 0, 0))
        cache = pl.BlockSpec((1, nh, dh, wbuf), lambda *g: (seq0 + linear_step(*g), 0, 0, 0))
        out = pl.BlockSpec((1, nt, WIDTH_B), lambda *g: (linear_step(*g), 0, 0))
        mspecs = [pl.BlockSpec(m.shape, lambda *g: (0, 0), pipeline_mode=pl.Buffered(1))
                  for m in masks]
        return tok, cache, out, mspecs

    return masks, specs


def _mlp_attn(x, oa, ob, p, qb, kn, vn, cache_kt, cache_vt, tm, seq0):
    n, d = x.shape
    nt = qb.shape[1]
    steps = n // tm
    assert n % tm == 0 and seq0 + steps <= qb.shape[0]
    masks, specs = _attn_sample_specs(qb, cache_kt, seq0)
    tok, cache, tok_out, mspecs = specs(lambda i: i)
    row = lambda w: pl.BlockSpec((tm, w), lambda i: (i, 0))
    const = lambda a: pl.BlockSpec(a.shape, lambda i: (0, 0), pipeline_mode=pl.Buffered(1))
    return pl.pallas_call(
        _mlp_attn_kernel,
        grid=(steps,),
        in_specs=[row(d), row(WIDTH_A), row(WIDTH_B), const(p["w_o"]), const(p["n2"]),
                  const(p["w_up"]), const(p["w_down"]), tok, tok, tok, cache, cache] + mspecs,
        out_specs=[row(d), tok_out],
        out_shape=[jax.ShapeDtypeStruct((n, d), F32),
                   jax.ShapeDtypeStruct((steps, nt, WIDTH_B), F32)],
        compiler_params=pltpu.CompilerParams(
            dimension_semantics=("arbitrary",), vmem_limit_bytes=VMEM_LIMIT),
        name="mlp_attn",
    )(x, oa, ob, p["w_o"], p["n2"], p["w_up"], p["w_down"], qb, kn, vn, cache_kt, cache_vt, *masks)


def _layer_params(norm1_g, w_in, conv_w, a_log, dt_bias, delta_norm_g, q_norm_g, k_norm_g, w_o,
                  norm2_g, w_up, w_down):
    d = w_in.shape[0]
    n_gate = 2 * N_HEADS_A
    gate0 = CONV_CH + WIDTH_A
    w_a = w_in[:, :gate0].astype(BF16)
    w_b = w_in[:, gate0 + n_gate:].astype(BF16)
    w_g = jnp.pad(w_in[:, gate0:gate0 + n_gate], ((0, 0), (0, LANES - n_gate))).astype(BF16)
    lane_pad = lambda a: jnp.zeros((1, LANES), F32).at[0, N_HEADS_A:n_gate].set(a.astype(F32))
    hid = np.arange(WIDTH_B) // HEAD_DIM_B
    head_mean = jnp.asarray((hid[:, None] == hid[None, :]).astype(np.float32) / HEAD_DIM_B, BF16)
    return {
        "n1": norm1_g.reshape(1, d).astype(F32), "w_a": w_a, "w_b": w_b, "w_g": w_g, "conv_w": conv_w.astype(F32),
        "alog": lane_pad(a_log), "dtb": lane_pad(dt_bias),
        "qng": jnp.tile(q_norm_g.astype(F32), N_HEADS_B).reshape(1, WIDTH_B),
        "kng": jnp.tile(k_norm_g.astype(F32), N_HEADS_B).reshape(1, WIDTH_B),
        "hm": head_mean, "dng": delta_norm_g.reshape(1, HEAD_DIM_A).astype(F32),
        "w_o": w_o.astype(BF16), "n2": norm2_g.reshape(1, d).astype(F32),
        "w_up": w_up.astype(BF16), "w_down": w_down.astype(BF16),
    }


MLP_TILE = 256
PROJ_TILE = 256


def _layer(xp, xs, state_conv, s0_s, cache_k, cache_v, p):
    b, t, d = xp.shape
    nb, nt, _ = xs.shape
    n = b * t
    pbuf = min(MAX_WINDOW, t)

    qa, ka, va, z, gt, qb, kn, vn, conv_s = _proj_sample(xs, state_conv, p, bt=min(nb, 64))
    oa_s, s_s = _delta(qa, ka, va, z, gt, s0_s, p["dng"], chunk=nt, nchunk=1, bb=min(nb, 16),
                       out_dtype=F32)
    ckt, cvt = cache_k.transpose(0, 2, 3, 1), cache_v.transpose(0, 2, 3, 1)
    attn_s = (qb, kn, vn, ckt, cvt)

    steps = n // PROJ_TILE
    fuse = n % PROJ_TILE == 0 and PROJ_TILE == MLP_TILE and 2 * steps == nb
    qa, ka, va, z, gt, kp, vp, conv_p, *dil = _proj_prompt(
        xp, p, tm=PROJ_TILE, pbuf=pbuf, attn=attn_s if fuse else None)
    window = lambda a: a.reshape(b, N_HEADS_B, HEAD_DIM_B, pbuf).transpose(0, 3, 1, 2)
    kp, vp = window(kp), window(vp)
    s0_p = jnp.zeros((b, N_HEADS_A, HEAD_DIM_A, HEAD_DIM_A), F32)
    oa_p, s_p = _delta(qa, ka, va, z, gt, s0_p, p["dng"], chunk=DELTA_CHUNK, nchunk=2, bb=b,
                       out_dtype=BF16)
    ob_p = _attn_prompt(dil[0:3], dil[3:6], dil[6:9])

    mlp_in = (xp.reshape(n, d), oa_p.reshape(n, WIDTH_A), ob_p.reshape(n, WIDTH_B), p)
    if fuse:
        yp, ob_s2 = _mlp_attn(*mlp_in, *attn_s, tm=MLP_TILE, seq0=steps)
        ob_s = jnp.concatenate([dil[9], ob_s2], axis=0)
    else:
        yp = _mlp(*mlp_in, tm=MLP_TILE)
        ob_s = _attn_sample(*attn_s)
    ys = _mlp(xs.reshape(nb * nt, d), oa_s.reshape(nb * nt, WIDTH_A),
              ob_s.reshape(nb * nt, WIDTH_B), p, tm=min(MLP_TILE, nb * nt))
    heads = lambda a: a.reshape(nb, nt, N_HEADS_B, HEAD_DIM_B)
    return ((yp.reshape(b, t, d), kp, vp, s_p, conv_p),
            (ys.reshape(nb, nt, d), heads(kn), heads(vn), s_s, conv_s))


def kernel(x_prompt, x_sample, cache_swa_k, cache_swa_v, state_delta, state_conv, norm1_g, w_in,
           conv_w, a_log, dt_bias, delta_norm_g, q_norm_g, k_norm_g, w_o, norm2_g, w_up, w_down):
    depth = w_in.shape[0]
    b, s, _ = x_prompt.shape
    nb, nt, _ = x_sample.shape
    wbuf = cache_swa_k.shape[2]
    assert s % ATTN_TILE == 0 and nt == SUBLANES and wbuf == MAX_WINDOW
    yp, ys = x_prompt, x_sample
    outs = [[] for _ in range(8)]
    for layer in range(depth):
        p = _layer_params(norm1_g[layer], w_in[layer], conv_w[layer], a_log[layer], dt_bias[layer],
                          delta_norm_g[layer], q_norm_g[layer], k_norm_g[layer], w_o[layer],
                          norm2_g[layer], w_up[layer], w_down[layer])
        (yp, kp, vp, dp, cp), (ys, kn, vn, dn, cn) = _layer(
            yp, ys, state_conv[layer], state_delta[layer], cache_swa_k[layer], cache_swa_v[layer], p)
        for lst, val in zip(outs, (kp, vp, dp, cp, kn, vn, dn, cn)):
            lst.append(val)
    return (yp, ys) + tuple(jnp.stack(o) for o in outs)
```

```python
import functools

import numpy as np
import jax
import jax.numpy as jnp
from jax import lax
from jax.experimental import pallas as pl
from jax.experimental.pallas import tpu as pltpu

F32 = jnp.float32
BF16 = jnp.bfloat16

N_HEADS_A = 4
HEAD_DIM_A = 128
WIDTH_A = N_HEADS_A * HEAD_DIM_A
N_HEADS_B = 8
HEAD_DIM_B = 64
WIDTH_B = N_HEADS_B * HEAD_DIM_B
CONV_WIDTH = 4
CONV_CH = 3 * WIDTH_A
DELTA_CHUNK = 64
MAX_WINDOW = 2048
NORM_EPS = 1e-6
LANES = 128
SUBLANES = 8
NEG = -1e30
LOG2_E = 1.4426950408889634

C_QKV, C_Z = 0, CONV_CH
C_QB, C_KB, C_VB = 0, WIDTH_B, 2 * WIDTH_B

ATTN_DILATIONS = (1, 4, 16)
ATTN_BAND = 128
ATTN_TILE = 2048
ATTN_UNROLL = 2
VMEM_LIMIT = 56 * 1024 * 1024


def _dot(a, b):
    return jnp.dot(a.astype(BF16), b.astype(BF16), preferred_element_type=F32)


def _dot_nt(a, b):
    return lax.dot_general(a.astype(BF16), b.astype(BF16), (((1,), (1,)), ((), ())),
                           preferred_element_type=F32)


def _dot_tn(a, b):
    return lax.dot_general(a.astype(BF16), b.astype(BF16), (((0,), (0,)), ((), ())),
                           preferred_element_type=F32)


def _split2(x):
    hi = x.astype(BF16)
    lo = (x - hi.astype(F32)).astype(BF16)
    return hi, lo


def _split3(x):
    hi = x.astype(BF16)
    r = x - hi.astype(F32)
    mid = r.astype(BF16)
    lo = (r - mid.astype(F32)).astype(BF16)
    return hi, mid, lo


def _dot_exact_lhs(mask_bf16, x):
    hi, mid, lo = _split3(x)
    d = lambda p: jnp.dot(mask_bf16, p, preferred_element_type=F32)
    return d(hi) + d(mid) + d(lo)


def _sigmoid(x):
    return 0.5 * jnp.tanh(0.5 * x) + 0.5


def _silu(x):
    return x * _sigmoid(x)


def _softplus(x):
    return jnp.maximum(x, 0.0) + jnp.log1p(jnp.exp(-jnp.abs(x)))


def _rmsnorm(x, g):
    return x * lax.rsqrt(jnp.mean(x * x, axis=-1, keepdims=True) + NORM_EPS) * g


def _proj_body(x, ext_ref, shift, pad, refs, nsplit=1):
    n1_ref, wa_ref, wb_ref, wg_ref = refs[0:4]
    rows = x.shape[0] // nsplit
    raws = []
    for part in range(nsplit):
        hn = _rmsnorm(x[part * rows:(part + 1) * rows], n1_ref[...]).astype(BF16)
        proj = lambda w, hn=hn: jnp.dot(hn, w, preferred_element_type=F32)
        raws.append((proj(wa_ref[:, C_QKV:C_QKV + CONV_CH]),
                     proj(wb_ref[:, C_QB:C_QB + WIDTH_B]), proj(wb_ref[:, C_KB:C_KB + WIDTH_B]),
                     proj(wb_ref[:, C_VB:C_VB + WIDTH_B]),
                     proj(wa_ref[:, C_Z:C_Z + WIDTH_A]), proj(wg_ref[...])))
    outs = [_proj_epilogue(raw, ext_ref, shift, pad + part * rows, refs)
            for part, raw in enumerate(raws)]
    return tuple(jnp.concatenate(vals, axis=0) if nsplit > 1 else vals[0] for vals in zip(*outs))


def _proj_epilogue(raw, ext_ref, shift, pad, refs):
    cw_ref, al_ref, dtb_ref, qng_ref, kng_ref, hm_ref = refs[4:]
    u, qb, kb, vb, z, gc = raw
    tm = u.shape[0]
    ext_ref[pad:pad + tm, :] = u
    cw = cw_ref[...]
    y = u * cw[3:4, :]
    for i in range(CONV_WIDTH - 1):
        off = pad - (CONV_WIDTH - 1 - i) * shift
        y = y + ext_ref[off:off + tm, :] * cw[i:i + 1, :]
    y = _silu(y)
    qa, ka = [], []
    for h in range(N_HEADS_A):
        lo = h * HEAD_DIM_A
        qh = y[:, lo:lo + HEAD_DIM_A]
        qa.append(qh * (lax.rsqrt(
            jnp.sum(qh * qh, axis=-1, keepdims=True) + NORM_EPS) * HEAD_DIM_A ** -0.5))
        kh = y[:, WIDTH_A + lo:WIDTH_A + lo + HEAD_DIM_A]
        ka.append(kh * lax.rsqrt(jnp.sum(kh * kh, axis=-1, keepdims=True) + NORM_EPS))
    qa = jnp.concatenate(qa, axis=1)
    ka = jnp.concatenate(ka, axis=1)
    va = y[:, 2 * WIDTH_A:3 * WIDTH_A]

    hm = hm_ref[...]

    def headnorm(v, g):
        ms = jnp.dot((v * v).astype(BF16), hm, preferred_element_type=F32)
        return v * lax.rsqrt(ms + NORM_EPS) * g

    qb = headnorm(qb, qng_ref[...])
    kb = headnorm(kb, kng_ref[...])

    lane = lax.broadcasted_iota(jnp.int32, gc.shape, 1)
    beta = _sigmoid(gc)
    g = -jnp.exp(al_ref[...]) * _softplus(gc + dtb_ref[...])
    gates = jnp.where(lane < N_HEADS_A, beta, g)
    return qa, ka, va, z, gates, qb, kb, vb


PROJ_SPLIT = 2
N_PROJ_PARAMS = 10
N_ATTN_SAMPLE_INPUTS = 7


def _proj_prompt_kernel(*refs, first_win, with_attn):
    x_ref, params = refs[0], refs[1:1 + N_PROJ_PARAMS]
    pos = 1 + N_PROJ_PARAMS
    attn_in = refs[pos:pos + N_ATTN_SAMPLE_INPUTS] if with_attn else ()
    pos += len(attn_in)
    outs = refs[pos:pos + 17]
    pos += 17
    attn_out = refs[pos:pos + 1] if with_attn else ()
    pos += len(attn_out)
    scratch = refs[pos:]
    stages = [_proj_prompt_stages(x_ref, params, outs, scratch, first_win)]
    if with_attn:
        stages.insert(0, _attn_sample_stages(*attn_in, *attn_out))
    _run_stages(*stages)


def _proj_prompt_stages(x_ref, params, outs, scratch, first_win):
    (qa_ref, ka_ref, va_ref, z_ref, gt_ref, kb_ref, vb_ref, cn_ref,
     q1_ref, q4_ref, q16_ref, k1_ref, k4_ref, k16_ref, v1_ref, v4_ref, v16_ref) = outs
    ext_ref, dil_ref = scratch
    t = pl.program_id(1)
    tm = x_ref.shape[1]

    @pl.when(t == 0)
    def _():
        ext_ref[0:SUBLANES, :] = jnp.zeros((SUBLANES, CONV_CH), F32)

    @pl.when(t > 0)
    def _():
        ext_ref[0:SUBLANES, :] = ext_ref[tm:tm + SUBLANES, :]

    qa, ka, va, z, gates, qb, kb, vb = _proj_body(x_ref[0], ext_ref, 1, SUBLANES, params,
                                                  nsplit=PROJ_SPLIT)
    yield
    for r, v in zip((qa_ref, ka_ref, va_ref, z_ref, gt_ref), (qa, ka, va, z, gates)):
        r[0] = v
    cn_ref[0] = ext_ref[tm + SUBLANES - (CONV_WIDTH - 1):tm + SUBLANES, :]

    @pl.when(t >= first_win)
    def _():
        kb_ref[0] = kb.T
        vb_ref[0] = vb.T

    yield
    nchunk = WIDTH_B // LANES
    q_att = qb * (HEAD_DIM_B ** -0.5 * LOG2_E)
    for ai, (val, r1, r4, r16) in enumerate(((q_att, q1_ref, q4_ref, q16_ref),
                                              (kb, k1_ref, k4_ref, k16_ref),
                                              (vb, v1_ref, v4_ref, v16_ref))):
        r1[0] = val.astype(BF16)
        for c in range(nchunk):
            dil_ref[0, ai * nchunk + c] = val[:, c * LANES:(c + 1) * LANES]
        n4, n16 = tm // 4, tm // 16
        for r in range(4):
            parts = [dil_ref[0, ai * nchunk + c, pl.ds(r, n4, stride=4), :] for c in range(nchunk)]
            r4[0, r] = jnp.concatenate(parts, axis=1).astype(BF16)
            for c in range(nchunk):
                dil_ref[1, ai * nchunk + c, r * n4:(r + 1) * n4, :] = parts[c]
        for r in range(4):
            for r2 in range(4):
                r16[0, r + 4 * r2] = jnp.concatenate(
                    [dil_ref[1, ai * nchunk + c, pl.ds(r * n4 + r2, n16, stride=4), :]
                     for c in range(nchunk)], axis=1).astype(BF16)


def _proj_sample_kernel(x_ref, st_ref, n1_ref, wa_ref, wb_ref, wg_ref, cw_ref, al_ref, dtb_ref,
                        qng_ref, kng_ref, hm_ref, qa_ref, ka_ref, va_ref, z_ref, gt_ref, qb_ref, kb_ref, vb_ref,
                        cn_ref, ext_ref):
    nt, nb, d = x_ref.shape
    ncv = CONV_WIDTH - 1
    tm = nb * nt
    pad = ncv * nb
    ext_ref[0:pad, :] = st_ref[...].reshape(pad, CONV_CH)
    vals = _proj_body(x_ref[...].reshape(tm, d), ext_ref, nb, pad,
                      (n1_ref, wa_ref, wb_ref, wg_ref, cw_ref, al_ref, dtb_ref, qng_ref, kng_ref,
                       hm_ref))
    for r, v in zip((qa_ref, ka_ref, va_ref, z_ref, gt_ref, qb_ref, kb_ref, vb_ref), vals):
        r[...] = v.reshape(r.shape)
    cn_ref[...] = ext_ref[tm:tm + pad, :].reshape(ncv, nb, CONV_CH)


def _proj_params(p):
    full = lambda a: pl.BlockSpec(a.shape, lambda *_: (0,) * a.ndim,
                                  pipeline_mode=pl.Buffered(1))
    arrs = (p["n1"], p["w_a"], p["w_b"], p["w_g"], p["conv_w"], p["alog"], p["dtb"], p["qng"],
            p["kng"], p["hm"])
    return arrs, [full(a) for a in arrs]


def _proj_prompt(x, p, tm, pbuf, attn=None):
    b, t, d = x.shape
    assert pbuf % tm == 0 and t % tm == 0
    first_win = (t - pbuf) // tm
    arrs, specs = _proj_params(p)
    assert len(arrs) == N_PROJ_PARAMS
    steps_t = t // tm
    attn_args, attn_in_specs, attn_out_shape, attn_out_specs = [], [], [], []
    if attn is not None:
        qb, kn, vn, ckt, cvt = attn
        assert b * steps_t <= qb.shape[0]
        masks, mk = _attn_sample_specs(qb, ckt, 0)
        tok, cache, tok_out, mspecs = mk(lambda i, j: i * steps_t + j)
        attn_args = [qb, kn, vn, ckt, cvt] + masks
        attn_in_specs = [tok, tok, tok, cache, cache] + mspecs
        assert len(attn_args) == N_ATTN_SAMPLE_INPUTS
        attn_out_shape = [jax.ShapeDtypeStruct((b * steps_t, qb.shape[1], WIDTH_B), F32)]
        attn_out_specs = [tok_out]
    row = lambda w: pl.BlockSpec((1, tm, w), lambda i, j: (i, j, 0))
    widths = (WIDTH_A, WIDTH_A, WIDTH_A, WIDTH_A, LANES)
    out_shape = [jax.ShapeDtypeStruct((b, t, w), F32) for w in widths]
    out_specs = [row(w) for w in widths]
    for _ in range(2):
        out_shape.append(jax.ShapeDtypeStruct((b, WIDTH_B, pbuf), F32))
        out_specs.append(pl.BlockSpec((1, WIDTH_B, tm),
                                      lambda i, j: (i, 0, jnp.maximum(j - first_win, 0))))
    out_shape.append(jax.ShapeDtypeStruct((b, CONV_WIDTH - 1, CONV_CH), F32))
    out_specs.append(pl.BlockSpec((1, CONV_WIDTH - 1, CONV_CH), lambda i, j: (i, 0, 0)))
    for _ in range(3):
        out_shape.append(jax.ShapeDtypeStruct((b, t, WIDTH_B), BF16))
        out_specs.append(row(WIDTH_B))
        for dil in ATTN_DILATIONS[1:]:
            out_shape.append(jax.ShapeDtypeStruct((b, dil, t // dil, WIDTH_B), BF16))
            out_specs.append(pl.BlockSpec((1, dil, tm // dil, WIDTH_B), lambda i, j: (i, 0, j, 0)))
    return pl.pallas_call(
        functools.partial(_proj_prompt_kernel, first_win=first_win, with_attn=attn is not None),
        grid=(b, steps_t),
        in_specs=[row(d)] + specs + attn_in_specs,
        out_specs=out_specs + attn_out_specs,
        out_shape=out_shape + attn_out_shape,
        scratch_shapes=[pltpu.VMEM((tm + SUBLANES, CONV_CH), F32),
                        pltpu.VMEM((2, 3 * WIDTH_B // LANES, tm, LANES), F32)],
        compiler_params=pltpu.CompilerParams(
            dimension_semantics=("arbitrary", "arbitrary"), vmem_limit_bytes=VMEM_LIMIT),
        name="proj_prompt",
    )(x, *arrs, *attn_args)


def _proj_sample(x, state_conv, p, bt):
    nb, nt, d = x.shape
    ncv = CONV_WIDTH - 1
    arrs, specs = _proj_params(p)
    blk = lambda r, w: pl.BlockSpec((r, bt, w), lambda i: (0, i, 0))
    widths = (WIDTH_A, WIDTH_A, WIDTH_A, WIDTH_A, LANES, WIDTH_B, WIDTH_B, WIDTH_B)
    out_shape = [jax.ShapeDtypeStruct((nt, nb, w), F32) for w in widths]
    out_shape.append(jax.ShapeDtypeStruct((ncv, nb, CONV_CH), F32))
    out_specs = [blk(nt, w) for w in widths] + [blk(ncv, CONV_CH)]
    outs = pl.pallas_call(
        _proj_sample_kernel,
        grid=(nb // bt,),
        in_specs=[blk(nt, d), blk(ncv, CONV_CH)] + specs,
        out_specs=out_specs,
        out_shape=out_shape,
        scratch_shapes=[pltpu.VMEM((bt * (nt + ncv), CONV_CH), F32)],
        compiler_params=pltpu.CompilerParams(
            dimension_semantics=("arbitrary",), vmem_limit_bytes=VMEM_LIMIT),
        name="proj_sample",
    )(x.transpose(1, 0, 2), state_conv.transpose(1, 0, 2), *arrs)
    return [o.transpose(1, 0, 2) for o in outs]


def _delta_kernel(q_ref, k_ref, v_ref, z_ref, gt_ref, s0_ref, ng_ref, o_ref, sf_ref, s_scr,
                  *, chunk, nchunk, bb):
    c = chunk
    t = pl.program_id(1)

    @pl.when(t == 0)
    def _():
        s_scr[...] = s0_ref[...]

    ii = lax.broadcasted_iota(jnp.int32, (c, c), 0)
    jj = lax.broadcasted_iota(jnp.int32, (c, c), 1)
    causal = ii >= jj
    strict = ii > jj
    tril_bf = causal.astype(F32).astype(BF16)
    eye = (ii == jj).astype(F32)
    level_masks = []
    s = 1
    while s < c:
        level_masks.append(((ii // (2 * s)) == (jj // (2 * s))) & ((ii & s) != 0) & ((jj & s) == 0))
        s *= 2
    ng = ng_ref[...]

    heads = range(N_HEADS_A)
    items = [(bi, ci, h) for bi in range(bb) for ci in range(nchunk) for h in heads]
    pre = {}
    for bi in range(bb):
        for ci in range(nchunk):
            rows = slice(ci * c, (ci + 1) * c)
            gt = gt_ref[bi, rows, :]
            gcum = _dot_exact_lhs(tril_bf, gt)
            if c < LANES:
                gpad = jnp.concatenate([gcum, jnp.zeros((LANES - c, LANES), F32)], axis=0)
            else:
                gpad = gcum
            gcum_t = gpad.T
            for h in heads:
                lo = h * HEAD_DIM_A
                q = q_ref[bi, rows, lo:lo + HEAD_DIM_A]
                k = k_ref[bi, rows, lo:lo + HEAD_DIM_A]
                v = v_ref[bi, rows, lo:lo + HEAD_DIM_A]
                beta = gt[:, h:h + 1]
                g_col = gcum[:, N_HEADS_A + h:N_HEADS_A + h + 1]
                g_row = gcum_t[N_HEADS_A + h:N_HEADS_A + h + 1, 0:c]
                g_last = gcum[c - 1:c, N_HEADS_A + h:N_HEADS_A + h + 1]
                decay = jnp.where(causal, jnp.exp(jnp.where(causal, g_col - g_row, 0.0)), 0.0)
                exp_g = jnp.exp(g_col)
                kb = k * beta
                pre[bi, ci, h] = dict(
                    q=q, k=k, kb=kb, decay=decay, g_last=g_last,
                    rhs=jnp.concatenate([v * beta, kb * exp_g], axis=1),
                    q_dec=q * exp_g, k_dec=k * jnp.exp(g_last - g_col))
    lmat = {it: jnp.where(strict, _dot_nt(pre[it]["kb"], pre[it]["k"]) * pre[it]["decay"], 0.0)
            for it in items}
    qk = {it: _dot_nt(pre[it]["q"], pre[it]["k"]) * pre[it]["decay"] for it in items}
    tinv = {it: eye - jnp.where(level_masks[0], lmat[it], 0.0) for it in items}
    for msk in level_masks[1:]:
        te = {it: _dot(tinv[it], jnp.where(msk, lmat[it], 0.0)) for it in items}
        tinv = {it: tinv[it] - _dot(te[it], tinv[it]) for it in items}
    uw = {it: _dot(tinv[it], pre[it]["rhs"]) for it in items}

    for bi in range(bb):
        st = {h: s_scr[bi, h] for h in heads}
        for ci in range(nchunk):
            rows = slice(ci * c, (ci + 1) * c)
            ws = {h: _dot(jnp.concatenate([uw[bi, ci, h][:, HEAD_DIM_A:2 * HEAD_DIM_A],
                                           pre[bi, ci, h]["q_dec"]], axis=0), st[h])
                  for h in heads}
            v_new = {h: uw[bi, ci, h][:, 0:HEAD_DIM_A] - ws[h][0:c] for h in heads}
            o = {h: ws[h][c:2 * c] + _dot(qk[bi, ci, h], v_new[h]) for h in heads}
            st = {h: st[h] * jnp.exp(pre[bi, ci, h]["g_last"])
                  + _dot_tn(pre[bi, ci, h]["k_dec"], v_new[h]) for h in heads}
            for h in heads:
                lo = h * HEAD_DIM_A
                zz = z_ref[bi, rows, lo:lo + HEAD_DIM_A]
                o_ref[bi, rows, lo:lo + HEAD_DIM_A] = (
                    _rmsnorm(o[h], ng) * _silu(zz)).astype(o_ref.dtype)
        for h in heads:
            s_scr[bi, h] = st[h]

    @pl.when(t == pl.num_programs(1) - 1)
    def _():
        sf_ref[...] = s_scr[...]


def _delta(qa, ka, va, z, gt, s0, ng, *, chunk, nchunk, bb, out_dtype):
    b, t, _ = qa.shape
    tc = chunk * nchunk
    row = lambda w: pl.BlockSpec((bb, tc, w), lambda i, j: (i, j, 0))
    sspec = pl.BlockSpec((bb, N_HEADS_A, HEAD_DIM_A, HEAD_DIM_A), lambda i, j: (i, 0, 0, 0))
    return pl.pallas_call(
        functools.partial(_delta_kernel, chunk=chunk, nchunk=nchunk, bb=bb),
        grid=(b // bb, t // tc),
        in_specs=[row(WIDTH_A)] * 4 + [row(LANES), sspec,
                                       pl.BlockSpec((1, HEAD_DIM_A), lambda i, j: (0, 0))],
        out_specs=[row(WIDTH_A), sspec],
        out_shape=[jax.ShapeDtypeStruct((b, t, WIDTH_A), out_dtype),
                   jax.ShapeDtypeStruct((b, N_HEADS_A, HEAD_DIM_A, HEAD_DIM_A), F32)],
        scratch_shapes=[pltpu.VMEM((bb, N_HEADS_A, HEAD_DIM_A, HEAD_DIM_A), F32)],
        compiler_params=pltpu.CompilerParams(
            dimension_semantics=("arbitrary", "arbitrary"), vmem_limit_bytes=VMEM_LIMIT),
        name="delta_c%d" % chunk,
    )(qa, ka, va, z, gt, s0, ng)


def _attn_prompt_kernel(q1, q4, q16, k1p, k1c, k4p, k4c, k16p, k16c, v1p, v1c, v4p, v4c, v16p, v16c,
                        o_ref, acc, mrep, *, tq):
    t = pl.program_id(2)
    rr = ATTN_BAND
    ii = lax.broadcasted_iota(jnp.int32, (rr, rr), 0)
    jj = lax.broadcasted_iota(jnp.int32, (rr, rr), 1)
    lower = jj <= ii
    upper = jj >= ii
    lane = lax.broadcasted_iota(jnp.int32, (rr, LANES), 1)
    own = [(lane >= hh * HEAD_DIM_B) & (lane < (hh + 1) * HEAD_DIM_B) for hh in range(2)]
    pen_t = jnp.where(t > 0, jnp.float32(0.0), jnp.float32(NEG))
    one = jnp.ones((rr, LANES), BF16)
    zero = jnp.zeros((rr, LANES), BF16)

    ld = lambda ref_idx: ref_idx[0][ref_idx[1]]

    def weights(units):
        chains = [(j, hh) for j in range(len(units)) for hh in range(2)]
        qv = {ch: jnp.where(own[ch[1]], ld(units[ch[0]]["q"]), zero) for ch in chains}
        kc = [ld(u["kc"]) for u in units]
        kp = [ld(u["kp"]) for u in units]
        raw = {ch: (_dot_nt(qv[ch], kc[ch[0]]), _dot_nt(qv[ch], kp[ch[0]])) for ch in chains}
        out = {}
        for ch in chains:
            s_c = jnp.where(lower, raw[ch][0], NEG)
            s_p = raw[ch][1]
            if units[ch[0]]["pen"] is not None:
                s_p = s_p + units[ch[0]]["pen"]
            s_p = jnp.where(upper, s_p, NEG)
            m = jnp.max(jnp.maximum(s_c, s_p), axis=-1, keepdims=True)
            out[ch] = (jnp.exp2(s_c - m).astype(BF16), jnp.exp2(s_p - m).astype(BF16), m)
        return out

    def accumulate(units, wts, first):
        vc = [ld(u["vc"]) for u in units]
        vp = [ld(u["vp"]) for u in units]
        pv = {ch: jnp.dot(jnp.concatenate([p_c, p_p], axis=1),
                          jnp.concatenate([jnp.where(own[ch[1]], vc[ch[0]], one),
                                           jnp.where(own[ch[1]], vp[ch[0]], one)], axis=0),
                          preferred_element_type=F32)
              for ch, (p_c, p_p, _) in wts.items()}
        for ch, (_, _, m) in wts.items():
            j, hh = ch
            rows = units[j]["rows"]
            m_b = jnp.broadcast_to(m, (rr, LANES))
            if first:
                acc[hh, rows, :] = pv[ch]
                mrep[hh, rows, :] = m_b
            else:
                m_old = mrep[hh, rows, :]
                m_new = jnp.maximum(m_old, m_b)
                acc[hh, rows, :] = (acc[hh, rows, :] * jnp.exp2(m_old - m_new)
                                    + pv[ch] * jnp.exp2(m_b - m_new))
                mrep[hh, rows, :] = m_new

    uu = ATTN_UNROLL
    blk = lambda j: slice(j * rr, (j + 1) * rr)

    d16, d4 = ATTN_DILATIONS[2], ATTN_DILATIONS[1]
    assert tq // d16 == rr and (tq // (d4 * rr)) % uu == 0 and d16 % uu == 0
    trips = []
    full = slice(None)
    for it in range(d16 // uu):
        trips.append((True, [dict(
            q=(q16, (0, r)), kc=(k16c, (0, r)), kp=(k16p, (0, r)), vc=(v16c, (0, r)),
            vp=(v16p, (0, r)), pen=pen_t, rows=pl.ds(r, rr, stride=d16))
            for r in range(it * uu, (it + 1) * uu)]))
    for it in range(tq // (rr * uu)):
        units = []
        for j in range(it * uu, (it + 1) * uu):
            cur = (0, blk(j), full)
            prv = (0, blk(j - 1), full) if j > 0 else (0, slice(tq - rr, tq), full)
            units.append(dict(q=(q1, cur), kc=(k1c, cur), vc=(v1c, cur),
                              kp=(k1c if j > 0 else k1p, prv), vp=(v1c if j > 0 else v1p, prv),
                              pen=None if j > 0 else pen_t, rows=pl.ds(j * rr, rr)))
        trips.append((False, units))
    nb4 = tq // d4 // rr
    for r in range(d4):
        for jt in range(nb4 // uu):
            units = []
            for j in range(jt * uu, (jt + 1) * uu):
                cur = (0, r, blk(j), full)
                prv = (0, r, blk(j - 1), full) if j > 0 else (0, r, blk(nb4 - 1), full)
                units.append(dict(q=(q4, cur), kc=(k4c, cur), vc=(v4c, cur),
                                  kp=(k4c if j > 0 else k4p, prv),
                                  vp=(v4c if j > 0 else v4p, prv),
                                  pen=None if j > 0 else pen_t,
                                  rows=pl.ds(r + d4 * rr * j, rr, stride=d4)))
            trips.append((False, units))

    pending = None
    for first, units in trips:
        wts = weights(units)
        if pending is not None:
            accumulate(*pending)
        pending = (units, wts, first)
    accumulate(*pending)

    lane_t = lax.broadcasted_iota(jnp.int32, (tq, LANES), 1)
    a0 = acc[0]
    a1 = acc[1]
    l0 = a0[:, HEAD_DIM_B:HEAD_DIM_B + 1]
    l1 = a1[:, 0:1]
    o_ref[0] = jnp.where(lane_t < HEAD_DIM_B, a0 / l0, a1 / l1).astype(o_ref.dtype)


def _attn_prompt(q, k, v):
    b, t, _ = q[0].shape
    tq = ATTN_TILE
    nt = t // tq
    prev_j = lambda j: jnp.maximum(j - 1, 0)
    cur, prev = [], []
    for dil in ATTN_DILATIONS:
        if dil == 1:
            cur.append(pl.BlockSpec((1, tq, LANES), lambda i, h, j: (i, j, h)))
            prev.append(pl.BlockSpec((1, tq, LANES), lambda i, h, j: (i, prev_j(j), h)))
        else:
            cur.append(pl.BlockSpec((1, dil, tq // dil, LANES), lambda i, h, j: (i, 0, j, h)))
            prev.append(pl.BlockSpec((1, dil, tq // dil, LANES),
                                     lambda i, h, j: (i, 0, prev_j(j), h)))
    kv_specs = [s for pc in zip(prev, cur) for s in pc]
    kv_args = lambda x: [a for xd in x for a in (xd, xd)]
    return pl.pallas_call(
        functools.partial(_attn_prompt_kernel, tq=tq),
        grid=(b, WIDTH_B // LANES, nt),
        in_specs=cur + kv_specs + kv_specs,
        out_specs=cur[0],
        out_shape=jax.ShapeDtypeStruct((b, t, WIDTH_B), BF16),
        scratch_shapes=[pltpu.VMEM((2, tq, LANES), F32), pltpu.VMEM((2, tq, LANES), F32)],
        compiler_params=pltpu.CompilerParams(
            dimension_semantics=("arbitrary", "arbitrary", "arbitrary"),
            vmem_limit_bytes=VMEM_LIMIT),
        name="attn_prompt",
    )(*q, *kv_args(k), *kv_args(v))


def _multiplicity(delta):
    delta = np.asarray(delta)
    ok = delta >= 0
    m = ((delta <= 128).astype(np.float32)
         + ((delta <= 512) & (delta % 4 == 0)).astype(np.float32)
         + ((delta <= 2048) & (delta % 16 == 0)).astype(np.float32))
    return np.where(ok, m, 0.0).astype(np.float32)


def _sample_masks(wbuf, nt):
    i = np.arange(nt)[:, None]
    return (_multiplicity(wbuf + i - np.arange(wbuf)[None, :]),
            _multiplicity(i - np.arange(nt)[None, :]))


def _run_stages(*gens):
    gens = list(gens)
    while gens:
        for g in list(gens):
            if next(g, StopIteration) is StopIteration:
                gens.remove(g)


def _attn_sample_kernel(*refs):
    _run_stages(_attn_sample_stages(*refs))


def _attn_sample_stages(q_ref, kn_ref, vn_ref, kt_ref, vt_ref, mc_ref, mn_ref, o_ref):
    dh = HEAD_DIM_B
    heads = range(N_HEADS_B)
    mc = mc_ref[...]
    mn = mn_ref[...]
    sl = lambda r, h: r[0, :, h * dh:(h + 1) * dh]
    q = {h: (sl(q_ref, h) * (dh ** -0.5)).astype(BF16) for h in heads}
    s_c = {h: jnp.where(mc > 0.0, _dot(q[h], kt_ref[0, h]), NEG) for h in heads}
    s_n = {h: jnp.where(mn > 0.0, _dot_nt(q[h], sl(kn_ref, h)), NEG) for h in heads}
    yield
    m = {h: jnp.maximum(jnp.max(s_c[h], axis=-1, keepdims=True),
                        jnp.max(s_n[h], axis=-1, keepdims=True)) for h in heads}
    p_c = {h: (mc * jnp.exp(s_c[h] - m[h])).astype(BF16) for h in heads}
    p_n = {h: (mn * jnp.exp(s_n[h] - m[h])).astype(BF16) for h in heads}
    den = {h: (jnp.sum(p_c[h].astype(F32), axis=-1, keepdims=True)
               + jnp.sum(p_n[h].astype(F32), axis=-1, keepdims=True)) for h in heads}
    yield
    out = {h: _dot_nt(p_c[h], vt_ref[0, h]) + _dot(p_n[h], sl(vn_ref, h)) for h in heads}
    yield
    o_ref[0] = jnp.concatenate([out[h] / den[h] for h in heads], axis=1)


def _attn_sample(qb, kn, vn, cache_kt, cache_vt):
    nb, nt, _ = qb.shape
    _, nh, dh, wbuf = cache_kt.shape
    masks = [jnp.asarray(m) for m in _sample_masks(wbuf, nt)]
    tok = pl.BlockSpec((1, nt, WIDTH_B), lambda i: (i, 0, 0))
    cache = pl.BlockSpec((1, nh, dh, wbuf), lambda i: (i, 0, 0, 0))
    mspec = lambda m: pl.BlockSpec(m.shape, lambda i: (0, 0))
    return pl.pallas_call(
        _attn_sample_kernel,
        grid=(nb,),
        in_specs=[tok, tok, tok, cache, cache] + [mspec(m) for m in masks],
        out_specs=tok,
        out_shape=jax.ShapeDtypeStruct((nb, nt, WIDTH_B), F32),
        compiler_params=pltpu.CompilerParams(
            dimension_semantics=("arbitrary",), vmem_limit_bytes=VMEM_LIMIT),
        name="attn_sample",
    )(qb, kn, vn, cache_kt, cache_vt, *masks)


FF_CHUNK = 1024


def _mlp_kernel(*refs):
    _run_stages(_mlp_stages(*refs))


def _mlp_stages(x_ref, oa_ref, ob_ref, wo_ref, n2_ref, wu_ref, wd_ref, y_ref):
    mix = jnp.concatenate([oa_ref[...].astype(BF16), ob_ref[...].astype(BF16)], axis=1)
    h1 = x_ref[...] + jnp.dot(mix, wo_ref[...], preferred_element_type=F32)
    hn = _rmsnorm(h1, n2_ref[...]).astype(BF16)
    yield
    chunks = range(0, wu_ref.shape[1], FF_CHUNK)
    hid = [jnp.dot(hn, wu_ref[:, c0:c0 + FF_CHUNK], preferred_element_type=F32) for c0 in chunks]
    act = [jnp.square(jnp.maximum(h, 0.0)).astype(BF16) for h in hid]
    yield
    y = h1
    for c0, a in zip(chunks, act):
        y = y + jnp.dot(a, wd_ref[c0:c0 + FF_CHUNK, :], preferred_element_type=F32)
    yield
    y_ref[...] = y


def _mlp(x, oa, ob, p, tm):
    n, d = x.shape
    row = lambda w: pl.BlockSpec((tm, w), lambda i: (i, 0))
    const = lambda a: pl.BlockSpec(a.shape, lambda i: (0, 0))
    return pl.pallas_call(
        _mlp_kernel,
        grid=(n // tm,),
        in_specs=[row(d), row(WIDTH_A), row(WIDTH_B), const(p["w_o"]), const(p["n2"]),
                  const(p["w_up"]), const(p["w_down"])],
        out_specs=row(d),
        out_shape=jax.ShapeDtypeStruct((n, d), F32),
        compiler_params=pltpu.CompilerParams(
            dimension_semantics=("arbitrary",), vmem_limit_bytes=VMEM_LIMIT),
        name="mlp",
    )(x, oa, ob, p["w_o"], p["n2"], p["w_up"], p["w_down"])


def _mlp_attn_kernel(x_ref, oa_ref, ob_ref, wo_ref, n2_ref, wu_ref, wd_ref,
                     q_ref, kn_ref, vn_ref, kt_ref, vt_ref, mc_ref, mn_ref, y_ref, o_ref):
    _run_stages(_attn_sample_stages(q_ref, kn_ref, vn_ref, kt_ref, vt_ref, mc_ref, mn_ref, o_ref),
                _mlp_stages(x_ref, oa_ref, ob_ref, wo_ref, n2_ref, wu_ref, wd_ref, y_ref))


def _attn_sample_specs(qb, cache_kt, seq0):
    nb, nt, _ = qb.shape
    _, nh, dh, wbuf = cache_kt.shape
    masks = [jnp.asarray(m) for m in _sample_masks(wbuf, nt)]

    def specs(linear_step):
        tok = pl.BlockSpec((1, nt, WIDTH_B), lambda *g: (seq0 + linear_step(*g), 0, 0))
        cache = pl.BlockSpec((1, nh, dh, wbuf), lambda *g: (seq0 + linear_step(*g), 0, 0, 0))
        out = pl.BlockSpec((1, nt, WIDTH_B), lambda *g: (linear_step(*g), 0, 0))
        mspecs = [pl.BlockSpec(m.shape, lambda *g: (0, 0), pipeline_mode=pl.Buffered(1))
                  for m in masks]
        return tok, cache, out, mspecs

    return masks, specs


def _mlp_attn(x, oa, ob, p, qb, kn, vn, cache_kt, cache_vt, tm, seq0):
    n, d = x.shape
    nt = qb.shape[1]
    steps = n // tm
    assert n % tm == 0 and seq0 + steps <= qb.shape[0]
    masks, specs = _attn_sample_specs(qb, cache_kt, seq0)
    tok, cache, tok_out, mspecs = specs(lambda i: i)
    row = lambda w: pl.BlockSpec((tm, w), lambda i: (i, 0))
    const = lambda a: pl.BlockSpec(a.shape, lambda i: (0, 0), pipeline_mode=pl.Buffered(1))
    return pl.pallas_call(
        _mlp_attn_kernel,
        grid=(steps,),
        in_specs=[row(d), row(WIDTH_A), row(WIDTH_B), const(p["w_o"]), const(p["n2"]),
                  const(p["w_up"]), const(p["w_down"]), tok, tok, tok, cache, cache] + mspecs,
        out_specs=[row(d), tok_out],
        out_shape=[jax.ShapeDtypeStruct((n, d), F32),
                   jax.ShapeDtypeStruct((steps, nt, WIDTH_B), F32)],
        compiler_params=pltpu.CompilerParams(
            dimension_semantics=("arbitrary",), vmem_limit_bytes=VMEM_LIMIT),
        name="mlp_attn",
    )(x, oa, ob, p["w_o"], p["n2"], p["w_up"], p["w_down"], qb, kn, vn, cache_kt, cache_vt, *masks)


def _layer_params(norm1_g, w_in, conv_w, a_log, dt_bias, delta_norm_g, q_norm_g, k_norm_g, w_o,
                  norm2_g, w_up, w_down):
    d = w_in.shape[0]
    n_gate = 2 * N_HEADS_A
    gate0 = CONV_CH + WIDTH_A
    w_a = w_in[:, :gate0].astype(BF16)
    w_b = w_in[:, gate0 + n_gate:].astype(BF16)
    w_g = jnp.pad(w_in[:, gate0:gate0 + n_gate], ((0, 0), (0, LANES - n_gate))).astype(BF16)
    lane_pad = lambda a: jnp.zeros((1, LANES), F32).at[0, N_HEADS_A:n_gate].set(a.astype(F32))
    hid = np.arange(WIDTH_B) // HEAD_DIM_B
    head_mean = jnp.asarray((hid[:, None] == hid[None, :]).astype(np.float32) / HEAD_DIM_B, BF16)
    return {
        "n1": norm1_g.reshape(1, d).astype(F32), "w_a": w_a, "w_b": w_b, "w_g": w_g, "conv_w": conv_w.astype(F32),
        "alog": lane_pad(a_log), "dtb": lane_pad(dt_bias),
        "qng": jnp.tile(q_norm_g.astype(F32), N_HEADS_B).reshape(1, WIDTH_B),
        "kng": jnp.tile(k_norm_g.astype(F32), N_HEADS_B).reshape(1, WIDTH_B),
        "hm": head_mean, "dng": delta_norm_g.reshape(1, HEAD_DIM_A).astype(F32),
        "w_o": w_o.astype(BF16), "n2": norm2_g.reshape(1, d).astype(F32),
        "w_up": w_up.astype(BF16), "w_down": w_down.astype(BF16),
    }


MLP_TILE = 256
PROJ_TILE = 256


def _layer(xp, xs, state_conv, s0_s, cache_k, cache_v, p):
    b, t, d = xp.shape
    nb, nt, _ = xs.shape
    n = b * t
    pbuf = min(MAX_WINDOW, t)

    qa, ka, va, z, gt, qb, kn, vn, conv_s = _proj_sample(xs, state_conv, p, bt=min(nb, 64))
    oa_s, s_s = _delta(qa, ka, va, z, gt, s0_s, p["dng"], chunk=nt, nchunk=1, bb=min(nb, 16),
                       out_dtype=F32)
    ckt, cvt = cache_k.transpose(0, 2, 3, 1), cache_v.transpose(0, 2, 3, 1)
    attn_s = (qb, kn, vn, ckt, cvt)

    steps = n // PROJ_TILE
    fuse = n % PROJ_TILE == 0 and PROJ_TILE == MLP_TILE and 2 * steps == nb
    qa, ka, va, z, gt, kp, vp, conv_p, *dil = _proj_prompt(
        xp, p, tm=PROJ_TILE, pbuf=pbuf, attn=attn_s if fuse else None)
    window = lambda a: a.reshape(b, N_HEADS_B, HEAD_DIM_B, pbuf).transpose(0, 3, 1, 2)
    kp, vp = window(kp), window(vp)
    s0_p = jnp.zeros((b, N_HEADS_A, HEAD_DIM_A, HEAD_DIM_A), F32)
    oa_p, s_p = _delta(qa, ka, va, z, gt, s0_p, p["dng"], chunk=DELTA_CHUNK, nchunk=2, bb=b,
                       out_dtype=BF16)
    ob_p = _attn_prompt(dil[0:3], dil[3:6], dil[6:9])

    mlp_in = (xp.reshape(n, d), oa_p.reshape(n, WIDTH_A), ob_p.reshape(n, WIDTH_B), p)
    if fuse:
        yp, ob_s2 = _mlp_attn(*mlp_in, *attn_s, tm=MLP_TILE, seq0=steps)
        ob_s = jnp.concatenate([dil[9], ob_s2], axis=0)
    else:
        yp = _mlp(*mlp_in, tm=MLP_TILE)
        ob_s = _attn_sample(*attn_s)
    ys = _mlp(xs.reshape(nb * nt, d), oa_s.reshape(nb * nt, WIDTH_A),
              ob_s.reshape(nb * nt, WIDTH_B), p, tm=min(MLP_TILE, nb * nt))
    heads = lambda a: a.reshape(nb, nt, N_HEADS_B, HEAD_DIM_B)
    return ((yp.reshape(b, t, d), kp, vp, s_p, conv_p),
            (ys.reshape(nb, nt, d), heads(kn), heads(vn), s_s, conv_s))


def kernel(x_prompt, x_sample, cache_swa_k, cache_swa_v, state_delta, state_conv, norm1_g, w_in,
           conv_w, a_log, dt_bias, delta_norm_g, q_norm_g, k_norm_g, w_o, norm2_g, w_up, w_down):
    depth = w_in.shape[0]
    b, s, _ = x_prompt.shape
    nb, nt, _ = x_sample.shape
    wbuf = cache_swa_k.shape[2]
    assert s % ATTN_TILE == 0 and nt == SUBLANES and wbuf == MAX_WINDOW
    yp, ys = x_prompt, x_sample
    outs = [[] for _ in range(8)]
    for layer in range(depth):
        p = _layer_params(norm1_g[layer], w_in[layer], conv_w[layer], a_log[layer], dt_bias[layer],
                          delta_norm_g[layer], q_norm_g[layer], k_norm_g[layer], w_o[layer],
                          norm2_g[layer], w_up[layer], w_down[layer])
        (yp, kp, vp, dp, cp), (ys, kn, vn, dn, cn) = _layer(
            yp, ys, state_conv[layer], state_delta[layer], cache_swa_k[layer], cache_swa_v[layer], p)
        for lst, val in zip(outs, (kp, vp, dp, cp, kn, vn, dn, cn)):
            lst.append(val)
    return (yp, ys) + tuple(jnp.stack(o) for o in outs)
```

```python
import functools

import numpy as np
import jax
import jax.numpy as jnp
from jax import lax
from jax.experimental import pallas as pl
from jax.experimental.pallas import tpu as pltpu

F32 = jnp.float32
BF16 = jnp.bfloat16

N_HEADS_A = 4
HEAD_DIM_A = 128
WIDTH_A = N_HEADS_A * HEAD_DIM_A
N_HEADS_B = 8
HEAD_DIM_B = 64
WIDTH_B = N_HEADS_B * HEAD_DIM_B
CONV_WIDTH = 4
CONV_CH = 3 * WIDTH_A
DELTA_CHUNK = 64
MAX_WINDOW = 2048
NORM_EPS = 1e-6
LANES = 128
SUBLANES = 8
NEG = -1e30
LOG2_E = 1.4426950408889634

C_QKV, C_Z = 0, CONV_CH
C_QB, C_KB, C_VB = 0, WIDTH_B, 2 * WIDTH_B

ATTN_DILATIONS = (1, 4, 16)
ATTN_BAND = 128
ATTN_TILE = 2048
ATTN_UNROLL = 2
VMEM_LIMIT = 56 * 1024 * 1024


def _dot(a, b):
    return jnp.dot(a.astype(BF16), b.astype(BF16), preferred_element_type=F32)


def _dot_nt(a, b):
    return lax.dot_general(a.astype(BF16), b.astype(BF16), (((1,), (1,)), ((), ())),
                           preferred_element_type=F32)


def _dot_tn(a, b):
    return lax.dot_general(a.astype(BF16), b.astype(BF16), (((0,), (0,)), ((), ())),
                           preferred_element_type=F32)


def _split2(x):
    hi = x.astype(BF16)
    lo = (x - hi.astype(F32)).astype(BF16)
    return hi, lo


def _split3(x):
    hi = x.astype(BF16)
    r = x - hi.astype(F32)
    mid = r.astype(BF16)
    lo = (r - mid.astype(F32)).astype(BF16)
    return hi, mid, lo


def _dot_exact_lhs(mask_bf16, x):
    hi, mid, lo = _split3(x)
    d = lambda p: jnp.dot(mask_bf16, p, preferred_element_type=F32)
    return d(hi) + d(mid) + d(lo)


def _sigmoid(x):
    return 0.5 * jnp.tanh(0.5 * x) + 0.5


def _silu(x):
    return x * _sigmoid(x)


def _softplus(x):
    return jnp.maximum(x, 0.0) + jnp.log1p(jnp.exp(-jnp.abs(x)))


def _rmsnorm(x, g):
    return x * lax.rsqrt(jnp.mean(x * x, axis=-1, keepdims=True) + NORM_EPS) * g


def _proj_body(x, ext_ref, shift, pad, refs, nsplit=1):
    n1_ref, wa_ref, wb_ref, wg_ref = refs[0:4]
    rows = x.shape[0] // nsplit
    raws = []
    for part in range(nsplit):
        hn = _rmsnorm(x[part * rows:(part + 1) * rows], n1_ref[...]).astype(BF16)
        proj = lambda w, hn=hn: jnp.dot(hn, w, preferred_element_type=F32)
        raws.append((proj(wa_ref[:, C_QKV:C_QKV + CONV_CH]),
                     proj(wb_ref[:, C_QB:C_QB + WIDTH_B]), proj(wb_ref[:, C_KB:C_KB + WIDTH_B]),
                     proj(wb_ref[:, C_VB:C_VB + WIDTH_B]),
                     proj(wa_ref[:, C_Z:C_Z + WIDTH_A]), proj(wg_ref[...])))
    outs = [_proj_epilogue(raw, ext_ref, shift, pad + part * rows, refs)
            for part, raw in enumerate(raws)]
    return tuple(jnp.concatenate(vals, axis=0) if nsplit > 1 else vals[0] for vals in zip(*outs))


def _proj_epilogue(raw, ext_ref, shift, pad, refs):
    cw_ref, al_ref, dtb_ref, qng_ref, kng_ref, hm_ref = refs[4:]
    u, qb, kb, vb, z, gc = raw
    tm = u.shape[0]
    ext_ref[pad:pad + tm, :] = u
    cw = cw_ref[...]
    y = u * cw[3:4, :]
    for i in range(CONV_WIDTH - 1):
        off = pad - (CONV_WIDTH - 1 - i) * shift
        y = y + ext_ref[off:off + tm, :] * cw[i:i + 1, :]
    y = _silu(y)
    qa, ka = [], []
    for h in range(N_HEADS_A):
        lo = h * HEAD_DIM_A
        qh = y[:, lo:lo + HEAD_DIM_A]
        qa.append(qh * (lax.rsqrt(
            jnp.sum(qh * qh, axis=-1, keepdims=True) + NORM_EPS) * HEAD_DIM_A ** -0.5))
        kh = y[:, WIDTH_A + lo:WIDTH_A + lo + HEAD_DIM_A]
        ka.append(kh * lax.rsqrt(jnp.sum(kh * kh, axis=-1, keepdims=True) + NORM_EPS))
    qa = jnp.concatenate(qa, axis=1)
    ka = jnp.concatenate(ka, axis=1)
    va = y[:, 2 * WIDTH_A:3 * WIDTH_A]

    hm = hm_ref[...]

    def headnorm(v, g):
        ms = jnp.dot((v * v).astype(BF16), hm, preferred_element_type=F32)
        return v * lax.rsqrt(ms + NORM_EPS) * g

    qb = headnorm(qb, qng_ref[...])
    kb = headnorm(kb, kng_ref[...])

    lane = lax.broadcasted_iota(jnp.int32, gc.shape, 1)
    beta = _sigmoid(gc)
    g = -jnp.exp(al_ref[...]) * _softplus(gc + dtb_ref[...])
    gates = jnp.where(lane < N_HEADS_A, beta, g)
    return qa, ka, va, z, gates, qb, kb, vb


PROJ_SPLIT = 2
N_PROJ_PARAMS = 10
N_ATTN_SAMPLE_INPUTS = 7


def _proj_prompt_kernel(*refs, first_win, with_attn):
    x_ref, params = refs[0], refs[1:1 + N_PROJ_PARAMS]
    pos = 1 + N_PROJ_PARAMS
    attn_in = refs[pos:pos + N_ATTN_SAMPLE_INPUTS] if with_attn else ()
    pos += len(attn_in)
    outs = refs[pos:pos + 17]
    pos += 17
    attn_out = refs[pos:pos + 1] if with_attn else ()
    pos += len(attn_out)
    scratch = refs[pos:]
    stages = [_proj_prompt_stages(x_ref, params, outs, scratch, first_win)]
    if with_attn:
        stages.insert(0, _attn_sample_stages(*attn_in, *attn_out))
    _run_stages(*stages)


def _proj_prompt_stages(x_ref, params, outs, scratch, first_win):
    (qa_ref, ka_ref, va_ref, z_ref, gt_ref, kb_ref, vb_ref, cn_ref,
     q1_ref, q4_ref, q16_ref, k1_ref, k4_ref, k16_ref, v1_ref, v4_ref, v16_ref) = outs
    ext_ref, dil_ref = scratch
    t = pl.program_id(1)
    tm = x_ref.shape[1]

    @pl.when(t == 0)
    def _():
        ext_ref[0:SUBLANES, :] = jnp.zeros((SUBLANES, CONV_CH), F32)

    @pl.when(t > 0)
    def _():
        ext_ref[0:SUBLANES, :] = ext_ref[tm:tm + SUBLANES, :]

    qa, ka, va, z, gates, qb, kb, vb = _proj_body(x_ref[0], ext_ref, 1, SUBLANES, params,
                                                  nsplit=PROJ_SPLIT)
    yield
    for r, v in zip((qa_ref, ka_ref, va_ref, z_ref, gt_ref), (qa, ka, va, z, gates)):
        r[0] = v
    cn_ref[0] = ext_ref[tm + SUBLANES - (CONV_WIDTH - 1):tm + SUBLANES, :]

    @pl.when(t >= first_win)
    def _():
        kb_ref[0] = kb.T
        vb_ref[0] = vb.T

    yield
    nchunk = WIDTH_B // LANES
    q_att = qb * (HEAD_DIM_B ** -0.5 * LOG2_E)
    for ai, (val, r1, r4, r16) in enumerate(((q_att, q1_ref, q4_ref, q16_ref),
                                              (kb, k1_ref, k4_ref, k16_ref),
                                              (vb, v1_ref, v4_ref, v16_ref))):
        r1[0] = val.astype(BF16)
        for c in range(nchunk):
            dil_ref[0, ai * nchunk + c] = val[:, c * LANES:(c + 1) * LANES]
        n4, n16 = tm // 4, tm // 16
        for r in range(4):
            parts = [dil_ref[0, ai * nchunk + c, pl.ds(r, n4, stride=4), :] for c in range(nchunk)]
            r4[0, r] = jnp.concatenate(parts, axis=1).astype(BF16)
            for c in range(nchunk):
                dil_ref[1, ai * nchunk + c, r * n4:(r + 1) * n4, :] = parts[c]
        for r in range(4):
            for r2 in range(4):
                r16[0, r + 4 * r2] = jnp.concatenate(
                    [dil_ref[1, ai * nchunk + c, pl.ds(r * n4 + r2, n16, stride=4), :]
                     for c in range(nchunk)], axis=1).astype(BF16)


def _proj_sample_kernel(x_ref, st_ref, n1_ref, wa_ref, wb_ref, wg_ref, cw_ref, al_ref, dtb_ref,
                        qng_ref, kng_ref, hm_ref, qa_ref, ka_ref, va_ref, z_ref, gt_ref, qb_ref, kb_ref, vb_ref,
                        cn_ref, ext_ref):
    nt, nb, d = x_ref.shape
    ncv = CONV_WIDTH - 1
    tm = nb * nt
    pad = ncv * nb
    ext_ref[0:pad, :] = st_ref[...].reshape(pad, CONV_CH)
    vals = _proj_body(x_ref[...].reshape(tm, d), ext_ref, nb, pad,
                      (n1_ref, wa_ref, wb_ref, wg_ref, cw_ref, al_ref, dtb_ref, qng_ref, kng_ref,
                       hm_ref))
    for r, v in zip((qa_ref, ka_ref, va_ref, z_ref, gt_ref, qb_ref, kb_ref, vb_ref), vals):
        r[...] = v.reshape(r.shape)
    cn_ref[...] = ext_ref[tm:tm + pad, :].reshape(ncv, nb, CONV_CH)


def _proj_params(p):
    full = lambda a: pl.BlockSpec(a.shape, lambda *_: (0,) * a.ndim,
                                  pipeline_mode=pl.Buffered(1))
    arrs = (p["n1"], p["w_a"], p["w_b"], p["w_g"], p["conv_w"], p["alog"], p["dtb"], p["qng"],
            p["kng"], p["hm"])
    return arrs, [full(a) for a in arrs]


def _proj_prompt(x, p, tm, pbuf, attn=None):
    b, t, d = x.shape
    assert pbuf % tm == 0 and t % tm == 0
    first_win = (t - pbuf) // tm
    arrs, specs = _proj_params(p)
    assert len(arrs) == N_PROJ_PARAMS
    steps_t = t // tm
    attn_args, attn_in_specs, attn_out_shape, attn_out_specs = [], [], [], []
    if attn is not None:
        qb, kn, vn, ckt, cvt = attn
        assert b * steps_t <= qb.shape[0]
        masks, mk = _attn_sample_specs(qb, ckt, 0)
        tok, cache, tok_out, mspecs = mk(lambda i, j: i * steps_t + j)
        attn_args = [qb, kn, vn, ckt, cvt] + masks
        attn_in_specs = [tok, tok, tok, cache, cache] + mspecs
        assert len(attn_args) == N_ATTN_SAMPLE_INPUTS
        attn_out_shape = [jax.ShapeDtypeStruct((b * steps_t, qb.shape[1], WIDTH_B), F32)]
        attn_out_specs = [tok_out]
    row = lambda w: pl.BlockSpec((1, tm, w), lambda i, j: (i, j, 0))
    widths = (WIDTH_A, WIDTH_A, WIDTH_A, WIDTH_A, LANES)
    out_shape = [jax.ShapeDtypeStruct((b, t, w), F32) for w in widths]
    out_specs = [row(w) for w in widths]
    for _ in range(2):
        out_shape.append(jax.ShapeDtypeStruct((b, WIDTH_B, pbuf), F32))
        out_specs.append(pl.BlockSpec((1, WIDTH_B, tm),
                                      lambda i, j: (i, 0, jnp.maximum(j - first_win, 0))))
    out_shape.append(jax.ShapeDtypeStruct((b, CONV_WIDTH - 1, CONV_CH), F32))
    out_specs.append(pl.BlockSpec((1, CONV_WIDTH - 1, CONV_CH), lambda i, j: (i, 0, 0)))
    for _ in range(3):
        out_shape.append(jax.ShapeDtypeStruct((b, t, WIDTH_B), BF16))
        out_specs.append(row(WIDTH_B))
        for dil in ATTN_DILATIONS[1:]:
            out_shape.append(jax.ShapeDtypeStruct((b, dil, t // dil, WIDTH_B), BF16))
            out_specs.append(pl.BlockSpec((1, dil, tm // dil, WIDTH_B), lambda i, j: (i, 0, j, 0)))
    return pl.pallas_call(
        functools.partial(_proj_prompt_kernel, first_win=first_win, with_attn=attn is not None),
        grid=(b, steps_t),
        in_specs=[row(d)] + specs + attn_in_specs,
        out_specs=out_specs + attn_out_specs,
        out_shape=out_shape + attn_out_shape,
        scratch_shapes=[pltpu.VMEM((tm + SUBLANES, CONV_CH), F32),
                        pltpu.VMEM((2, 3 * WIDTH_B // LANES, tm, LANES), F32)],
        compiler_params=pltpu.CompilerParams(
            dimension_semantics=("arbitrary", "arbitrary"), vmem_limit_bytes=VMEM_LIMIT),
        name="proj_prompt",
    )(x, *arrs, *attn_args)


def _proj_sample(x, state_conv, p, bt):
    nb, nt, d = x.shape
    ncv = CONV_WIDTH - 1
    arrs, specs = _proj_params(p)
    blk = lambda r, w: pl.BlockSpec((r, bt, w), lambda i: (0, i, 0))
    widths = (WIDTH_A, WIDTH_A, WIDTH_A, WIDTH_A, LANES, WIDTH_B, WIDTH_B, WIDTH_B)
    out_shape = [jax.ShapeDtypeStruct((nt, nb, w), F32) for w in widths]
    out_shape.append(jax.ShapeDtypeStruct((ncv, nb, CONV_CH), F32))
    out_specs = [blk(nt, w) for w in widths] + [blk(ncv, CONV_CH)]
    outs = pl.pallas_call(
        _proj_sample_kernel,
        grid=(nb // bt,),
        in_specs=[blk(nt, d), blk(ncv, CONV_CH)] + specs,
        out_specs=out_specs,
        out_shape=out_shape,
        scratch_shapes=[pltpu.VMEM((bt * (nt + ncv), CONV_CH), F32)],
        compiler_params=pltpu.CompilerParams(
            dimension_semantics=("arbitrary",), vmem_limit_bytes=VMEM_LIMIT),
        name="proj_sample",
    )(x.transpose(1, 0, 2), state_conv.transpose(1, 0, 2), *arrs)
    return [o.transpose(1, 0, 2) for o in outs]


def _delta_kernel(q_ref, k_ref, v_ref, z_ref, gt_ref, s0_ref, ng_ref, o_ref, sf_ref, s_scr,
                  *, chunk, nchunk, bb):
    c = chunk
    t = pl.program_id(1)

    @pl.when(t == 0)
    def _():
        s_scr[...] = s0_ref[...]

    ii = lax.broadcasted_iota(jnp.int32, (c, c), 0)
    jj = lax.broadcasted_iota(jnp.int32, (c, c), 1)
    causal = ii >= jj
    strict = ii > jj
    tril_bf = causal.astype(F32).astype(BF16)
    eye = (ii == jj).astype(F32)
    level_masks = []
    s = 1
    while s < c:
        level_masks.append(((ii // (2 * s)) == (jj // (2 * s))) & ((ii & s) != 0) & ((jj & s) == 0))
        s *= 2
    ng = ng_ref[...]

    heads = range(N_HEADS_A)
    items = [(bi, ci, h) for bi in range(bb) for ci in range(nchunk) for h in heads]
    pre = {}
    for bi in range(bb):
        for ci in range(nchunk):
            rows = slice(ci * c, (ci + 1) * c)
            gt = gt_ref[bi, rows, :]
            gcum = _dot_exact_lhs(tril_bf, gt)
            if c < LANES:
                gpad = jnp.concatenate([gcum, jnp.zeros((LANES - c, LANES), F32)], axis=0)
            else:
                gpad = gcum
            gcum_t = gpad.T
            for h in heads:
                lo = h * HEAD_DIM_A
                q = q_ref[bi, rows, lo:lo + HEAD_DIM_A]
                k = k_ref[bi, rows, lo:lo + HEAD_DIM_A]
                v = v_ref[bi, rows, lo:lo + HEAD_DIM_A]
                beta = gt[:, h:h + 1]
                g_col = gcum[:, N_HEADS_A + h:N_HEADS_A + h + 1]
                g_row = gcum_t[N_HEADS_A + h:N_HEADS_A + h + 1, 0:c]
                g_last = gcum[c - 1:c, N_HEADS_A + h:N_HEADS_A + h + 1]
                decay = jnp.where(causal, jnp.exp(jnp.where(causal, g_col - g_row, 0.0)), 0.0)
                exp_g = jnp.exp(g_col)
                kb = k * beta
                pre[bi, ci, h] = dict(
                    q=q, k=k, kb=kb, decay=decay, g_last=g_last,
                    rhs=jnp.concatenate([v * beta, kb * exp_g], axis=1),
                    q_dec=q * exp_g, k_dec=k * jnp.exp(g_last - g_col))
    lmat = {it: jnp.where(strict, _dot_nt(pre[it]["kb"], pre[it]["k"]) * pre[it]["decay"], 0.0)
            for it in items}
    qk = {it: _dot_nt(pre[it]["q"], pre[it]["k"]) * pre[it]["decay"] for it in items}
    tinv = {it: eye - jnp.where(level_masks[0], lmat[it], 0.0) for it in items}
    for msk in level_masks[1:]:
        te = {it: _dot(tinv[it], jnp.where(msk, lmat[it], 0.0)) for it in items}
        tinv = {it: tinv[it] - _dot(te[it], tinv[it]) for it in items}
    uw = {it: _dot(tinv[it], pre[it]["rhs"]) for it in items}

    for bi in range(bb):
        st = {h: s_scr[bi, h] for h in heads}
        for ci in range(nchunk):
            rows = slice(ci * c, (ci + 1) * c)
            ws = {h: _dot(jnp.concatenate([uw[bi, ci, h][:, HEAD_DIM_A:2 * HEAD_DIM_A],
                                           pre[bi, ci, h]["q_dec"]], axis=0), st[h])
                  for h in heads}
            v_new = {h: uw[bi, ci, h][:, 0:HEAD_DIM_A] - ws[h][0:c] for h in heads}
            o = {h: ws[h][c:2 * c] + _dot(qk[bi, ci, h], v_new[h]) for h in heads}
            st = {h: st[h] * jnp.exp(pre[bi, ci, h]["g_last"])
                  + _dot_tn(pre[bi, ci, h]["k_dec"], v_new[h]) for h in heads}
            for h in heads:
                lo = h * HEAD_DIM_A
                zz = z_ref[bi, rows, lo:lo + HEAD_DIM_A]
                o_ref[bi, rows, lo:lo + HEAD_DIM_A] = (
                    _rmsnorm(o[h], ng) * _silu(zz)).astype(o_ref.dtype)
        for h in heads:
            s_scr[bi, h] = st[h]

    @pl.when(t == pl.num_programs(1) - 1)
    def _():
        sf_ref[...] = s_scr[...]


def _delta(qa, ka, va, z, gt, s0, ng, *, chunk, nchunk, bb, out_dtype):
    b, t, _ = qa.shape
    tc = chunk * nchunk
    row = lambda w: pl.BlockSpec((bb, tc, w), lambda i, j: (i, j, 0))
    sspec = pl.BlockSpec((bb, N_HEADS_A, HEAD_DIM_A, HEAD_DIM_A), lambda i, j: (i, 0, 0, 0))
    return pl.pallas_call(
        functools.partial(_delta_kernel, chunk=chunk, nchunk=nchunk, bb=bb),
        grid=(b // bb, t // tc),
        in_specs=[row(WIDTH_A)] * 4 + [row(LANES), sspec,
                                       pl.BlockSpec((1, HEAD_DIM_A), lambda i, j: (0, 0))],
        out_specs=[row(WIDTH_A), sspec],
        out_shape=[jax.ShapeDtypeStruct((b, t, WIDTH_A), out_dtype),
                   jax.ShapeDtypeStruct((b, N_HEADS_A, HEAD_DIM_A, HEAD_DIM_A), F32)],
        scratch_shapes=[pltpu.VMEM((bb, N_HEADS_A, HEAD_DIM_A, HEAD_DIM_A), F32)],
        compiler_params=pltpu.CompilerParams(
            dimension_semantics=("arbitrary", "arbitrary"), vmem_limit_bytes=VMEM_LIMIT),
        name="delta_c%d" % chunk,
    )(qa, ka, va, z, gt, s0, ng)


def _attn_prompt_kernel(q1, q4, q16, k1p, k1c, k4p, k4c, k16p, k16c, v1p, v1c, v4p, v4c, v16p, v16c,
                        o_ref, acc, mrep, *, tq):
    t = pl.program_id(2)
    rr = ATTN_BAND
    ii = lax.broadcasted_iota(jnp.int32, (rr, rr), 0)
    jj = lax.broadcasted_iota(jnp.int32, (rr, rr), 1)
    lower = jj <= ii
    upper = jj >= ii
    lane = lax.broadcasted_iota(jnp.int32, (rr, LANES), 1)
    own = [(lane >= hh * HEAD_DIM_B) & (lane < (hh + 1) * HEAD_DIM_B) for hh in range(2)]
    pen_t = jnp.where(t > 0, jnp.float32(0.0), jnp.float32(NEG))
    one = jnp.ones((rr, LANES), BF16)
    zero = jnp.zeros((rr, LANES), BF16)

    ld = lambda ref_idx: ref_idx[0][ref_idx[1]]

    def weights(units):
        chains = [(j, hh) for j in range(len(units)) for hh in range(2)]
        qv = {ch: jnp.where(own[ch[1]], ld(units[ch[0]]["q"]), zero) for ch in chains}
        kcat = [jnp.concatenate([ld(u["kc"]), ld(u["kp"])], axis=0) for u in units]
        raw = {ch: _dot_nt(qv[ch], kcat[ch[0]]) for ch in chains}
        out = {}
        for ch in chains:
            s_c = jnp.where(lower, raw[ch][:, 0:rr], NEG)
            s_p = raw[ch][:, rr:2 * rr]
            if units[ch[0]]["pen"] is not None:
                s_p = s_p + units[ch[0]]["pen"]
            s_p = jnp.where(upper, s_p, NEG)
            m = jnp.max(jnp.maximum(s_c, s_p), axis=-1, keepdims=True)
            out[ch] = (jnp.exp2(s_c - m).astype(BF16), jnp.exp2(s_p - m).astype(BF16), m)
        return out

    def accumulate(units, wts, first):
        vc = [ld(u["vc"]) for u in units]
        vp = [ld(u["vp"]) for u in units]
        pv = {ch: jnp.dot(jnp.concatenate([p_c, p_p], axis=1),
                          jnp.concatenate([jnp.where(own[ch[1]], vc[ch[0]], one),
                                           jnp.where(own[ch[1]], vp[ch[0]], one)], axis=0),
                          preferred_element_type=F32)
              for ch, (p_c, p_p, _) in wts.items()}
        for ch, (_, _, m) in wts.items():
            j, hh = ch
            rows = units[j]["rows"]
            m_b = jnp.broadcast_to(m, (rr, LANES))
            if first:
                acc[hh, rows, :] = pv[ch]
                mrep[hh, rows, :] = m_b
            else:
                m_old = mrep[hh, rows, :]
                m_new = jnp.maximum(m_old, m_b)
                acc[hh, rows, :] = (acc[hh, rows, :] * jnp.exp2(m_old - m_new)
                                    + pv[ch] * jnp.exp2(m_b - m_new))
                mrep[hh, rows, :] = m_new

    uu = ATTN_UNROLL
    blk = lambda j: slice(j * rr, (j + 1) * rr)

    d16, d4 = ATTN_DILATIONS[2], ATTN_DILATIONS[1]
    assert tq // d16 == rr and (tq // (d4 * rr)) % uu == 0 and d16 % uu == 0
    trips = []
    full = slice(None)
    for it in range(d16 // uu):
        trips.append((True, [dict(
            q=(q16, (0, r)), kc=(k16c, (0, r)), kp=(k16p, (0, r)), vc=(v16c, (0, r)),
            vp=(v16p, (0, r)), pen=pen_t, rows=pl.ds(r, rr, stride=d16))
            for r in range(it * uu, (it + 1) * uu)]))
    for it in range(tq // (rr * uu)):
        units = []
        for j in range(it * uu, (it + 1) * uu):
            cur = (0, blk(j), full)
            prv = (0, blk(j - 1), full) if j > 0 else (0, slice(tq - rr, tq), full)
            units.append(dict(q=(q1, cur), kc=(k1c, cur), vc=(v1c, cur),
                              kp=(k1c if j > 0 else k1p, prv), vp=(v1c if j > 0 else v1p, prv),
                              pen=None if j > 0 else pen_t, rows=pl.ds(j * rr, rr)))
        trips.append((False, units))
    nb4 = tq // d4 // rr
    for r in range(d4):
        for jt in range(nb4 // uu):
            units = []
            for j in range(jt * uu, (jt + 1) * uu):
                cur = (0, r, blk(j), full)
                prv = (0, r, blk(j - 1), full) if j > 0 else (0, r, blk(nb4 - 1), full)
                units.append(dict(q=(q4, cur), kc=(k4c, cur), vc=(v4c, cur),
                                  kp=(k4c if j > 0 else k4p, prv),
                                  vp=(v4c if j > 0 else v4p, prv),
                                  pen=None if j > 0 else pen_t,
                                  rows=pl.ds(r + d4 * rr * j, rr, stride=d4)))
            trips.append((False, units))

    pending = None
    for first, units in trips:
        wts = weights(units)
        if pending is not None:
            accumulate(*pending)
        pending = (units, wts, first)
    accumulate(*pending)

    lane_t = lax.broadcasted_iota(jnp.int32, (tq, LANES), 1)
    a0 = acc[0]
    a1 = acc[1]
    l0 = a0[:, HEAD_DIM_B:HEAD_DIM_B + 1]
    l1 = a1[:, 0:1]
    o_ref[0] = jnp.where(lane_t < HEAD_DIM_B, a0 / l0, a1 / l1).astype(o_ref.dtype)


def _attn_prompt(q, k, v):
    b, t, _ = q[0].shape
    tq = ATTN_TILE
    nt = t // tq
    prev_j = lambda j: jnp.maximum(j - 1, 0)
    cur, prev = [], []
    for dil in ATTN_DILATIONS:
        if dil == 1:
            cur.append(pl.BlockSpec((1, tq, LANES), lambda i, h, j: (i, j, h)))
            prev.append(pl.BlockSpec((1, tq, LANES), lambda i, h, j: (i, prev_j(j), h)))
        else:
            cur.append(pl.BlockSpec((1, dil, tq // dil, LANES), lambda i, h, j: (i, 0, j, h)))
            prev.append(pl.BlockSpec((1, dil, tq // dil, LANES),
                                     lambda i, h, j: (i, 0, prev_j(j), h)))
    kv_specs = [s for pc in zip(prev, cur) for s in pc]
    kv_args = lambda x: [a for xd in x for a in (xd, xd)]
    return pl.pallas_call(
        functools.partial(_attn_prompt_kernel, tq=tq),
        grid=(b, WIDTH_B // LANES, nt),
        in_specs=cur + kv_specs + kv_specs,
        out_specs=cur[0],
        out_shape=jax.ShapeDtypeStruct((b, t, WIDTH_B), BF16),
        scratch_shapes=[pltpu.VMEM((2, tq, LANES), F32), pltpu.VMEM((2, tq, LANES), F32)],
        compiler_params=pltpu.CompilerParams(
            dimension_semantics=("arbitrary", "arbitrary", "arbitrary"),
            vmem_limit_bytes=VMEM_LIMIT),
        name="attn_prompt",
    )(*q, *kv_args(k), *kv_args(v))


def _multiplicity(delta):
    delta = np.asarray(delta)
    ok = delta >= 0
    m = ((delta <= 128).astype(np.float32)
         + ((delta <= 512) & (delta % 4 == 0)).astype(np.float32)
         + ((delta <= 2048) & (delta % 16 == 0)).astype(np.float32))
    return np.where(ok, m, 0.0).astype(np.float32)


def _sample_masks(wbuf, nt):
    i = np.arange(nt)[:, None]
    return (_multiplicity(wbuf + i - np.arange(wbuf)[None, :]),
            _multiplicity(i - np.arange(nt)[None, :]))


def _run_stages(*gens):
    gens = list(gens)
    while gens:
        for g in list(gens):
            if next(g, StopIteration) is StopIteration:
                gens.remove(g)


def _attn_sample_kernel(*refs):
    _run_stages(_attn_sample_stages(*refs))


def _attn_sample_stages(q_ref, kn_ref, vn_ref, kt_ref, vt_ref, mc_ref, mn_ref, o_ref):
    dh = HEAD_DIM_B
    heads = range(N_HEADS_B)
    mc = mc_ref[...]
    mn = mn_ref[...]
    sl = lambda r, h: r[0, :, h * dh:(h + 1) * dh]
    q = {h: (sl(q_ref, h) * (dh ** -0.5)).astype(BF16) for h in heads}
    s_c = {h: jnp.where(mc > 0.0, _dot(q[h], kt_ref[0, h]), NEG) for h in heads}
    s_n = {h: jnp.where(mn > 0.0, _dot_nt(q[h], sl(kn_ref, h)), NEG) for h in heads}
    yield
    m = {h: jnp.maximum(jnp.max(s_c[h], axis=-1, keepdims=True),
                        jnp.max(s_n[h], axis=-1, keepdims=True)) for h in heads}
    p_c = {h: (mc * jnp.exp(s_c[h] - m[h])).astype(BF16) for h in heads}
    p_n = {h: (mn * jnp.exp(s_n[h] - m[h])).astype(BF16) for h in heads}
    den = {h: (jnp.sum(p_c[h].astype(F32), axis=-1, keepdims=True)
               + jnp.sum(p_n[h].astype(F32), axis=-1, keepdims=True)) for h in heads}
    yield
    out = {h: _dot_nt(p_c[h], vt_ref[0, h]) + _dot(p_n[h], sl(vn_ref, h)) for h in heads}
    yield
    o_ref[0] = jnp.concatenate([out[h] / den[h] for h in heads], axis=1)


def _attn_sample(qb, kn, vn, cache_kt, cache_vt):
    nb, nt, _ = qb.shape
    _, nh, dh, wbuf = cache_kt.shape
    masks = [jnp.asarray(m) for m in _sample_masks(wbuf, nt)]
    tok = pl.BlockSpec((1, nt, WIDTH_B), lambda i: (i, 0, 0))
    cache = pl.BlockSpec((1, nh, dh, wbuf), lambda i: (i, 0, 0, 0))
    mspec = lambda m: pl.BlockSpec(m.shape, lambda i: (0, 0))
    return pl.pallas_call(
        _attn_sample_kernel,
        grid=(nb,),
        in_specs=[tok, tok, tok, cache, cache] + [mspec(m) for m in masks],
        out_specs=tok,
        out_shape=jax.ShapeDtypeStruct((nb, nt, WIDTH_B), F32),
        compiler_params=pltpu.CompilerParams(
            dimension_semantics=("arbitrary",), vmem_limit_bytes=VMEM_LIMIT),
        name="attn_sample",
    )(qb, kn, vn, cache_kt, cache_vt, *masks)


FF_CHUNK = 1024


def _mlp_kernel(*refs):
    _run_stages(_mlp_stages(*refs))


def _mlp_stages(x_ref, oa_ref, ob_ref, wo_ref, n2_ref, wu_ref, wd_ref, y_ref):
    mix = jnp.concatenate([oa_ref[...].astype(BF16), ob_ref[...].astype(BF16)], axis=1)
    h1 = x_ref[...] + jnp.dot(mix, wo_ref[...], preferred_element_type=F32)
    hn = _rmsnorm(h1, n2_ref[...]).astype(BF16)
    yield
    chunks = range(0, wu_ref.shape[1], FF_CHUNK)
    hid = [jnp.dot(hn, wu_ref[:, c0:c0 + FF_CHUNK], preferred_element_type=F32) for c0 in chunks]
    act = [jnp.square(jnp.maximum(h, 0.0)).astype(BF16) for h in hid]
    yield
    y = h1
    for c0, a in zip(chunks, act):
        y = y + jnp.dot(a, wd_ref[c0:c0 + FF_CHUNK, :], preferred_element_type=F32)
    yield
    y_ref[...] = y


def _mlp(x, oa, ob, p, tm):
    n, d = x.shape
    row = lambda w: pl.BlockSpec((tm, w), lambda i: (i, 0))
    const = lambda a: pl.BlockSpec(a.shape, lambda i: (0, 0))
    return pl.pallas_call(
        _mlp_kernel,
        grid=(n // tm,),
        in_specs=[row(d), row(WIDTH_A), row(WIDTH_B), const(p["w_o"]), const(p["n2"]),
                  const(p["w_up"]), const(p["w_down"])],
        out_specs=row(d),
        out_shape=jax.ShapeDtypeStruct((n, d), F32),
        compiler_params=pltpu.CompilerParams(
            dimension_semantics=("arbitrary",), vmem_limit_bytes=VMEM_LIMIT),
        name="mlp",
    )(x, oa, ob, p["w_o"], p["n2"], p["w_up"], p["w_down"])


def _mlp_attn_kernel(x_ref, oa_ref, ob_ref, wo_ref, n2_ref, wu_ref, wd_ref,
                     q_ref, kn_ref, vn_ref, kt_ref, vt_ref, mc_ref, mn_ref, y_ref, o_ref):
    _run_stages(_attn_sample_stages(q_ref, kn_ref, vn_ref, kt_ref, vt_ref, mc_ref, mn_ref, o_ref),
                _mlp_stages(x_ref, oa_ref, ob_ref, wo_ref, n2_ref, wu_ref, wd_ref, y_ref))


def _attn_sample_specs(qb, cache_kt, seq0):
    nb, nt, _ = qb.shape
    _, nh, dh, wbuf = cache_kt.shape
    masks = [jnp.asarray(m) for m in _sample_masks(wbuf, nt)]

    def specs(linear_step):
        tok = pl.BlockSpec((1, nt, WIDTH_B), lambda *g: (seq0 + linear_step(*g), 0, 0))
        cache = pl.BlockSpec((1, nh, dh, wbuf), lambda *g: (seq0 + linear_step(*g), 0, 0, 0))
        out = pl.BlockSpec((1, nt, WIDTH_B), lambda *g: (linear_step(*g), 0, 0))
        mspecs = [pl.BlockSpec(m.shape, lambda *g: (0, 0), pipeline_mode=pl.Buffered(1))
                  for m in masks]
        return tok, cache, out, mspecs

    return masks, specs


def _mlp_attn(x, oa, ob, p, qb, kn, vn, cache_kt, cache_vt, tm, seq0):
    n, d = x.shape
    nt = qb.shape[1]
    steps = n // tm
    assert n % tm == 0 and seq0 + steps <= qb.shape[0]
    masks, specs = _attn_sample_specs(qb, cache_kt, seq0)
    tok, cache, tok_out, mspecs = specs(lambda i: i)
    row = lambda w: pl.BlockSpec((tm, w), lambda i: (i, 0))
    const = lambda a: pl.BlockSpec(a.shape, lambda i: (0, 0), pipeline_mode=pl.Buffered(1))
    return pl.pallas_call(
        _mlp_attn_kernel,
        grid=(steps,),
        in_specs=[row(d), row(WIDTH_A), row(WIDTH_B), const(p["w_o"]), const(p["n2"]),
                  const(p["w_up"]), const(p["w_down"]), tok, tok, tok, cache, cache] + mspecs,
        out_specs=[row(d), tok_out],
        out_shape=[jax.ShapeDtypeStruct((n, d), F32),
                   jax.ShapeDtypeStruct((steps, nt, WIDTH_B), F32)],
        compiler_params=pltpu.CompilerParams(
            dimension_semantics=("arbitrary",), vmem_limit_bytes=VMEM_LIMIT),
        name="mlp_attn",
    )(x, oa, ob, p["w_o"], p["n2"], p["w_up"], p["w_down"], qb, kn, vn, cache_kt, cache_vt, *masks)


def _layer_params(norm1_g, w_in, conv_w, a_log, dt_bias, delta_norm_g, q_norm_g, k_norm_g, w_o,
                  norm2_g, w_up, w_down):
    d = w_in.shape[0]
    n_gate = 2 * N_HEADS_A
    gate0 = CONV_CH + WIDTH_A
    w_a = w_in[:, :gate0].astype(BF16)
    w_b = w_in[:, gate0 + n_gate:].astype(BF16)
    w_g = jnp.pad(w_in[:, gate0:gate0 + n_gate], ((0, 0), (0, LANES - n_gate))).astype(BF16)
    lane_pad = lambda a: jnp.zeros((1, LANES), F32).at[0, N_HEADS_A:n_gate].set(a.astype(F32))
    hid = np.arange(WIDTH_B) // HEAD_DIM_B
    head_mean = jnp.asarray((hid[:, None] == hid[None, :]).astype(np.float32) / HEAD_DIM_B, BF16)
    return {
        "n1": norm1_g.reshape(1, d).astype(F32), "w_a": w_a, "w_b": w_b, "w_g": w_g, "conv_w": conv_w.astype(F32),
        "alog": lane_pad(a_log), "dtb": lane_pad(dt_bias),
        "qng": jnp.tile(q_norm_g.astype(F32), N_HEADS_B).reshape(1, WIDTH_B),
        "kng": jnp.tile(k_norm_g.astype(F32), N_HEADS_B).reshape(1, WIDTH_B),
        "hm": head_mean, "dng": delta_norm_g.reshape(1, HEAD_DIM_A).astype(F32),
        "w_o": w_o.astype(BF16), "n2": norm2_g.reshape(1, d).astype(F32),
        "w_up": w_up.astype(BF16), "w_down": w_down.astype(BF16),
    }


MLP_TILE = 256
PROJ_TILE = 256


def _layer(xp, xs, state_conv, s0_s, cache_k, cache_v, p):
    b, t, d = xp.shape
    nb, nt, _ = xs.shape
    n = b * t
    pbuf = min(MAX_WINDOW, t)

    qa, ka, va, z, gt, qb, kn, vn, conv_s = _proj_sample(xs, state_conv, p, bt=min(nb, 64))
    oa_s, s_s = _delta(qa, ka, va, z, gt, s0_s, p["dng"], chunk=nt, nchunk=1, bb=min(nb, 16),
                       out_dtype=F32)
    ckt, cvt = cache_k.transpose(0, 2, 3, 1), cache_v.transpose(0, 2, 3, 1)
    attn_s = (qb, kn, vn, ckt, cvt)

    steps = n // PROJ_TILE
    fuse = n % PROJ_TILE == 0 and PROJ_TILE == MLP_TILE and 2 * steps == nb
    qa, ka, va, z, gt, kp, vp, conv_p, *dil = _proj_prompt(
        xp, p, tm=PROJ_TILE, pbuf=pbuf, attn=attn_s if fuse else None)
    window = lambda a: a.reshape(b, N_HEADS_B, HEAD_DIM_B, pbuf).transpose(0, 3, 1, 2)
    kp, vp = window(kp), window(vp)
    s0_p = jnp.zeros((b, N_HEADS_A, HEAD_DIM_A, HEAD_DIM_A), F32)
    oa_p, s_p = _delta(qa, ka, va, z, gt, s0_p, p["dng"], chunk=DELTA_CHUNK, nchunk=2, bb=b,
                       out_dtype=BF16)
    ob_p = _attn_prompt(dil[0:3], dil[3:6], dil[6:9])

    mlp_in = (xp.reshape(n, d), oa_p.reshape(n, WIDTH_A), ob_p.reshape(n, WIDTH_B), p)
    if fuse:
        yp, ob_s2 = _mlp_attn(*mlp_in, *attn_s, tm=MLP_TILE, seq0=steps)
        ob_s = jnp.concatenate([dil[9], ob_s2], axis=0)
    else:
        yp = _mlp(*mlp_in, tm=MLP_TILE)
        ob_s = _attn_sample(*attn_s)
    ys = _mlp(xs.reshape(nb * nt, d), oa_s.reshape(nb * nt, WIDTH_A),
              ob_s.reshape(nb * nt, WIDTH_B), p, tm=min(MLP_TILE, nb * nt))
    heads = lambda a: a.reshape(nb, nt, N_HEADS_B, HEAD_DIM_B)
    return ((yp.reshape(b, t, d), kp, vp, s_p, conv_p),
            (ys.reshape(nb, nt, d), heads(kn), heads(vn), s_s, conv_s))


def kernel(x_prompt, x_sample, cache_swa_k, cache_swa_v, state_delta, state_conv, norm1_g, w_in,
           conv_w, a_log, dt_bias, delta_norm_g, q_norm_g, k_norm_g, w_o, norm2_g, w_up, w_down):
    depth = w_in.shape[0]
    b, s, _ = x_prompt.shape
    nb, nt, _ = x_sample.shape
    wbuf = cache_swa_k.shape[2]
    assert s % ATTN_TILE == 0 and nt == SUBLANES and wbuf == MAX_WINDOW
    yp, ys = x_prompt, x_sample
    outs = [[] for _ in range(8)]
    for layer in range(depth):
        p = _layer_params(norm1_g[layer], w_in[layer], conv_w[layer], a_log[layer], dt_bias[layer],
                          delta_norm_g[layer], q_norm_g[layer], k_norm_g[layer], w_o[layer],
                          norm2_g[layer], w_up[layer], w_down[layer])
        (yp, kp, vp, dp, cp), (ys, kn, vn, dn, cn) = _layer(
            yp, ys, state_conv[layer], state_delta[layer], cache_swa_k[layer], cache_swa_v[layer], p)
        for lst, val in zip(outs, (kp, vp, dp, cp, kn, vn, dn, cn)):
            lst.append(val)
    return (yp, ys) + tuple(jnp.stack(o) for o in outs)
```

```python
import functools

import numpy as np
import jax
import jax.numpy as jnp
from jax import lax
from jax.experimental import pallas as pl
from jax.experimental.pallas import tpu as pltpu

F32 = jnp.float32
BF16 = jnp.bfloat16

N_HEADS_A = 4
HEAD_DIM_A = 128
WIDTH_A = N_HEADS_A * HEAD_DIM_A
N_HEADS_B = 8
HEAD_DIM_B = 64
WIDTH_B = N_HEADS_B * HEAD_DIM_B
CONV_WIDTH = 4
CONV_CH = 3 * WIDTH_A
DELTA_CHUNK = 64
MAX_WINDOW = 2048
NORM_EPS = 1e-6
LANES = 128
SUBLANES = 8
NEG = -1e30
LOG2_E = 1.4426950408889634

C_QKV, C_Z = 0, CONV_CH
C_QB, C_KB, C_VB = 0, WIDTH_B, 2 * WIDTH_B

ATTN_DILATIONS = (1, 4, 16)
ATTN_BAND = 128
ATTN_TILE = 2048
ATTN_UNROLL = 2
VMEM_LIMIT = 56 * 1024 * 1024


def _dot(a, b):
    return jnp.dot(a.astype(BF16), b.astype(BF16), preferred_element_type=F32)


def _dot_nt(a, b):
    return lax.dot_general(a.astype(BF16), b.astype(BF16), (((1,), (1,)), ((), ())),
                           preferred_element_type=F32)


def _dot_tn(a, b):
    return lax.dot_general(a.astype(BF16), b.astype(BF16), (((0,), (0,)), ((), ())),
                           preferred_element_type=F32)


def _split2(x):
    hi = x.astype(BF16)
    lo = (x - hi.astype(F32)).astype(BF16)
    return hi, lo


def _split3(x):
    hi = x.astype(BF16)
    r = x - hi.astype(F32)
    mid = r.astype(BF16)
    lo = (r - mid.astype(F32)).astype(BF16)
    return hi, mid, lo


def _dot_exact_lhs(mask_bf16, x):
    hi, mid, lo = _split3(x)
    d = lambda p: jnp.dot(mask_bf16, p, preferred_element_type=F32)
    return d(hi) + d(mid) + d(lo)


def _sigmoid(x):
    return 0.5 * jnp.tanh(0.5 * x) + 0.5


def _silu(x):
    return x * _sigmoid(x)


def _softplus(x):
    return jnp.maximum(x, 0.0) + jnp.log1p(jnp.exp(-jnp.abs(x)))


def _rmsnorm(x, g):
    return x * lax.rsqrt(jnp.mean(x * x, axis=-1, keepdims=True) + NORM_EPS) * g


def _proj_body(x, ext_ref, shift, pad, refs, nsplit=1):
    n1_ref, wa_ref, wb_ref, wg_ref = refs[0:4]
    rows = x.shape[0] // nsplit
    raws = []
    for part in range(nsplit):
        hn = _rmsnorm(x[part * rows:(part + 1) * rows], n1_ref[...]).astype(BF16)
        proj = lambda w, hn=hn: jnp.dot(hn, w, preferred_element_type=F32)
        raws.append((proj(wa_ref[:, C_QKV:C_QKV + CONV_CH]),
                     proj(wb_ref[:, C_QB:C_QB + WIDTH_B]), proj(wb_ref[:, C_KB:C_KB + WIDTH_B]),
                     proj(wb_ref[:, C_VB:C_VB + WIDTH_B]),
                     proj(wa_ref[:, C_Z:C_Z + WIDTH_A]), proj(wg_ref[...])))
    outs = [_proj_epilogue(raw, ext_ref, shift, pad + part * rows, refs)
            for part, raw in enumerate(raws)]
    return tuple(jnp.concatenate(vals, axis=0) if nsplit > 1 else vals[0] for vals in zip(*outs))


def _proj_epilogue(raw, ext_ref, shift, pad, refs):
    cw_ref, al_ref, dtb_ref, qng_ref, kng_ref, hm_ref = refs[4:]
    u, qb, kb, vb, z, gc = raw
    tm = u.shape[0]
    ext_ref[pad:pad + tm, :] = u
    cw = cw_ref[...]
    y = u * cw[3:4, :]
    for i in range(CONV_WIDTH - 1):
        off = pad - (CONV_WIDTH - 1 - i) * shift
        y = y + ext_ref[off:off + tm, :] * cw[i:i + 1, :]
    y = _silu(y)
    qa, ka = [], []
    for h in range(N_HEADS_A):
        lo = h * HEAD_DIM_A
        qh = y[:, lo:lo + HEAD_DIM_A]
        qa.append(qh * (lax.rsqrt(
            jnp.sum(qh * qh, axis=-1, keepdims=True) + NORM_EPS) * HEAD_DIM_A ** -0.5))
        kh = y[:, WIDTH_A + lo:WIDTH_A + lo + HEAD_DIM_A]
        ka.append(kh * lax.rsqrt(jnp.sum(kh * kh, axis=-1, keepdims=True) + NORM_EPS))
    qa = jnp.concatenate(qa, axis=1)
    ka = jnp.concatenate(ka, axis=1)
    va = y[:, 2 * WIDTH_A:3 * WIDTH_A]

    hm = hm_ref[...]

    def headnorm(v, g):
        ms = jnp.dot((v * v).astype(BF16), hm, preferred_element_type=F32)
        return v * lax.rsqrt(ms + NORM_EPS) * g

    qb = headnorm(qb, qng_ref[...])
    kb = headnorm(kb, kng_ref[...])

    lane = lax.broadcasted_iota(jnp.int32, gc.shape, 1)
    beta = _sigmoid(gc)
    g = -jnp.exp(al_ref[...]) * _softplus(gc + dtb_ref[...])
    gates = jnp.where(lane < N_HEADS_A, beta, g)
    return qa, ka, va, z, gates, qb, kb, vb


PROJ_SPLIT = 2
N_PROJ_PARAMS = 10
N_ATTN_SAMPLE_INPUTS = 7


def _proj_prompt_kernel(*refs, first_win, with_attn):
    x_ref, params = refs[0], refs[1:1 + N_PROJ_PARAMS]
    pos = 1 + N_PROJ_PARAMS
    attn_in = refs[pos:pos + N_ATTN_SAMPLE_INPUTS] if with_attn else ()
    pos += len(attn_in)
    outs = refs[pos:pos + 17]
    pos += 17
    attn_out = refs[pos:pos + 1] if with_attn else ()
    pos += len(attn_out)
    scratch = refs[pos:]
    stages = [_proj_prompt_stages(x_ref, params, outs, scratch, first_win)]
    if with_attn:
        stages.insert(0, _attn_sample_stages(*attn_in, *attn_out))
    _run_stages(*stages)


def _proj_prompt_stages(x_ref, params, outs, scratch, first_win):
    (qa_ref, ka_ref, va_ref, z_ref, gt_ref, kb_ref, vb_ref, cn_ref,
     q1_ref, q4_ref, q16_ref, k1_ref, k4_ref, k16_ref, v1_ref, v4_ref, v16_ref) = outs
    ext_ref, dil_ref = scratch
    t = pl.program_id(1)
    tm = x_ref.shape[1]

    @pl.when(t == 0)
    def _():
        ext_ref[0:SUBLANES, :] = jnp.zeros((SUBLANES, CONV_CH), F32)

    @pl.when(t > 0)
    def _():
        ext_ref[0:SUBLANES, :] = ext_ref[tm:tm + SUBLANES, :]

    qa, ka, va, z, gates, qb, kb, vb = _proj_body(x_ref[0], ext_ref, 1, SUBLANES, params,
                                                  nsplit=PROJ_SPLIT)
    yield
    for r, v in zip((qa_ref, ka_ref, va_ref, z_ref, gt_ref), (qa, ka, va, z, gates)):
        r[0] = v
    cn_ref[0] = ext_ref[tm + SUBLANES - (CONV_WIDTH - 1):tm + SUBLANES, :]

    @pl.when(t >= first_win)
    def _():
        kb_ref[0] = kb.T
        vb_ref[0] = vb.T

    yield
    nchunk = WIDTH_B // LANES
    q_att = qb * (HEAD_DIM_B ** -0.5 * LOG2_E)
    for ai, (val, r1, r4, r16) in enumerate(((q_att, q1_ref, q4_ref, q16_ref),
                                              (kb, k1_ref, k4_ref, k16_ref),
                                              (vb, v1_ref, v4_ref, v16_ref))):
        r1[0] = val.astype(BF16)
        for c in range(nchunk):
            dil_ref[0, ai * nchunk + c] = val[:, c * LANES:(c + 1) * LANES]
        n4, n16 = tm // 4, tm // 16
        for r in range(4):
            parts = [dil_ref[0, ai * nchunk + c, pl.ds(r, n4, stride=4), :] for c in range(nchunk)]
            r4[0, r] = jnp.concatenate(parts, axis=1).astype(BF16)
            for c in range(nchunk):
                dil_ref[1, ai * nchunk + c, r * n4:(r + 1) * n4, :] = parts[c]
        for r in range(4):
            for r2 in range(4):
                r16[0, r + 4 * r2] = jnp.concatenate(
                    [dil_ref[1, ai * nchunk + c, pl.ds(r * n4 + r2, n16, stride=4), :]
                     for c in range(nchunk)], axis=1).astype(BF16)


def _proj_sample_kernel(x_ref, st_ref, n1_ref, wa_ref, wb_ref, wg_ref, cw_ref, al_ref, dtb_ref,
                        qng_ref, kng_ref, hm_ref, qa_ref, ka_ref, va_ref, z_ref, gt_ref, qb_ref, kb_ref, vb_ref,
                        cn_ref, ext_ref):
    nt, nb, d = x_ref.shape
    ncv = CONV_WIDTH - 1
    tm = nb * nt
    pad = ncv * nb
    ext_ref[0:pad, :] = st_ref[...].reshape(pad, CONV_CH)
    vals = _proj_body(x_ref[...].reshape(tm, d), ext_ref, nb, pad,
                      (n1_ref, wa_ref, wb_ref, wg_ref, cw_ref, al_ref, dtb_ref, qng_ref, kng_ref,
                       hm_ref))
    for r, v in zip((qa_ref, ka_ref, va_ref, z_ref, gt_ref, qb_ref, kb_ref, vb_ref), vals):
        r[...] = v.reshape(r.shape)
    cn_ref[...] = ext_ref[tm:tm + pad, :].reshape(ncv, nb, CONV_CH)


def _proj_params(p):
    full = lambda a: pl.BlockSpec(a.shape, lambda *_: (0,) * a.ndim,
                                  pipeline_mode=pl.Buffered(1))
    arrs = (p["n1"], p["w_a"], p["w_b"], p["w_g"], p["conv_w"], p["alog"], p["dtb"], p["qng"],
            p["kng"], p["hm"])
    return arrs, [full(a) for a in arrs]


def _proj_prompt(x, p, tm, pbuf, attn=None):
    b, t, d = x.shape
    assert pbuf % tm == 0 and t % tm == 0
    first_win = (t - pbuf) // tm
    arrs, specs = _proj_params(p)
    assert len(arrs) == N_PROJ_PARAMS
    steps_t = t // tm
    attn_args, attn_in_specs, attn_out_shape, attn_out_specs = [], [], [], []
    if attn is not None:
        qb, kn, vn, ckt, cvt = attn
        assert b * steps_t <= qb.shape[0]
        masks, mk = _attn_sample_specs(qb, ckt, 0)
        tok, cache, tok_out, mspecs = mk(lambda i, j: i * steps_t + j)
        attn_args = [qb, kn, vn, ckt, cvt] + masks
        attn_in_specs = [tok, tok, tok, cache, cache] + mspecs
        assert len(attn_args) == N_ATTN_SAMPLE_INPUTS
        attn_out_shape = [jax.ShapeDtypeStruct((b * steps_t, qb.shape[1], WIDTH_B), F32)]
        attn_out_specs = [tok_out]
    row = lambda w: pl.BlockSpec((1, tm, w), lambda i, j: (i, j, 0))
    widths = (WIDTH_A, WIDTH_A, WIDTH_A, WIDTH_A, LANES)
    out_shape = [jax.ShapeDtypeStruct((b, t, w), F32) for w in widths]
    out_specs = [row(w) for w in widths]
    for _ in range(2):
        out_shape.append(jax.ShapeDtypeStruct((b, WIDTH_B, pbuf), F32))
        out_specs.append(pl.BlockSpec((1, WIDTH_B, tm),
                                      lambda i, j: (i, 0, jnp.maximum(j - first_win, 0))))
    out_shape.append(jax.ShapeDtypeStruct((b, CONV_WIDTH - 1, CONV_CH), F32))
    out_specs.append(pl.BlockSpec((1, CONV_WIDTH - 1, CONV_CH), lambda i, j: (i, 0, 0)))
    for _ in range(3):
        out_shape.append(jax.ShapeDtypeStruct((b, t, WIDTH_B), BF16))
        out_specs.append(row(WIDTH_B))
        for dil in ATTN_DILATIONS[1:]:
            out_shape.append(jax.ShapeDtypeStruct((b, dil, t // dil, WIDTH_B), BF16))
            out_specs.append(pl.BlockSpec((1, dil, tm // dil, WIDTH_B), lambda i, j: (i, 0, j, 0)))
    return pl.pallas_call(
        functools.partial(_proj_prompt_kernel, first_win=first_win, with_attn=attn is not None),
        grid=(b, steps_t),
        in_specs=[row(d)] + specs + attn_in_specs,
        out_specs=out_specs + attn_out_specs,
        out_shape=out_shape + attn_out_shape,
        scratch_shapes=[pltpu.VMEM((tm + SUBLANES, CONV_CH), F32),
                        pltpu.VMEM((2, 3 * WIDTH_B // LANES, tm, LANES), F32)],
        compiler_params=pltpu.CompilerParams(
            dimension_semantics=("arbitrary", "arbitrary"), vmem_limit_bytes=VMEM_LIMIT),
        name="proj_prompt",
    )(x, *arrs, *attn_args)


def _proj_sample(x, state_conv, p, bt):
    nb, nt, d = x.shape
    ncv = CONV_WIDTH - 1
    arrs, specs = _proj_params(p)
    blk = lambda r, w: pl.BlockSpec((r, bt, w), lambda i: (0, i, 0))
    widths = (WIDTH_A, WIDTH_A, WIDTH_A, WIDTH_A, LANES, WIDTH_B, WIDTH_B, WIDTH_B)
    out_shape = [jax.ShapeDtypeStruct((nt, nb, w), F32) for w in widths]
    out_shape.append(jax.ShapeDtypeStruct((ncv, nb, CONV_CH), F32))
    out_specs = [blk(nt, w) for w in widths] + [blk(ncv, CONV_CH)]
    outs = pl.pallas_call(
        _proj_sample_kernel,
        grid=(nb // bt,),
        in_specs=[blk(nt, d), blk(ncv, CONV_CH)] + specs,
        out_specs=out_specs,
        out_shape=out_shape,
        scratch_shapes=[pltpu.VMEM((bt * (nt + ncv), CONV_CH), F32)],
        compiler_params=pltpu.CompilerParams(
            dimension_semantics=("arbitrary",), vmem_limit_bytes=VMEM_LIMIT),
        name="proj_sample",
    )(x.transpose(1, 0, 2), state_conv.transpose(1, 0, 2), *arrs)
    return [o.transpose(1, 0, 2) for o in outs]


def _delta_kernel(q_ref, k_ref, v_ref, z_ref, gt_ref, s0_ref, ng_ref, o_ref, sf_ref, s_scr,
                  *, chunk, nchunk, bb):
    c = chunk
    t = pl.program_id(1)

    @pl.when(t == 0)
    def _():
        s_scr[...] = s0_ref[...]

    ii = lax.broadcasted_iota(jnp.int32, (c, c), 0)
    jj = lax.broadcasted_iota(jnp.int32, (c, c), 1)
    causal = ii >= jj
    strict = ii > jj
    tril_bf = causal.astype(F32).astype(BF16)
    eye = (ii == jj).astype(F32)
    level_masks = []
    s = 1
    while s < c:
        level_masks.append(((ii // (2 * s)) == (jj // (2 * s))) & ((ii & s) != 0) & ((jj & s) == 0))
        s *= 2
    ng = ng_ref[...]

    heads = range(N_HEADS_A)
    items = [(bi, ci, h) for bi in range(bb) for ci in range(nchunk) for h in heads]
    pre = {}
    for bi in range(bb):
        for ci in range(nchunk):
            rows = slice(ci * c, (ci + 1) * c)
            gt = gt_ref[bi, rows, :]
            gcum = _dot_exact_lhs(tril_bf, gt)
            if c < LANES:
                gpad = jnp.concatenate([gcum, jnp.zeros((LANES - c, LANES), F32)], axis=0)
            else:
                gpad = gcum
            gcum_t = gpad.T
            for h in heads:
                lo = h * HEAD_DIM_A
                q = q_ref[bi, rows, lo:lo + HEAD_DIM_A]
                k = k_ref[bi, rows, lo:lo + HEAD_DIM_A]
                v = v_ref[bi, rows, lo:lo + HEAD_DIM_A]
                beta = gt[:, h:h + 1]
                g_col = gcum[:, N_HEADS_A + h:N_HEADS_A + h + 1]
                g_row = gcum_t[N_HEADS_A + h:N_HEADS_A + h + 1, 0:c]
                g_last = gcum[c - 1:c, N_HEADS_A + h:N_HEADS_A + h + 1]
                decay = jnp.where(causal, jnp.exp(jnp.where(causal, g_col - g_row, 0.0)), 0.0)
                exp_g = jnp.exp(g_col)
                kb = k * beta
                pre[bi, ci, h] = dict(
                    q=q, k=k, kb=kb, decay=decay, g_last=g_last,
                    rhs=jnp.concatenate([v * beta, kb * exp_g], axis=1),
                    q_dec=q * exp_g, k_dec=k * jnp.exp(g_last - g_col))
    kq = {it: _dot_nt(jnp.concatenate([pre[it]["kb"], pre[it]["q"]], axis=0), pre[it]["k"])
          for it in items}
    lmat = {it: jnp.where(strict, kq[it][0:c] * pre[it]["decay"], 0.0) for it in items}
    qk = {it: kq[it][c:2 * c] * pre[it]["decay"] for it in items}
    tinv = {it: eye - jnp.where(level_masks[0], lmat[it], 0.0) for it in items}
    for msk in level_masks[1:]:
        te = {it: _dot(tinv[it], jnp.where(msk, lmat[it], 0.0)) for it in items}
        tinv = {it: tinv[it] - _dot(te[it], tinv[it]) for it in items}
    uw = {it: _dot(tinv[it], pre[it]["rhs"]) for it in items}

    for bi in range(bb):
        st = {h: s_scr[bi, h] for h in heads}
        for ci in range(nchunk):
            rows = slice(ci * c, (ci + 1) * c)
            ws = {h: _dot(jnp.concatenate([uw[bi, ci, h][:, HEAD_DIM_A:2 * HEAD_DIM_A],
                                           pre[bi, ci, h]["q_dec"]], axis=0), st[h])
                  for h in heads}
            v_new = {h: uw[bi, ci, h][:, 0:HEAD_DIM_A] - ws[h][0:c] for h in heads}
            if c == DELTA_CHUNK:
                both = {h: _dot(jnp.concatenate([qk[bi, ci, h], pre[bi, ci, h]["k_dec"].T], axis=0),
                                v_new[h]) for h in heads}
                o = {h: ws[h][c:2 * c] + both[h][0:c] for h in heads}
                st = {h: st[h] * jnp.exp(pre[bi, ci, h]["g_last"]) + both[h][c:c + HEAD_DIM_A]
                      for h in heads}
            else:
                o = {h: ws[h][c:2 * c] + _dot(qk[bi, ci, h], v_new[h]) for h in heads}
                st = {h: st[h] * jnp.exp(pre[bi, ci, h]["g_last"])
                      + _dot_tn(pre[bi, ci, h]["k_dec"], v_new[h]) for h in heads}
            for h in heads:
                lo = h * HEAD_DIM_A
                zz = z_ref[bi, rows, lo:lo + HEAD_DIM_A]
                o_ref[bi, rows, lo:lo + HEAD_DIM_A] = (
                    _rmsnorm(o[h], ng) * _silu(zz)).astype(o_ref.dtype)
        for h in heads:
            s_scr[bi, h] = st[h]

    @pl.when(t == pl.num_programs(1) - 1)
    def _():
        sf_ref[...] = s_scr[...]


def _delta(qa, ka, va, z, gt, s0, ng, *, chunk, nchunk, bb, out_dtype):
    b, t, _ = qa.shape
    tc = chunk * nchunk
    row = lambda w: pl.BlockSpec((bb, tc, w), lambda i, j: (i, j, 0))
    sspec = pl.BlockSpec((bb, N_HEADS_A, HEAD_DIM_A, HEAD_DIM_A), lambda i, j: (i, 0, 0, 0))
    return pl.pallas_call(
        functools.partial(_delta_kernel, chunk=chunk, nchunk=nchunk, bb=bb),
        grid=(b // bb, t // tc),
        in_specs=[row(WIDTH_A)] * 4 + [row(LANES), sspec,
                                       pl.BlockSpec((1, HEAD_DIM_A), lambda i, j: (0, 0))],
        out_specs=[row(WIDTH_A), sspec],
        out_shape=[jax.ShapeDtypeStruct((b, t, WIDTH_A), out_dtype),
                   jax.ShapeDtypeStruct((b, N_HEADS_A, HEAD_DIM_A, HEAD_DIM_A), F32)],
        scratch_shapes=[pltpu.VMEM((bb, N_HEADS_A, HEAD_DIM_A, HEAD_DIM_A), F32)],
        compiler_params=pltpu.CompilerParams(
            dimension_semantics=("arbitrary", "arbitrary"), vmem_limit_bytes=VMEM_LIMIT),
        name="delta_c%d" % chunk,
    )(qa, ka, va, z, gt, s0, ng)


def _attn_prompt_kernel(q1, q4, q16, k1p, k1c, k4p, k4c, k16p, k16c, v1p, v1c, v4p, v4c, v16p, v16c,
                        o_ref, acc, mrep, *, tq):
    t = pl.program_id(2)
    rr = ATTN_BAND
    ii = lax.broadcasted_iota(jnp.int32, (rr, rr), 0)
    jj = lax.broadcasted_iota(jnp.int32, (rr, rr), 1)
    lower = jj <= ii
    upper = jj >= ii
    lane = lax.broadcasted_iota(jnp.int32, (rr, LANES), 1)
    own = [(lane >= hh * HEAD_DIM_B) & (lane < (hh + 1) * HEAD_DIM_B) for hh in range(2)]
    pen_t = jnp.where(t > 0, jnp.float32(0.0), jnp.float32(NEG))
    one = jnp.ones((rr, LANES), BF16)
    zero = jnp.zeros((rr, LANES), BF16)

    ld = lambda ref_idx: ref_idx[0][ref_idx[1]]

    def weights(units):
        chains = [(j, hh) for j in range(len(units)) for hh in range(2)]
        qv = {ch: jnp.where(own[ch[1]], ld(units[ch[0]]["q"]), zero) for ch in chains}
        kcat = [jnp.concatenate([ld(u["kc"]), ld(u["kp"])], axis=0) for u in units]
        raw = {ch: _dot_nt(qv[ch], kcat[ch[0]]) for ch in chains}
        out = {}
        for ch in chains:
            s_c = jnp.where(lower, raw[ch][:, 0:rr], NEG)
            s_p = raw[ch][:, rr:2 * rr]
            if units[ch[0]]["pen"] is not None:
                s_p = s_p + units[ch[0]]["pen"]
            s_p = jnp.where(upper, s_p, NEG)
            m = jnp.max(jnp.maximum(s_c, s_p), axis=-1, keepdims=True)
            out[ch] = (jnp.exp2(s_c - m).astype(BF16), jnp.exp2(s_p - m).astype(BF16), m)
        return out

    def accumulate(units, wts, first):
        vc = [ld(u["vc"]) for u in units]
        vp = [ld(u["vp"]) for u in units]
        pv = {ch: jnp.dot(jnp.concatenate([p_c, p_p], axis=1),
                          jnp.concatenate([jnp.where(own[ch[1]], vc[ch[0]], one),
                                           jnp.where(own[ch[1]], vp[ch[0]], one)], axis=0),
                          preferred_element_type=F32)
              for ch, (p_c, p_p, _) in wts.items()}
        for ch, (_, _, m) in wts.items():
            j, hh = ch
            rows = units[j]["rows"]
            m_b = jnp.broadcast_to(m, (rr, LANES))
            if first:
                acc[hh, rows, :] = pv[ch]
                mrep[hh, rows, :] = m_b
            else:
                m_old = mrep[hh, rows, :]
                m_new = jnp.maximum(m_old, m_b)
                acc[hh, rows, :] = (acc[hh, rows, :] * jnp.exp2(m_old - m_new)
                                    + pv[ch] * jnp.exp2(m_b - m_new))
                mrep[hh, rows, :] = m_new

    uu = ATTN_UNROLL
    blk = lambda j: slice(j * rr, (j + 1) * rr)

    d16, d4 = ATTN_DILATIONS[2], ATTN_DILATIONS[1]
    assert tq // d16 == rr and (tq // (d4 * rr)) % uu == 0 and d16 % uu == 0
    trips = []
    full = slice(None)
    for it in range(d16 // uu):
        trips.append((True, [dict(
            q=(q16, (0, r)), kc=(k16c, (0, r)), kp=(k16p, (0, r)), vc=(v16c, (0, r)),
            vp=(v16p, (0, r)), pen=pen_t, rows=pl.ds(r, rr, stride=d16))
            for r in range(it * uu, (it + 1) * uu)]))
    for it in range(tq // (rr * uu)):
        units = []
        for j in range(it * uu, (it + 1) * uu):
            cur = (0, blk(j), full)
            prv = (0, blk(j - 1), full) if j > 0 else (0, slice(tq - rr, tq), full)
            units.append(dict(q=(q1, cur), kc=(k1c, cur), vc=(v1c, cur),
                              kp=(k1c if j > 0 else k1p, prv), vp=(v1c if j > 0 else v1p, prv),
                              pen=None if j > 0 else pen_t, rows=pl.ds(j * rr, rr)))
        trips.append((False, units))
    nb4 = tq // d4 // rr
    for r in range(d4):
        for jt in range(nb4 // uu):
            units = []
            for j in range(jt * uu, (jt + 1) * uu):
                cur = (0, r, blk(j), full)
                prv = (0, r, blk(j - 1), full) if j > 0 else (0, r, blk(nb4 - 1), full)
                units.append(dict(q=(q4, cur), kc=(k4c, cur), vc=(v4c, cur),
                                  kp=(k4c if j > 0 else k4p, prv),
                                  vp=(v4c if j > 0 else v4p, prv),
                                  pen=None if j > 0 else pen_t,
                                  rows=pl.ds(r + d4 * rr * j, rr, stride=d4)))
            trips.append((False, units))

    pending = None
    for first, units in trips:
        wts = weights(units)
        if pending is not None:
            accumulate(*pending)
        pending = (units, wts, first)
    accumulate(*pending)

    lane_t = lax.broadcasted_iota(jnp.int32, (tq, LANES), 1)
    a0 = acc[0]
    a1 = acc[1]
    l0 = a0[:, HEAD_DIM_B:HEAD_DIM_B + 1]
    l1 = a1[:, 0:1]
    o_ref[0] = jnp.where(lane_t < HEAD_DIM_B, a0 / l0, a1 / l1).astype(o_ref.dtype)


def _attn_prompt(q, k, v):
    b, t, _ = q[0].shape
    tq = ATTN_TILE
    nt = t // tq
    prev_j = lambda j: jnp.maximum(j - 1, 0)
    cur, prev = [], []
    for dil in ATTN_DILATIONS:
        if dil == 1:
            cur.append(pl.BlockSpec((1, tq, LANES), lambda i, h, j: (i, j, h)))
            prev.append(pl.BlockSpec((1, tq, LANES), lambda i, h, j: (i, prev_j(j), h)))
        else:
            cur.append(pl.BlockSpec((1, dil, tq // dil, LANES), lambda i, h, j: (i, 0, j, h)))
            prev.append(pl.BlockSpec((1, dil, tq // dil, LANES),
                                     lambda i, h, j: (i, 0, prev_j(j), h)))
    kv_specs = [s for pc in zip(prev, cur) for s in pc]
    kv_args = lambda x: [a for xd in x for a in (xd, xd)]
    return pl.pallas_call(
        functools.partial(_attn_prompt_kernel, tq=tq),
        grid=(b, WIDTH_B // LANES, nt),
        in_specs=cur + kv_specs + kv_specs,
        out_specs=cur[0],
        out_shape=jax.ShapeDtypeStruct((b, t, WIDTH_B), BF16),
        scratch_shapes=[pltpu.VMEM((2, tq, LANES), F32), pltpu.VMEM((2, tq, LANES), F32)],
        compiler_params=pltpu.CompilerParams(
            dimension_semantics=("arbitrary", "arbitrary", "arbitrary"),
            vmem_limit_bytes=VMEM_LIMIT),
        name="attn_prompt",
    )(*q, *kv_args(k), *kv_args(v))


def _multiplicity(delta):
    delta = np.asarray(delta)
    ok = delta >= 0
    m = ((delta <= 128).astype(np.float32)
         + ((delta <= 512) & (delta % 4 == 0)).astype(np.float32)
         + ((delta <= 2048) & (delta % 16 == 0)).astype(np.float32))
    return np.where(ok, m, 0.0).astype(np.float32)


def _sample_masks(wbuf, nt):
    i = np.arange(nt)[:, None]
    return (_multiplicity(wbuf + i - np.arange(wbuf)[None, :]),
            _multiplicity(i - np.arange(nt)[None, :]))


def _run_stages(*gens):
    gens = list(gens)
    while gens:
        for g in list(gens):
            if next(g, StopIteration) is StopIteration:
                gens.remove(g)


def _attn_sample_kernel(*refs):
    _run_stages(_attn_sample_stages(*refs))


def _attn_sample_stages(q_ref, kn_ref, vn_ref, kt_ref, vt_ref, mc_ref, mn_ref, o_ref):
    dh = HEAD_DIM_B
    heads = range(N_HEADS_B)
    mc = mc_ref[...]
    mn = mn_ref[...]
    sl = lambda r, h: r[0, :, h * dh:(h + 1) * dh]
    q = {h: (sl(q_ref, h) * (dh ** -0.5)).astype(BF16) for h in heads}
    s_c = {h: jnp.where(mc > 0.0, _dot(q[h], kt_ref[0, h]), NEG) for h in heads}
    s_n = {h: jnp.where(mn > 0.0, _dot_nt(q[h], sl(kn_ref, h)), NEG) for h in heads}
    yield
    m = {h: jnp.maximum(jnp.max(s_c[h], axis=-1, keepdims=True),
                        jnp.max(s_n[h], axis=-1, keepdims=True)) for h in heads}
    p_c = {h: (mc * jnp.exp(s_c[h] - m[h])).astype(BF16) for h in heads}
    p_n = {h: (mn * jnp.exp(s_n[h] - m[h])).astype(BF16) for h in heads}
    den = {h: (jnp.sum(p_c[h].astype(F32), axis=-1, keepdims=True)
               + jnp.sum(p_n[h].astype(F32), axis=-1, keepdims=True)) for h in heads}
    yield
    out = {h: _dot_nt(p_c[h], vt_ref[0, h]) + _dot(p_n[h], sl(vn_ref, h)) for h in heads}
    yield
    o_ref[0] = jnp.concatenate([out[h] / den[h] for h in heads], axis=1)


def _attn_sample(qb, kn, vn, cache_kt, cache_vt):
    nb, nt, _ = qb.shape
    _, nh, dh, wbuf = cache_kt.shape
    masks = [jnp.asarray(m) for m in _sample_masks(wbuf, nt)]
    tok = pl.BlockSpec((1, nt, WIDTH_B), lambda i: (i, 0, 0))
    cache = pl.BlockSpec((1, nh, dh, wbuf), lambda i: (i, 0, 0, 0))
    mspec = lambda m: pl.BlockSpec(m.shape, lambda i: (0, 0))
    return pl.pallas_call(
        _attn_sample_kernel,
        grid=(nb,),
        in_specs=[tok, tok, tok, cache, cache] + [mspec(m) for m in masks],
        out_specs=tok,
        out_shape=jax.ShapeDtypeStruct((nb, nt, WIDTH_B), F32),
        compiler_params=pltpu.CompilerParams(
            dimension_semantics=("arbitrary",), vmem_limit_bytes=VMEM_LIMIT),
        name="attn_sample",
    )(qb, kn, vn, cache_kt, cache_vt, *masks)


FF_CHUNK = 1024


def _mlp_kernel(*refs):
    _run_stages(_mlp_stages(*refs))


def _mlp_stages(x_ref, oa_ref, ob_ref, wo_ref, n2_ref, wu_ref, wd_ref, y_ref):
    mix = jnp.concatenate([oa_ref[...].astype(BF16), ob_ref[...].astype(BF16)], axis=1)
    h1 = x_ref[...] + jnp.dot(mix, wo_ref[...], preferred_element_type=F32)
    hn = _rmsnorm(h1, n2_ref[...]).astype(BF16)
    yield
    chunks = range(0, wu_ref.shape[1], FF_CHUNK)
    hid = [jnp.dot(hn, wu_ref[:, c0:c0 + FF_CHUNK], preferred_element_type=F32) for c0 in chunks]
    act = [jnp.square(jnp.maximum(h, 0.0)).astype(BF16) for h in hid]
    yield
    y = h1
    for c0, a in zip(chunks, act):
        y = y + jnp.dot(a, wd_ref[c0:c0 + FF_CHUNK, :], preferred_element_type=F32)
    yield
    y_ref[...] = y


def _mlp(x, oa, ob, p, tm):
    n, d = x.shape
    row = lambda w: pl.BlockSpec((tm, w), lambda i: (i, 0))
    const = lambda a: pl.BlockSpec(a.shape, lambda i: (0, 0))
    return pl.pallas_call(
        _mlp_kernel,
        grid=(n // tm,),
        in_specs=[row(d), row(WIDTH_A), row(WIDTH_B), const(p["w_o"]), const(p["n2"]),
                  const(p["w_up"]), const(p["w_down"])],
        out_specs=row(d),
        out_shape=jax.ShapeDtypeStruct((n, d), F32),
        compiler_params=pltpu.CompilerParams(
            dimension_semantics=("arbitrary",), vmem_limit_bytes=VMEM_LIMIT),
        name="mlp",
    )(x, oa, ob, p["w_o"], p["n2"], p["w_up"], p["w_down"])


def _mlp_attn_kernel(x_ref, oa_ref, ob_ref, wo_ref, n2_ref, wu_ref, wd_ref,
                     q_ref, kn_ref, vn_ref, kt_ref, vt_ref, mc_ref, mn_ref, y_ref, o_ref):
    _run_stages(_attn_sample_stages(q_ref, kn_ref, vn_ref, kt_ref, vt_ref, mc_ref, mn_ref, o_ref),
                _mlp_stages(x_ref, oa_ref, ob_ref, wo_ref, n2_ref, wu_ref, wd_ref, y_ref))


def _attn_sample_specs(qb, cache_kt, seq0):
    nb, nt, _ = qb.shape
    _, nh, dh, wbuf = cache_kt.shape
    masks = [jnp.asarray(m) for m in _sample_masks(wbuf, nt)]

    def specs(linear_step):
        tok = pl.BlockSpec((1, nt, WIDTH_B), lambda *g: (seq0 + linear_step(*g), 0, 0))
        cache = pl.BlockSpec((1, nh, dh, wbuf), lambda *g: (seq0 + linear_step(*g), 0, 0, 0))
        out = pl.BlockSpec((1, nt, WIDTH_B), lambda *g: (linear_step(*g), 0, 0))
        mspecs = [pl.BlockSpec(m.shape, lambda *g: (0, 0), pipeline_mode=pl.Buffered(1))
                  for m in masks]
        return tok, cache, out, mspecs

    return masks, specs


def _mlp_attn(x, oa, ob, p, qb, kn, vn, cache_kt, cache_vt, tm, seq0):
    n, d = x.shape
    nt = qb.shape[1]
    steps = n // tm
    assert n % tm == 0 and seq0 + steps <= qb.shape[0]
    masks, specs = _attn_sample_specs(qb, cache_kt, seq0)
    tok, cache, tok_out, mspecs = specs(lambda i: i)
    row = lambda w: pl.BlockSpec((tm, w), lambda i: (i, 0))
    const = lambda a: pl.BlockSpec(a.shape, lambda i: (0, 0), pipeline_mode=pl.Buffered(1))
    return pl.pallas_call(
        _mlp_attn_kernel,
        grid=(steps,),
        in_specs=[row(d), row(WIDTH_A), row(WIDTH_B), const(p["w_o"]), const(p["n2"]),
                  const(p["w_up"]), const(p["w_down"]), tok, tok, tok, cache, cache] + mspecs,
        out_specs=[row(d), tok_out],
        out_shape=[jax.ShapeDtypeStruct((n, d), F32),
                   jax.ShapeDtypeStruct((steps, nt, WIDTH_B), F32)],
        compiler_params=pltpu.CompilerParams(
            dimension_semantics=("arbitrary",), vmem_limit_bytes=VMEM_LIMIT),
        name="mlp_attn",
    )(x, oa, ob, p["w_o"], p["n2"], p["w_up"], p["w_down"], qb, kn, vn, cache_kt, cache_vt, *masks)


def _layer_params(norm1_g, w_in, conv_w, a_log, dt_bias, delta_norm_g, q_norm_g, k_norm_g, w_o,
                  norm2_g, w_up, w_down):
    d = w_in.shape[0]
    n_gate = 2 * N_HEADS_A
    gate0 = CONV_CH + WIDTH_A
    w_a = w_in[:, :gate0].astype(BF16)
    w_b = w_in[:, gate0 + n_gate:].astype(BF16)
    w_g = jnp.pad(w_in[:, gate0:gate0 + n_gate], ((0, 0), (0, LANES - n_gate))).astype(BF16)
    lane_pad = lambda a: jnp.zeros((1, LANES), F32).at[0, N_HEADS_A:n_gate].set(a.astype(F32))
    hid = np.arange(WIDTH_B) // HEAD_DIM_B
    head_mean = jnp.asarray((hid[:, None] == hid[None, :]).astype(np.float32) / HEAD_DIM_B, BF16)
    return {
        "n1": norm1_g.reshape(1, d).astype(F32), "w_a": w_a, "w_b": w_b, "w_g": w_g, "conv_w": conv_w.astype(F32),
        "alog": lane_pad(a_log), "dtb": lane_pad(dt_bias),
        "qng": jnp.tile(q_norm_g.astype(F32), N_HEADS_B).reshape(1, WIDTH_B),
        "kng": jnp.tile(k_norm_g.astype(F32), N_HEADS_B).reshape(1, WIDTH_B),
        "hm": head_mean, "dng": delta_norm_g.reshape(1, HEAD_DIM_A).astype(F32),
        "w_o": w_o.astype(BF16), "n2": norm2_g.reshape(1, d).astype(F32),
        "w_up": w_up.astype(BF16), "w_down": w_down.astype(BF16),
    }


MLP_TILE = 256
PROJ_TILE = 256


def _layer(xp, xs, state_conv, s0_s, cache_k, cache_v, p):
    b, t, d = xp.shape
    nb, nt, _ = xs.shape
    n = b * t
    pbuf = min(MAX_WINDOW, t)

    qa, ka, va, z, gt, qb, kn, vn, conv_s = _proj_sample(xs, state_conv, p, bt=min(nb, 64))
    oa_s, s_s = _delta(qa, ka, va, z, gt, s0_s, p["dng"], chunk=nt, nchunk=1, bb=min(nb, 16),
                       out_dtype=F32)
    ckt, cvt = cache_k.transpose(0, 2, 3, 1), cache_v.transpose(0, 2, 3, 1)
    attn_s = (qb, kn, vn, ckt, cvt)

    steps = n // PROJ_TILE
    fuse = n % PROJ_TILE == 0 and PROJ_TILE == MLP_TILE and 2 * steps == nb
    qa, ka, va, z, gt, kp, vp, conv_p, *dil = _proj_prompt(
        xp, p, tm=PROJ_TILE, pbuf=pbuf, attn=attn_s if fuse else None)
    window = lambda a: a.reshape(b, N_HEADS_B, HEAD_DIM_B, pbuf).transpose(0, 3, 1, 2)
    kp, vp = window(kp), window(vp)
    s0_p = jnp.zeros((b, N_HEADS_A, HEAD_DIM_A, HEAD_DIM_A), F32)
    oa_p, s_p = _delta(qa, ka, va, z, gt, s0_p, p["dng"], chunk=DELTA_CHUNK, nchunk=2, bb=b,
                       out_dtype=BF16)
    ob_p = _attn_prompt(dil[0:3], dil[3:6], dil[6:9])

    mlp_in = (xp.reshape(n, d), oa_p.reshape(n, WIDTH_A), ob_p.reshape(n, WIDTH_B), p)
    if fuse:
        yp, ob_s2 = _mlp_attn(*mlp_in, *attn_s, tm=MLP_TILE, seq0=steps)
        ob_s = jnp.concatenate([dil[9], ob_s2], axis=0)
    else:
        yp = _mlp(*mlp_in, tm=MLP_TILE)
        ob_s = _attn_sample(*attn_s)
    ys = _mlp(xs.reshape(nb * nt, d), oa_s.reshape(nb * nt, WIDTH_A),
              ob_s.reshape(nb * nt, WIDTH_B), p, tm=min(MLP_TILE, nb * nt))
    heads = lambda a: a.reshape(nb, nt, N_HEADS_B, HEAD_DIM_B)
    return ((yp.reshape(b, t, d), kp, vp, s_p, conv_p),
            (ys.reshape(nb, nt, d), heads(kn), heads(vn), s_s, conv_s))


def kernel(x_prompt, x_sample, cache_swa_k, cache_swa_v, state_delta, state_conv, norm1_g, w_in,
           conv_w, a_log, dt_bias, delta_norm_g, q_norm_g, k_norm_g, w_o, norm2_g, w_up, w_down):
    depth = w_in.shape[0]
    b, s, _ = x_prompt.shape
    nb, nt, _ = x_sample.shape
    wbuf = cache_swa_k.shape[2]
    assert s % ATTN_TILE == 0 and nt == SUBLANES and wbuf == MAX_WINDOW
    yp, ys = x_prompt, x_sample
    outs = [[] for _ in range(8)]
    for layer in range(depth):
        p = _layer_params(norm1_g[layer], w_in[layer], conv_w[layer], a_log[layer], dt_bias[layer],
                          delta_norm_g[layer], q_norm_g[layer], k_norm_g[layer], w_o[layer],
                          norm2_g[layer], w_up[layer], w_down[layer])
        (yp, kp, vp, dp, cp), (ys, kn, vn, dn, cn) = _layer(
            yp, ys, state_conv[layer], state_delta[layer], cache_swa_k[layer], cache_swa_v[layer], p)
        for lst, val in zip(outs, (kp, vp, dp, cp, kn, vn, dn, cn)):
            lst.append(val)
    return (yp, ys) + tuple(jnp.stack(o) for o in outs)
```

```python
import functools

import numpy as np
import jax
import jax.numpy as jnp
from jax import lax
from jax.experimental import pallas as pl
from jax.experimental.pallas import tpu as pltpu

F32 = jnp.float32
BF16 = jnp.bfloat16

N_HEADS_A = 4
HEAD_DIM_A = 128
WIDTH_A = N_HEADS_A * HEAD_DIM_A
N_HEADS_B = 8
HEAD_DIM_B = 64
WIDTH_B = N_HEADS_B * HEAD_DIM_B
CONV_WIDTH = 4
CONV_CH = 3 * WIDTH_A
DELTA_CHUNK = 64
MAX_WINDOW = 2048
NORM_EPS = 1e-6
LANES = 128
SUBLANES = 8
NEG = -1e30
LOG2_E = 1.4426950408889634

C_QKV, C_Z = 0, CONV_CH
C_QB, C_KB, C_VB = 0, WIDTH_B, 2 * WIDTH_B

ATTN_DILATIONS = (1, 4, 16)
ATTN_BAND = 128
ATTN_TILE = 2048
ATTN_UNROLL = 2
VMEM_LIMIT = 56 * 1024 * 1024


def _dot(a, b):
    return jnp.dot(a.astype(BF16), b.astype(BF16), preferred_element_type=F32)


def _dot_nt(a, b):
    return lax.dot_general(a.astype(BF16), b.astype(BF16), (((1,), (1,)), ((), ())),
                           preferred_element_type=F32)


def _dot_tn(a, b):
    return lax.dot_general(a.astype(BF16), b.astype(BF16), (((0,), (0,)), ((), ())),
                           preferred_element_type=F32)


def _split3(x):
    hi = x.astype(BF16)
    r = x - hi.astype(F32)
    mid = r.astype(BF16)
    lo = (r - mid.astype(F32)).astype(BF16)
    return hi, mid, lo


def _dot_exact_lhs(mask_bf16, x):
    hi, mid, lo = _split3(x)
    d = lambda p: jnp.dot(mask_bf16, p, preferred_element_type=F32)
    return d(hi) + d(mid) + d(lo)


def _sigmoid(x):
    return 0.5 * jnp.tanh(0.5 * x) + 0.5


def _silu(x):
    return x * _sigmoid(x)


def _softplus(x):
    return jnp.maximum(x, 0.0) + jnp.log1p(jnp.exp(-jnp.abs(x)))


def _rmsnorm(x, g):
    return x * lax.rsqrt(jnp.mean(x * x, axis=-1, keepdims=True) + NORM_EPS) * g


def _proj_body(x, ext_ref, shift, pad, refs, nsplit=1):
    n1_ref, wa_ref, wb_ref, wg_ref = refs[0:4]
    rows = x.shape[0] // nsplit
    raws = []
    for part in range(nsplit):
        hn = _rmsnorm(x[part * rows:(part + 1) * rows], n1_ref[...]).astype(BF16)
        proj = lambda w, hn=hn: jnp.dot(hn, w, preferred_element_type=F32)
        raws.append((proj(wa_ref[:, C_QKV:C_QKV + CONV_CH]),
                     proj(wb_ref[:, C_QB:C_QB + WIDTH_B]), proj(wb_ref[:, C_KB:C_KB + WIDTH_B]),
                     proj(wb_ref[:, C_VB:C_VB + WIDTH_B]),
                     proj(wa_ref[:, C_Z:C_Z + WIDTH_A]), proj(wg_ref[...])))
    outs = [_proj_epilogue(raw, ext_ref, shift, pad + part * rows, refs)
            for part, raw in enumerate(raws)]
    return tuple(jnp.concatenate(vals, axis=0) if nsplit > 1 else vals[0] for vals in zip(*outs))


def _proj_epilogue(raw, ext_ref, shift, pad, refs):
    cw_ref, al_ref, dtb_ref, qng_ref, kng_ref, hm_ref = refs[4:]
    u, qb, kb, vb, z, gc = raw
    tm = u.shape[0]
    ext_ref[pad:pad + tm, :] = u
    cw = cw_ref[...]
    y = u * cw[3:4, :]
    for i in range(CONV_WIDTH - 1):
        off = pad - (CONV_WIDTH - 1 - i) * shift
        y = y + ext_ref[off:off + tm, :] * cw[i:i + 1, :]
    y = _silu(y)
    qa, ka = [], []
    for h in range(N_HEADS_A):
        lo = h * HEAD_DIM_A
        qh = y[:, lo:lo + HEAD_DIM_A]
        qa.append(qh * (lax.rsqrt(
            jnp.sum(qh * qh, axis=-1, keepdims=True) + NORM_EPS) * HEAD_DIM_A ** -0.5))
        kh = y[:, WIDTH_A + lo:WIDTH_A + lo + HEAD_DIM_A]
        ka.append(kh * lax.rsqrt(jnp.sum(kh * kh, axis=-1, keepdims=True) + NORM_EPS))
    qa = jnp.concatenate(qa, axis=1)
    ka = jnp.concatenate(ka, axis=1)
    va = y[:, 2 * WIDTH_A:3 * WIDTH_A]

    hm = hm_ref[...]

    def headnorm(v, g):
        ms = jnp.dot((v * v).astype(BF16), hm, preferred_element_type=F32)
        return v * lax.rsqrt(ms + NORM_EPS) * g

    qb = headnorm(qb, qng_ref[...])
    kb = headnorm(kb, kng_ref[...])

    lane = lax.broadcasted_iota(jnp.int32, gc.shape, 1)
    beta = _sigmoid(gc)
    g = -jnp.exp(al_ref[...]) * _softplus(gc + dtb_ref[...])
    gates = jnp.where(lane < N_HEADS_A, beta, g)
    return qa, ka, va, z, gates, qb, kb, vb


PROJ_SPLIT = 2
N_PROJ_PARAMS = 10
N_ATTN_SAMPLE_INPUTS = 7


def _proj_prompt_kernel(*refs, first_win, with_attn):
    x_ref, params = refs[0], refs[1:1 + N_PROJ_PARAMS]
    pos = 1 + N_PROJ_PARAMS
    attn_in = refs[pos:pos + N_ATTN_SAMPLE_INPUTS] if with_attn else ()
    pos += len(attn_in)
    outs = refs[pos:pos + 17]
    pos += 17
    attn_out = refs[pos:pos + 1] if with_attn else ()
    pos += len(attn_out)
    scratch = refs[pos:]
    stages = [_proj_prompt_stages(x_ref, params, outs, scratch, first_win)]
    if with_attn:
        stages.insert(0, _attn_sample_stages(*attn_in, *attn_out))
    _run_stages(*stages)


def _proj_prompt_stages(x_ref, params, outs, scratch, first_win):
    (qa_ref, ka_ref, va_ref, z_ref, gt_ref, kb_ref, vb_ref, cn_ref,
     q1_ref, q4_ref, q16_ref, k1_ref, k4_ref, k16_ref, v1_ref, v4_ref, v16_ref) = outs
    ext_ref, dil_ref = scratch
    t = pl.program_id(1)
    tm = x_ref.shape[1]

    @pl.when(t == 0)
    def _():
        ext_ref[0:SUBLANES, :] = jnp.zeros((SUBLANES, CONV_CH), F32)

    @pl.when(t > 0)
    def _():
        ext_ref[0:SUBLANES, :] = ext_ref[tm:tm + SUBLANES, :]

    qa, ka, va, z, gates, qb, kb, vb = _proj_body(x_ref[0], ext_ref, 1, SUBLANES, params,
                                                  nsplit=PROJ_SPLIT)
    yield
    for r, v in zip((qa_ref, ka_ref, va_ref, z_ref, gt_ref), (qa, ka, va, z, gates)):
        r[0] = v
    cn_ref[0] = ext_ref[tm + SUBLANES - (CONV_WIDTH - 1):tm + SUBLANES, :]

    @pl.when(t >= first_win)
    def _():
        kb_ref[0] = kb.T
        vb_ref[0] = vb.T

    yield
    nchunk = WIDTH_B // LANES
    q_att = qb * (HEAD_DIM_B ** -0.5 * LOG2_E)
    for ai, (val, r1, r4, r16) in enumerate(((q_att, q1_ref, q4_ref, q16_ref),
                                              (kb, k1_ref, k4_ref, k16_ref),
                                              (vb, v1_ref, v4_ref, v16_ref))):
        r1[0] = val.astype(BF16)
        for c in range(nchunk):
            dil_ref[0, ai * nchunk + c] = val[:, c * LANES:(c + 1) * LANES]
        n4, n16 = tm // 4, tm // 16
        for r in range(4):
            parts = [dil_ref[0, ai * nchunk + c, pl.ds(r, n4, stride=4), :] for c in range(nchunk)]
            r4[0, r] = jnp.concatenate(parts, axis=1).astype(BF16)
            for c in range(nchunk):
                dil_ref[1, ai * nchunk + c, r * n4:(r + 1) * n4, :] = parts[c]
        for r in range(4):
            for r2 in range(4):
                r16[0, r + 4 * r2] = jnp.concatenate(
                    [dil_ref[1, ai * nchunk + c, pl.ds(r * n4 + r2, n16, stride=4), :]
                     for c in range(nchunk)], axis=1).astype(BF16)


def _proj_sample_kernel(x_ref, st_ref, n1_ref, wa_ref, wb_ref, wg_ref, cw_ref, al_ref, dtb_ref,
                        qng_ref, kng_ref, hm_ref, qa_ref, ka_ref, va_ref, z_ref, gt_ref, qb_ref, kb_ref, vb_ref,
                        cn_ref, ext_ref):
    nt, nb, d = x_ref.shape
    ncv = CONV_WIDTH - 1
    tm = nb * nt
    pad = ncv * nb
    ext_ref[0:pad, :] = st_ref[...].reshape(pad, CONV_CH)
    vals = _proj_body(x_ref[...].reshape(tm, d), ext_ref, nb, pad,
                      (n1_ref, wa_ref, wb_ref, wg_ref, cw_ref, al_ref, dtb_ref, qng_ref, kng_ref,
                       hm_ref))
    for r, v in zip((qa_ref, ka_ref, va_ref, z_ref, gt_ref, qb_ref, kb_ref, vb_ref), vals):
        r[...] = v.reshape(r.shape)
    cn_ref[...] = ext_ref[tm:tm + pad, :].reshape(ncv, nb, CONV_CH)


def _proj_params(p):
    full = lambda a: pl.BlockSpec(a.shape, lambda *_: (0,) * a.ndim,
                                  pipeline_mode=pl.Buffered(1))
    arrs = (p["n1"], p["w_a"], p["w_b"], p["w_g"], p["conv_w"], p["alog"], p["dtb"], p["qng"],
            p["kng"], p["hm"])
    return arrs, [full(a) for a in arrs]


def _proj_prompt(x, p, tm, pbuf, attn=None):
    b, t, d = x.shape
    assert pbuf % tm == 0 and t % tm == 0
    first_win = (t - pbuf) // tm
    arrs, specs = _proj_params(p)
    assert len(arrs) == N_PROJ_PARAMS
    steps_t = t // tm
    attn_args, attn_in_specs, attn_out_shape, attn_out_specs = [], [], [], []
    if attn is not None:
        qb, kn, vn, ckt, cvt = attn
        assert b * steps_t <= qb.shape[0]
        masks, mk = _attn_sample_specs(qb, ckt, 0)
        tok, cache, tok_out, mspecs = mk(lambda i, j: i * steps_t + j)
        attn_args = [qb, kn, vn, ckt, cvt] + masks
        attn_in_specs = [tok, tok, tok, cache, cache] + mspecs
        assert len(attn_args) == N_ATTN_SAMPLE_INPUTS
        attn_out_shape = [jax.ShapeDtypeStruct((b * steps_t, qb.shape[1], WIDTH_B), F32)]
        attn_out_specs = [tok_out]
    row = lambda w: pl.BlockSpec((1, tm, w), lambda i, j: (i, j, 0))
    widths = (WIDTH_A, WIDTH_A, WIDTH_A, WIDTH_A, LANES)
    out_shape = [jax.ShapeDtypeStruct((b, t, w), F32) for w in widths]
    out_specs = [row(w) for w in widths]
    for _ in range(2):
        out_shape.append(jax.ShapeDtypeStruct((b, WIDTH_B, pbuf), F32))
        out_specs.append(pl.BlockSpec((1, WIDTH_B, tm),
                                      lambda i, j: (i, 0, jnp.maximum(j - first_win, 0))))
    out_shape.append(jax.ShapeDtypeStruct((b, CONV_WIDTH - 1, CONV_CH), F32))
    out_specs.append(pl.BlockSpec((1, CONV_WIDTH - 1, CONV_CH), lambda i, j: (i, 0, 0)))
    for _ in range(3):
        out_shape.append(jax.ShapeDtypeStruct((b, t, WIDTH_B), BF16))
        out_specs.append(row(WIDTH_B))
        for dil in ATTN_DILATIONS[1:]:
            out_shape.append(jax.ShapeDtypeStruct((b, dil, t // dil, WIDTH_B), BF16))
            out_specs.append(pl.BlockSpec((1, dil, tm // dil, WIDTH_B), lambda i, j: (i, 0, j, 0)))
    return pl.pallas_call(
        functools.partial(_proj_prompt_kernel, first_win=first_win, with_attn=attn is not None),
        grid=(b, steps_t),
        in_specs=[row(d)] + specs + attn_in_specs,
        out_specs=out_specs + attn_out_specs,
        out_shape=out_shape + attn_out_shape,
        scratch_shapes=[pltpu.VMEM((tm + SUBLANES, CONV_CH), F32),
                        pltpu.VMEM((2, 3 * WIDTH_B // LANES, tm, LANES), F32)],
        compiler_params=pltpu.CompilerParams(
            dimension_semantics=("arbitrary", "arbitrary"), vmem_limit_bytes=VMEM_LIMIT),
        name="proj_prompt",
    )(x, *arrs, *attn_args)


def _proj_sample(x, state_conv, p, bt):
    nb, nt, d = x.shape
    ncv = CONV_WIDTH - 1
    arrs, specs = _proj_params(p)
    blk = lambda r, w: pl.BlockSpec((r, bt, w), lambda i: (0, i, 0))
    widths = (WIDTH_A, WIDTH_A, WIDTH_A, WIDTH_A, LANES, WIDTH_B, WIDTH_B, WIDTH_B)
    out_shape = [jax.ShapeDtypeStruct((nt, nb, w), F32) for w in widths]
    out_shape.append(jax.ShapeDtypeStruct((ncv, nb, CONV_CH), F32))
    out_specs = [blk(nt, w) for w in widths] + [blk(ncv, CONV_CH)]
    outs = pl.pallas_call(
        _proj_sample_kernel,
        grid=(nb // bt,),
        in_specs=[blk(nt, d), blk(ncv, CONV_CH)] + specs,
        out_specs=out_specs,
        out_shape=out_shape,
        scratch_shapes=[pltpu.VMEM((bt * (nt + ncv), CONV_CH), F32)],
        compiler_params=pltpu.CompilerParams(
            dimension_semantics=("arbitrary",), vmem_limit_bytes=VMEM_LIMIT),
        name="proj_sample",
    )(x.transpose(1, 0, 2), state_conv.transpose(1, 0, 2), *arrs)
    return [o.transpose(1, 0, 2) for o in outs]


def _delta_kernel(q_ref, k_ref, v_ref, z_ref, gt_ref, s0_ref, ng_ref, o_ref, sf_ref, s_scr,
                  *, chunk, nchunk, bb):
    c = chunk
    t = pl.program_id(1)

    @pl.when(t == 0)
    def _():
        s_scr[...] = s0_ref[...]

    ii = lax.broadcasted_iota(jnp.int32, (c, c), 0)
    jj = lax.broadcasted_iota(jnp.int32, (c, c), 1)
    causal = ii >= jj
    strict = ii > jj
    tril_bf = causal.astype(F32).astype(BF16)
    eye = (ii == jj).astype(F32)
    level_masks = []
    s = 1
    while s < c:
        level_masks.append(((ii // (2 * s)) == (jj // (2 * s))) & ((ii & s) != 0) & ((jj & s) == 0))
        s *= 2
    ng = ng_ref[...]

    heads = range(N_HEADS_A)
    items = [(bi, ci, h) for bi in range(bb) for ci in range(nchunk) for h in heads]
    pre = {}
    for bi in range(bb):
        for ci in range(nchunk):
            rows = slice(ci * c, (ci + 1) * c)
            gt = gt_ref[bi, rows, :]
            gcum = _dot_exact_lhs(tril_bf, gt)
            if c < LANES:
                gpad = jnp.concatenate([gcum, jnp.zeros((LANES - c, LANES), F32)], axis=0)
            else:
                gpad = gcum
            gcum_t = gpad.T
            for h in heads:
                lo = h * HEAD_DIM_A
                q = q_ref[bi, rows, lo:lo + HEAD_DIM_A]
                k = k_ref[bi, rows, lo:lo + HEAD_DIM_A]
                v = v_ref[bi, rows, lo:lo + HEAD_DIM_A]
                beta = gt[:, h:h + 1]
                g_col = gcum[:, N_HEADS_A + h:N_HEADS_A + h + 1]
                g_row = gcum_t[N_HEADS_A + h:N_HEADS_A + h + 1, 0:c]
                g_last = gcum[c - 1:c, N_HEADS_A + h:N_HEADS_A + h + 1]
                decay = jnp.where(causal, jnp.exp(jnp.where(causal, g_col - g_row, 0.0)), 0.0)
                exp_g = jnp.exp(g_col)
                kb = k * beta
                pre[bi, ci, h] = dict(
                    q=q, k=k, kb=kb, decay=decay, g_last=g_last,
                    rhs=jnp.concatenate([v * beta, kb * exp_g], axis=1),
                    q_dec=q * exp_g, k_dec=k * jnp.exp(g_last - g_col))
    lmat = {it: jnp.where(strict, _dot_nt(pre[it]["kb"], pre[it]["k"]) * pre[it]["decay"], 0.0)
            for it in items}
    qk = {it: _dot_nt(pre[it]["q"], pre[it]["k"]) * pre[it]["decay"] for it in items}
    tinv = {it: eye - jnp.where(level_masks[0], lmat[it], 0.0) for it in items}
    for msk in level_masks[1:]:
        te = {it: _dot(tinv[it], jnp.where(msk, lmat[it], 0.0)) for it in items}
        tinv = {it: tinv[it] - _dot(te[it], tinv[it]) for it in items}
    uw = {it: _dot(tinv[it], pre[it]["rhs"]) for it in items}

    for bi in range(bb):
        st = {h: s_scr[bi, h] for h in heads}
        for ci in range(nchunk):
            rows = slice(ci * c, (ci + 1) * c)
            ws = {h: _dot(jnp.concatenate([uw[bi, ci, h][:, HEAD_DIM_A:2 * HEAD_DIM_A],
                                           pre[bi, ci, h]["q_dec"]], axis=0), st[h])
                  for h in heads}
            v_new = {h: uw[bi, ci, h][:, 0:HEAD_DIM_A] - ws[h][0:c] for h in heads}
            o = {h: ws[h][c:2 * c] + _dot(qk[bi, ci, h], v_new[h]) for h in heads}
            st = {h: st[h] * jnp.exp(pre[bi, ci, h]["g_last"])
                  + _dot_tn(pre[bi, ci, h]["k_dec"], v_new[h]) for h in heads}
            for h in heads:
                lo = h * HEAD_DIM_A
                zz = z_ref[bi, rows, lo:lo + HEAD_DIM_A]
                o_ref[bi, rows, lo:lo + HEAD_DIM_A] = (
                    _rmsnorm(o[h], ng) * _silu(zz)).astype(o_ref.dtype)
        for h in heads:
            s_scr[bi, h] = st[h]

    @pl.when(t == pl.num_programs(1) - 1)
    def _():
        sf_ref[...] = s_scr[...]


def _delta(qa, ka, va, z, gt, s0, ng, *, chunk, nchunk, bb, out_dtype):
    b, t, _ = qa.shape
    tc = chunk * nchunk
    row = lambda w: pl.BlockSpec((bb, tc, w), lambda i, j: (i, j, 0))
    sspec = pl.BlockSpec((bb, N_HEADS_A, HEAD_DIM_A, HEAD_DIM_A), lambda i, j: (i, 0, 0, 0))
    return pl.pallas_call(
        functools.partial(_delta_kernel, chunk=chunk, nchunk=nchunk, bb=bb),
        grid=(b // bb, t // tc),
        in_specs=[row(WIDTH_A)] * 4 + [row(LANES), sspec,
                                       pl.BlockSpec((1, HEAD_DIM_A), lambda i, j: (0, 0))],
        out_specs=[row(WIDTH_A), sspec],
        out_shape=[jax.ShapeDtypeStruct((b, t, WIDTH_A), out_dtype),
                   jax.ShapeDtypeStruct((b, N_HEADS_A, HEAD_DIM_A, HEAD_DIM_A), F32)],
        scratch_shapes=[pltpu.VMEM((bb, N_HEADS_A, HEAD_DIM_A, HEAD_DIM_A), F32)],
        compiler_params=pltpu.CompilerParams(
            dimension_semantics=("arbitrary", "arbitrary"), vmem_limit_bytes=VMEM_LIMIT),
        name="delta_c%d" % chunk,
    )(qa, ka, va, z, gt, s0, ng)


def _attn_prompt_kernel(q1, q4, q16, k1p, k1c, k4p, k4c, k16p, k16c, v1p, v1c, v4p, v4c, v16p, v16c,
                        o_ref, acc, mrep, *, tq):
    t = pl.program_id(2)
    rr = ATTN_BAND
    ii = lax.broadcasted_iota(jnp.int32, (rr, rr), 0)
    jj = lax.broadcasted_iota(jnp.int32, (rr, rr), 1)
    lower = jj <= ii
    upper = jj >= ii
    lane = lax.broadcasted_iota(jnp.int32, (rr, LANES), 1)
    own = [(lane >= hh * HEAD_DIM_B) & (lane < (hh + 1) * HEAD_DIM_B) for hh in range(2)]
    pen_t = jnp.where(t > 0, jnp.float32(0.0), jnp.float32(NEG))
    one = jnp.ones((rr, LANES), BF16)
    zero = jnp.zeros((rr, LANES), BF16)

    ld = lambda ref_idx: ref_idx[0][ref_idx[1]]

    def weights(units):
        chains = [(j, hh) for j in range(len(units)) for hh in range(2)]
        qv = {ch: jnp.where(own[ch[1]], ld(units[ch[0]]["q"]), zero) for ch in chains}
        kcat = [jnp.concatenate([ld(u["kc"]), ld(u["kp"])], axis=0) for u in units]
        raw = {ch: _dot_nt(qv[ch], kcat[ch[0]]) for ch in chains}
        out = {}
        for ch in chains:
            s_c = jnp.where(lower, raw[ch][:, 0:rr], NEG)
            s_p = raw[ch][:, rr:2 * rr]
            if units[ch[0]]["pen"] is not None:
                s_p = s_p + units[ch[0]]["pen"]
            s_p = jnp.where(upper, s_p, NEG)
            m = jnp.max(jnp.maximum(s_c, s_p), axis=-1, keepdims=True)
            out[ch] = (jnp.exp2(s_c - m).astype(BF16), jnp.exp2(s_p - m).astype(BF16), m)
        return out

    def accumulate(units, wts, first):
        vc = [ld(u["vc"]) for u in units]
        vp = [ld(u["vp"]) for u in units]
        pv = {ch: jnp.dot(jnp.concatenate([p_c, p_p], axis=1),
                          jnp.concatenate([jnp.where(own[ch[1]], vc[ch[0]], one),
                                           jnp.where(own[ch[1]], vp[ch[0]], one)], axis=0),
                          preferred_element_type=F32)
              for ch, (p_c, p_p, _) in wts.items()}
        for ch, (_, _, m) in wts.items():
            j, hh = ch
            rows = units[j]["rows"]
            m_b = jnp.broadcast_to(m, (rr, LANES))
            if first:
                acc[hh, rows, :] = pv[ch]
                mrep[hh, rows, :] = m_b
            else:
                m_old = mrep[hh, rows, :]
                m_new = jnp.maximum(m_old, m_b)
                acc[hh, rows, :] = (acc[hh, rows, :] * jnp.exp2(m_old - m_new)
                                    + pv[ch] * jnp.exp2(m_b - m_new))
                mrep[hh, rows, :] = m_new

    uu = ATTN_UNROLL
    blk = lambda j: slice(j * rr, (j + 1) * rr)

    d16, d4 = ATTN_DILATIONS[2], ATTN_DILATIONS[1]
    assert tq // d16 == rr and (tq // (d4 * rr)) % uu == 0 and d16 % uu == 0
    trips = []
    full = slice(None)
    for it in range(d16 // uu):
        trips.append((True, [dict(
            q=(q16, (0, r)), kc=(k16c, (0, r)), kp=(k16p, (0, r)), vc=(v16c, (0, r)),
            vp=(v16p, (0, r)), pen=pen_t, rows=pl.ds(r, rr, stride=d16))
            for r in range(it * uu, (it + 1) * uu)]))
    for it in range(tq // (rr * uu)):
        units = []
        for j in range(it * uu, (it + 1) * uu):
            cur = (0, blk(j), full)
            prv = (0, blk(j - 1), full) if j > 0 else (0, slice(tq - rr, tq), full)
            units.append(dict(q=(q1, cur), kc=(k1c, cur), vc=(v1c, cur),
                              kp=(k1c if j > 0 else k1p, prv), vp=(v1c if j > 0 else v1p, prv),
                              pen=None if j > 0 else pen_t, rows=pl.ds(j * rr, rr)))
        trips.append((False, units))
    nb4 = tq // d4 // rr
    for r in range(d4):
        for jt in range(nb4 // uu):
            units = []
            for j in range(jt * uu, (jt + 1) * uu):
                cur = (0, r, blk(j), full)
                prv = (0, r, blk(j - 1), full) if j > 0 else (0, r, blk(nb4 - 1), full)
                units.append(dict(q=(q4, cur), kc=(k4c, cur), vc=(v4c, cur),
                                  kp=(k4c if j > 0 else k4p, prv),
                                  vp=(v4c if j > 0 else v4p, prv),
                                  pen=None if j > 0 else pen_t,
                                  rows=pl.ds(r + d4 * rr * j, rr, stride=d4)))
            trips.append((False, units))

    pending = None
    for first, units in trips:
        wts = weights(units)
        if pending is not None:
            accumulate(*pending)
        pending = (units, wts, first)
    accumulate(*pending)

    lane_t = lax.broadcasted_iota(jnp.int32, (tq, LANES), 1)
    a0 = acc[0]
    a1 = acc[1]
    l0 = a0[:, HEAD_DIM_B:HEAD_DIM_B + 1]
    l1 = a1[:, 0:1]
    o_ref[0] = jnp.where(lane_t < HEAD_DIM_B, a0 / l0, a1 / l1).astype(o_ref.dtype)


def _attn_prompt(q, k, v):
    b, t, _ = q[0].shape
    tq = ATTN_TILE
    nt = t // tq
    prev_j = lambda j: jnp.maximum(j - 1, 0)
    cur, prev = [], []
    for dil in ATTN_DILATIONS:
        if dil == 1:
            cur.append(pl.BlockSpec((1, tq, LANES), lambda i, h, j: (i, j, h)))
            prev.append(pl.BlockSpec((1, tq, LANES), lambda i, h, j: (i, prev_j(j), h)))
        else:
            cur.append(pl.BlockSpec((1, dil, tq // dil, LANES), lambda i, h, j: (i, 0, j, h)))
            prev.append(pl.BlockSpec((1, dil, tq // dil, LANES),
                                     lambda i, h, j: (i, 0, prev_j(j), h)))
    kv_specs = [s for pc in zip(prev, cur) for s in pc]
    kv_args = lambda x: [a for xd in x for a in (xd, xd)]
    return pl.pallas_call(
        functools.partial(_attn_prompt_kernel, tq=tq),
        grid=(b, WIDTH_B // LANES, nt),
        in_specs=cur + kv_specs + kv_specs,
        out_specs=cur[0],
        out_shape=jax.ShapeDtypeStruct((b, t, WIDTH_B), BF16),
        scratch_shapes=[pltpu.VMEM((2, tq, LANES), F32), pltpu.VMEM((2, tq, LANES), F32)],
        compiler_params=pltpu.CompilerParams(
            dimension_semantics=("arbitrary", "arbitrary", "arbitrary"),
            vmem_limit_bytes=VMEM_LIMIT),
        name="attn_prompt",
    )(*q, *kv_args(k), *kv_args(v))


def _multiplicity(delta):
    delta = np.asarray(delta)
    ok = delta >= 0
    m = ((delta <= 128).astype(np.float32)
         + ((delta <= 512) & (delta % 4 == 0)).astype(np.float32)
         + ((delta <= 2048) & (delta % 16 == 0)).astype(np.float32))
    return np.where(ok, m, 0.0).astype(np.float32)


def _sample_masks(wbuf, nt):
    i = np.arange(nt)[:, None]
    return (_multiplicity(wbuf + i - np.arange(wbuf)[None, :]),
            _multiplicity(i - np.arange(nt)[None, :]))


def _run_stages(*gens):
    gens = list(gens)
    while gens:
        for g in list(gens):
            if next(g, StopIteration) is StopIteration:
                gens.remove(g)


def _attn_sample_kernel(*refs):
    _run_stages(_attn_sample_stages(*refs))


def _attn_sample_stages(q_ref, kn_ref, vn_ref, kt_ref, vt_ref, mc_ref, mn_ref, o_ref):
    dh = HEAD_DIM_B
    heads = range(N_HEADS_B)
    mc = mc_ref[...]
    mn = mn_ref[...]
    sl = lambda r, h: r[0, :, h * dh:(h + 1) * dh]
    q = {h: (sl(q_ref, h) * (dh ** -0.5)).astype(BF16) for h in heads}
    s_c = {h: jnp.where(mc > 0.0, _dot(q[h], kt_ref[0, h]), NEG) for h in heads}
    s_n = {h: jnp.where(mn > 0.0, _dot_nt(q[h], sl(kn_ref, h)), NEG) for h in heads}
    yield
    m = {h: jnp.maximum(jnp.max(s_c[h], axis=-1, keepdims=True),
                        jnp.max(s_n[h], axis=-1, keepdims=True)) for h in heads}
    p_c = {h: (mc * jnp.exp(s_c[h] - m[h])).astype(BF16) for h in heads}
    p_n = {h: (mn * jnp.exp(s_n[h] - m[h])).astype(BF16) for h in heads}
    den = {h: (jnp.sum(p_c[h].astype(F32), axis=-1, keepdims=True)
               + jnp.sum(p_n[h].astype(F32), axis=-1, keepdims=True)) for h in heads}
    yield
    out = {h: _dot_nt(p_c[h], vt_ref[0, h]) + _dot(p_n[h], sl(vn_ref, h)) for h in heads}
    yield
    o_ref[0] = jnp.concatenate([out[h] / den[h] for h in heads], axis=1)


def _attn_sample(qb, kn, vn, cache_kt, cache_vt):
    nb, nt, _ = qb.shape
    _, nh, dh, wbuf = cache_kt.shape
    masks = [jnp.asarray(m) for m in _sample_masks(wbuf, nt)]
    tok = pl.BlockSpec((1, nt, WIDTH_B), lambda i: (i, 0, 0))
    cache = pl.BlockSpec((1, nh, dh, wbuf), lambda i: (i, 0, 0, 0))
    mspec = lambda m: pl.BlockSpec(m.shape, lambda i: (0, 0))
    return pl.pallas_call(
        _attn_sample_kernel,
        grid=(nb,),
        in_specs=[tok, tok, tok, cache, cache] + [mspec(m) for m in masks],
        out_specs=tok,
        out_shape=jax.ShapeDtypeStruct((nb, nt, WIDTH_B), F32),
        compiler_params=pltpu.CompilerParams(
            dimension_semantics=("arbitrary",), vmem_limit_bytes=VMEM_LIMIT),
        name="attn_sample",
    )(qb, kn, vn, cache_kt, cache_vt, *masks)


FF_CHUNK = 1024


def _mlp_kernel(*refs):
    _run_stages(_mlp_stages(*refs))


def _mlp_stages(x_ref, oa_ref, ob_ref, wo_ref, n2_ref, wu_ref, wd_ref, y_ref):
    mix = jnp.concatenate([oa_ref[...].astype(BF16), ob_ref[...].astype(BF16)], axis=1)
    h1 = x_ref[...] + jnp.dot(mix, wo_ref[...], preferred_element_type=F32)
    hn = _rmsnorm(h1, n2_ref[...]).astype(BF16)
    yield
    chunks = range(0, wu_ref.shape[1], FF_CHUNK)
    hid = [jnp.dot(hn, wu_ref[:, c0:c0 + FF_CHUNK], preferred_element_type=F32) for c0 in chunks]
    act = [jnp.square(jnp.maximum(h, 0.0)).astype(BF16) for h in hid]
    yield
    y = h1
    for c0, a in zip(chunks, act):
        y = y + jnp.dot(a, wd_ref[c0:c0 + FF_CHUNK, :], preferred_element_type=F32)
    yield
    y_ref[...] = y


def _mlp(x, oa, ob, p, tm):
    n, d = x.shape
    row = lambda w: pl.BlockSpec((tm, w), lambda i: (i, 0))
    const = lambda a: pl.BlockSpec(a.shape, lambda i: (0, 0))
    return pl.pallas_call(
        _mlp_kernel,
        grid=(n // tm,),
        in_specs=[row(d), row(WIDTH_A), row(WIDTH_B), const(p["w_o"]), const(p["n2"]),
                  const(p["w_up"]), const(p["w_down"])],
        out_specs=row(d),
        out_shape=jax.ShapeDtypeStruct((n, d), F32),
        compiler_params=pltpu.CompilerParams(
            dimension_semantics=("arbitrary",), vmem_limit_bytes=VMEM_LIMIT),
        name="mlp",
    )(x, oa, ob, p["w_o"], p["n2"], p["w_up"], p["w_down"])


def _mlp_attn_kernel(x_ref, oa_ref, ob_ref, wo_ref, n2_ref, wu_ref, wd_ref,
                     q_ref, kn_ref, vn_ref, kt_ref, vt_ref, mc_ref, mn_ref, y_ref, o_ref):
    _run_stages(_attn_sample_stages(q_ref, kn_ref, vn_ref, kt_ref, vt_ref, mc_ref, mn_ref, o_ref),
                _mlp_stages(x_ref, oa_ref, ob_ref, wo_ref, n2_ref, wu_ref, wd_ref, y_ref))


def _attn_sample_specs(qb, cache_kt, seq0):
    nb, nt, _ = qb.shape
    _, nh, dh, wbuf = cache_kt.shape
    masks = [jnp.asarray(m) for m in _sample_masks(wbuf, nt)]

    def specs(linear_step):
        tok = pl.BlockSpec((1, nt, WIDTH_B), lambda *g: (seq0 + linear_step(*g), 0, 0))
        cache = pl.BlockSpec((1, nh, dh, wbuf), lambda *g: (seq0 + linear_step(*g), 0, 0, 0))
        out = pl.BlockSpec((1, nt, WIDTH_B), lambda *g: (linear_step(*g), 0, 0))
        mspecs = [pl.BlockSpec(m.shape, lambda *g: (0, 0), pipeline_mode=pl.Buffered(1))
                  for m in masks]
        return tok, cache, out, mspecs

    return masks, specs


def _mlp_attn(x, oa, ob, p, qb, kn, vn, cache_kt, cache_vt, tm, seq0):
    n, d = x.shape
    nt = qb.shape[1]
    steps = n // tm
    assert n % tm == 0 and seq0 + steps <= qb.shape[0]
    masks, specs = _attn_sample_specs(qb, cache_kt, seq0)
    tok, cache, tok_out, mspecs = specs(lambda i: i)
    row = lambda w: pl.BlockSpec((tm, w), lambda i: (i, 0))
    const = lambda a: pl.BlockSpec(a.shape, lambda i: (0, 0), pipeline_mode=pl.Buffered(1))
    return pl.pallas_call(
        _mlp_attn_kernel,
        grid=(steps,),
        in_specs=[row(d), row(WIDTH_A), row(WIDTH_B), const(p["w_o"]), const(p["n2"]),
                  const(p["w_up"]), const(p["w_down"]), tok, tok, tok, cache, cache] + mspecs,
        out_specs=[row(d), tok_out],
        out_shape=[jax.ShapeDtypeStruct((n, d), F32),
                   jax.ShapeDtypeStruct((steps, nt, WIDTH_B), F32)],
        compiler_params=pltpu.CompilerParams(
            dimension_semantics=("arbitrary",), vmem_limit_bytes=VMEM_LIMIT),
        name="mlp_attn",
    )(x, oa, ob, p["w_o"], p["n2"], p["w_up"], p["w_down"], qb, kn, vn, cache_kt, cache_vt, *masks)


def _layer_params(norm1_g, w_in, conv_w, a_log, dt_bias, delta_norm_g, q_norm_g, k_norm_g, w_o,
                  norm2_g, w_up, w_down):
    d = w_in.shape[0]
    n_gate = 2 * N_HEADS_A
    gate0 = CONV_CH + WIDTH_A
    w_a = w_in[:, :gate0].astype(BF16)
    w_b = w_in[:, gate0 + n_gate:].astype(BF16)
    w_g = jnp.pad(w_in[:, gate0:gate0 + n_gate], ((0, 0), (0, LANES - n_gate))).astype(BF16)
    lane_pad = lambda a: jnp.zeros((1, LANES), F32).at[0, N_HEADS_A:n_gate].set(a.astype(F32))
    hid = np.arange(WIDTH_B) // HEAD_DIM_B
    head_mean = jnp.asarray((hid[:, None] == hid[None, :]).astype(np.float32) / HEAD_DIM_B, BF16)
    return {
        "n1": norm1_g.reshape(1, d).astype(F32), "w_a": w_a, "w_b": w_b, "w_g": w_g, "conv_w": conv_w.astype(F32),
        "alog": lane_pad(a_log), "dtb": lane_pad(dt_bias),
        "qng": jnp.tile(q_norm_g.astype(F32), N_HEADS_B).reshape(1, WIDTH_B),
        "kng": jnp.tile(k_norm_g.astype(F32), N_HEADS_B).reshape(1, WIDTH_B),
        "hm": head_mean, "dng": delta_norm_g.reshape(1, HEAD_DIM_A).astype(F32),
        "w_o": w_o.astype(BF16), "n2": norm2_g.reshape(1, d).astype(F32),
        "w_up": w_up.astype(BF16), "w_down": w_down.astype(BF16),
    }


MLP_TILE = 256
PROJ_TILE = 256


def _layer(xp, xs, state_conv, s0_s, cache_k, cache_v, p):
    b, t, d = xp.shape
    nb, nt, _ = xs.shape
    n = b * t
    pbuf = min(MAX_WINDOW, t)

    qa, ka, va, z, gt, qb, kn, vn, conv_s = _proj_sample(xs, state_conv, p, bt=min(nb, 64))
    oa_s, s_s = _delta(qa, ka, va, z, gt, s0_s, p["dng"], chunk=nt, nchunk=1, bb=min(nb, 16),
                       out_dtype=F32)
    ckt, cvt = cache_k.transpose(0, 2, 3, 1), cache_v.transpose(0, 2, 3, 1)
    attn_s = (qb, kn, vn, ckt, cvt)

    steps = n // PROJ_TILE
    fuse = n % PROJ_TILE == 0 and PROJ_TILE == MLP_TILE and 2 * steps == nb
    qa, ka, va, z, gt, kp, vp, conv_p, *dil = _proj_prompt(
        xp, p, tm=PROJ_TILE, pbuf=pbuf, attn=attn_s if fuse else None)
    window = lambda a: a.reshape(b, N_HEADS_B, HEAD_DIM_B, pbuf).transpose(0, 3, 1, 2)
    kp, vp = window(kp), window(vp)
    s0_p = jnp.zeros((b, N_HEADS_A, HEAD_DIM_A, HEAD_DIM_A), F32)
    oa_p, s_p = _delta(qa, ka, va, z, gt, s0_p, p["dng"], chunk=DELTA_CHUNK, nchunk=2, bb=b,
                       out_dtype=BF16)
    ob_p = _attn_prompt(dil[0:3], dil[3:6], dil[6:9])

    mlp_in = (xp.reshape(n, d), oa_p.reshape(n, WIDTH_A), ob_p.reshape(n, WIDTH_B), p)
    if fuse:
        yp, ob_s2 = _mlp_attn(*mlp_in, *attn_s, tm=MLP_TILE, seq0=steps)
        ob_s = jnp.concatenate([dil[9], ob_s2], axis=0)
    else:
        yp = _mlp(*mlp_in, tm=MLP_TILE)
        ob_s = _attn_sample(*attn_s)
    ys = _mlp(xs.reshape(nb * nt, d), oa_s.reshape(nb * nt, WIDTH_A),
              ob_s.reshape(nb * nt, WIDTH_B), p, tm=min(MLP_TILE, nb * nt))
    heads = lambda a: a.reshape(nb, nt, N_HEADS_B, HEAD_DIM_B)
    return ((yp.reshape(b, t, d), kp, vp, s_p, conv_p),
            (ys.reshape(nb, nt, d), heads(kn), heads(vn), s_s, conv_s))


def kernel(x_prompt, x_sample, cache_swa_k, cache_swa_v, state_delta, state_conv, norm1_g, w_in,
           conv_w, a_log, dt_bias, delta_norm_g, q_norm_g, k_norm_g, w_o, norm2_g, w_up, w_down):
    depth = w_in.shape[0]
    b, s, _ = x_prompt.shape
    nb, nt, _ = x_sample.shape
    wbuf = cache_swa_k.shape[2]
    assert s % ATTN_TILE == 0 and nt == SUBLANES and wbuf == MAX_WINDOW
    yp, ys = x_prompt, x_sample
    outs = [[] for _ in range(8)]
    for layer in range(depth):
        p = _layer_params(norm1_g[layer], w_in[layer], conv_w[layer], a_log[layer], dt_bias[layer],
                          delta_norm_g[layer], q_norm_g[layer], k_norm_g[layer], w_o[layer],
                          norm2_g[layer], w_up[layer], w_down[layer])
        (yp, kp, vp, dp, cp), (ys, kn, vn, dn, cn) = _layer(
            yp, ys, state_conv[layer], state_delta[layer], cache_swa_k[layer], cache_swa_v[layer], p)
        for lst, val in zip(outs, (kp, vp, dp, cp, kn, vn, dn, cn)):
            lst.append(val)
    return (yp, ys) + tuple(jnp.stack(o) for o in outs)
```

```python
import functools

import numpy as np
import jax
import jax.numpy as jnp
from jax import lax
from jax.experimental import pallas as pl
from jax.experimental.pallas import tpu as pltpu

F32 = jnp.float32
BF16 = jnp.bfloat16

N_HEADS_A = 4
HEAD_DIM_A = 128
WIDTH_A = N_HEADS_A * HEAD_DIM_A
N_HEADS_B = 8
HEAD_DIM_B = 64
WIDTH_B = N_HEADS_B * HEAD_DIM_B
CONV_WIDTH = 4
CONV_CH = 3 * WIDTH_A
DELTA_CHUNK = 64
MAX_WINDOW = 2048
NORM_EPS = 1e-6
LANES = 128
SUBLANES = 8
NEG = -1e30
LOG2_E = 1.4426950408889634

C_QKV, C_Z = 0, CONV_CH
C_QB, C_KB, C_VB = 0, WIDTH_B, 2 * WIDTH_B

ATTN_DILATIONS = (1, 4, 16)
ATTN_BAND = 128
ATTN_TILE = 2048
ATTN_UNROLL = 2
VMEM_LIMIT = 56 * 1024 * 1024


def _dot(a, b):
    return jnp.dot(a.astype(BF16), b.astype(BF16), preferred_element_type=F32)


def _dot_nt(a, b):
    return lax.dot_general(a.astype(BF16), b.astype(BF16), (((1,), (1,)), ((), ())),
                           preferred_element_type=F32)


def _dot_tn(a, b):
    return lax.dot_general(a.astype(BF16), b.astype(BF16), (((0,), (0,)), ((), ())),
                           preferred_element_type=F32)


def _split3(x):
    hi = x.astype(BF16)
    r = x - hi.astype(F32)
    mid = r.astype(BF16)
    lo = (r - mid.astype(F32)).astype(BF16)
    return hi, mid, lo


def _dot_exact_lhs(mask_bf16, x):
    hi, mid, lo = _split3(x)
    d = lambda p: jnp.dot(mask_bf16, p, preferred_element_type=F32)
    return d(hi) + d(mid) + d(lo)


def _sigmoid(x):
    return 0.5 * jnp.tanh(0.5 * x) + 0.5


def _silu(x):
    return x * _sigmoid(x)


def _softplus(x):
    return jnp.maximum(x, 0.0) + jnp.log1p(jnp.exp(-jnp.abs(x)))


def _rmsnorm(x, g):
    return x * lax.rsqrt(jnp.mean(x * x, axis=-1, keepdims=True) + NORM_EPS) * g


def _proj_body(x, ext_ref, shift, pad, refs, nsplit=1):
    n1_ref, wa_ref, wb_ref, wg_ref = refs[0:4]
    rows = x.shape[0] // nsplit
    raws = []
    for part in range(nsplit):
        hn = _rmsnorm(x[part * rows:(part + 1) * rows], n1_ref[...]).astype(BF16)
        proj = lambda w, hn=hn: jnp.dot(hn, w, preferred_element_type=F32)
        raws.append((proj(wa_ref[:, C_QKV:C_QKV + CONV_CH]),
                     proj(wb_ref[:, C_QB:C_QB + WIDTH_B]), proj(wb_ref[:, C_KB:C_KB + WIDTH_B]),
                     proj(wb_ref[:, C_VB:C_VB + WIDTH_B]),
                     proj(wa_ref[:, C_Z:C_Z + WIDTH_A]), proj(wg_ref[...])))
    outs = [_proj_epilogue(raw, ext_ref, shift, pad + part * rows, refs)
            for part, raw in enumerate(raws)]
    return tuple(jnp.concatenate(vals, axis=0) if nsplit > 1 else vals[0] for vals in zip(*outs))


def _proj_epilogue(raw, ext_ref, shift, pad, refs):
    cw_ref, al_ref, dtb_ref, qng_ref, kng_ref, hm_ref = refs[4:]
    u, qb, kb, vb, z, gc = raw
    tm = u.shape[0]
    ext_ref[pad:pad + tm, :] = u
    cw = cw_ref[...]
    y = u * cw[3:4, :]
    for i in range(CONV_WIDTH - 1):
        off = pad - (CONV_WIDTH - 1 - i) * shift
        y = y + ext_ref[off:off + tm, :] * cw[i:i + 1, :]
    y = _silu(y)
    qa, ka = [], []
    for h in range(N_HEADS_A):
        lo = h * HEAD_DIM_A
        qh = y[:, lo:lo + HEAD_DIM_A]
        qa.append(qh * (lax.rsqrt(
            jnp.sum(qh * qh, axis=-1, keepdims=True) + NORM_EPS) * HEAD_DIM_A ** -0.5))
        kh = y[:, WIDTH_A + lo:WIDTH_A + lo + HEAD_DIM_A]
        ka.append(kh * lax.rsqrt(jnp.sum(kh * kh, axis=-1, keepdims=True) + NORM_EPS))
    qa = jnp.concatenate(qa, axis=1)
    ka = jnp.concatenate(ka, axis=1)
    va = y[:, 2 * WIDTH_A:3 * WIDTH_A]

    hm = hm_ref[...]

    def headnorm(v, g):
        ms = jnp.dot((v * v).astype(BF16), hm, preferred_element_type=F32)
        return v * lax.rsqrt(ms + NORM_EPS) * g

    qb = headnorm(qb, qng_ref[...])
    kb = headnorm(kb, kng_ref[...])

    lane = lax.broadcasted_iota(jnp.int32, gc.shape, 1)
    beta = _sigmoid(gc)
    g = -jnp.exp(al_ref[...]) * _softplus(gc + dtb_ref[...])
    gates = jnp.where(lane < N_HEADS_A, beta, g)
    return qa, ka, va, z, gates, qb, kb, vb


PROJ_SPLIT = 2
N_PROJ_PARAMS = 10
N_ATTN_SAMPLE_INPUTS = 7


def _proj_prompt_kernel(*refs, first_win, with_attn):
    x_ref, params = refs[0], refs[1:1 + N_PROJ_PARAMS]
    pos = 1 + N_PROJ_PARAMS
    attn_in = refs[pos:pos + N_ATTN_SAMPLE_INPUTS] if with_attn else ()
    pos += len(attn_in)
    outs = refs[pos:pos + 17]
    pos += 17
    attn_out = refs[pos:pos + 1] if with_attn else ()
    pos += len(attn_out)
    scratch = refs[pos:]
    stages = [_proj_prompt_stages(x_ref, params, outs, scratch, first_win)]
    if with_attn:
        stages.insert(0, _attn_sample_stages(*attn_in, *attn_out))
    _run_stages(*stages)


def _proj_prompt_stages(x_ref, params, outs, scratch, first_win):
    (qa_ref, ka_ref, va_ref, z_ref, gt_ref, kb_ref, vb_ref, cn_ref,
     q1_ref, q4_ref, q16_ref, k1_ref, k4_ref, k16_ref, v1_ref, v4_ref, v16_ref) = outs
    ext_ref, dil_ref = scratch
    t = pl.program_id(1)
    tm = x_ref.shape[1]

    @pl.when(t == 0)
    def _():
        ext_ref[0:SUBLANES, :] = jnp.zeros((SUBLANES, CONV_CH), F32)

    @pl.when(t > 0)
    def _():
        ext_ref[0:SUBLANES, :] = ext_ref[tm:tm + SUBLANES, :]

    qa, ka, va, z, gates, qb, kb, vb = _proj_body(x_ref[0], ext_ref, 1, SUBLANES, params,
                                                  nsplit=PROJ_SPLIT)
    yield
    for r, v in zip((qa_ref, ka_ref, va_ref, z_ref, gt_ref), (qa, ka, va, z, gates)):
        r[0] = v
    cn_ref[0] = ext_ref[tm + SUBLANES - (CONV_WIDTH - 1):tm + SUBLANES, :]

    @pl.when(t >= first_win)
    def _():
        kb_ref[0] = kb.T
        vb_ref[0] = vb.T

    yield
    nchunk = WIDTH_B // LANES
    q_att = qb * (HEAD_DIM_B ** -0.5 * LOG2_E)
    for ai, (val, r1, r4, r16) in enumerate(((q_att, q1_ref, q4_ref, q16_ref),
                                              (kb, k1_ref, k4_ref, k16_ref),
                                              (vb, v1_ref, v4_ref, v16_ref))):
        r1[0] = val.astype(BF16)
        for c in range(nchunk):
            dil_ref[0, ai * nchunk + c] = val[:, c * LANES:(c + 1) * LANES]
        n4, n16 = tm // 4, tm // 16
        for r in range(4):
            parts = [dil_ref[0, ai * nchunk + c, pl.ds(r, n4, stride=4), :] for c in range(nchunk)]
            r4[0, r] = jnp.concatenate(parts, axis=1).astype(BF16)
            for c in range(nchunk):
                dil_ref[1, ai * nchunk + c, r * n4:(r + 1) * n4, :] = parts[c]
        for r in range(4):
            for r2 in range(4):
                r16[0, r + 4 * r2] = jnp.concatenate(
                    [dil_ref[1, ai * nchunk + c, pl.ds(r * n4 + r2, n16, stride=4), :]
                     for c in range(nchunk)], axis=1).astype(BF16)


def _proj_sample_kernel(x_ref, st_ref, n1_ref, wa_ref, wb_ref, wg_ref, cw_ref, al_ref, dtb_ref,
                        qng_ref, kng_ref, hm_ref, qa_ref, ka_ref, va_ref, z_ref, gt_ref, qb_ref, kb_ref, vb_ref,
                        cn_ref, ext_ref):
    nt, nb, d = x_ref.shape
    ncv = CONV_WIDTH - 1
    tm = nb * nt
    pad = ncv * nb
    ext_ref[0:pad, :] = st_ref[...].reshape(pad, CONV_CH)
    vals = _proj_body(x_ref[...].reshape(tm, d), ext_ref, nb, pad,
                      (n1_ref, wa_ref, wb_ref, wg_ref, cw_ref, al_ref, dtb_ref, qng_ref, kng_ref,
                       hm_ref))
    for r, v in zip((qa_ref, ka_ref, va_ref, z_ref, gt_ref, qb_ref, kb_ref, vb_ref), vals):
        for t in range(nt):
            r[:, t, :] = v[t * nb:(t + 1) * nb]
    cn_ref[...] = ext_ref[tm:tm + pad, :].reshape(ncv, nb, CONV_CH)


def _proj_params(p):
    full = lambda a: pl.BlockSpec(a.shape, lambda *_: (0,) * a.ndim,
                                  pipeline_mode=pl.Buffered(1))
    arrs = (p["n1"], p["w_a"], p["w_b"], p["w_g"], p["conv_w"], p["alog"], p["dtb"], p["qng"],
            p["kng"], p["hm"])
    return arrs, [full(a) for a in arrs]


def _proj_prompt(x, p, tm, pbuf, attn=None):
    b, t, d = x.shape
    assert pbuf % tm == 0 and t % tm == 0
    first_win = (t - pbuf) // tm
    arrs, specs = _proj_params(p)
    assert len(arrs) == N_PROJ_PARAMS
    steps_t = t // tm
    attn_args, attn_in_specs, attn_out_shape, attn_out_specs = [], [], [], []
    if attn is not None:
        qb, kn, vn, ckt, cvt = attn
        assert b * steps_t <= qb.shape[0]
        masks, mk = _attn_sample_specs(qb, ckt, 0)
        tok, cache, tok_out, mspecs = mk(lambda i, j: i * steps_t + j)
        attn_args = [qb, kn, vn, ckt, cvt] + masks
        attn_in_specs = [tok, tok, tok, cache, cache] + mspecs
        assert len(attn_args) == N_ATTN_SAMPLE_INPUTS
        attn_out_shape = [jax.ShapeDtypeStruct((b * steps_t, qb.shape[1], WIDTH_B), F32)]
        attn_out_specs = [tok_out]
    row = lambda w: pl.BlockSpec((1, tm, w), lambda i, j: (i, j, 0))
    widths = (WIDTH_A, WIDTH_A, WIDTH_A, WIDTH_A, LANES)
    out_shape = [jax.ShapeDtypeStruct((b, t, w), F32) for w in widths]
    out_specs = [row(w) for w in widths]
    for _ in range(2):
        out_shape.append(jax.ShapeDtypeStruct((b, WIDTH_B, pbuf), F32))
        out_specs.append(pl.BlockSpec((1, WIDTH_B, tm),
                                      lambda i, j: (i, 0, jnp.maximum(j - first_win, 0))))
    out_shape.append(jax.ShapeDtypeStruct((b, CONV_WIDTH - 1, CONV_CH), F32))
    out_specs.append(pl.BlockSpec((1, CONV_WIDTH - 1, CONV_CH), lambda i, j: (i, 0, 0)))
    for _ in range(3):
        out_shape.append(jax.ShapeDtypeStruct((b, t, WIDTH_B), BF16))
        out_specs.append(row(WIDTH_B))
        for dil in ATTN_DILATIONS[1:]:
            out_shape.append(jax.ShapeDtypeStruct((b, dil, t // dil, WIDTH_B), BF16))
            out_specs.append(pl.BlockSpec((1, dil, tm // dil, WIDTH_B), lambda i, j: (i, 0, j, 0)))
    return pl.pallas_call(
        functools.partial(_proj_prompt_kernel, first_win=first_win, with_attn=attn is not None),
        grid=(b, steps_t),
        in_specs=[row(d)] + specs + attn_in_specs,
        out_specs=out_specs + attn_out_specs,
        out_shape=out_shape + attn_out_shape,
        scratch_shapes=[pltpu.VMEM((tm + SUBLANES, CONV_CH), F32),
                        pltpu.VMEM((2, 3 * WIDTH_B // LANES, tm, LANES), F32)],
        compiler_params=pltpu.CompilerParams(
            dimension_semantics=("arbitrary", "arbitrary"), vmem_limit_bytes=VMEM_LIMIT),
        name="proj_prompt",
    )(x, *arrs, *attn_args)


def _proj_sample(x, state_conv, p, bt):
    nb, nt, d = x.shape
    ncv = CONV_WIDTH - 1
    arrs, specs = _proj_params(p)
    blk = lambda r, w: pl.BlockSpec((r, bt, w), lambda i: (0, i, 0))
    widths = (WIDTH_A, WIDTH_A, WIDTH_A, WIDTH_A, LANES, WIDTH_B, WIDTH_B, WIDTH_B)
    out_shape = [jax.ShapeDtypeStruct((nb, nt, w), F32) for w in widths]
    out_shape.append(jax.ShapeDtypeStruct((ncv, nb, CONV_CH), F32))
    out_specs = [pl.BlockSpec((bt, nt, w), lambda i: (i, 0, 0)) for w in widths]
    out_specs.append(blk(ncv, CONV_CH))
    outs = pl.pallas_call(
        _proj_sample_kernel,
        grid=(nb // bt,),
        in_specs=[blk(nt, d), blk(ncv, CONV_CH)] + specs,
        out_specs=out_specs,
        out_shape=out_shape,
        scratch_shapes=[pltpu.VMEM((bt * (nt + ncv), CONV_CH), F32)],
        compiler_params=pltpu.CompilerParams(
            dimension_semantics=("arbitrary",), vmem_limit_bytes=VMEM_LIMIT),
        name="proj_sample",
    )(x.transpose(1, 0, 2), state_conv.transpose(1, 0, 2), *arrs)
    return list(outs[:-1]) + [outs[-1].transpose(1, 0, 2)]


def _delta_kernel(q_ref, k_ref, v_ref, z_ref, gt_ref, s0_ref, ng_ref, o_ref, sf_ref, s_scr,
                  *, chunk, nchunk, bb):
    c = chunk
    t = pl.program_id(1)

    @pl.when(t == 0)
    def _():
        s_scr[...] = s0_ref[...]

    ii = lax.broadcasted_iota(jnp.int32, (c, c), 0)
    jj = lax.broadcasted_iota(jnp.int32, (c, c), 1)
    causal = ii >= jj
    strict = ii > jj
    tril_bf = causal.astype(F32).astype(BF16)
    eye = (ii == jj).astype(F32)
    level_masks = []
    s = 1
    while s < c:
        level_masks.append(((ii // (2 * s)) == (jj // (2 * s))) & ((ii & s) != 0) & ((jj & s) == 0))
        s *= 2
    ng = ng_ref[...]

    heads = range(N_HEADS_A)
    items = [(bi, ci, h) for bi in range(bb) for ci in range(nchunk) for h in heads]
    pre = {}
    for bi in range(bb):
        for ci in range(nchunk):
            rows = slice(ci * c, (ci + 1) * c)
            gt = gt_ref[bi, rows, :]
            gcum = _dot_exact_lhs(tril_bf, gt)
            if c < LANES:
                gpad = jnp.concatenate([gcum, jnp.zeros((LANES - c, LANES), F32)], axis=0)
            else:
                gpad = gcum
            gcum_t = gpad.T
            for h in heads:
                lo = h * HEAD_DIM_A
                q = q_ref[bi, rows, lo:lo + HEAD_DIM_A]
                k = k_ref[bi, rows, lo:lo + HEAD_DIM_A]
                v = v_ref[bi, rows, lo:lo + HEAD_DIM_A]
                beta = gt[:, h:h + 1]
                g_col = gcum[:, N_HEADS_A + h:N_HEADS_A + h + 1]
                g_row = gcum_t[N_HEADS_A + h:N_HEADS_A + h + 1, 0:c]
                g_last = gcum[c - 1:c, N_HEADS_A + h:N_HEADS_A + h + 1]
                decay = jnp.where(causal, jnp.exp(jnp.where(causal, g_col - g_row, 0.0)), 0.0)
                exp_g = jnp.exp(g_col)
                kb = k * beta
                pre[bi, ci, h] = dict(
                    q=q, k=k, kb=kb, decay=decay, g_last=g_last,
                    rhs=jnp.concatenate([v * beta, kb * exp_g], axis=1),
                    q_dec=q * exp_g, k_dec=k * jnp.exp(g_last - g_col))
    lmat = {it: jnp.where(strict, _dot_nt(pre[it]["kb"], pre[it]["k"]) * pre[it]["decay"], 0.0)
            for it in items}
    qk = {it: _dot_nt(pre[it]["q"], pre[it]["k"]) * pre[it]["decay"] for it in items}
    tinv = {it: eye - jnp.where(level_masks[0], lmat[it], 0.0) for it in items}
    for msk in level_masks[1:]:
        te = {it: _dot(tinv[it], jnp.where(msk, lmat[it], 0.0)) for it in items}
        tinv = {it: tinv[it] - _dot(te[it], tinv[it]) for it in items}
    uw = {it: _dot(tinv[it], pre[it]["rhs"]) for it in items}

    for bi in range(bb):
        st = {h: s_scr[bi, h] for h in heads}
        for ci in range(nchunk):
            rows = slice(ci * c, (ci + 1) * c)
            ws = {h: _dot(jnp.concatenate([uw[bi, ci, h][:, HEAD_DIM_A:2 * HEAD_DIM_A],
                                           pre[bi, ci, h]["q_dec"]], axis=0), st[h])
                  for h in heads}
            v_new = {h: uw[bi, ci, h][:, 0:HEAD_DIM_A] - ws[h][0:c] for h in heads}
            o = {h: ws[h][c:2 * c] + _dot(qk[bi, ci, h], v_new[h]) for h in heads}
            st = {h: st[h] * jnp.exp(pre[bi, ci, h]["g_last"])
                  + _dot_tn(pre[bi, ci, h]["k_dec"], v_new[h]) for h in heads}
            for h in heads:
                lo = h * HEAD_DIM_A
                zz = z_ref[bi, rows, lo:lo + HEAD_DIM_A]
                o_ref[bi, rows, lo:lo + HEAD_DIM_A] = (
                    _rmsnorm(o[h], ng) * _silu(zz)).astype(o_ref.dtype)
        for h in heads:
            s_scr[bi, h] = st[h]

    @pl.when(t == pl.num_programs(1) - 1)
    def _():
        sf_ref[...] = s_scr[...]


def _delta(qa, ka, va, z, gt, s0, ng, *, chunk, nchunk, bb, out_dtype):
    b, t, _ = qa.shape
    tc = chunk * nchunk
    row = lambda w: pl.BlockSpec((bb, tc, w), lambda i, j: (i, j, 0))
    sspec = pl.BlockSpec((bb, N_HEADS_A, HEAD_DIM_A, HEAD_DIM_A), lambda i, j: (i, 0, 0, 0))
    return pl.pallas_call(
        functools.partial(_delta_kernel, chunk=chunk, nchunk=nchunk, bb=bb),
        grid=(b // bb, t // tc),
        in_specs=[row(WIDTH_A)] * 4 + [row(LANES), sspec,
                                       pl.BlockSpec((1, HEAD_DIM_A), lambda i, j: (0, 0))],
        out_specs=[row(WIDTH_A), sspec],
        out_shape=[jax.ShapeDtypeStruct((b, t, WIDTH_A), out_dtype),
                   jax.ShapeDtypeStruct((b, N_HEADS_A, HEAD_DIM_A, HEAD_DIM_A), F32)],
        scratch_shapes=[pltpu.VMEM((bb, N_HEADS_A, HEAD_DIM_A, HEAD_DIM_A), F32)],
        compiler_params=pltpu.CompilerParams(
            dimension_semantics=("arbitrary", "arbitrary"), vmem_limit_bytes=VMEM_LIMIT),
        name="delta_c%d" % chunk,
    )(qa, ka, va, z, gt, s0, ng)


def _attn_prompt_kernel(q1, q4, q16, k1p, k1c, k4p, k4c, k16p, k16c, v1p, v1c, v4p, v4c, v16p, v16c,
                        o_ref, acc, mrep, *, tq):
    t = pl.program_id(2)
    rr = ATTN_BAND
    ii = lax.broadcasted_iota(jnp.int32, (rr, rr), 0)
    jj = lax.broadcasted_iota(jnp.int32, (rr, rr), 1)
    lower = jj <= ii
    upper = jj >= ii
    lane = lax.broadcasted_iota(jnp.int32, (rr, LANES), 1)
    own = [(lane >= hh * HEAD_DIM_B) & (lane < (hh + 1) * HEAD_DIM_B) for hh in range(2)]
    pen_t = jnp.where(t > 0, jnp.float32(0.0), jnp.float32(NEG))
    one = jnp.ones((rr, LANES), BF16)
    zero = jnp.zeros((rr, LANES), BF16)

    ld = lambda ref_idx: ref_idx[0][ref_idx[1]]

    def weights(units):
        chains = [(j, hh) for j in range(len(units)) for hh in range(2)]
        qv = {ch: jnp.where(own[ch[1]], ld(units[ch[0]]["q"]), zero) for ch in chains}
        kcat = [jnp.concatenate([ld(u["kc"]), ld(u["kp"])], axis=0) for u in units]
        raw = {ch: _dot_nt(qv[ch], kcat[ch[0]]) for ch in chains}
        out = {}
        for ch in chains:
            s_c = jnp.where(lower, raw[ch][:, 0:rr], NEG)
            s_p = raw[ch][:, rr:2 * rr]
            if units[ch[0]]["pen"] is not None:
                s_p = s_p + units[ch[0]]["pen"]
            s_p = jnp.where(upper, s_p, NEG)
            m = jnp.max(jnp.maximum(s_c, s_p), axis=-1, keepdims=True)
            out[ch] = (jnp.exp2(s_c - m).astype(BF16), jnp.exp2(s_p - m).astype(BF16), m)
        return out

    def accumulate(units, wts, first):
        vc = [ld(u["vc"]) for u in units]
        vp = [ld(u["vp"]) for u in units]
        pv = {ch: jnp.dot(jnp.concatenate([p_c, p_p], axis=1),
                          jnp.concatenate([jnp.where(own[ch[1]], vc[ch[0]], one),
                                           jnp.where(own[ch[1]], vp[ch[0]], one)], axis=0),
                          preferred_element_type=F32)
              for ch, (p_c, p_p, _) in wts.items()}
        for ch, (_, _, m) in wts.items():
            j, hh = ch
            rows = units[j]["rows"]
            m_b = jnp.broadcast_to(m, (rr, LANES))
            if first:
                acc[hh, rows, :] = pv[ch]
                mrep[hh, rows, :] = m_b
            else:
                m_old = mrep[hh, rows, :]
                m_new = jnp.maximum(m_old, m_b)
                acc[hh, rows, :] = (acc[hh, rows, :] * jnp.exp2(m_old - m_new)
                                    + pv[ch] * jnp.exp2(m_b - m_new))
                mrep[hh, rows, :] = m_new

    uu = ATTN_UNROLL
    blk = lambda j: slice(j * rr, (j + 1) * rr)

    d16, d4 = ATTN_DILATIONS[2], ATTN_DILATIONS[1]
    assert tq // d16 == rr and (tq // (d4 * rr)) % uu == 0 and d16 % uu == 0
    trips = []
    full = slice(None)
    for it in range(d16 // uu):
        trips.append((True, [dict(
            q=(q16, (0, r)), kc=(k16c, (0, r)), kp=(k16p, (0, r)), vc=(v16c, (0, r)),
            vp=(v16p, (0, r)), pen=pen_t, rows=pl.ds(r, rr, stride=d16))
            for r in range(it * uu, (it + 1) * uu)]))
    for it in range(tq // (rr * uu)):
        units = []
        for j in range(it * uu, (it + 1) * uu):
            cur = (0, blk(j), full)
            prv = (0, blk(j - 1), full) if j > 0 else (0, slice(tq - rr, tq), full)
            units.append(dict(q=(q1, cur), kc=(k1c, cur), vc=(v1c, cur),
                              kp=(k1c if j > 0 else k1p, prv), vp=(v1c if j > 0 else v1p, prv),
                              pen=None if j > 0 else pen_t, rows=pl.ds(j * rr, rr)))
        trips.append((False, units))
    nb4 = tq // d4 // rr
    for r in range(d4):
        for jt in range(nb4 // uu):
            units = []
            for j in range(jt * uu, (jt + 1) * uu):
                cur = (0, r, blk(j), full)
                prv = (0, r, blk(j - 1), full) if j > 0 else (0, r, blk(nb4 - 1), full)
                units.append(dict(q=(q4, cur), kc=(k4c, cur), vc=(v4c, cur),
                                  kp=(k4c if j > 0 else k4p, prv),
                                  vp=(v4c if j > 0 else v4p, prv),
                                  pen=None if j > 0 else pen_t,
                                  rows=pl.ds(r + d4 * rr * j, rr, stride=d4)))
            trips.append((False, units))

    pending = None
    for first, units in trips:
        wts = weights(units)
        if pending is not None:
            accumulate(*pending)
        pending = (units, wts, first)
    accumulate(*pending)

    lane_t = lax.broadcasted_iota(jnp.int32, (tq, LANES), 1)
    a0 = acc[0]
    a1 = acc[1]
    l0 = a0[:, HEAD_DIM_B:HEAD_DIM_B + 1]
    l1 = a1[:, 0:1]
    o_ref[0] = jnp.where(lane_t < HEAD_DIM_B, a0 / l0, a1 / l1).astype(o_ref.dtype)


def _attn_prompt(q, k, v):
    b, t, _ = q[0].shape
    tq = ATTN_TILE
    nt = t // tq
    prev_j = lambda j: jnp.maximum(j - 1, 0)
    cur, prev = [], []
    for dil in ATTN_DILATIONS:
        if dil == 1:
            cur.append(pl.BlockSpec((1, tq, LANES), lambda i, h, j: (i, j, h)))
            prev.append(pl.BlockSpec((1, tq, LANES), lambda i, h, j: (i, prev_j(j), h)))
        else:
            cur.append(pl.BlockSpec((1, dil, tq // dil, LANES), lambda i, h, j: (i, 0, j, h)))
            prev.append(pl.BlockSpec((1, dil, tq // dil, LANES),
                                     lambda i, h, j: (i, 0, prev_j(j), h)))
    kv_specs = [s for pc in zip(prev, cur) for s in pc]
    kv_args = lambda x: [a for xd in x for a in (xd, xd)]
    return pl.pallas_call(
        functools.partial(_attn_prompt_kernel, tq=tq),
        grid=(b, WIDTH_B // LANES, nt),
        in_specs=cur + kv_specs + kv_specs,
        out_specs=cur[0],
        out_shape=jax.ShapeDtypeStruct((b, t, WIDTH_B), BF16),
        scratch_shapes=[pltpu.VMEM((2, tq, LANES), F32), pltpu.VMEM((2, tq, LANES), F32)],
        compiler_params=pltpu.CompilerParams(
            dimension_semantics=("arbitrary", "arbitrary", "arbitrary"),
            vmem_limit_bytes=VMEM_LIMIT),
        name="attn_prompt",
    )(*q, *kv_args(k), *kv_args(v))


def _multiplicity(delta):
    delta = np.asarray(delta)
    ok = delta >= 0
    m = ((delta <= 128).astype(np.float32)
         + ((delta <= 512) & (delta % 4 == 0)).astype(np.float32)
         + ((delta <= 2048) & (delta % 16 == 0)).astype(np.float32))
    return np.where(ok, m, 0.0).astype(np.float32)


def _sample_masks(wbuf, nt):
    i = np.arange(nt)[:, None]
    return (_multiplicity(wbuf + i - np.arange(wbuf)[None, :]),
            _multiplicity(i - np.arange(nt)[None, :]))


def _run_stages(*gens):
    gens = list(gens)
    while gens:
        for g in list(gens):
            if next(g, StopIteration) is StopIteration:
                gens.remove(g)


def _attn_sample_kernel(*refs):
    _run_stages(_attn_sample_stages(*refs))


def _attn_sample_stages(q_ref, kn_ref, vn_ref, kt_ref, vt_ref, mc_ref, mn_ref, o_ref):
    dh = HEAD_DIM_B
    heads = range(N_HEADS_B)
    mc = mc_ref[...]
    mn = mn_ref[...]
    sl = lambda r, h: r[0, :, h * dh:(h + 1) * dh]
    q = {h: (sl(q_ref, h) * (dh ** -0.5)).astype(BF16) for h in heads}
    s_c = {h: jnp.where(mc > 0.0, _dot(q[h], kt_ref[0, h]), NEG) for h in heads}
    s_n = {h: jnp.where(mn > 0.0, _dot_nt(q[h], sl(kn_ref, h)), NEG) for h in heads}
    yield
    m = {h: jnp.maximum(jnp.max(s_c[h], axis=-1, keepdims=True),
                        jnp.max(s_n[h], axis=-1, keepdims=True)) for h in heads}
    p_c = {h: (mc * jnp.exp(s_c[h] - m[h])).astype(BF16) for h in heads}
    p_n = {h: (mn * jnp.exp(s_n[h] - m[h])).astype(BF16) for h in heads}
    den = {h: (jnp.sum(p_c[h].astype(F32), axis=-1, keepdims=True)
               + jnp.sum(p_n[h].astype(F32), axis=-1, keepdims=True)) for h in heads}
    yield
    out = {h: _dot_nt(p_c[h], vt_ref[0, h]) + _dot(p_n[h], sl(vn_ref, h)) for h in heads}
    yield
    o_ref[0] = jnp.concatenate([out[h] / den[h] for h in heads], axis=1)


def _attn_sample(qb, kn, vn, cache_kt, cache_vt):
    nb, nt, _ = qb.shape
    _, nh, dh, wbuf = cache_kt.shape
    masks = [jnp.asarray(m) for m in _sample_masks(wbuf, nt)]
    tok = pl.BlockSpec((1, nt, WIDTH_B), lambda i: (i, 0, 0))
    cache = pl.BlockSpec((1, nh, dh, wbuf), lambda i: (i, 0, 0, 0))
    mspec = lambda m: pl.BlockSpec(m.shape, lambda i: (0, 0))
    return pl.pallas_call(
        _attn_sample_kernel,
        grid=(nb,),
        in_specs=[tok, tok, tok, cache, cache] + [mspec(m) for m in masks],
        out_specs=tok,
        out_shape=jax.ShapeDtypeStruct((nb, nt, WIDTH_B), F32),
        compiler_params=pltpu.CompilerParams(
            dimension_semantics=("arbitrary",), vmem_limit_bytes=VMEM_LIMIT),
        name="attn_sample",
    )(qb, kn, vn, cache_kt, cache_vt, *masks)


FF_CHUNK = 1024


def _mlp_kernel(*refs):
    _run_stages(_mlp_stages(*refs))


def _mlp_stages(x_ref, oa_ref, ob_ref, wo_ref, n2_ref, wu_ref, wd_ref, y_ref):
    mix = jnp.concatenate([oa_ref[...].astype(BF16), ob_ref[...].astype(BF16)], axis=1)
    h1 = x_ref[...] + jnp.dot(mix, wo_ref[...], preferred_element_type=F32)
    hn = _rmsnorm(h1, n2_ref[...]).astype(BF16)
    yield
    chunks = range(0, wu_ref.shape[1], FF_CHUNK)
    hid = [jnp.dot(hn, wu_ref[:, c0:c0 + FF_CHUNK], preferred_element_type=F32) for c0 in chunks]
    act = [jnp.square(jnp.maximum(h, 0.0)).astype(BF16) for h in hid]
    yield
    y = h1
    for c0, a in zip(chunks, act):
        y = y + jnp.dot(a, wd_ref[c0:c0 + FF_CHUNK, :], preferred_element_type=F32)
    yield
    y_ref[...] = y


def _mlp(x, oa, ob, p, tm):
    n, d = x.shape
    row = lambda w: pl.BlockSpec((tm, w), lambda i: (i, 0))
    const = lambda a: pl.BlockSpec(a.shape, lambda i: (0, 0))
    return pl.pallas_call(
        _mlp_kernel,
        grid=(n // tm,),
        in_specs=[row(d), row(WIDTH_A), row(WIDTH_B), const(p["w_o"]), const(p["n2"]),
                  const(p["w_up"]), const(p["w_down"])],
        out_specs=row(d),
        out_shape=jax.ShapeDtypeStruct((n, d), F32),
        compiler_params=pltpu.CompilerParams(
            dimension_semantics=("arbitrary",), vmem_limit_bytes=VMEM_LIMIT),
        name="mlp",
    )(x, oa, ob, p["w_o"], p["n2"], p["w_up"], p["w_down"])


def _mlp_attn_kernel(x_ref, oa_ref, ob_ref, wo_ref, n2_ref, wu_ref, wd_ref,
                     q_ref, kn_ref, vn_ref, kt_ref, vt_ref, mc_ref, mn_ref, y_ref, o_ref):
    _run_stages(_attn_sample_stages(q_ref, kn_ref, vn_ref, kt_ref, vt_ref, mc_ref, mn_ref, o_ref),
                _mlp_stages(x_ref, oa_ref, ob_ref, wo_ref, n2_ref, wu_ref, wd_ref, y_ref))


def _attn_sample_specs(qb, cache_kt, seq0):
    nb, nt, _ = qb.shape
    _, nh, dh, wbuf = cache_kt.shape
    masks = [jnp.asarray(m) for m in _sample_masks(wbuf, nt)]

    def specs(linear_step):
        tok = pl.BlockSpec((1, nt, WIDTH_B), lambda *g: (seq0 + linear_step(*g), 0, 0))
        cache = pl.BlockSpec((1, nh, dh, wbuf), lambda *g: (seq0 + linear_step(*g), 0, 0, 0))
        out = pl.BlockSpec((1, nt, WIDTH_B), lambda *g: (linear_step(*g), 0, 0))
        mspecs = [pl.BlockSpec(m.shape, lambda *g: (0, 0), pipeline_mode=pl.Buffered(1))
                  for m in masks]
        return tok, cache, out, mspecs

    return masks, specs


def _mlp_attn(x, oa, ob, p, qb, kn, vn, cache_kt, cache_vt, tm, seq0):
    n, d = x.shape
    nt = qb.shape[1]
    steps = n // tm
    assert n % tm == 0 and seq0 + steps <= qb.shape[0]
    masks, specs = _attn_sample_specs(qb, cache_kt, seq0)
    tok, cache, tok_out, mspecs = specs(lambda i: i)
    row = lambda w: pl.BlockSpec((tm, w), lambda i: (i, 0))
    const = lambda a: pl.BlockSpec(a.shape, lambda i: (0, 0), pipeline_mode=pl.Buffered(1))
    return pl.pallas_call(
        _mlp_attn_kernel,
        grid=(steps,),
        in_specs=[row(d), row(WIDTH_A), row(WIDTH_B), const(p["w_o"]), const(p["n2"]),
                  const(p["w_up"]), const(p["w_down"]), tok, tok, tok, cache, cache] + mspecs,
        out_specs=[row(d), tok_out],
        out_shape=[jax.ShapeDtypeStruct((n, d), F32),
                   jax.ShapeDtypeStruct((steps, nt, WIDTH_B), F32)],
        compiler_params=pltpu.CompilerParams(
            dimension_semantics=("arbitrary",), vmem_limit_bytes=VMEM_LIMIT),
        name="mlp_attn",
    )(x, oa, ob, p["w_o"], p["n2"], p["w_up"], p["w_down"], qb, kn, vn, cache_kt, cache_vt, *masks)


def _layer_params(norm1_g, w_in, conv_w, a_log, dt_bias, delta_norm_g, q_norm_g, k_norm_g, w_o,
                  norm2_g, w_up, w_down):
    d = w_in.shape[0]
    n_gate = 2 * N_HEADS_A
    gate0 = CONV_CH + WIDTH_A
    w_a = w_in[:, :gate0].astype(BF16)
    w_b = w_in[:, gate0 + n_gate:].astype(BF16)
    w_g = jnp.pad(w_in[:, gate0:gate0 + n_gate], ((0, 0), (0, LANES - n_gate))).astype(BF16)
    lane_pad = lambda a: jnp.zeros((1, LANES), F32).at[0, N_HEADS_A:n_gate].set(a.astype(F32))
    hid = np.arange(WIDTH_B) // HEAD_DIM_B
    head_mean = jnp.asarray((hid[:, None] == hid[None, :]).astype(np.float32) / HEAD_DIM_B, BF16)
    return {
        "n1": norm1_g.reshape(1, d).astype(F32), "w_a": w_a, "w_b": w_b, "w_g": w_g, "conv_w": conv_w.astype(F32),
        "alog": lane_pad(a_log), "dtb": lane_pad(dt_bias),
        "qng": jnp.tile(q_norm_g.astype(F32), N_HEADS_B).reshape(1, WIDTH_B),
        "kng": jnp.tile(k_norm_g.astype(F32), N_HEADS_B).reshape(1, WIDTH_B),
        "hm": head_mean, "dng": delta_norm_g.reshape(1, HEAD_DIM_A).astype(F32),
        "w_o": w_o.astype(BF16), "n2": norm2_g.reshape(1, d).astype(F32),
        "w_up": w_up.astype(BF16), "w_down": w_down.astype(BF16),
    }


MLP_TILE = 256
PROJ_TILE = 256


def _layer(xp, xs, state_conv, s0_s, cache_k, cache_v, p):
    b, t, d = xp.shape
    nb, nt, _ = xs.shape
    n = b * t
    pbuf = min(MAX_WINDOW, t)

    qa, ka, va, z, gt, qb, kn, vn, conv_s = _proj_sample(xs, state_conv, p, bt=min(nb, 64))
    oa_s, s_s = _delta(qa, ka, va, z, gt, s0_s, p["dng"], chunk=nt, nchunk=1, bb=min(nb, 16),
                       out_dtype=F32)
    ckt, cvt = cache_k.transpose(0, 2, 3, 1), cache_v.transpose(0, 2, 3, 1)
    attn_s = (qb, kn, vn, ckt, cvt)

    steps = n // PROJ_TILE
    fuse = n % PROJ_TILE == 0 and PROJ_TILE == MLP_TILE and 2 * steps == nb
    qa, ka, va, z, gt, kp, vp, conv_p, *dil = _proj_prompt(
        xp, p, tm=PROJ_TILE, pbuf=pbuf, attn=attn_s if fuse else None)
    window = lambda a: a.reshape(b, N_HEADS_B, HEAD_DIM_B, pbuf).transpose(0, 3, 1, 2)
    kp, vp = window(kp), window(vp)
    s0_p = jnp.zeros((b, N_HEADS_A, HEAD_DIM_A, HEAD_DIM_A), F32)
    oa_p, s_p = _delta(qa, ka, va, z, gt, s0_p, p["dng"], chunk=DELTA_CHUNK, nchunk=2, bb=b,
                       out_dtype=BF16)
    ob_p = _attn_prompt(dil[0:3], dil[3:6], dil[6:9])

    mlp_in = (xp.reshape(n, d), oa_p.reshape(n, WIDTH_A), ob_p.reshape(n, WIDTH_B), p)
    if fuse:
        yp, ob_s2 = _mlp_attn(*mlp_in, *attn_s, tm=MLP_TILE, seq0=steps)
        ob_s = jnp.concatenate([dil[9], ob_s2], axis=0)
    else:
        yp = _mlp(*mlp_in, tm=MLP_TILE)
        ob_s = _attn_sample(*attn_s)
    ys = _mlp(xs.reshape(nb * nt, d), oa_s.reshape(nb * nt, WIDTH_A),
              ob_s.reshape(nb * nt, WIDTH_B), p, tm=min(MLP_TILE, nb * nt))
    heads = lambda a: a.reshape(nb, nt, N_HEADS_B, HEAD_DIM_B)
    return ((yp.reshape(b, t, d), kp, vp, s_p, conv_p),
            (ys.reshape(nb, nt, d), heads(kn), heads(vn), s_s, conv_s))


def kernel(x_prompt, x_sample, cache_swa_k, cache_swa_v, state_delta, state_conv, norm1_g, w_in,
           conv_w, a_log, dt_bias, delta_norm_g, q_norm_g, k_norm_g, w_o, norm2_g, w_up, w_down):
    depth = w_in.shape[0]
    b, s, _ = x_prompt.shape
    nb, nt, _ = x_sample.shape
    wbuf = cache_swa_k.shape[2]
    assert s % ATTN_TILE == 0 and nt == SUBLANES and wbuf == MAX_WINDOW
    yp, ys = x_prompt, x_sample
    outs = [[] for _ in range(8)]
    for layer in range(depth):
        p = _layer_params(norm1_g[layer], w_in[layer], conv_w[layer], a_log[layer], dt_bias[layer],
                          delta_norm_g[layer], q_norm_g[layer], k_norm_g[layer], w_o[layer],
                          norm2_g[layer], w_up[layer], w_down[layer])
        (yp, kp, vp, dp, cp), (ys, kn, vn, dn, cn) = _layer(
            yp, ys, state_conv[layer], state_delta[layer], cache_swa_k[layer], cache_swa_v[layer], p)
        for lst, val in zip(outs, (kp, vp, dp, cp, kn, vn, dn, cn)):
            lst.append(val)
    return (yp, ys) + tuple(jnp.stack(o) for o in outs)
```

```python
import functools

import numpy as np
import jax
import jax.numpy as jnp
from jax import lax
from jax.experimental import pallas as pl
from jax.experimental.pallas import tpu as pltpu

F32 = jnp.float32
BF16 = jnp.bfloat16

N_HEADS_A = 4
HEAD_DIM_A = 128
WIDTH_A = N_HEADS_A * HEAD_DIM_A
N_HEADS_B = 8
HEAD_DIM_B = 64
WIDTH_B = N_HEADS_B * HEAD_DIM_B
CONV_WIDTH = 4
CONV_CH = 3 * WIDTH_A
DELTA_CHUNK = 64
MAX_WINDOW = 2048
NORM_EPS = 1e-6
LANES = 128
SUBLANES = 8
NEG = -1e30
LOG2_E = 1.4426950408889634

C_QKV, C_Z = 0, CONV_CH
C_QB, C_KB, C_VB = 0, WIDTH_B, 2 * WIDTH_B

ATTN_DILATIONS = (1, 4, 16)
ATTN_BAND = 128
ATTN_TILE = 2048
ATTN_UNROLL = 2
VMEM_LIMIT = 56 * 1024 * 1024


def _dot(a, b):
    return jnp.dot(a.astype(BF16), b.astype(BF16), preferred_element_type=F32)


def _dot_nt(a, b):
    return lax.dot_general(a.astype(BF16), b.astype(BF16), (((1,), (1,)), ((), ())),
                           preferred_element_type=F32)


def _dot_tn(a, b):
    return lax.dot_general(a.astype(BF16), b.astype(BF16), (((0,), (0,)), ((), ())),
                           preferred_element_type=F32)


def _split3(x):
    hi = x.astype(BF16)
    r = x - hi.astype(F32)
    mid = r.astype(BF16)
    lo = (r - mid.astype(F32)).astype(BF16)
    return hi, mid, lo


def _dot_exact_lhs(mask_bf16, x):
    hi, mid, lo = _split3(x)
    d = lambda p: jnp.dot(mask_bf16, p, preferred_element_type=F32)
    return d(hi) + d(mid) + d(lo)


def _sigmoid(x):
    return 0.5 * jnp.tanh(0.5 * x) + 0.5


def _silu(x):
    return x * _sigmoid(x)


def _softplus(x):
    return jnp.maximum(x, 0.0) + jnp.log1p(jnp.exp(-jnp.abs(x)))


def _rmsnorm(x, g):
    return x * lax.rsqrt(jnp.mean(x * x, axis=-1, keepdims=True) + NORM_EPS) * g


def _proj_body(x, ext_ref, shift, pad, refs, nsplit=1):
    n1_ref, wa_ref, wb_ref, wg_ref = refs[0:4]
    rows = x.shape[0] // nsplit
    raws = []
    for part in range(nsplit):
        hn = _rmsnorm(x[part * rows:(part + 1) * rows], n1_ref[...]).astype(BF16)
        proj = lambda w, hn=hn: jnp.dot(hn, w, preferred_element_type=F32)
        raws.append((proj(wa_ref[:, C_QKV:C_QKV + CONV_CH]),
                     proj(wb_ref[:, C_QB:C_QB + WIDTH_B]), proj(wb_ref[:, C_KB:C_KB + WIDTH_B]),
                     proj(wb_ref[:, C_VB:C_VB + WIDTH_B]),
                     proj(wa_ref[:, C_Z:C_Z + WIDTH_A]), proj(wg_ref[...])))
    outs = [_proj_epilogue(raw, ext_ref, shift, pad + part * rows, refs)
            for part, raw in enumerate(raws)]
    return tuple(jnp.concatenate(vals, axis=0) if nsplit > 1 else vals[0] for vals in zip(*outs))


def _proj_epilogue(raw, ext_ref, shift, pad, refs):
    cw_ref, al_ref, dtb_ref, qng_ref, kng_ref, hm_ref = refs[4:]
    u, qb, kb, vb, z, gc = raw
    tm = u.shape[0]
    ext_ref[pad:pad + tm, :] = u
    cw = cw_ref[...]
    y = u * cw[3:4, :]
    for i in range(CONV_WIDTH - 1):
        off = pad - (CONV_WIDTH - 1 - i) * shift
        y = y + ext_ref[off:off + tm, :] * cw[i:i + 1, :]
    y = _silu(y)
    qa, ka = [], []
    for h in range(N_HEADS_A):
        lo = h * HEAD_DIM_A
        qh = y[:, lo:lo + HEAD_DIM_A]
        qa.append(qh * (lax.rsqrt(
            jnp.sum(qh * qh, axis=-1, keepdims=True) + NORM_EPS) * HEAD_DIM_A ** -0.5))
        kh = y[:, WIDTH_A + lo:WIDTH_A + lo + HEAD_DIM_A]
        ka.append(kh * lax.rsqrt(jnp.sum(kh * kh, axis=-1, keepdims=True) + NORM_EPS))
    qa = jnp.concatenate(qa, axis=1)
    ka = jnp.concatenate(ka, axis=1)
    va = y[:, 2 * WIDTH_A:3 * WIDTH_A]

    hm = hm_ref[...]

    def headnorm(v, g):
        ms = jnp.dot((v * v).astype(BF16), hm, preferred_element_type=F32)
        return v * lax.rsqrt(ms + NORM_EPS) * g

    qb = headnorm(qb, qng_ref[...])
    kb = headnorm(kb, kng_ref[...])

    lane = lax.broadcasted_iota(jnp.int32, gc.shape, 1)
    beta = _sigmoid(gc)
    g = -jnp.exp(al_ref[...]) * _softplus(gc + dtb_ref[...])
    gates = jnp.where(lane < N_HEADS_A, beta, g)
    return qa, ka, va, z, gates, qb, kb, vb


PROJ_SPLIT = 2
N_PROJ_PARAMS = 10
N_ATTN_SAMPLE_INPUTS = 7


def _proj_prompt_kernel(*refs, first_win, with_attn):
    x_ref, params = refs[0], refs[1:1 + N_PROJ_PARAMS]
    pos = 1 + N_PROJ_PARAMS
    attn_in = refs[pos:pos + N_ATTN_SAMPLE_INPUTS] if with_attn else ()
    pos += len(attn_in)
    outs = refs[pos:pos + 17]
    pos += 17
    attn_out = refs[pos:pos + 1] if with_attn else ()
    pos += len(attn_out)
    scratch = refs[pos:]
    stages = [_proj_prompt_stages(x_ref, params, outs, scratch, first_win)]
    if with_attn:
        stages.insert(0, _attn_sample_stages(*attn_in, *attn_out))
    _run_stages(*stages)


def _proj_prompt_stages(x_ref, params, outs, scratch, first_win):
    (qa_ref, ka_ref, va_ref, z_ref, gt_ref, kb_ref, vb_ref, cn_ref,
     q1_ref, q4_ref, q16_ref, k1_ref, k4_ref, k16_ref, v1_ref, v4_ref, v16_ref) = outs
    ext_ref, dil_ref = scratch
    t = pl.program_id(1)
    tm = x_ref.shape[1]

    @pl.when(t == 0)
    def _():
        ext_ref[0:SUBLANES, :] = jnp.zeros((SUBLANES, CONV_CH), F32)

    @pl.when(t > 0)
    def _():
        ext_ref[0:SUBLANES, :] = ext_ref[tm:tm + SUBLANES, :]

    qa, ka, va, z, gates, qb, kb, vb = _proj_body(x_ref[0], ext_ref, 1, SUBLANES, params,
                                                  nsplit=PROJ_SPLIT)
    yield
    for r, v in zip((qa_ref, ka_ref, va_ref, z_ref, gt_ref), (qa, ka, va, z, gates)):
        r[0] = v
    cn_ref[0] = ext_ref[tm + SUBLANES - (CONV_WIDTH - 1):tm + SUBLANES, :]

    @pl.when(t >= first_win)
    def _():
        kb_ref[0] = kb.T
        vb_ref[0] = vb.T

    yield
    nchunk = WIDTH_B // LANES
    q_att = qb * (HEAD_DIM_B ** -0.5 * LOG2_E)
    for ai, (val, r1, r4, r16) in enumerate(((q_att, q1_ref, q4_ref, q16_ref),
                                              (kb, k1_ref, k4_ref, k16_ref),
                                              (vb, v1_ref, v4_ref, v16_ref))):
        r1[0] = val.astype(BF16)
        for c in range(nchunk):
            dil_ref[0, ai * nchunk + c] = val[:, c * LANES:(c + 1) * LANES]
        n4, n16 = tm // 4, tm // 16
        for r in range(4):
            parts = [dil_ref[0, ai * nchunk + c, pl.ds(r, n4, stride=4), :] for c in range(nchunk)]
            r4[0, r] = jnp.concatenate(parts, axis=1).astype(BF16)
            for c in range(nchunk):
                dil_ref[1, ai * nchunk + c, r * n4:(r + 1) * n4, :] = parts[c]
        for r in range(4):
            for r2 in range(4):
                r16[0, r + 4 * r2] = jnp.concatenate(
                    [dil_ref[1, ai * nchunk + c, pl.ds(r * n4 + r2, n16, stride=4), :]
                     for c in range(nchunk)], axis=1).astype(BF16)


def _proj_sample_kernel(x_ref, st_ref, n1_ref, wa_ref, wb_ref, wg_ref, cw_ref, al_ref, dtb_ref,
                        qng_ref, kng_ref, hm_ref, qa_ref, ka_ref, va_ref, z_ref, gt_ref, qb_ref, kb_ref, vb_ref,
                        cn_ref, ext_ref):
    nb, nt, d = x_ref.shape
    ncv = CONV_WIDTH - 1
    tm = nb * nt
    pad = ncv * nb
    for i in range(ncv):
        ext_ref[i * nb:(i + 1) * nb, :] = st_ref[:, i, :]
    x_tm = jnp.concatenate([x_ref[:, t, :] for t in range(nt)], axis=0)
    vals = _proj_body(x_tm, ext_ref, nb, pad,
                      (n1_ref, wa_ref, wb_ref, wg_ref, cw_ref, al_ref, dtb_ref, qng_ref, kng_ref,
                       hm_ref))
    for r, v in zip((qa_ref, ka_ref, va_ref, z_ref, gt_ref, qb_ref, kb_ref, vb_ref), vals):
        for t in range(nt):
            r[:, t, :] = v[t * nb:(t + 1) * nb]
    for i in range(ncv):
        cn_ref[:, i, :] = ext_ref[tm + i * nb:tm + (i + 1) * nb, :]


def _proj_params(p):
    full = lambda a: pl.BlockSpec(a.shape, lambda *_: (0,) * a.ndim,
                                  pipeline_mode=pl.Buffered(1))
    arrs = (p["n1"], p["w_a"], p["w_b"], p["w_g"], p["conv_w"], p["alog"], p["dtb"], p["qng"],
            p["kng"], p["hm"])
    return arrs, [full(a) for a in arrs]


def _proj_prompt(x, p, tm, pbuf, attn=None):
    b, t, d = x.shape
    assert pbuf % tm == 0 and t % tm == 0
    first_win = (t - pbuf) // tm
    arrs, specs = _proj_params(p)
    assert len(arrs) == N_PROJ_PARAMS
    steps_t = t // tm
    attn_args, attn_in_specs, attn_out_shape, attn_out_specs = [], [], [], []
    if attn is not None:
        qb, kn, vn, ckt, cvt = attn
        assert b * steps_t <= qb.shape[0]
        masks, mk = _attn_sample_specs(qb, ckt, 0)
        tok, cache, tok_out, mspecs = mk(lambda i, j: i * steps_t + j)
        attn_args = [qb, kn, vn, ckt, cvt] + masks
        attn_in_specs = [tok, tok, tok, cache, cache] + mspecs
        assert len(attn_args) == N_ATTN_SAMPLE_INPUTS
        attn_out_shape = [jax.ShapeDtypeStruct((b * steps_t, qb.shape[1], WIDTH_B), F32)]
        attn_out_specs = [tok_out]
    row = lambda w: pl.BlockSpec((1, tm, w), lambda i, j: (i, j, 0))
    widths = (WIDTH_A, WIDTH_A, WIDTH_A, WIDTH_A, LANES)
    out_shape = [jax.ShapeDtypeStruct((b, t, w), F32) for w in widths]
    out_specs = [row(w) for w in widths]
    for _ in range(2):
        out_shape.append(jax.ShapeDtypeStruct((b, WIDTH_B, pbuf), F32))
        out_specs.append(pl.BlockSpec((1, WIDTH_B, tm),
                                      lambda i, j: (i, 0, jnp.maximum(j - first_win, 0))))
    out_shape.append(jax.ShapeDtypeStruct((b, CONV_WIDTH - 1, CONV_CH), F32))
    out_specs.append(pl.BlockSpec((1, CONV_WIDTH - 1, CONV_CH), lambda i, j: (i, 0, 0)))
    for _ in range(3):
        out_shape.append(jax.ShapeDtypeStruct((b, t, WIDTH_B), BF16))
        out_specs.append(row(WIDTH_B))
        for dil in ATTN_DILATIONS[1:]:
            out_shape.append(jax.ShapeDtypeStruct((b, dil, t // dil, WIDTH_B), BF16))
            out_specs.append(pl.BlockSpec((1, dil, tm // dil, WIDTH_B), lambda i, j: (i, 0, j, 0)))
    return pl.pallas_call(
        functools.partial(_proj_prompt_kernel, first_win=first_win, with_attn=attn is not None),
        grid=(b, steps_t),
        in_specs=[row(d)] + specs + attn_in_specs,
        out_specs=out_specs + attn_out_specs,
        out_shape=out_shape + attn_out_shape,
        scratch_shapes=[pltpu.VMEM((tm + SUBLANES, CONV_CH), F32),
                        pltpu.VMEM((2, 3 * WIDTH_B // LANES, tm, LANES), F32)],
        compiler_params=pltpu.CompilerParams(
            dimension_semantics=("arbitrary", "arbitrary"), vmem_limit_bytes=VMEM_LIMIT),
        name="proj_prompt",
    )(x, *arrs, *attn_args)


def _proj_sample(x, state_conv, p, bt):
    nb, nt, d = x.shape
    ncv = CONV_WIDTH - 1
    arrs, specs = _proj_params(p)
    blk = lambda r, w: pl.BlockSpec((bt, r, w), lambda i: (i, 0, 0))
    widths = (WIDTH_A, WIDTH_A, WIDTH_A, WIDTH_A, LANES, WIDTH_B, WIDTH_B, WIDTH_B)
    out_shape = [jax.ShapeDtypeStruct((nb, nt, w), F32) for w in widths]
    out_shape.append(jax.ShapeDtypeStruct((nb, ncv, CONV_CH), F32))
    out_specs = [blk(nt, w) for w in widths] + [blk(ncv, CONV_CH)]
    return pl.pallas_call(
        _proj_sample_kernel,
        grid=(nb // bt,),
        in_specs=[blk(nt, d), blk(ncv, CONV_CH)] + specs,
        out_specs=out_specs,
        out_shape=out_shape,
        scratch_shapes=[pltpu.VMEM((bt * (nt + ncv), CONV_CH), F32)],
        compiler_params=pltpu.CompilerParams(
            dimension_semantics=("arbitrary",), vmem_limit_bytes=VMEM_LIMIT),
        name="proj_sample",
    )(x, state_conv, *arrs)


def _delta_kernel(q_ref, k_ref, v_ref, z_ref, gt_ref, s0_ref, ng_ref, o_ref, sf_ref, s_scr,
                  *, chunk, nchunk, bb):
    c = chunk
    t = pl.program_id(1)

    @pl.when(t == 0)
    def _():
        s_scr[...] = s0_ref[...]

    ii = lax.broadcasted_iota(jnp.int32, (c, c), 0)
    jj = lax.broadcasted_iota(jnp.int32, (c, c), 1)
    causal = ii >= jj
    strict = ii > jj
    tril_bf = causal.astype(F32).astype(BF16)
    eye = (ii == jj).astype(F32)
    level_masks = []
    s = 1
    while s < c:
        level_masks.append(((ii // (2 * s)) == (jj // (2 * s))) & ((ii & s) != 0) & ((jj & s) == 0))
        s *= 2
    ng = ng_ref[...]

    heads = range(N_HEADS_A)
    items = [(bi, ci, h) for bi in range(bb) for ci in range(nchunk) for h in heads]
    pre = {}
    for bi in range(bb):
        for ci in range(nchunk):
            rows = slice(ci * c, (ci + 1) * c)
            gt = gt_ref[bi, rows, :]
            gcum = _dot_exact_lhs(tril_bf, gt)
            if c < LANES:
                gpad = jnp.concatenate([gcum, jnp.zeros((LANES - c, LANES), F32)], axis=0)
            else:
                gpad = gcum
            gcum_t = gpad.T
            for h in heads:
                lo = h * HEAD_DIM_A
                q = q_ref[bi, rows, lo:lo + HEAD_DIM_A]
                k = k_ref[bi, rows, lo:lo + HEAD_DIM_A]
                v = v_ref[bi, rows, lo:lo + HEAD_DIM_A]
                beta = gt[:, h:h + 1]
                g_col = gcum[:, N_HEADS_A + h:N_HEADS_A + h + 1]
                g_row = gcum_t[N_HEADS_A + h:N_HEADS_A + h + 1, 0:c]
                g_last = gcum[c - 1:c, N_HEADS_A + h:N_HEADS_A + h + 1]
                decay = jnp.where(causal, jnp.exp(jnp.where(causal, g_col - g_row, 0.0)), 0.0)
                exp_g = jnp.exp(g_col)
                kb = k * beta
                pre[bi, ci, h] = dict(
                    q=q, k=k, kb=kb, decay=decay, g_last=g_last,
                    rhs=jnp.concatenate([v * beta, kb * exp_g], axis=1),
                    q_dec=q * exp_g, k_dec=k * jnp.exp(g_last - g_col))
    lmat = {it: jnp.where(strict, _dot_nt(pre[it]["kb"], pre[it]["k"]) * pre[it]["decay"], 0.0)
            for it in items}
    qk = {it: _dot_nt(pre[it]["q"], pre[it]["k"]) * pre[it]["decay"] for it in items}
    tinv = {it: eye - jnp.where(level_masks[0], lmat[it], 0.0) for it in items}
    for msk in level_masks[1:]:
        te = {it: _dot(tinv[it], jnp.where(msk, lmat[it], 0.0)) for it in items}
        tinv = {it: tinv[it] - _dot(te[it], tinv[it]) for it in items}
    uw = {it: _dot(tinv[it], pre[it]["rhs"]) for it in items}

    for bi in range(bb):
        st = {h: s_scr[bi, h] for h in heads}
        for ci in range(nchunk):
            rows = slice(ci * c, (ci + 1) * c)
            ws = {h: _dot(jnp.concatenate([uw[bi, ci, h][:, HEAD_DIM_A:2 * HEAD_DIM_A],
                                           pre[bi, ci, h]["q_dec"]], axis=0), st[h])
                  for h in heads}
            v_new = {h: uw[bi, ci, h][:, 0:HEAD_DIM_A] - ws[h][0:c] for h in heads}
            o = {h: ws[h][c:2 * c] + _dot(qk[bi, ci, h], v_new[h]) for h in heads}
            st = {h: st[h] * jnp.exp(pre[bi, ci, h]["g_last"])
                  + _dot_tn(pre[bi, ci, h]["k_dec"], v_new[h]) for h in heads}
            for h in heads:
                lo = h * HEAD_DIM_A
                zz = z_ref[bi, rows, lo:lo + HEAD_DIM_A]
                o_ref[bi, rows, lo:lo + HEAD_DIM_A] = (
                    _rmsnorm(o[h], ng) * _silu(zz)).astype(o_ref.dtype)
        for h in heads:
            s_scr[bi, h] = st[h]

    @pl.when(t == pl.num_programs(1) - 1)
    def _():
        sf_ref[...] = s_scr[...]


def _delta(qa, ka, va, z, gt, s0, ng, *, chunk, nchunk, bb, out_dtype):
    b, t, _ = qa.shape
    tc = chunk * nchunk
    row = lambda w: pl.BlockSpec((bb, tc, w), lambda i, j: (i, j, 0))
    sspec = pl.BlockSpec((bb, N_HEADS_A, HEAD_DIM_A, HEAD_DIM_A), lambda i, j: (i, 0, 0, 0))
    return pl.pallas_call(
        functools.partial(_delta_kernel, chunk=chunk, nchunk=nchunk, bb=bb),
        grid=(b // bb, t // tc),
        in_specs=[row(WIDTH_A)] * 4 + [row(LANES), sspec,
                                       pl.BlockSpec((1, HEAD_DIM_A), lambda i, j: (0, 0))],
        out_specs=[row(WIDTH_A), sspec],
        out_shape=[jax.ShapeDtypeStruct((b, t, WIDTH_A), out_dtype),
                   jax.ShapeDtypeStruct((b, N_HEADS_A, HEAD_DIM_A, HEAD_DIM_A), F32)],
        scratch_shapes=[pltpu.VMEM((bb, N_HEADS_A, HEAD_DIM_A, HEAD_DIM_A), F32)],
        compiler_params=pltpu.CompilerParams(
            dimension_semantics=("arbitrary", "arbitrary"), vmem_limit_bytes=VMEM_LIMIT),
        name="delta_c%d" % chunk,
    )(qa, ka, va, z, gt, s0, ng)


def _attn_prompt_kernel(q1, q4, q16, k1p, k1c, k4p, k4c, k16p, k16c, v1p, v1c, v4p, v4c, v16p, v16c,
                        o_ref, acc, mrep, *, tq):
    t = pl.program_id(2)
    rr = ATTN_BAND
    ii = lax.broadcasted_iota(jnp.int32, (rr, rr), 0)
    jj = lax.broadcasted_iota(jnp.int32, (rr, rr), 1)
    lower = jj <= ii
    upper = jj >= ii
    lane = lax.broadcasted_iota(jnp.int32, (rr, LANES), 1)
    own = [(lane >= hh * HEAD_DIM_B) & (lane < (hh + 1) * HEAD_DIM_B) for hh in range(2)]
    pen_t = jnp.where(t > 0, jnp.float32(0.0), jnp.float32(NEG))
    one = jnp.ones((rr, LANES), BF16)
    zero = jnp.zeros((rr, LANES), BF16)

    ld = lambda ref_idx: ref_idx[0][ref_idx[1]]

    def weights(units):
        chains = [(j, hh) for j in range(len(units)) for hh in range(2)]
        qv = {ch: jnp.where(own[ch[1]], ld(units[ch[0]]["q"]), zero) for ch in chains}
        kcat = [jnp.concatenate([ld(u["kc"]), ld(u["kp"])], axis=0) for u in units]
        raw = {ch: _dot_nt(qv[ch], kcat[ch[0]]) for ch in chains}
        out = {}
        for ch in chains:
            s_c = jnp.where(lower, raw[ch][:, 0:rr], NEG)
            s_p = raw[ch][:, rr:2 * rr]
            if units[ch[0]]["pen"] is not None:
                s_p = s_p + units[ch[0]]["pen"]
            s_p = jnp.where(upper, s_p, NEG)
            m = jnp.max(jnp.maximum(s_c, s_p), axis=-1, keepdims=True)
            out[ch] = (jnp.exp2(s_c - m).astype(BF16), jnp.exp2(s_p - m).astype(BF16), m)
        return out

    def accumulate(units, wts, first):
        vc = [ld(u["vc"]) for u in units]
        vp = [ld(u["vp"]) for u in units]
        pv = {ch: jnp.dot(jnp.concatenate([p_c, p_p], axis=1),
                          jnp.concatenate([jnp.where(own[ch[1]], vc[ch[0]], one),
                                           jnp.where(own[ch[1]], vp[ch[0]], one)], axis=0),
                          preferred_element_type=F32)
              for ch, (p_c, p_p, _) in wts.items()}
        for ch, (_, _, m) in wts.items():
            j, hh = ch
            rows = units[j]["rows"]
            m_b = jnp.broadcast_to(m, (rr, LANES))
            if first:
                acc[hh, rows, :] = pv[ch]
                mrep[hh, rows, :] = m_b
            else:
                m_old = mrep[hh, rows, :]
                m_new = jnp.maximum(m_old, m_b)
                acc[hh, rows, :] = (acc[hh, rows, :] * jnp.exp2(m_old - m_new)
                                    + pv[ch] * jnp.exp2(m_b - m_new))
                mrep[hh, rows, :] = m_new

    uu = ATTN_UNROLL
    blk = lambda j: slice(j * rr, (j + 1) * rr)

    d16, d4 = ATTN_DILATIONS[2], ATTN_DILATIONS[1]
    assert tq // d16 == rr and (tq // (d4 * rr)) % uu == 0 and d16 % uu == 0
    trips = []
    full = slice(None)
    for it in range(d16 // uu):
        trips.append((True, [dict(
            q=(q16, (0, r)), kc=(k16c, (0, r)), kp=(k16p, (0, r)), vc=(v16c, (0, r)),
            vp=(v16p, (0, r)), pen=pen_t, rows=pl.ds(r, rr, stride=d16))
            for r in range(it * uu, (it + 1) * uu)]))
    for it in range(tq // (rr * uu)):
        units = []
        for j in range(it * uu, (it + 1) * uu):
            cur = (0, blk(j), full)
            prv = (0, blk(j - 1), full) if j > 0 else (0, slice(tq - rr, tq), full)
            units.append(dict(q=(q1, cur), kc=(k1c, cur), vc=(v1c, cur),
                              kp=(k1c if j > 0 else k1p, prv), vp=(v1c if j > 0 else v1p, prv),
                              pen=None if j > 0 else pen_t, rows=pl.ds(j * rr, rr)))
        trips.append((False, units))
    nb4 = tq // d4 // rr
    for r in range(d4):
        for jt in range(nb4 // uu):
            units = []
            for j in range(jt * uu, (jt + 1) * uu):
                cur = (0, r, blk(j), full)
                prv = (0, r, blk(j - 1), full) if j > 0 else (0, r, blk(nb4 - 1), full)
                units.append(dict(q=(q4, cur), kc=(k4c, cur), vc=(v4c, cur),
                                  kp=(k4c if j > 0 else k4p, prv),
                                  vp=(v4c if j > 0 else v4p, prv),
                                  pen=None if j > 0 else pen_t,
                                  rows=pl.ds(r + d4 * rr * j, rr, stride=d4)))
            trips.append((False, units))

    pending = None
    for first, units in trips:
        wts = weights(units)
        if pending is not None:
            accumulate(*pending)
        pending = (units, wts, first)
    accumulate(*pending)

    lane_t = lax.broadcasted_iota(jnp.int32, (tq, LANES), 1)
    a0 = acc[0]
    a1 = acc[1]
    l0 = a0[:, HEAD_DIM_B:HEAD_DIM_B + 1]
    l1 = a1[:, 0:1]
    o_ref[0] = jnp.where(lane_t < HEAD_DIM_B, a0 / l0, a1 / l1).astype(o_ref.dtype)


def _attn_prompt(q, k, v):
    b, t, _ = q[0].shape
    tq = ATTN_TILE
    nt = t // tq
    prev_j = lambda j: jnp.maximum(j - 1, 0)
    cur, prev = [], []
    for dil in ATTN_DILATIONS:
        if dil == 1:
            cur.append(pl.BlockSpec((1, tq, LANES), lambda i, h, j: (i, j, h)))
            prev.append(pl.BlockSpec((1, tq, LANES), lambda i, h, j: (i, prev_j(j), h)))
        else:
            cur.append(pl.BlockSpec((1, dil, tq // dil, LANES), lambda i, h, j: (i, 0, j, h)))
            prev.append(pl.BlockSpec((1, dil, tq // dil, LANES),
                                     lambda i, h, j: (i, 0, prev_j(j), h)))
    kv_specs = [s for pc in zip(prev, cur) for s in pc]
    kv_args = lambda x: [a for xd in x for a in (xd, xd)]
    return pl.pallas_call(
        functools.partial(_attn_prompt_kernel, tq=tq),
        grid=(b, WIDTH_B // LANES, nt),
        in_specs=cur + kv_specs + kv_specs,
        out_specs=cur[0],
        out_shape=jax.ShapeDtypeStruct((b, t, WIDTH_B), BF16),
        scratch_shapes=[pltpu.VMEM((2, tq, LANES), F32), pltpu.VMEM((2, tq, LANES), F32)],
        compiler_params=pltpu.CompilerParams(
            dimension_semantics=("arbitrary", "arbitrary", "arbitrary"),
            vmem_limit_bytes=VMEM_LIMIT),
        name="attn_prompt",
    )(*q, *kv_args(k), *kv_args(v))


def _multiplicity(delta):
    delta = np.asarray(delta)
    ok = delta >= 0
    m = ((delta <= 128).astype(np.float32)
         + ((delta <= 512) & (delta % 4 == 0)).astype(np.float32)
         + ((delta <= 2048) & (delta % 16 == 0)).astype(np.float32))
    return np.where(ok, m, 0.0).astype(np.float32)


def _sample_masks(wbuf, nt):
    i = np.arange(nt)[:, None]
    return (_multiplicity(wbuf + i - np.arange(wbuf)[None, :]),
            _multiplicity(i - np.arange(nt)[None, :]))


def _run_stages(*gens):
    gens = list(gens)
    while gens:
        for g in list(gens):
            if next(g, StopIteration) is StopIteration:
                gens.remove(g)


def _attn_sample_kernel(*refs):
    _run_stages(_attn_sample_stages(*refs))


def _attn_sample_stages(q_ref, kn_ref, vn_ref, kt_ref, vt_ref, mc_ref, mn_ref, o_ref):
    dh = HEAD_DIM_B
    heads = range(N_HEADS_B)
    mc = mc_ref[...]
    mn = mn_ref[...]
    sl = lambda r, h: r[0, :, h * dh:(h + 1) * dh]
    q = {h: (sl(q_ref, h) * (dh ** -0.5)).astype(BF16) for h in heads}
    s_c = {h: jnp.where(mc > 0.0, _dot(q[h], kt_ref[0, h]), NEG) for h in heads}
    s_n = {h: jnp.where(mn > 0.0, _dot_nt(q[h], sl(kn_ref, h)), NEG) for h in heads}
    yield
    m = {h: jnp.maximum(jnp.max(s_c[h], axis=-1, keepdims=True),
                        jnp.max(s_n[h], axis=-1, keepdims=True)) for h in heads}
    p_c = {h: (mc * jnp.exp(s_c[h] - m[h])).astype(BF16) for h in heads}
    p_n = {h: (mn * jnp.exp(s_n[h] - m[h])).astype(BF16) for h in heads}
    den = {h: (jnp.sum(p_c[h].astype(F32), axis=-1, keepdims=True)
               + jnp.sum(p_n[h].astype(F32), axis=-1, keepdims=True)) for h in heads}
    yield
    out = {h: _dot_nt(p_c[h], vt_ref[0, h]) + _dot(p_n[h], sl(vn_ref, h)) for h in heads}
    yield
    o_ref[0] = jnp.concatenate([out[h] / den[h] for h in heads], axis=1)


def _attn_sample(qb, kn, vn, cache_kt, cache_vt):
    nb, nt, _ = qb.shape
    _, nh, dh, wbuf = cache_kt.shape
    masks = [jnp.asarray(m) for m in _sample_masks(wbuf, nt)]
    tok = pl.BlockSpec((1, nt, WIDTH_B), lambda i: (i, 0, 0))
    cache = pl.BlockSpec((1, nh, dh, wbuf), lambda i: (i, 0, 0, 0))
    mspec = lambda m: pl.BlockSpec(m.shape, lambda i: (0, 0))
    return pl.pallas_call(
        _attn_sample_kernel,
        grid=(nb,),
        in_specs=[tok, tok, tok, cache, cache] + [mspec(m) for m in masks],
        out_specs=tok,
        out_shape=jax.ShapeDtypeStruct((nb, nt, WIDTH_B), F32),
        compiler_params=pltpu.CompilerParams(
            dimension_semantics=("arbitrary",), vmem_limit_bytes=VMEM_LIMIT),
        name="attn_sample",
    )(qb, kn, vn, cache_kt, cache_vt, *masks)


FF_CHUNK = 1024


def _mlp_kernel(*refs):
    _run_stages(_mlp_stages(*refs))


def _mlp_stages(x_ref, oa_ref, ob_ref, wo_ref, n2_ref, wu_ref, wd_ref, y_ref):
    mix = jnp.concatenate([oa_ref[...].astype(BF16), ob_ref[...].astype(BF16)], axis=1)
    h1 = x_ref[...] + jnp.dot(mix, wo_ref[...], preferred_element_type=F32)
    hn = _rmsnorm(h1, n2_ref[...]).astype(BF16)
    yield
    chunks = range(0, wu_ref.shape[1], FF_CHUNK)
    hid = [jnp.dot(hn, wu_ref[:, c0:c0 + FF_CHUNK], preferred_element_type=F32) for c0 in chunks]
    act = [jnp.square(jnp.maximum(h, 0.0)).astype(BF16) for h in hid]
    yield
    y = h1
    for c0, a in zip(chunks, act):
        y = y + jnp.dot(a, wd_ref[c0:c0 + FF_CHUNK, :], preferred_element_type=F32)
    yield
    y_ref[...] = y


def _mlp(x, oa, ob, p, tm):
    n, d = x.shape
    row = lambda w: pl.BlockSpec((tm, w), lambda i: (i, 0))
    const = lambda a: pl.BlockSpec(a.shape, lambda i: (0, 0))
    return pl.pallas_call(
        _mlp_kernel,
        grid=(n // tm,),
        in_specs=[row(d), row(WIDTH_A), row(WIDTH_B), const(p["w_o"]), const(p["n2"]),
                  const(p["w_up"]), const(p["w_down"])],
        out_specs=row(d),
        out_shape=jax.ShapeDtypeStruct((n, d), F32),
        compiler_params=pltpu.CompilerParams(
            dimension_semantics=("arbitrary",), vmem_limit_bytes=VMEM_LIMIT),
        name="mlp",
    )(x, oa, ob, p["w_o"], p["n2"], p["w_up"], p["w_down"])


def _mlp_attn_kernel(x_ref, oa_ref, ob_ref, wo_ref, n2_ref, wu_ref, wd_ref,
                     q_ref, kn_ref, vn_ref, kt_ref, vt_ref, mc_ref, mn_ref, y_ref, o_ref):
    _run_stages(_attn_sample_stages(q_ref, kn_ref, vn_ref, kt_ref, vt_ref, mc_ref, mn_ref, o_ref),
                _mlp_stages(x_ref, oa_ref, ob_ref, wo_ref, n2_ref, wu_ref, wd_ref, y_ref))


def _attn_sample_specs(qb, cache_kt, seq0):
    nb, nt, _ = qb.shape
    _, nh, dh, wbuf = cache_kt.shape
    masks = [jnp.asarray(m) for m in _sample_masks(wbuf, nt)]

    def specs(linear_step):
        tok = pl.BlockSpec((1, nt, WIDTH_B), lambda *g: (seq0 + linear_step(*g), 0, 0))
        cache = pl.BlockSpec((1, nh, dh, wbuf), lambda *g: (seq0 + linear_step(*g), 0, 0, 0))
        out = pl.BlockSpec((1, nt, WIDTH_B), lambda *g: (linear_step(*g), 0, 0))
        mspecs = [pl.BlockSpec(m.shape, lambda *g: (0, 0), pipeline_mode=pl.Buffered(1))
                  for m in masks]
        return tok, cache, out, mspecs

    return masks, specs


def _mlp_attn(x, oa, ob, p, qb, kn, vn, cache_kt, cache_vt, tm, seq0):
    n, d = x.shape
    nt = qb.shape[1]
    steps = n // tm
    assert n % tm == 0 and seq0 + steps <= qb.shape[0]
    masks, specs = _attn_sample_specs(qb, cache_kt, seq0)
    tok, cache, tok_out, mspecs = specs(lambda i: i)
    row = lambda w: pl.BlockSpec((tm, w), lambda i: (i, 0))
    const = lambda a: pl.BlockSpec(a.shape, lambda i: (0, 0), pipeline_mode=pl.Buffered(1))
    return pl.pallas_call(
        _mlp_attn_kernel,
        grid=(steps,),
        in_specs=[row(d), row(WIDTH_A), row(WIDTH_B), const(p["w_o"]), const(p["n2"]),
                  const(p["w_up"]), const(p["w_down"]), tok, tok, tok, cache, cache] + mspecs,
        out_specs=[row(d), tok_out],
        out_shape=[jax.ShapeDtypeStruct((n, d), F32),
                   jax.ShapeDtypeStruct((steps, nt, WIDTH_B), F32)],
        compiler_params=pltpu.CompilerParams(
            dimension_semantics=("arbitrary",), vmem_limit_bytes=VMEM_LIMIT),
        name="mlp_attn",
    )(x, oa, ob, p["w_o"], p["n2"], p["w_up"], p["w_down"], qb, kn, vn, cache_kt, cache_vt, *masks)


def _layer_params(norm1_g, w_in, conv_w, a_log, dt_bias, delta_norm_g, q_norm_g, k_norm_g, w_o,
                  norm2_g, w_up, w_down):
    d = w_in.shape[0]
    n_gate = 2 * N_HEADS_A
    gate0 = CONV_CH + WIDTH_A
    w_a = w_in[:, :gate0].astype(BF16)
    w_b = w_in[:, gate0 + n_gate:].astype(BF16)
    w_g = jnp.pad(w_in[:, gate0:gate0 + n_gate], ((0, 0), (0, LANES - n_gate))).astype(BF16)
    lane_pad = lambda a: jnp.zeros((1, LANES), F32).at[0, N_HEADS_A:n_gate].set(a.astype(F32))
    hid = np.arange(WIDTH_B) // HEAD_DIM_B
    head_mean = jnp.asarray((hid[:, None] == hid[None, :]).astype(np.float32) / HEAD_DIM_B, BF16)
    return {
        "n1": norm1_g.reshape(1, d).astype(F32), "w_a": w_a, "w_b": w_b, "w_g": w_g, "conv_w": conv_w.astype(F32),
        "alog": lane_pad(a_log), "dtb": lane_pad(dt_bias),
        "qng": jnp.tile(q_norm_g.astype(F32), N_HEADS_B).reshape(1, WIDTH_B),
        "kng": jnp.tile(k_norm_g.astype(F32), N_HEADS_B).reshape(1, WIDTH_B),
        "hm": head_mean, "dng": delta_norm_g.reshape(1, HEAD_DIM_A).astype(F32),
        "w_o": w_o.astype(BF16), "n2": norm2_g.reshape(1, d).astype(F32),
        "w_up": w_up.astype(BF16), "w_down": w_down.astype(BF16),
    }


MLP_TILE = 256
PROJ_TILE = 256


def _layer(xp, xs, state_conv, s0_s, cache_k, cache_v, p):
    b, t, d = xp.shape
    nb, nt, _ = xs.shape
    n = b * t
    pbuf = min(MAX_WINDOW, t)

    qa, ka, va, z, gt, qb, kn, vn, conv_s = _proj_sample(xs, state_conv, p, bt=min(nb, 64))
    oa_s, s_s = _delta(qa, ka, va, z, gt, s0_s, p["dng"], chunk=nt, nchunk=1, bb=min(nb, 16),
                       out_dtype=F32)
    ckt, cvt = cache_k.transpose(0, 2, 3, 1), cache_v.transpose(0, 2, 3, 1)
    attn_s = (qb, kn, vn, ckt, cvt)

    steps = n // PROJ_TILE
    fuse = n % PROJ_TILE == 0 and PROJ_TILE == MLP_TILE and 2 * steps == nb
    qa, ka, va, z, gt, kp, vp, conv_p, *dil = _proj_prompt(
        xp, p, tm=PROJ_TILE, pbuf=pbuf, attn=attn_s if fuse else None)
    window = lambda a: a.reshape(b, N_HEADS_B, HEAD_DIM_B, pbuf).transpose(0, 3, 1, 2)
    kp, vp = window(kp), window(vp)
    s0_p = jnp.zeros((b, N_HEADS_A, HEAD_DIM_A, HEAD_DIM_A), F32)
    oa_p, s_p = _delta(qa, ka, va, z, gt, s0_p, p["dng"], chunk=DELTA_CHUNK, nchunk=2, bb=b,
                       out_dtype=BF16)
    ob_p = _attn_prompt(dil[0:3], dil[3:6], dil[6:9])

    mlp_in = (xp.reshape(n, d), oa_p.reshape(n, WIDTH_A), ob_p.reshape(n, WIDTH_B), p)
    if fuse:
        yp, ob_s2 = _mlp_attn(*mlp_in, *attn_s, tm=MLP_TILE, seq0=steps)
        ob_s = jnp.concatenate([dil[9], ob_s2], axis=0)
    else:
        yp = _mlp(*mlp_in, tm=MLP_TILE)
        ob_s = _attn_sample(*attn_s)
    ys = _mlp(xs.reshape(nb * nt, d), oa_s.reshape(nb * nt, WIDTH_A),
              ob_s.reshape(nb * nt, WIDTH_B), p, tm=min(MLP_TILE, nb * nt))
    heads = lambda a: a.reshape(nb, nt, N_HEADS_B, HEAD_DIM_B)
    return ((yp.reshape(b, t, d), kp, vp, s_p, conv_p),
            (ys.reshape(nb, nt, d), heads(kn), heads(vn), s_s, conv_s))


def kernel(x_prompt, x_sample, cache_swa_k, cache_swa_v, state_delta, state_conv, norm1_g, w_in,
           conv_w, a_log, dt_bias, delta_norm_g, q_norm_g, k_norm_g, w_o, norm2_g, w_up, w_down):
    depth = w_in.shape[0]
    b, s, _ = x_prompt.shape
    nb, nt, _ = x_sample.shape
    wbuf = cache_swa_k.shape[2]
    assert s % ATTN_TILE == 0 and nt == SUBLANES and wbuf == MAX_WINDOW
    yp, ys = x_prompt, x_sample
    outs = [[] for _ in range(8)]
    for layer in range(depth):
        p = _layer_params(norm1_g[layer], w_in[layer], conv_w[layer], a_log[layer], dt_bias[layer],
                          delta_norm_g[layer], q_norm_g[layer], k_norm_g[layer], w_o[layer],
                          norm2_g[layer], w_up[layer], w_down[layer])
        (yp, kp, vp, dp, cp), (ys, kn, vn, dn, cn) = _layer(
            yp, ys, state_conv[layer], state_delta[layer], cache_swa_k[layer], cache_swa_v[layer], p)
        for lst, val in zip(outs, (kp, vp, dp, cp, kn, vn, dn, cn)):
            lst.append(val)
    return (yp, ys) + tuple(jnp.stack(o) for o in outs)
```
